```python
import jax, jax.numpy as jnp
from jax import lax
import numpy as np

D_MODEL = 1024
BATCH = 1
SEQ = 16384
DEPTH = 1
DEC_BATCH = 8
DEC_SEQ = 4096
PAST_LEN = 128

GRID_W = 64
NA_HEADS = 8
NA_HEAD_DIM = 64
NA_WIDTH = NA_HEADS * NA_HEAD_DIM
NA_ROWS = 8
NA_COLS = 16
HG_HEADS = 4
HG_KEY_DIM = 128
HG_VAL_DIM = 128
HG_WIDTH = HG_HEADS * HG_KEY_DIM
HG_CHUNK = 64
N_EXPERTS = 256
TOP_K = 8
N_GROUPS = 8
TOPK_GROUPS = 4
D_EXPERT = 256
D_SHARED = 256
ROUTE_SCALE = 2.5
MOE_BLOCK = 128
N_MOD = 6
RMS_EPS = 1e-6
IN_COLS = 3 * NA_WIDTH + 5 * HG_WIDTH + 2 * D_MODEL

kernel_name = "hybrid_natten_hgrn2_moe_encoder"


def rms_norm(x, g):
    xf = x.astype(jnp.float32)
    y = xf * lax.rsqrt(jnp.mean(xf * xf, axis=-1, keepdims=True) + RMS_EPS)
    return (y * g.astype(jnp.float32)).astype(x.dtype)


def neighbourhood_attention(q, k, v, rpb):
    B, T = q.shape[0], q.shape[1]
    rows = T // GRID_W
    kr = min(NA_ROWS, rows)
    grid = lambda a: a.reshape(B, rows, GRID_W, NA_HEADS, NA_HEAD_DIM)
    qg, kg, vg = grid(q), grid(k), grid(v)
    col = jnp.arange(GRID_W)
    col_start = jnp.clip(col - NA_COLS // 2, 0, GRID_W - NA_COLS)
    col_idx = col_start[:, None] + jnp.arange(NA_COLS)[None, :]
    col_off = col_idx - col[:, None]
    scale = NA_HEAD_DIM ** -0.5

    def one_row(r):
        rs = jnp.clip(r - kr // 2, 0, rows - kr)
        k_rows = lax.dynamic_slice_in_dim(kg, rs, kr, axis=1)
        v_rows = lax.dynamic_slice_in_dim(vg, rs, kr, axis=1)
        k_win = k_rows[:, :, col_idx]
        v_win = v_rows[:, :, col_idx]
        q_r = lax.dynamic_index_in_dim(qg, r, axis=1, keepdims=False)
        s = jnp.einsum('bchd,bkcjhd->bhckj', q_r, k_win).astype(jnp.float32) * scale
        row_off = rs + jnp.arange(kr) - r
        bias = rpb[:, row_off + NA_ROWS - 1][:, :, col_off + NA_COLS - 1]
        s = s + jnp.transpose(bias, (0, 2, 1, 3)).astype(jnp.float32)[None]
        p = jax.nn.softmax(s.reshape(B, NA_HEADS, GRID_W, kr * NA_COLS), axis=-1)
        p = p.reshape(B, NA_HEADS, GRID_W, kr, NA_COLS).astype(v.dtype)
        return jnp.einsum('bhckj,bkcjhd->bchd', p, v_win)

    out = lax.map(one_row, jnp.arange(rows))
    return jnp.transpose(out, (1, 0, 2, 3, 4)).reshape(B, T, NA_WIDTH)


def hgrn2_scan(q, log_f, kin, v):
    B, T = q.shape[0], q.shape[1]
    n = T // HG_CHUNK
    chunk = lambda a: jnp.transpose(a.reshape(B, n, HG_CHUNK, HG_HEADS, a.shape[-1]), (1, 0, 3, 2, 4))
    tri = jnp.tril(jnp.ones((HG_CHUNK, HG_CHUNK), dtype=bool))[:, :, None]

    def step(S, inp):
        qc, lfc, kc, vc = inp
        G = jnp.cumsum(lfc, axis=2)
        diff = G[:, :, :, None, :] - G[:, :, None, :, :]
        decay = jnp.exp(jnp.where(tri, diff, -jnp.inf))
        A = jnp.einsum('bhtk,bhtsk,bhsk->bhts', qc, decay, kc)
        o = jnp.einsum('bhts,bhsv->bhtv', A, vc) + jnp.einsum('bhtk,bhkv->bhtv', qc * jnp.exp(G), S)
        G_last = G[:, :, -1]
        S = jnp.exp(G_last)[..., None] * S + jnp.einsum(
            'bhsk,bhsv->bhkv', kc * jnp.exp(G_last[:, :, None, :] - G), vc)
        return S, o

    S0 = jnp.zeros((B, HG_HEADS, HG_KEY_DIM, HG_VAL_DIM), jnp.float32)
    _, o = lax.scan(step, S0, (chunk(q), chunk(log_f), chunk(kin), chunk(v)))
    return jnp.transpose(o, (1, 0, 3, 2, 4)).reshape(B, T, HG_HEADS, HG_VAL_DIM)


def hgrn2_mixer(q, zf_fwd, zf_bwd, i_in, g_out, lb, norm_w):
    B, T = q.shape[0], q.shape[1]
    heads = lambda a: a.reshape(B, T, HG_HEADS, -1).astype(jnp.float32)
    lbh = lb.reshape(HG_HEADS, HG_KEY_DIM).astype(jnp.float32)

    def gates(z):
        zh = heads(z)
        f = lbh + (1.0 - lbh) * jax.nn.sigmoid(zh)
        return jnp.log(f), (1.0 - lbh) * jax.nn.sigmoid(-zh)

    qh, vh = heads(q), heads(i_in)
    lf_f, k_f = gates(zf_fwd)
    lf_b, k_b = gates(zf_bwd)
    flip = lambda a: jnp.flip(a, axis=1)
    o = hgrn2_scan(qh, lf_f, k_f, vh) + flip(hgrn2_scan(flip(qh), flip(lf_b), flip(k_b), flip(vh)))
    o = o * lax.rsqrt(jnp.mean(o * o, axis=-1, keepdims=True) + RMS_EPS)
    o = o * norm_w.reshape(HG_HEADS, HG_VAL_DIM).astype(jnp.float32)
    return o.reshape(B, T, HG_WIDTH).astype(q.dtype) * jax.nn.silu(g_out)


def moe_ffn(h, w_router, b_router, w_exp_gate, w_exp_up, w_exp_down, w_sh_gate, w_sh_up, w_sh_down):
    T = h.shape[0]
    scores = jax.nn.sigmoid(h.astype(jnp.float32) @ w_router.astype(jnp.float32))
    sel = scores + b_router.astype(jnp.float32)
    per_group = N_EXPERTS // N_GROUPS
    group_score = lax.top_k(sel.reshape(T, N_GROUPS, per_group), 2)[0].sum(-1)
    _, gidx = lax.top_k(group_score, TOPK_GROUPS)
    gmask = jnp.sum(jax.nn.one_hot(gidx, N_GROUPS, dtype=jnp.float32), axis=1) > 0
    emask = jnp.repeat(gmask, per_group, axis=-1)
    _, eidx = lax.top_k(jnp.where(emask, sel, -jnp.inf), TOP_K)
    w = jnp.take_along_axis(scores, eidx, axis=-1)
    w = w / jnp.sum(w, axis=-1, keepdims=True) * ROUTE_SCALE

    n_assign = T * TOP_K
    flat_e = eidx.reshape(-1)
    flat_tok = jnp.repeat(jnp.arange(T, dtype=jnp.int32), TOP_K)
    flat_w = w.reshape(-1)
    order = jnp.argsort(flat_e, stable=True)
    se, stok, sw = flat_e[order], flat_tok[order], flat_w[order]
    counts = jnp.bincount(flat_e, length=N_EXPERTS)
    starts = jnp.cumsum(counts) - counts
    padded = (counts + MOE_BLOCK - 1) // MOE_BLOCK * MOE_BLOCK
    pend = jnp.cumsum(padded)
    pstarts = pend - padded
    pos = pstarts[se] + (jnp.arange(n_assign) - starts[se])
    n_rows = n_assign + N_EXPERTS * MOE_BLOCK
    n_blocks = n_rows // MOE_BLOCK
    buf_tok = jnp.full((n_rows,), T, jnp.int32).at[pos].set(stok)
    buf_w = jnp.zeros((n_rows,), jnp.float32).at[pos].set(sw)
    block_e = jnp.clip(jnp.searchsorted(pend, jnp.arange(n_blocks) * MOE_BLOCK, side='right'),
                       0, N_EXPERTS - 1).astype(jnp.int32)
    h_pad = jnp.concatenate([h, jnp.zeros((1, h.shape[1]), h.dtype)], axis=0)

    def block(y, inp):
        tok, e, wb = inp
        xb = h_pad[tok]
        a = xb @ w_exp_gate[e]
        b = xb @ w_exp_up[e]
        out = (jax.nn.silu(a) * b) @ w_exp_down[e]
        return y.at[tok].add(out * wb[:, None].astype(out.dtype)), None

    y0 = jnp.zeros((T + 1, h.shape[1]), h.dtype)
    y, _ = lax.scan(block, y0, (buf_tok.reshape(n_blocks, MOE_BLOCK), block_e,
                                buf_w.reshape(n_blocks, MOE_BLOCK)))
    shared = (jax.nn.silu(h @ w_sh_gate) * (h @ w_sh_up)) @ w_sh_down
    return y[:T] + shared


def encoder_layer(x, c, lb, w_ada, b_ada, norm_mix, w_in, na_rpb, hg_norm, w_branch_a, w_branch_b,
                  w_out, norm_ffn, w_router, b_router, w_exp_gate, w_exp_up, w_exp_down,
                  w_sh_gate, w_sh_up, w_sh_down):
    B, T, D = x.shape
    mod = jax.nn.silu(c) @ w_ada + b_ada
    sh1, sc1, g1, sh2, sc2, g2 = [m[:, None, :] for m in jnp.split(mod, N_MOD, axis=-1)]

    u = rms_norm(x, norm_mix) * (1 + sc1) + sh1
    proj = u @ w_in
    cuts = np.cumsum([NA_WIDTH] * 3 + [HG_WIDTH] * 5 + [D_MODEL]).tolist()
    (na_q, na_k, na_v, hg_q, hg_ff, hg_fb, hg_i, hg_g, gate_a, gate_b) = jnp.split(proj, cuts, axis=-1)
    hd = lambda a: a.reshape(B, T, NA_HEADS, NA_HEAD_DIM)
    y_a = neighbourhood_attention(hd(na_q), hd(na_k), hd(na_v), na_rpb) @ w_branch_a
    y_b = hgrn2_mixer(hg_q, hg_ff, hg_fb, hg_i, hg_g, lb, hg_norm) @ w_branch_b
    merged = jax.nn.sigmoid(gate_a) * y_a + jax.nn.sigmoid(gate_b) * y_b
    x = x + g1 * (merged @ w_out)

    u2 = rms_norm(x, norm_ffn) * (1 + sc2) + sh2
    f = moe_ffn(u2.reshape(B * T, D), w_router, b_router, w_exp_gate, w_exp_up, w_exp_down,
                w_sh_gate, w_sh_up, w_sh_down).reshape(B, T, D)
    return x + g2 * f


def trunk(x, c, w_ada, b_ada, norm_mix, w_in, na_rpb, hg_lb, hg_norm, w_branch_a, w_branch_b,
          w_out, norm_ffn, w_router, b_router, w_exp_gate, w_exp_up, w_exp_down,
          w_sh_gate, w_sh_up, w_sh_down, norm_final):
    lb_all = jnp.cumsum(jax.nn.softmax(hg_lb.astype(jnp.float32), axis=0), axis=0)
    for l in range(DEPTH):
        x = encoder_layer(x, c, lb_all[l], w_ada[l], b_ada[l], norm_mix[l], w_in[l], na_rpb[l],
                          hg_norm[l], w_branch_a[l], w_branch_b[l], w_out[l], norm_ffn[l],
                          w_router[l], b_router[l], w_exp_gate[l], w_exp_up[l], w_exp_down[l],
                          w_sh_gate[l], w_sh_up[l], w_sh_down[l])
    return rms_norm(x, norm_final)


def setup_inputs(seed: int = 0) -> dict:
    key = jax.random.key(seed)
    ks = jax.random.split(key, 26)
    nrm = lambda k, shape, s: jax.random.normal(k, shape, jnp.float32) * s
    L, D, E = DEPTH, D_MODEL, N_EXPERTS
    return {
        "x_prompt": nrm(ks[0], (BATCH, SEQ, D), 1.0),
        "x_sample": nrm(ks[1], (DEC_BATCH, DEC_SEQ, D), 1.0),
        "c_prompt": nrm(ks[2], (BATCH, D), 1.0),
        "c_sample": nrm(ks[3], (DEC_BATCH, D), 1.0),
        "w_ada": nrm(ks[4], (L, D, N_MOD * D), 0.5 * D ** -0.5),
        "b_ada": nrm(ks[5], (L, N_MOD * D), 0.02),
        "norm_mix": 1.0 + nrm(ks[6], (L, D), 0.02),
        "w_in": nrm(ks[7], (L, D, IN_COLS), D ** -0.5),
        "na_rpb": nrm(ks[8], (L, NA_HEADS, 2 * NA_ROWS - 1, 2 * NA_COLS - 1), 0.5),
        "hg_lb": 1.0 + nrm(ks[9], (L + 1, HG_WIDTH), 0.1),
        "hg_norm": 1.0 + nrm(ks[10], (L, HG_WIDTH), 0.02),
        "w_branch_a": nrm(ks[11], (L, NA_WIDTH, D), NA_WIDTH ** -0.5),
        "w_branch_b": nrm(ks[12], (L, HG_WIDTH, D), HG_WIDTH ** -0.5),
        "w_out": nrm(ks[13], (L, D, D), D ** -0.5),
        "norm_ffn": 1.0 + nrm(ks[14], (L, D), 0.02),
        "w_router": nrm(ks[15], (L, D, E), D ** -0.5),
        "b_router": nrm(ks[16], (L, E), 0.01),
        "w_exp_gate": nrm(ks[17], (L, E, D, D_EXPERT), D ** -0.5),
        "w_exp_up": nrm(ks[18], (L, E, D, D_EXPERT), D ** -0.5),
        "w_exp_down": nrm(ks[19], (L, E, D_EXPERT, D), D_EXPERT ** -0.5),
        "w_sh_gate": nrm(ks[20], (L, D, D_SHARED), D ** -0.5),
        "w_sh_up": nrm(ks[21], (L, D, D_SHARED), D ** -0.5),
        "w_sh_down": nrm(ks[22], (L, D_SHARED, D), D_SHARED ** -0.5),
        "norm_final": 1.0 + nrm(ks[23], (D,), 0.02),
    }


def reference(x_prompt, x_sample, c_prompt, c_sample, w_ada, b_ada, norm_mix, w_in, na_rpb, hg_lb,
              hg_norm, w_branch_a, w_branch_b, w_out, norm_ffn, w_router, b_router, w_exp_gate,
              w_exp_up, w_exp_down, w_sh_gate, w_sh_up, w_sh_down, norm_final):
    y_prompt = trunk(x_prompt, c_prompt, w_ada, b_ada, norm_mix, w_in, na_rpb, hg_lb, hg_norm,
                     w_branch_a, w_branch_b, w_out, norm_ffn, w_router, b_router, w_exp_gate,
                     w_exp_up, w_exp_down, w_sh_gate, w_sh_up, w_sh_down, norm_final)
    y_sample = trunk(x_sample, c_sample, w_ada, b_ada, norm_mix, w_in, na_rpb, hg_lb, hg_norm,
                     w_branch_a, w_branch_b, w_out, norm_ffn, w_router, b_router, w_exp_gate,
                     w_exp_up, w_exp_down, w_sh_gate, w_sh_up, w_sh_down, norm_final)
    return (y_prompt, y_sample)
```

```python
import functools

import jax
import jax.numpy as jnp
import numpy as np
from jax import lax
from jax.experimental import pallas as pl
from jax.experimental.pallas import tpu as pltpu

D_MODEL = 1024
GRID_W = 64
NA_HEADS = 8
NA_HEAD_DIM = 64
NA_WIDTH = NA_HEADS * NA_HEAD_DIM
NA_ROWS = 8
NA_COLS = 16
HG_HEADS = 4
HG_KEY_DIM = 128
HG_WIDTH = HG_HEADS * HG_KEY_DIM
HG_CHUNK = 64
N_EXPERTS = 256
TOP_K = 8
N_GROUPS = 8
TOPK_GROUPS = 4
D_EXPERT = 256
ROUTE_SCALE = 2.5
N_MOD = 6
RMS_EPS = 1e-6

MOE_BLOCK = 256
NA_GROUP = 8
NA_TOK = NA_GROUP * GRID_W
HG_STEP = 256
MASK_VALUE = -1e30

F32 = jnp.float32
BF16 = jnp.bfloat16
HIGHEST = lax.Precision.HIGHEST
NT_DIMS = (((1,), (1,)), ((), ()))
TN_DIMS = (((0,), (0,)), ((), ()))

SLAB_Q, SLAB_K, SLAB_V, SLAB_HQ = (0, 0), (0, 1), (1, 0), (1, 1)
SLAB_FF, SLAB_FB, SLAB_HI, SLAB_HG = (2, 0), (2, 1), (3, 0), (3, 1)
SLAB_GA, SLAB_GB = 4, 5


def _params(vmem_mb, sem=None):
    kw = dict(vmem_limit_bytes=vmem_mb * 1024 * 1024)
    if sem is not None:
        kw["dimension_semantics"] = sem
    return pltpu.CompilerParams(**kw)


class _Seqs:
    def __init__(self, bp, tp, bs, ts):
        self.bp, self.tp, self.bs, self.ts = bp, tp, bs, ts
        self.np_ = bp * tp
        self.n = bp * tp + bs * ts
        self.nseq = bp + bs

    def info(self, t0):
        in_p = t0 < self.np_
        rel = jnp.maximum(t0 - self.np_, 0)
        sid = jnp.where(in_p, t0 // self.tp, self.bp + rel // self.ts)
        start = jnp.where(in_p, (t0 // self.tp) * self.tp, self.np_ + (rel // self.ts) * self.ts)
        length = jnp.where(in_p, self.tp, self.ts)
        return sid, start, length


def _ada_kernel(c_ref, w_ref, b_ref, o_ref):
    c = c_ref[...]
    a = c * jax.nn.sigmoid(c)
    o_ref[...] = jnp.dot(a, w_ref[...], precision=HIGHEST, preferred_element_type=F32) + b_ref[...]


def _ada(c_pad, w_ada, b_ada):
    rows = c_pad.shape[0]
    n_out = w_ada.shape[1]
    tn = 1024
    return pl.pallas_call(
        _ada_kernel,
        grid=(n_out // tn,),
        in_specs=[pl.BlockSpec((rows, D_MODEL), lambda j: (0, 0)),
                  pl.BlockSpec((D_MODEL, tn), lambda j: (0, j)),
                  pl.BlockSpec((1, tn), lambda j: (0, j))],
        out_specs=pl.BlockSpec((rows, tn), lambda j: (0, j)),
        out_shape=jax.ShapeDtypeStruct((rows, n_out), F32),
        compiler_params=_params(32),
        name="ada",
    )(c_pad, w_ada, b_ada.reshape(1, n_out))


def _rms(x):
    return x * lax.rsqrt(jnp.mean(x * x, axis=-1, keepdims=True) + RMS_EPS)


def _inproj_kernel(x_ref, mod_ref, g_ref, w_ref, o_ref, u_scr):
    @pl.when(pl.program_id(1) == 0)
    def _():
        y = _rms(x_ref[...]) * g_ref[...]
        u = y * (1.0 + mod_ref[1:2, :]) + mod_ref[0:1, :]
        u_scr[...] = u.astype(BF16)

    o_ref[...] = jnp.dot(u_scr[...], w_ref[...], preferred_element_type=F32).astype(o_ref.dtype)


def _inproj(seqs, x, mod, norm_mix, w_in_bf):
    n = seqs.n
    tm = min(1024, seqs.tp, seqs.ts)
    tn = 1024
    n_slab = w_in_bf.shape[1] // tn
    return pl.pallas_call(
        _inproj_kernel,
        grid=(n // tm, n_slab),
        in_specs=[pl.BlockSpec((tm, D_MODEL), lambda i, j: (i, 0)),
                  pl.BlockSpec((None, N_MOD, D_MODEL), lambda i, j: (seqs.info(i * tm)[0], 0, 0)),
                  pl.BlockSpec((1, D_MODEL), lambda i, j: (0, 0)),
                  pl.BlockSpec((D_MODEL, tn), lambda i, j: (0, j))],
        out_specs=pl.BlockSpec((None, tm, tn), lambda i, j: (j, i, 0)),
        out_shape=jax.ShapeDtypeStruct((n_slab, n, tn), BF16),
        scratch_shapes=[pltpu.VMEM((tm, D_MODEL), BF16)],
        compiler_params=_params(40, ("arbitrary", "arbitrary")),
        name="inproj",
    )(x, mod, norm_mix.reshape(1, D_MODEL), w_in_bf)


def _na_bias_table(rpb):
    col = np.arange(GRID_W)
    cs = np.clip(col - NA_COLS // 2, 0, GRID_W - NA_COLS)
    valid = (col[None, :] >= cs[:, None]) & (col[None, :] < cs[:, None] + NA_COLS)
    coff = np.clip(col[None, :] - col[:, None] + NA_COLS - 1, 0, 2 * NA_COLS - 2)
    tabs = []
    for s in range(NA_ROWS):
        b = rpb[:, s:s + NA_ROWS][:, :, coff]
        b = jnp.where(valid[None, None], b.astype(F32), MASK_VALUE)
        tabs.append(jnp.transpose(b, (0, 2, 1, 3)).reshape(NA_HEADS, GRID_W, NA_ROWS * GRID_W))
    return jnp.stack(tabs)


def _na_geometry(seqs, g):
    _, start, length = seqs.info(g * NA_TOK)
    row0 = start // GRID_W
    rows = length // GRID_W
    gb0 = row0 // NA_GROUP
    nb = rows // NA_GROUP
    kb = gb0 + jnp.clip(g - gb0 - 1, 0, nb - 3)
    return row0, rows, kb


def _na_kernel(seqs, q_ref, k0, k1, k2, v0, v1, v2, bias_ref, o_ref, kw, vw):
    g = pl.program_id(0)
    row0, rows, kb = _na_geometry(seqs, g)
    for d, (kr, vr) in enumerate(((k0, v0), (k1, v1), (k2, v2))):
        kw[d * NA_TOK:(d + 1) * NA_TOK, :] = kr[...]
        vw[d * NA_TOK:(d + 1) * NA_TOK, :] = vr[...]
    scale = NA_HEAD_DIM ** -0.5

    def row_body(i, carry):
        r = g * NA_GROUP + i - row0
        rs = jnp.clip(r - NA_ROWS // 2, 0, rows - NA_ROWS)
        s_idx = rs - r + NA_ROWS - 1
        ko = pl.multiple_of((rs + row0 - kb * NA_GROUP) * GRID_W, GRID_W)
        qo = pl.multiple_of(i * GRID_W, GRID_W)
        outs = []
        for h in range(NA_HEADS):
            hs = slice(h * NA_HEAD_DIM, (h + 1) * NA_HEAD_DIM)
            qh = q_ref[pl.ds(qo, GRID_W), hs]
            kh = kw[pl.ds(ko, NA_ROWS * GRID_W), hs]
            vh = vw[pl.ds(ko, NA_ROWS * GRID_W), hs]
            s = lax.dot_general(qh, kh, NT_DIMS, preferred_element_type=F32) * scale
            s = s + bias_ref[s_idx, h]
            p = jnp.exp(s - jnp.max(s, axis=-1, keepdims=True))
            l = jnp.sum(p, axis=-1, keepdims=True)
            o = jnp.dot(p.astype(BF16), vh, preferred_element_type=F32)
            outs.append(o / l)
        o_ref[pl.ds(qo, GRID_W), :] = jnp.concatenate(outs, axis=1).astype(o_ref.dtype)
        return carry

    lax.fori_loop(0, NA_GROUP, row_body, 0)


def _na(seqs, proj, bias_tab):
    n = seqs.n

    def kv_spec(slab, d):
        return pl.BlockSpec((None, NA_TOK, NA_WIDTH),
                            lambda g: (slab[0], _na_geometry(seqs, g)[2] + d, slab[1]))

    return pl.pallas_call(
        functools.partial(_na_kernel, seqs),
        grid=(n // NA_TOK,),
        in_specs=[pl.BlockSpec((None, NA_TOK, NA_WIDTH), lambda g: (SLAB_Q[0], g, SLAB_Q[1]))]
        + [kv_spec(SLAB_K, d) for d in range(3)] + [kv_spec(SLAB_V, d) for d in range(3)]
        + [pl.BlockSpec(bias_tab.shape, lambda g: (0, 0, 0, 0))],
        out_specs=pl.BlockSpec((NA_TOK, NA_WIDTH), lambda g: (g, 0)),
        out_shape=jax.ShapeDtypeStruct((n, NA_WIDTH), BF16),
        scratch_shapes=[pltpu.VMEM((3 * NA_TOK, NA_WIDTH), BF16),
                        pltpu.VMEM((3 * NA_TOK, NA_WIDTH), BF16)],
        compiler_params=_params(48, ("arbitrary",)),
        name="natten",
    )(proj, proj, proj, proj, proj, proj, proj, bias_tab)


def _hg_chunk(q, z, v, lb, tri, mask, mid, last, st_ref):
    f = lb + (1.0 - lb) * jax.nn.sigmoid(z)
    lf = jnp.log(f)
    kin = (1.0 - lb) * jax.nn.sigmoid(-z)
    gcum = jnp.dot(tri, lf, precision=HIGHEST, preferred_element_type=F32)
    gm = gcum[mid:mid + 1, :]
    gl = gcum[last:last + 1, :]
    qa = (q * jnp.exp(gcum - gm)).astype(BF16)
    ka = (kin * jnp.exp(gm - gcum)).astype(BF16)
    qe = (q * jnp.exp(gcum)).astype(BF16)
    kd = (kin * jnp.exp(gl - gcum)).astype(BF16)
    eg = jnp.exp(gl)
    vb = v.astype(BF16)
    outs = []
    for h in range(HG_HEADS):
        hs = slice(h * HG_KEY_DIM, (h + 1) * HG_KEY_DIM)
        a = lax.dot_general(qa[:, hs], ka[:, hs], NT_DIMS, preferred_element_type=F32)
        a = jnp.where(mask, a, 0.0)
        st = st_ref[h]
        o = jnp.dot(a.astype(BF16), vb[:, hs], preferred_element_type=F32)
        o = o + lax.dot_general(qe[:, hs], st.astype(BF16), NT_DIMS, preferred_element_type=F32)
        st_ref[h] = st * eg[:, hs] + lax.dot_general(vb[:, hs], kd[:, hs], TN_DIMS,
                                                    preferred_element_type=F32)
        outs.append(o)
    return jnp.concatenate(outs, axis=1)


def _hg_kernel(seqs, qf_ref, zf_ref, vf_ref, qb_ref, zb_ref, vb_ref, lb_ref, of_ref, ob_ref,
               stf, stb):
    i = pl.program_id(0)
    nsteps = pl.num_programs(0)
    tf = i * HG_STEP
    tb = (nsteps - 1 - i) * HG_STEP
    _, start_f, _ = seqs.info(tf)
    _, start_b, len_b = seqs.info(tb)

    @pl.when(tf == start_f)
    def _():
        stf[...] = jnp.zeros_like(stf)

    @pl.when(tb + HG_STEP == start_b + len_b)
    def _():
        stb[...] = jnp.zeros_like(stb)

    lb = lb_ref[...]
    row = lax.broadcasted_iota(jnp.int32, (HG_CHUNK, HG_CHUNK), 0)
    col = lax.broadcasted_iota(jnp.int32, (HG_CHUNK, HG_CHUNK), 1)
    lower = row >= col
    upper = col >= row
    tri_f = lower.astype(F32)
    tri_b = upper.astype(F32)
    nchunk = HG_STEP // HG_CHUNK
    for c in range(nchunk):
        cs = slice(c * HG_CHUNK, (c + 1) * HG_CHUNK)
        of_ref[cs, :] = _hg_chunk(qf_ref[cs, :].astype(F32), zf_ref[cs, :].astype(F32),
                                  vf_ref[cs, :].astype(F32), lb, tri_f, lower,
                                  HG_CHUNK // 2 - 1, HG_CHUNK - 1, stf)
        cb = nchunk - 1 - c
        bs = slice(cb * HG_CHUNK, (cb + 1) * HG_CHUNK)
        ob_ref[bs, :] = _hg_chunk(qb_ref[bs, :].astype(F32), zb_ref[bs, :].astype(F32),
                                  vb_ref[bs, :].astype(F32), lb, tri_b, upper,
                                  HG_CHUNK // 2, 0, stb)


def _hgrn(seqs, proj, lb):
    n = seqs.n
    nsteps = n // HG_STEP

    def spec(slab, rev):
        if rev:
            return pl.BlockSpec((None, HG_STEP, HG_WIDTH), lambda i: (slab[0], nsteps - 1 - i, slab[1]))
        return pl.BlockSpec((None, HG_STEP, HG_WIDTH), lambda i: (slab[0], i, slab[1]))

    return pl.pallas_call(
        functools.partial(_hg_kernel, seqs),
        grid=(nsteps,),
        in_specs=[spec(SLAB_HQ, False), spec(SLAB_FF, False), spec(SLAB_HI, False),
                  spec(SLAB_HQ, True), spec(SLAB_FB, True), spec(SLAB_HI, True),
                  pl.BlockSpec((1, HG_WIDTH), lambda i: (0, 0))],
        out_specs=[pl.BlockSpec((HG_STEP, HG_WIDTH), lambda i: (i, 0)),
                   pl.BlockSpec((HG_STEP, HG_WIDTH), lambda i: (nsteps - 1 - i, 0))],
        out_shape=[jax.ShapeDtypeStruct((n, HG_WIDTH), F32)] * 2,
        scratch_shapes=[pltpu.VMEM((HG_HEADS, HG_KEY_DIM, HG_KEY_DIM), F32)] * 2,
        compiler_params=_params(32, ("arbitrary",)),
        name="hgrn2",
    )(proj, proj, proj, proj, proj, proj, lb.reshape(1, HG_WIDTH))


def _merge_kernel(x_ref, att_ref, of_ref, ob_ref, hg_ref, ga_ref, gb_ref, mod_ref, hgn_ref,
                  nffn_ref, wa_ref, wb_ref, wo_ref, wr_ref, x1_ref, u2_ref, lg_ref):
    o = of_ref[...] + ob_ref[...]
    parts = []
    for h in range(HG_HEADS):
        hs = slice(h * HG_KEY_DIM, (h + 1) * HG_KEY_DIM)
        parts.append(_rms(o[:, hs]))
    on = jnp.concatenate(parts, axis=1) * hgn_ref[...]
    gate = hg_ref[...].astype(F32)
    hb = (on * (gate * jax.nn.sigmoid(gate))).astype(BF16)
    ya = jnp.dot(att_ref[...], wa_ref[...], preferred_element_type=F32)
    yb = jnp.dot(hb, wb_ref[...], preferred_element_type=F32)
    merged = (jax.nn.sigmoid(ga_ref[...].astype(F32)) * ya
              + jax.nn.sigmoid(gb_ref[...].astype(F32)) * yb)
    x1 = x_ref[...] + mod_ref[2:3, :] * jnp.dot(merged.astype(BF16), wo_ref[...],
                                                preferred_element_type=F32)
    x1_ref[...] = x1
    u2 = _rms(x1) * nffn_ref[...] * (1.0 + mod_ref[4:5, :]) + mod_ref[3:4, :]
    u2_ref[...] = u2.astype(BF16)
    lg_ref[...] = jnp.dot(u2, wr_ref[...], precision=HIGHEST, preferred_element_type=F32)


def _merge(seqs, x, att, o_f, o_b, proj, mod, hg_norm, norm_ffn, wa, wb, wo, wr):
    n = seqs.n
    tm = 256
    tok = lambda i: (i, 0)
    const = lambda i: (0, 0)
    return pl.pallas_call(
        _merge_kernel,
        grid=(n // tm,),
        in_specs=[pl.BlockSpec((tm, D_MODEL), tok),
                  pl.BlockSpec((tm, NA_WIDTH), tok),
                  pl.BlockSpec((tm, HG_WIDTH), tok),
                  pl.BlockSpec((tm, HG_WIDTH), tok),
                  pl.BlockSpec((None, tm, HG_WIDTH), lambda i: (SLAB_HG[0], i, SLAB_HG[1])),
                  pl.BlockSpec((None, tm, D_MODEL), lambda i: (SLAB_GA, i, 0)),
                  pl.BlockSpec((None, tm, D_MODEL), lambda i: (SLAB_GB, i, 0)),
                  pl.BlockSpec((None, N_MOD, D_MODEL), lambda i: (seqs.info(i * tm)[0], 0, 0)),
                  pl.BlockSpec((1, HG_WIDTH), const),
                  pl.BlockSpec((1, D_MODEL), const),
                  pl.BlockSpec((NA_WIDTH, D_MODEL), const),
                  pl.BlockSpec((HG_WIDTH, D_MODEL), const),
                  pl.BlockSpec((D_MODEL, D_MODEL), const),
                  pl.BlockSpec((D_MODEL, N_EXPERTS), const)],
        out_specs=[pl.BlockSpec((tm, D_MODEL), tok),
                   pl.BlockSpec((tm, D_MODEL), tok),
                   pl.BlockSpec((tm, N_EXPERTS), tok)],
        out_shape=[jax.ShapeDtypeStruct((n, D_MODEL), F32),
                   jax.ShapeDtypeStruct((n, D_MODEL), BF16),
                   jax.ShapeDtypeStruct((n, N_EXPERTS), F32)],
        compiler_params=_params(48, ("arbitrary",)),
        name="merge",
    )(x, att, o_f, o_b, proj, proj, proj, mod, hg_norm.reshape(1, HG_WIDTH),
      norm_ffn.reshape(1, D_MODEL), wa, wb, wo, wr)


def _expert_kernel(be_ref, nused_ref, xs_ref, wg_ref, wu_ref, wd_ref, ys_ref):
    @pl.when(pl.program_id(0) < nused_ref[0])
    def _():
        xs = xs_ref[...]
        a = jnp.dot(xs, wg_ref[...], preferred_element_type=F32)
        b = jnp.dot(xs, wu_ref[...], preferred_element_type=F32)
        h = (a * jax.nn.sigmoid(a) * b).astype(BF16)
        ys_ref[...] = jnp.dot(h, wd_ref[...], preferred_element_type=F32).astype(ys_ref.dtype)


def _experts(xs, block_e, n_used, wg, wu, wd):
    n_rows = xs.shape[0]
    n_blocks = n_rows // MOE_BLOCK
    grid_spec = pltpu.PrefetchScalarGridSpec(
        num_scalar_prefetch=2,
        grid=(n_blocks,),
        in_specs=[pl.BlockSpec((MOE_BLOCK, D_MODEL), lambda i, be, nu: (i, 0)),
                  pl.BlockSpec((None, D_MODEL, D_EXPERT), lambda i, be, nu: (be[i], 0, 0)),
                  pl.BlockSpec((None, D_MODEL, D_EXPERT), lambda i, be, nu: (be[i], 0, 0)),
                  pl.BlockSpec((None, D_EXPERT, D_MODEL), lambda i, be, nu: (be[i], 0, 0))],
        out_specs=pl.BlockSpec((MOE_BLOCK, D_MODEL), lambda i, be, nu: (i, 0)),
    )
    return pl.pallas_call(
        _expert_kernel,
        grid_spec=grid_spec,
        out_shape=jax.ShapeDtypeStruct((n_rows, D_MODEL), BF16),
        compiler_params=_params(32, ("arbitrary",)),
        name="experts",
    )(block_e, n_used, xs, wg, wu, wd)


def _route(logits, b_router):
    n = logits.shape[0]
    scores = jax.nn.sigmoid(logits)
    sel = scores + b_router.astype(F32)
    per_group = N_EXPERTS // N_GROUPS
    group_score = lax.top_k(sel.reshape(n, N_GROUPS, per_group), 2)[0].sum(-1)
    _, gidx = lax.top_k(group_score, TOPK_GROUPS)
    gmask = jnp.sum(jax.nn.one_hot(gidx, N_GROUPS, dtype=F32), axis=1) > 0
    emask = jnp.repeat(gmask, per_group, axis=-1)
    _, eidx = lax.top_k(jnp.where(emask, sel, -jnp.inf), TOP_K)
    w = jnp.take_along_axis(scores, eidx, axis=-1)
    w = w / jnp.sum(w, axis=-1, keepdims=True) * ROUTE_SCALE
    return eidx, w


def _dispatch(eidx):
    n = eidx.shape[0]
    n_assign = n * TOP_K
    flat_e = eidx.reshape(-1).astype(jnp.int32)
    flat_tok = jnp.repeat(jnp.arange(n, dtype=jnp.int32), TOP_K)
    order = jnp.argsort(flat_e, stable=True)
    se, stok = flat_e[order], flat_tok[order]
    counts = jnp.bincount(flat_e, length=N_EXPERTS).astype(jnp.int32)
    starts = jnp.cumsum(counts) - counts
    padded = (counts + MOE_BLOCK - 1) // MOE_BLOCK * MOE_BLOCK
    pend = jnp.cumsum(padded)
    pstarts = pend - padded
    pos = pstarts[se] + (jnp.arange(n_assign, dtype=jnp.int32) - starts[se])
    n_rows = n_assign + N_EXPERTS * MOE_BLOCK
    n_blocks = n_rows // MOE_BLOCK
    buf_tok = jnp.zeros((n_rows,), jnp.int32).at[pos].set(stok)
    pos_tk = jnp.zeros((n_assign,), jnp.int32).at[order].set(pos).reshape(n, TOP_K)
    block_e = jnp.clip(jnp.searchsorted(pend, jnp.arange(n_blocks, dtype=jnp.int32) * MOE_BLOCK,
                                        side="right"), 0, N_EXPERTS - 1).astype(jnp.int32)
    n_used = (pend[-1] // MOE_BLOCK).astype(jnp.int32).reshape(1)
    return buf_tok, pos_tk, block_e, n_used


def _final_kernel(x1_ref, u2_ref, yg_ref, w_ref, mod_ref, nf_ref, wsg_ref, wsu_ref, wsd_ref, o_ref):
    u2 = u2_ref[...]
    a = jnp.dot(u2, wsg_ref[...], preferred_element_type=F32)
    b = jnp.dot(u2, wsu_ref[...], preferred_element_type=F32)
    f = jnp.dot((a * jax.nn.sigmoid(a) * b).astype(BF16), wsd_ref[...], preferred_element_type=F32)
    w = w_ref[...]
    for k in range(TOP_K):
        f = f + w[:, k:k + 1] * yg_ref[:, k * D_MODEL:(k + 1) * D_MODEL].astype(F32)
    x2 = x1_ref[...] + mod_ref[5:6, :] * f
    o_ref[...] = _rms(x2) * nf_ref[...]


def _final(seqs, x1, u2, yg, w, mod, norm_final, wsg, wsu, wsd):
    n = seqs.n
    tm = 256
    tok = lambda i: (i, 0)
    const = lambda i: (0, 0)
    return pl.pallas_call(
        _final_kernel,
        grid=(n // tm,),
        in_specs=[pl.BlockSpec((tm, D_MODEL), tok),
                  pl.BlockSpec((tm, D_MODEL), tok),
                  pl.BlockSpec((tm, TOP_K * D_MODEL), tok),
                  pl.BlockSpec((tm, TOP_K), tok),
                  pl.BlockSpec((None, N_MOD, D_MODEL), lambda i: (seqs.info(i * tm)[0], 0, 0)),
                  pl.BlockSpec((1, D_MODEL), const),
                  pl.BlockSpec((D_MODEL, D_EXPERT), const),
                  pl.BlockSpec((D_MODEL, D_EXPERT), const),
                  pl.BlockSpec((D_EXPERT, D_MODEL), const)],
        out_specs=pl.BlockSpec((tm, D_MODEL), tok),
        out_shape=jax.ShapeDtypeStruct((n, D_MODEL), F32),
        compiler_params=_params(48, ("arbitrary",)),
        name="final",
    )(x1, u2, yg, w, mod, norm_final.reshape(1, D_MODEL), wsg, wsu, wsd)


def _layer(seqs, x, c, w_ada, b_ada, norm_mix, w_in, na_rpb, hg_lb, hg_norm, w_branch_a,
           w_branch_b, w_out, norm_ffn, w_router, b_router, w_exp_gate, w_exp_up, w_exp_down,
           w_sh_gate, w_sh_up, w_sh_down, norm_final):
    n = seqs.n
    c_rows = -(-seqs.nseq // 8) * 8
    c_pad = jnp.zeros((c_rows, D_MODEL), F32).at[:seqs.nseq].set(c)
    mod = _ada(c_pad, w_ada[0], b_ada[0])[:seqs.nseq].reshape(seqs.nseq, N_MOD, D_MODEL)
    lb = jnp.cumsum(jax.nn.softmax(hg_lb.astype(F32), axis=0), axis=0)[0]

    proj = _inproj(seqs, x, mod, norm_mix[0], w_in[0].astype(BF16))
    att = _na(seqs, proj, _na_bias_table(na_rpb[0]))
    o_f, o_b = _hgrn(seqs, proj, lb)
    x1, u2, logits = _merge(seqs, x, att, o_f, o_b, proj, mod, hg_norm[0], norm_ffn[0],
                            w_branch_a[0].astype(BF16), w_branch_b[0].astype(BF16),
                            w_out[0].astype(BF16), w_router[0])

    eidx, w = _route(logits, b_router[0])
    buf_tok, pos_tk, block_e, n_used = _dispatch(eidx)
    xs = jnp.take(u2, buf_tok, axis=0)
    ys = _experts(xs, block_e, n_used, w_exp_gate[0].astype(BF16), w_exp_up[0].astype(BF16),
                  w_exp_down[0].astype(BF16))
    yg = jnp.take(ys, pos_tk.reshape(-1), axis=0).reshape(n, TOP_K * D_MODEL)
    return _final(seqs, x1, u2, yg, w, mod, norm_final, w_sh_gate[0].astype(BF16),
                  w_sh_up[0].astype(BF16), w_sh_down[0].astype(BF16))


def kernel(x_prompt, x_sample, c_prompt, c_sample, w_ada, b_ada, norm_mix, w_in, na_rpb, hg_lb, hg_norm, w_branch_a, w_branch_b, w_out, norm_ffn, w_router, b_router, w_exp_gate, w_exp_up, w_exp_down, w_sh_gate, w_sh_up, w_sh_down, norm_final):
    bp, tp, _ = x_prompt.shape
    bs, ts, _ = x_sample.shape
    seqs = _Seqs(bp, tp, bs, ts)
    x = jnp.concatenate([x_prompt.reshape(bp * tp, D_MODEL), x_sample.reshape(bs * ts, D_MODEL)])
    c = jnp.concatenate([c_prompt, c_sample])
    y = _layer(seqs, x, c, w_ada, b_ada, norm_mix, w_in, na_rpb, hg_lb, hg_norm, w_branch_a,
               w_branch_b, w_out, norm_ffn, w_router, b_router, w_exp_gate, w_exp_up, w_exp_down,
               w_sh_gate, w_sh_up, w_sh_down, norm_final)
    return (y[:seqs.np_].reshape(bp, tp, D_MODEL), y[seqs.np_:].reshape(bs, ts, D_MODEL))
```

```python
import functools

import jax
import jax.numpy as jnp
import numpy as np
from jax import lax
from jax.experimental import pallas as pl
from jax.experimental.pallas import tpu as pltpu

D_MODEL = 1024
GRID_W = 64
NA_HEADS = 8
NA_HEAD_DIM = 64
NA_WIDTH = NA_HEADS * NA_HEAD_DIM
NA_ROWS = 8
NA_COLS = 16
HG_HEADS = 4
HG_KEY_DIM = 128
HG_WIDTH = HG_HEADS * HG_KEY_DIM
HG_CHUNK = 64
N_EXPERTS = 256
TOP_K = 8
N_GROUPS = 8
TOPK_GROUPS = 4
D_EXPERT = 256
ROUTE_SCALE = 2.5
N_MOD = 6
RMS_EPS = 1e-6

MOE_BLOCK = 256
NA_GROUP = 8
NA_TOK = NA_GROUP * GRID_W
HG_STEP = 256
MASK_VALUE = -1e30

F32 = jnp.float32
BF16 = jnp.bfloat16
HIGHEST = lax.Precision.HIGHEST
NT_DIMS = (((1,), (1,)), ((), ()))
TN_DIMS = (((0,), (0,)), ((), ()))

SLAB_Q, SLAB_K, SLAB_V, SLAB_HQ = (0, 0), (0, 1), (1, 0), (1, 1)
SLAB_FF, SLAB_FB, SLAB_HI, SLAB_HG = (2, 0), (2, 1), (3, 0), (3, 1)
SLAB_GA, SLAB_GB = 4, 5


def _params(vmem_mb, sem=None):
    kw = dict(vmem_limit_bytes=vmem_mb * 1024 * 1024)
    if sem is not None:
        kw["dimension_semantics"] = sem
    return pltpu.CompilerParams(**kw)


class _Seqs:
    def __init__(self, bp, tp, bs, ts):
        self.bp, self.tp, self.bs, self.ts = bp, tp, bs, ts
        self.np_ = bp * tp
        self.n = bp * tp + bs * ts
        self.nseq = bp + bs

    def info(self, t0):
        in_p = t0 < self.np_
        rel = jnp.maximum(t0 - self.np_, 0)
        sid = jnp.where(in_p, t0 // self.tp, self.bp + rel // self.ts)
        start = jnp.where(in_p, (t0 // self.tp) * self.tp, self.np_ + (rel // self.ts) * self.ts)
        length = jnp.where(in_p, self.tp, self.ts)
        return sid, start, length


def _ada_kernel(c_ref, w_ref, b_ref, o_ref):
    c = c_ref[...]
    a = c * jax.nn.sigmoid(c)
    o_ref[...] = jnp.dot(a, w_ref[...], precision=HIGHEST, preferred_element_type=F32) + b_ref[...]


def _ada(c_pad, w_ada, b_ada):
    rows = c_pad.shape[0]
    n_out = w_ada.shape[1]
    tn = 1024
    return pl.pallas_call(
        _ada_kernel,
        grid=(n_out // tn,),
        in_specs=[pl.BlockSpec((rows, D_MODEL), lambda j: (0, 0)),
                  pl.BlockSpec((D_MODEL, tn), lambda j: (0, j)),
                  pl.BlockSpec((1, tn), lambda j: (0, j))],
        out_specs=pl.BlockSpec((rows, tn), lambda j: (0, j)),
        out_shape=jax.ShapeDtypeStruct((rows, n_out), F32),
        compiler_params=_params(32),
        name="ada",
    )(c_pad, w_ada, b_ada.reshape(1, n_out))


def _rms(x):
    return x * lax.rsqrt(jnp.mean(x * x, axis=-1, keepdims=True) + RMS_EPS)


def _inproj_kernel(x_ref, mod_ref, g_ref, w_ref, o_ref, u_scr):
    @pl.when(pl.program_id(1) == 0)
    def _():
        y = _rms(x_ref[...]) * g_ref[...]
        u = y * (1.0 + mod_ref[1:2, :]) + mod_ref[0:1, :]
        u_scr[...] = u.astype(BF16)

    o_ref[...] = jnp.dot(u_scr[...], w_ref[...], preferred_element_type=F32).astype(o_ref.dtype)


def _inproj(seqs, x, mod, norm_mix, w_in_bf):
    n = seqs.n
    tm = min(1024, seqs.tp, seqs.ts)
    tn = 1024
    n_slab = w_in_bf.shape[1] // tn
    return pl.pallas_call(
        _inproj_kernel,
        grid=(n // tm, n_slab),
        in_specs=[pl.BlockSpec((tm, D_MODEL), lambda i, j: (i, 0)),
                  pl.BlockSpec((None, N_MOD, D_MODEL), lambda i, j: (seqs.info(i * tm)[0], 0, 0)),
                  pl.BlockSpec((1, D_MODEL), lambda i, j: (0, 0)),
                  pl.BlockSpec((D_MODEL, tn), lambda i, j: (0, j))],
        out_specs=pl.BlockSpec((None, tm, tn), lambda i, j: (j, i, 0)),
        out_shape=jax.ShapeDtypeStruct((n_slab, n, tn), BF16),
        scratch_shapes=[pltpu.VMEM((tm, D_MODEL), BF16)],
        compiler_params=_params(40, ("arbitrary", "arbitrary")),
        name="inproj",
    )(x, mod, norm_mix.reshape(1, D_MODEL), w_in_bf)


def _na_bias_table(rpb):
    col = np.arange(GRID_W)
    cs = np.clip(col - NA_COLS // 2, 0, GRID_W - NA_COLS)
    valid = (col[None, :] >= cs[:, None]) & (col[None, :] < cs[:, None] + NA_COLS)
    coff = np.clip(col[None, :] - col[:, None] + NA_COLS - 1, 0, 2 * NA_COLS - 2)
    tabs = []
    for s in range(NA_ROWS):
        b = rpb[:, s:s + NA_ROWS][:, :, coff]
        b = jnp.where(valid[None, None], b.astype(F32), MASK_VALUE)
        tabs.append(jnp.transpose(b, (0, 2, 1, 3)).reshape(NA_HEADS, GRID_W, NA_ROWS * GRID_W))
    return jnp.stack(tabs)


def _na_geometry(seqs, g):
    _, start, length = seqs.info(g * NA_TOK)
    row0 = start // GRID_W
    rows = length // GRID_W
    gb0 = row0 // NA_GROUP
    nb = rows // NA_GROUP
    kb = gb0 + jnp.clip(g - gb0 - 1, 0, nb - 3)
    return row0, rows, kb


def _na_kernel(seqs, q_ref, k0, k1, k2, v0, v1, v2, bias_ref, o_ref, kw, vw):
    g = pl.program_id(0)
    row0, rows, kb = _na_geometry(seqs, g)
    for d, (kr, vr) in enumerate(((k0, v0), (k1, v1), (k2, v2))):
        kw[d * NA_TOK:(d + 1) * NA_TOK, :] = kr[...]
        vw[d * NA_TOK:(d + 1) * NA_TOK, :] = vr[...]
    scale = NA_HEAD_DIM ** -0.5

    def row_body(i, carry):
        r = g * NA_GROUP + i - row0
        rs = jnp.clip(r - NA_ROWS // 2, 0, rows - NA_ROWS)
        s_idx = rs - r + NA_ROWS - 1
        ko = pl.multiple_of((rs + row0 - kb * NA_GROUP) * GRID_W, GRID_W)
        qo = pl.multiple_of(i * GRID_W, GRID_W)
        outs = []
        for h in range(NA_HEADS):
            hs = slice(h * NA_HEAD_DIM, (h + 1) * NA_HEAD_DIM)
            qh = q_ref[pl.ds(qo, GRID_W), hs]
            kh = kw[pl.ds(ko, NA_ROWS * GRID_W), hs]
            vh = vw[pl.ds(ko, NA_ROWS * GRID_W), hs]
            s = lax.dot_general(qh, kh, NT_DIMS, preferred_element_type=F32) * scale
            s = s + bias_ref[s_idx, h]
            p = jnp.exp(s - jnp.max(s, axis=-1, keepdims=True))
            l = jnp.sum(p, axis=-1, keepdims=True)
            o = jnp.dot(p.astype(BF16), vh, preferred_element_type=F32)
            outs.append(o / l)
        o_ref[pl.ds(qo, GRID_W), :] = jnp.concatenate(outs, axis=1).astype(o_ref.dtype)
        return carry

    lax.fori_loop(0, NA_GROUP, row_body, 0)


def _na(seqs, proj, bias_tab):
    n = seqs.n

    def kv_spec(slab, d):
        return pl.BlockSpec((None, NA_TOK, NA_WIDTH),
                            lambda g: (slab[0], _na_geometry(seqs, g)[2] + d, slab[1]))

    return pl.pallas_call(
        functools.partial(_na_kernel, seqs),
        grid=(n // NA_TOK,),
        in_specs=[pl.BlockSpec((None, NA_TOK, NA_WIDTH), lambda g: (SLAB_Q[0], g, SLAB_Q[1]))]
        + [kv_spec(SLAB_K, d) for d in range(3)] + [kv_spec(SLAB_V, d) for d in range(3)]
        + [pl.BlockSpec(bias_tab.shape, lambda g: (0, 0, 0, 0))],
        out_specs=pl.BlockSpec((NA_TOK, NA_WIDTH), lambda g: (g, 0)),
        out_shape=jax.ShapeDtypeStruct((n, NA_WIDTH), BF16),
        scratch_shapes=[pltpu.VMEM((3 * NA_TOK, NA_WIDTH), BF16),
                        pltpu.VMEM((3 * NA_TOK, NA_WIDTH), BF16)],
        compiler_params=_params(48, ("arbitrary",)),
        name="natten",
    )(proj, proj, proj, proj, proj, proj, proj, bias_tab)


def _hg_chunk(q, z, v, lb, tri, mask, mid, last, st_ref):
    f = lb + (1.0 - lb) * jax.nn.sigmoid(z)
    lf = jnp.log(f)
    kin = (1.0 - lb) * jax.nn.sigmoid(-z)
    gcum = jnp.dot(tri, lf, precision=HIGHEST, preferred_element_type=F32)
    gm = gcum[mid:mid + 1, :]
    gl = gcum[last:last + 1, :]
    qa = (q * jnp.exp(gcum - gm)).astype(BF16)
    ka = (kin * jnp.exp(gm - gcum)).astype(BF16)
    qe = (q * jnp.exp(gcum)).astype(BF16)
    kd = (kin * jnp.exp(gl - gcum)).astype(BF16)
    eg = jnp.exp(gl)
    vb = v.astype(BF16)
    outs = []
    for h in range(HG_HEADS):
        hs = slice(h * HG_KEY_DIM, (h + 1) * HG_KEY_DIM)
        a = lax.dot_general(qa[:, hs], ka[:, hs], NT_DIMS, preferred_element_type=F32)
        a = jnp.where(mask, a, 0.0)
        st = st_ref[h]
        o = jnp.dot(a.astype(BF16), vb[:, hs], preferred_element_type=F32)
        o = o + lax.dot_general(qe[:, hs], st.astype(BF16), NT_DIMS, preferred_element_type=F32)
        st_ref[h] = st * eg[:, hs] + lax.dot_general(vb[:, hs], kd[:, hs], TN_DIMS,
                                                    preferred_element_type=F32)
        outs.append(o)
    return jnp.concatenate(outs, axis=1)


def _hg_kernel(seqs, qf_ref, zf_ref, vf_ref, qb_ref, zb_ref, vb_ref, lb_ref, of_ref, ob_ref,
               stf, stb):
    i = pl.program_id(0)
    nsteps = pl.num_programs(0)
    tf = i * HG_STEP
    tb = (nsteps - 1 - i) * HG_STEP
    _, start_f, _ = seqs.info(tf)
    _, start_b, len_b = seqs.info(tb)

    @pl.when(tf == start_f)
    def _():
        stf[...] = jnp.zeros_like(stf)

    @pl.when(tb + HG_STEP == start_b + len_b)
    def _():
        stb[...] = jnp.zeros_like(stb)

    lb = lb_ref[...]
    row = lax.broadcasted_iota(jnp.int32, (HG_CHUNK, HG_CHUNK), 0)
    col = lax.broadcasted_iota(jnp.int32, (HG_CHUNK, HG_CHUNK), 1)
    lower = row >= col
    upper = col >= row
    tri_f = lower.astype(F32)
    tri_b = upper.astype(F32)
    nchunk = HG_STEP // HG_CHUNK
    for c in range(nchunk):
        cs = slice(c * HG_CHUNK, (c + 1) * HG_CHUNK)
        of_ref[cs, :] = _hg_chunk(qf_ref[cs, :].astype(F32), zf_ref[cs, :].astype(F32),
                                  vf_ref[cs, :].astype(F32), lb, tri_f, lower,
                                  HG_CHUNK // 2 - 1, HG_CHUNK - 1, stf)
        cb = nchunk - 1 - c
        bs = slice(cb * HG_CHUNK, (cb + 1) * HG_CHUNK)
        ob_ref[bs, :] = _hg_chunk(qb_ref[bs, :].astype(F32), zb_ref[bs, :].astype(F32),
                                  vb_ref[bs, :].astype(F32), lb, tri_b, upper,
                                  HG_CHUNK // 2, 0, stb)


def _hgrn(seqs, proj, lb):
    n = seqs.n
    nsteps = n // HG_STEP

    def spec(slab, rev):
        if rev:
            return pl.BlockSpec((None, HG_STEP, HG_WIDTH), lambda i: (slab[0], nsteps - 1 - i, slab[1]))
        return pl.BlockSpec((None, HG_STEP, HG_WIDTH), lambda i: (slab[0], i, slab[1]))

    return pl.pallas_call(
        functools.partial(_hg_kernel, seqs),
        grid=(nsteps,),
        in_specs=[spec(SLAB_HQ, False), spec(SLAB_FF, False), spec(SLAB_HI, False),
                  spec(SLAB_HQ, True), spec(SLAB_FB, True), spec(SLAB_HI, True),
                  pl.BlockSpec((1, HG_WIDTH), lambda i: (0, 0))],
        out_specs=[pl.BlockSpec((HG_STEP, HG_WIDTH), lambda i: (i, 0)),
                   pl.BlockSpec((HG_STEP, HG_WIDTH), lambda i: (nsteps - 1 - i, 0))],
        out_shape=[jax.ShapeDtypeStruct((n, HG_WIDTH), F32)] * 2,
        scratch_shapes=[pltpu.VMEM((HG_HEADS, HG_KEY_DIM, HG_KEY_DIM), F32)] * 2,
        compiler_params=_params(32, ("arbitrary",)),
        name="hgrn2",
    )(proj, proj, proj, proj, proj, proj, lb.reshape(1, HG_WIDTH))


def _merge_kernel(x_ref, att_ref, of_ref, ob_ref, hg_ref, ga_ref, gb_ref, mod_ref, hgn_ref,
                  nffn_ref, wa_ref, wb_ref, wo_ref, wr_ref, x1_ref, u2_ref, lg_ref):
    o = of_ref[...] + ob_ref[...]
    parts = []
    for h in range(HG_HEADS):
        hs = slice(h * HG_KEY_DIM, (h + 1) * HG_KEY_DIM)
        parts.append(_rms(o[:, hs]))
    on = jnp.concatenate(parts, axis=1) * hgn_ref[...]
    gate = hg_ref[...].astype(F32)
    hb = (on * (gate * jax.nn.sigmoid(gate))).astype(BF16)
    ya = jnp.dot(att_ref[...], wa_ref[...], preferred_element_type=F32)
    yb = jnp.dot(hb, wb_ref[...], preferred_element_type=F32)
    merged = (jax.nn.sigmoid(ga_ref[...].astype(F32)) * ya
              + jax.nn.sigmoid(gb_ref[...].astype(F32)) * yb)
    x1 = x_ref[...] + mod_ref[2:3, :] * jnp.dot(merged.astype(BF16), wo_ref[...],
                                                preferred_element_type=F32)
    x1_ref[...] = x1
    u2 = _rms(x1) * nffn_ref[...] * (1.0 + mod_ref[4:5, :]) + mod_ref[3:4, :]
    u2_ref[...] = u2.astype(BF16)
    lg_ref[...] = lax.dot_general(wr_ref[...], u2, NT_DIMS, precision=HIGHEST,
                                  preferred_element_type=F32)


def _merge(seqs, x, att, o_f, o_b, proj, mod, hg_norm, norm_ffn, wa, wb, wo, wr):
    n = seqs.n
    tm = 256
    tok = lambda i: (i, 0)
    const = lambda i: (0, 0)
    return pl.pallas_call(
        _merge_kernel,
        grid=(n // tm,),
        in_specs=[pl.BlockSpec((tm, D_MODEL), tok),
                  pl.BlockSpec((tm, NA_WIDTH), tok),
                  pl.BlockSpec((tm, HG_WIDTH), tok),
                  pl.BlockSpec((tm, HG_WIDTH), tok),
                  pl.BlockSpec((None, tm, HG_WIDTH), lambda i: (SLAB_HG[0], i, SLAB_HG[1])),
                  pl.BlockSpec((None, tm, D_MODEL), lambda i: (SLAB_GA, i, 0)),
                  pl.BlockSpec((None, tm, D_MODEL), lambda i: (SLAB_GB, i, 0)),
                  pl.BlockSpec((None, N_MOD, D_MODEL), lambda i: (seqs.info(i * tm)[0], 0, 0)),
                  pl.BlockSpec((1, HG_WIDTH), const),
                  pl.BlockSpec((1, D_MODEL), const),
                  pl.BlockSpec((NA_WIDTH, D_MODEL), const),
                  pl.BlockSpec((HG_WIDTH, D_MODEL), const),
                  pl.BlockSpec((D_MODEL, D_MODEL), const),
                  pl.BlockSpec((N_EXPERTS, D_MODEL), const)],
        out_specs=[pl.BlockSpec((tm, D_MODEL), tok),
                   pl.BlockSpec((tm, D_MODEL), tok),
                   pl.BlockSpec((N_EXPERTS, tm), lambda i: (0, i))],
        out_shape=[jax.ShapeDtypeStruct((n, D_MODEL), F32),
                   jax.ShapeDtypeStruct((n, D_MODEL), BF16),
                   jax.ShapeDtypeStruct((N_EXPERTS, n), F32)],
        compiler_params=_params(48, ("arbitrary",)),
        name="merge",
    )(x, att, o_f, o_b, proj, proj, proj, mod, hg_norm.reshape(1, HG_WIDTH),
      norm_ffn.reshape(1, D_MODEL), wa, wb, wo, wr)


def _expert_kernel(be_ref, nused_ref, xs_ref, wg_ref, wu_ref, wd_ref, ys_ref):
    @pl.when(pl.program_id(0) < nused_ref[0])
    def _():
        xs = xs_ref[...]
        a = jnp.dot(xs, wg_ref[...], preferred_element_type=F32)
        b = jnp.dot(xs, wu_ref[...], preferred_element_type=F32)
        h = (a * jax.nn.sigmoid(a) * b).astype(BF16)
        ys_ref[...] = jnp.dot(h, wd_ref[...], preferred_element_type=F32).astype(ys_ref.dtype)


def _experts(xs, block_e, n_used, wg, wu, wd):
    n_rows = xs.shape[0]
    n_blocks = n_rows // MOE_BLOCK
    grid_spec = pltpu.PrefetchScalarGridSpec(
        num_scalar_prefetch=2,
        grid=(n_blocks,),
        in_specs=[pl.BlockSpec((MOE_BLOCK, D_MODEL), lambda i, be, nu: (i, 0)),
                  pl.BlockSpec((None, D_MODEL, D_EXPERT), lambda i, be, nu: (be[i], 0, 0)),
                  pl.BlockSpec((None, D_MODEL, D_EXPERT), lambda i, be, nu: (be[i], 0, 0)),
                  pl.BlockSpec((None, D_EXPERT, D_MODEL), lambda i, be, nu: (be[i], 0, 0))],
        out_specs=pl.BlockSpec((MOE_BLOCK, D_MODEL), lambda i, be, nu: (i, 0)),
    )
    return pl.pallas_call(
        _expert_kernel,
        grid_spec=grid_spec,
        out_shape=jax.ShapeDtypeStruct((n_rows, D_MODEL), BF16),
        compiler_params=_params(32, ("arbitrary",)),
        name="experts",
    )(block_e, n_used, xs, wg, wu, wd)


ROUTE_TILE = 128
ROUTE_STEP = 512
NEG_INF = float("-inf")


def _first_max(x, idx, big):
    m = jnp.max(x, axis=0, keepdims=True)
    first = jnp.min(jnp.where(x == m, idx, big), axis=0, keepdims=True)
    return m, first


def _route_tile(lg, bias, tri, base):
    per_group = N_EXPERTS // N_GROUPS
    scores = jax.nn.sigmoid(lg)
    sel = scores + bias
    lrow = lax.broadcasted_iota(jnp.int32, (per_group, ROUTE_TILE), 0)
    gs = []
    for g in range(N_GROUPS):
        x = sel[g * per_group:(g + 1) * per_group]
        m1, first = _first_max(x, lrow, per_group)
        m2 = jnp.max(jnp.where(lrow == first, NEG_INF, x), axis=0, keepdims=True)
        gs.append(m1 + m2)
    cur = jnp.concatenate(gs, axis=0)
    grow = lax.broadcasted_iota(jnp.int32, (N_GROUPS, ROUTE_TILE), 0)
    chosen = jnp.zeros((N_GROUPS, ROUTE_TILE), jnp.int32)
    for _ in range(TOPK_GROUPS):
        _, first = _first_max(cur, grow, N_GROUPS)
        hit = grow == first
        chosen = jnp.where(hit, 1, chosen)
        cur = jnp.where(hit, NEG_INF, cur)
    cur = jnp.concatenate(
        [jnp.where(chosen[g:g + 1] > 0, sel[g * per_group:(g + 1) * per_group], NEG_INF)
         for g in range(N_GROUPS)], axis=0)
    row = lax.broadcasted_iota(jnp.int32, (N_EXPERTS, ROUTE_TILE), 0)
    member = jnp.zeros((N_EXPERTS, ROUTE_TILE), F32)
    es, ws = [], []
    for _ in range(TOP_K):
        _, first = _first_max(cur, row, N_EXPERTS)
        hit = row == first
        es.append(first)
        ws.append(jnp.sum(jnp.where(hit, scores, 0.0), axis=0, keepdims=True))
        cur = jnp.where(hit, NEG_INF, cur)
        member = jnp.where(hit, 1.0, member)
    e = jnp.concatenate(es, axis=0)
    w = jnp.concatenate(ws, axis=0)
    w = w / jnp.sum(w, axis=0, keepdims=True) * ROUTE_SCALE
    before = jnp.dot(member.astype(BF16), tri, preferred_element_type=F32) + base
    rank = jnp.concatenate(
        [jnp.sum(jnp.where(row == es[k], before, 0.0), axis=0, keepdims=True) for k in range(TOP_K)],
        axis=0)
    return e, w, rank.astype(jnp.int32), member


def _route_kernel(lg_ref, b_ref, e_ref, w_ref, r_ref, cnt_ref, base):
    @pl.when(pl.program_id(0) == 0)
    def _():
        base[...] = jnp.zeros_like(base)

    r_i = lax.broadcasted_iota(jnp.int32, (ROUTE_TILE, ROUTE_TILE), 0)
    c_i = lax.broadcasted_iota(jnp.int32, (ROUTE_TILE, ROUTE_TILE), 1)
    tri = (r_i < c_i).astype(BF16)
    bias = b_ref[...]
    for j in range(ROUTE_STEP // ROUTE_TILE):
        ls = slice(j * ROUTE_TILE, (j + 1) * ROUTE_TILE)
        e, w, rank, member = _route_tile(lg_ref[:, ls], bias, tri, base[...])
        e_ref[:, ls] = e
        w_ref[:, ls] = w
        r_ref[:, ls] = rank
        base[...] = base[...] + jnp.sum(member, axis=1, keepdims=True)
    cnt_ref[...] = base[...]


def _route(logits_t, b_router):
    n = logits_t.shape[1]
    tok = lambda i: (0, i)
    return pl.pallas_call(
        _route_kernel,
        grid=(n // ROUTE_STEP,),
        in_specs=[pl.BlockSpec((N_EXPERTS, ROUTE_STEP), tok),
                  pl.BlockSpec((N_EXPERTS, 1), lambda i: (0, 0))],
        out_specs=[pl.BlockSpec((TOP_K, ROUTE_STEP), tok),
                   pl.BlockSpec((TOP_K, ROUTE_STEP), tok),
                   pl.BlockSpec((TOP_K, ROUTE_STEP), tok),
                   pl.BlockSpec((N_EXPERTS, 1), lambda i: (0, 0))],
        out_shape=[jax.ShapeDtypeStruct((TOP_K, n), jnp.int32),
                   jax.ShapeDtypeStruct((TOP_K, n), F32),
                   jax.ShapeDtypeStruct((TOP_K, n), jnp.int32),
                   jax.ShapeDtypeStruct((N_EXPERTS, 1), F32)],
        scratch_shapes=[pltpu.VMEM((N_EXPERTS, 1), F32)],
        compiler_params=_params(32, ("arbitrary",)),
        name="route",
    )(logits_t, b_router.reshape(N_EXPERTS, 1).astype(F32))


def _pos_kernel(e_ref, r_ref, ps_ref, pos_ref):
    row = lax.broadcasted_iota(jnp.int32, (N_EXPERTS, ROUTE_TILE), 0)
    pstart = ps_ref[...]
    for j in range(ROUTE_STEP // ROUTE_TILE):
        ls = slice(j * ROUTE_TILE, (j + 1) * ROUTE_TILE)
        e = e_ref[:, ls]
        off = jnp.concatenate(
            [jnp.sum(jnp.where(row == e[k:k + 1], pstart, 0.0), axis=0, keepdims=True)
             for k in range(TOP_K)], axis=0)
        pos_ref[:, ls] = off.astype(jnp.int32) + r_ref[:, ls]


def _positions(eidx, rank, pstarts):
    n = eidx.shape[1]
    tok = lambda i: (0, i)
    return pl.pallas_call(
        _pos_kernel,
        grid=(n // ROUTE_STEP,),
        in_specs=[pl.BlockSpec((TOP_K, ROUTE_STEP), tok),
                  pl.BlockSpec((TOP_K, ROUTE_STEP), tok),
                  pl.BlockSpec((N_EXPERTS, 1), lambda i: (0, 0))],
        out_specs=pl.BlockSpec((TOP_K, ROUTE_STEP), tok),
        out_shape=jax.ShapeDtypeStruct((TOP_K, n), jnp.int32),
        compiler_params=_params(32, ("arbitrary",)),
        name="positions",
    )(eidx, rank, pstarts.reshape(N_EXPERTS, 1).astype(F32))


def _block_tables(counts, n_blocks):
    counts = counts.reshape(N_EXPERTS).astype(jnp.int32)
    padded = (counts + MOE_BLOCK - 1) // MOE_BLOCK * MOE_BLOCK
    pend = jnp.cumsum(padded)
    pstarts = pend - padded
    first_row = jnp.arange(n_blocks, dtype=jnp.int32) * MOE_BLOCK
    block_e = jnp.sum((pend[None, :] <= first_row[:, None]).astype(jnp.int32), axis=1)
    block_e = jnp.minimum(block_e, N_EXPERTS - 1).astype(jnp.int32)
    n_used = (pend[-1] // MOE_BLOCK).astype(jnp.int32).reshape(1)
    return pstarts, block_e, n_used


def _final_kernel(x1_ref, u2_ref, yg_ref, w_ref, mod_ref, nf_ref, wsg_ref, wsu_ref, wsd_ref, o_ref):
    u2 = u2_ref[...]
    a = jnp.dot(u2, wsg_ref[...], preferred_element_type=F32)
    b = jnp.dot(u2, wsu_ref[...], preferred_element_type=F32)
    f = jnp.dot((a * jax.nn.sigmoid(a) * b).astype(BF16), wsd_ref[...], preferred_element_type=F32)
    w = w_ref[...]
    for k in range(TOP_K):
        f = f + w[:, k:k + 1] * yg_ref[k].astype(F32)
    x2 = x1_ref[...] + mod_ref[5:6, :] * f
    o_ref[...] = _rms(x2) * nf_ref[...]


def _final(seqs, x1, u2, yg, w, mod, norm_final, wsg, wsu, wsd):
    n = seqs.n
    tm = 256
    tok = lambda i: (i, 0)
    const = lambda i: (0, 0)
    return pl.pallas_call(
        _final_kernel,
        grid=(n // tm,),
        in_specs=[pl.BlockSpec((tm, D_MODEL), tok),
                  pl.BlockSpec((tm, D_MODEL), tok),
                  pl.BlockSpec((TOP_K, tm, D_MODEL), lambda i: (0, i, 0)),
                  pl.BlockSpec((tm, TOP_K), tok),
                  pl.BlockSpec((None, N_MOD, D_MODEL), lambda i: (seqs.info(i * tm)[0], 0, 0)),
                  pl.BlockSpec((1, D_MODEL), const),
                  pl.BlockSpec((D_MODEL, D_EXPERT), const),
                  pl.BlockSpec((D_MODEL, D_EXPERT), const),
                  pl.BlockSpec((D_EXPERT, D_MODEL), const)],
        out_specs=pl.BlockSpec((tm, D_MODEL), tok),
        out_shape=jax.ShapeDtypeStruct((n, D_MODEL), F32),
        compiler_params=_params(48, ("arbitrary",)),
        name="final",
    )(x1, u2, yg, w, mod, norm_final.reshape(1, D_MODEL), wsg, wsu, wsd)


def _layer(seqs, x, c, w_ada, b_ada, norm_mix, w_in, na_rpb, hg_lb, hg_norm, w_branch_a,
           w_branch_b, w_out, norm_ffn, w_router, b_router, w_exp_gate, w_exp_up, w_exp_down,
           w_sh_gate, w_sh_up, w_sh_down, norm_final):
    n = seqs.n
    c_rows = -(-seqs.nseq // 8) * 8
    c_pad = jnp.zeros((c_rows, D_MODEL), F32).at[:seqs.nseq].set(c)
    mod = _ada(c_pad, w_ada[0], b_ada[0])[:seqs.nseq].reshape(seqs.nseq, N_MOD, D_MODEL)
    lb = jnp.cumsum(jax.nn.softmax(hg_lb.astype(F32), axis=0), axis=0)[0]

    proj = _inproj(seqs, x, mod, norm_mix[0], w_in[0].astype(BF16))
    att = _na(seqs, proj, _na_bias_table(na_rpb[0]))
    o_f, o_b = _hgrn(seqs, proj, lb)
    x1, u2, logits = _merge(seqs, x, att, o_f, o_b, proj, mod, hg_norm[0], norm_ffn[0],
                            w_branch_a[0].astype(BF16), w_branch_b[0].astype(BF16),
                            w_out[0].astype(BF16), w_router[0].T)

    eidx, w, rank, counts = _route(logits, b_router[0])
    n_rows = n * TOP_K + N_EXPERTS * MOE_BLOCK
    pstarts, block_e, n_used = _block_tables(counts, n_rows // MOE_BLOCK)
    pos = _positions(eidx, rank, pstarts).reshape(-1)
    tok_ids = jnp.tile(jnp.arange(n, dtype=jnp.int32), TOP_K)
    buf_tok = jnp.zeros((n_rows,), jnp.int32).at[pos].set(tok_ids, unique_indices=True)
    xs = jnp.take(u2, buf_tok, axis=0)
    ys = _experts(xs, block_e, n_used, w_exp_gate[0].astype(BF16), w_exp_up[0].astype(BF16),
                  w_exp_down[0].astype(BF16))
    yg = jnp.take(ys, pos, axis=0).reshape(TOP_K, n, D_MODEL)
    return _final(seqs, x1, u2, yg, w.T, mod, norm_final, w_sh_gate[0].astype(BF16),
                  w_sh_up[0].astype(BF16), w_sh_down[0].astype(BF16))


def kernel(x_prompt, x_sample, c_prompt, c_sample, w_ada, b_ada, norm_mix, w_in, na_rpb, hg_lb, hg_norm, w_branch_a, w_branch_b, w_out, norm_ffn, w_router, b_router, w_exp_gate, w_exp_up, w_exp_down, w_sh_gate, w_sh_up, w_sh_down, norm_final):
    bp, tp, _ = x_prompt.shape
    bs, ts, _ = x_sample.shape
    seqs = _Seqs(bp, tp, bs, ts)
    x = jnp.concatenate([x_prompt.reshape(bp * tp, D_MODEL), x_sample.reshape(bs * ts, D_MODEL)])
    c = jnp.concatenate([c_prompt, c_sample])
    y = _layer(seqs, x, c, w_ada, b_ada, norm_mix, w_in, na_rpb, hg_lb, hg_norm, w_branch_a,
               w_branch_b, w_out, norm_ffn, w_router, b_router, w_exp_gate, w_exp_up, w_exp_down,
               w_sh_gate, w_sh_up, w_sh_down, norm_final)
    return (y[:seqs.np_].reshape(bp, tp, D_MODEL), y[seqs.np_:].reshape(bs, ts, D_MODEL))
```

```python
import functools

import jax
import jax.numpy as jnp
import numpy as np
from jax import lax
from jax.experimental import pallas as pl
from jax.experimental.pallas import tpu as pltpu
from jax.experimental.pallas import tpu_sc as plsc

D_MODEL = 1024
GRID_W = 64
NA_HEADS = 8
NA_HEAD_DIM = 64
NA_WIDTH = NA_HEADS * NA_HEAD_DIM
NA_ROWS = 8
NA_COLS = 16
HG_HEADS = 4
HG_KEY_DIM = 128
HG_WIDTH = HG_HEADS * HG_KEY_DIM
HG_CHUNK = 64
N_EXPERTS = 256
TOP_K = 8
N_GROUPS = 8
TOPK_GROUPS = 4
D_EXPERT = 256
ROUTE_SCALE = 2.5
N_MOD = 6
RMS_EPS = 1e-6

MOE_BLOCK = 256
NA_GROUP = 8
NA_TOK = NA_GROUP * GRID_W
HG_STEP = 256
MASK_VALUE = -1e30

F32 = jnp.float32
BF16 = jnp.bfloat16
HIGHEST = lax.Precision.HIGHEST
NT_DIMS = (((1,), (1,)), ((), ()))
TN_DIMS = (((0,), (0,)), ((), ()))

SLAB_Q, SLAB_K, SLAB_V, SLAB_HQ = (0, 0), (0, 1), (1, 0), (1, 1)
SLAB_FF, SLAB_FB, SLAB_HI, SLAB_HG = (2, 0), (2, 1), (3, 0), (3, 1)
SLAB_GA, SLAB_GB = 4, 5


def _params(vmem_mb, sem=None):
    kw = dict(vmem_limit_bytes=vmem_mb * 1024 * 1024)
    if sem is not None:
        kw["dimension_semantics"] = sem
    return pltpu.CompilerParams(**kw)


class _Seqs:
    def __init__(self, bp, tp, bs, ts):
        self.bp, self.tp, self.bs, self.ts = bp, tp, bs, ts
        self.np_ = bp * tp
        self.n = bp * tp + bs * ts
        self.nseq = bp + bs

    def info(self, t0):
        in_p = t0 < self.np_
        rel = jnp.maximum(t0 - self.np_, 0)
        sid = jnp.where(in_p, t0 // self.tp, self.bp + rel // self.ts)
        start = jnp.where(in_p, (t0 // self.tp) * self.tp, self.np_ + (rel // self.ts) * self.ts)
        length = jnp.where(in_p, self.tp, self.ts)
        return sid, start, length


def _ada_kernel(c_ref, w_ref, b_ref, o_ref):
    c = c_ref[...]
    a = c * jax.nn.sigmoid(c)
    o_ref[...] = jnp.dot(a, w_ref[...], precision=HIGHEST, preferred_element_type=F32) + b_ref[...]


def _ada(c_pad, w_ada, b_ada):
    rows = c_pad.shape[0]
    n_out = w_ada.shape[1]
    tn = 1024
    return pl.pallas_call(
        _ada_kernel,
        grid=(n_out // tn,),
        in_specs=[pl.BlockSpec((rows, D_MODEL), lambda j: (0, 0)),
                  pl.BlockSpec((D_MODEL, tn), lambda j: (0, j)),
                  pl.BlockSpec((1, tn), lambda j: (0, j))],
        out_specs=pl.BlockSpec((rows, tn), lambda j: (0, j)),
        out_shape=jax.ShapeDtypeStruct((rows, n_out), F32),
        compiler_params=_params(32),
        name="ada",
    )(c_pad, w_ada, b_ada.reshape(1, n_out))


def _rms(x):
    return x * lax.rsqrt(jnp.mean(x * x, axis=-1, keepdims=True) + RMS_EPS)


PACK_W = D_MODEL // 4


def _pack_rows(x):
    out = []
    for h in range(2):
        lo = x[:, (2 * h) * PACK_W:(2 * h + 1) * PACK_W].astype(BF16).astype(F32)
        hi = x[:, (2 * h + 1) * PACK_W:(2 * h + 2) * PACK_W].astype(BF16).astype(F32)
        out.append(lax.bitcast_convert_type(hi, jnp.uint32)
                   | (lax.bitcast_convert_type(lo, jnp.uint32) >> 16))
    return out


def _unpack_rows(p0, p1):
    quarters = []
    for p in (p0, p1):
        quarters.append(lax.bitcast_convert_type(p << 16, F32).astype(BF16))
        quarters.append(lax.bitcast_convert_type(p & jnp.uint32(0xFFFF0000), F32).astype(BF16))
    return quarters


def _dot_quarters(quarters, w_ref):
    acc = None
    for q, xq in enumerate(quarters):
        part = jnp.dot(xq, w_ref[q * PACK_W:(q + 1) * PACK_W, :], preferred_element_type=F32)
        acc = part if acc is None else acc + part
    return acc


def _two_stream_specs(seqs, tm, grid_rank=1):
    npt = seqs.np_ // tm
    nst = (seqs.n - seqs.np_) // tm
    if grid_rank == 1:
        p_map = lambda i: (jnp.minimum(i, npt - 1), 0)
        s_map = lambda i: (jnp.clip(i - npt, 0, nst - 1), 0)
    else:
        p_map = lambda i, j: (jnp.minimum(i, npt - 1), 0)
        s_map = lambda i, j: (jnp.clip(i - npt, 0, nst - 1), 0)
    return npt, pl.BlockSpec((tm, D_MODEL), p_map), pl.BlockSpec((tm, D_MODEL), s_map)


def _inproj_kernel(npt, xp_ref, xs_ref, mod_ref, g_ref, w_ref, o_ref, u_scr):
    @pl.when(pl.program_id(1) == 0)
    def _():
        x = jnp.where(pl.program_id(0) < npt, xp_ref[...], xs_ref[...])
        y = _rms(x) * g_ref[...]
        u = y * (1.0 + mod_ref[1:2, :]) + mod_ref[0:1, :]
        u_scr[...] = u.astype(BF16)

    o_ref[...] = jnp.dot(u_scr[...], w_ref[...], preferred_element_type=F32).astype(o_ref.dtype)


def _inproj(seqs, x_p, x_s, mod, norm_mix, w_in_bf):
    n = seqs.n
    tm = min(1024, seqs.tp, seqs.ts)
    tn = 1024
    n_slab = w_in_bf.shape[1] // tn
    npt, p_spec, s_spec = _two_stream_specs(seqs, tm, grid_rank=2)
    return pl.pallas_call(
        functools.partial(_inproj_kernel, npt),
        grid=(n // tm, n_slab),
        in_specs=[p_spec, s_spec,
                  pl.BlockSpec((None, N_MOD, D_MODEL), lambda i, j: (seqs.info(i * tm)[0], 0, 0)),
                  pl.BlockSpec((1, D_MODEL), lambda i, j: (0, 0)),
                  pl.BlockSpec((D_MODEL, tn), lambda i, j: (0, j))],
        out_specs=pl.BlockSpec((None, tm, tn), lambda i, j: (j, i, 0)),
        out_shape=jax.ShapeDtypeStruct((n_slab, n, tn), BF16),
        scratch_shapes=[pltpu.VMEM((tm, D_MODEL), BF16)],
        compiler_params=_params(40, ("arbitrary", "arbitrary")),
        name="inproj",
    )(x_p, x_s, mod, norm_mix.reshape(1, D_MODEL), w_in_bf)


def _na_bias_table(rpb):
    col = np.arange(GRID_W)
    cs = np.clip(col - NA_COLS // 2, 0, GRID_W - NA_COLS)
    valid = (col[None, :] >= cs[:, None]) & (col[None, :] < cs[:, None] + NA_COLS)
    coff = np.clip(col[None, :] - col[:, None] + NA_COLS - 1, 0, 2 * NA_COLS - 2)
    tabs = []
    for s in range(NA_ROWS):
        b = rpb[:, s:s + NA_ROWS][:, :, coff]
        b = jnp.where(valid[None, None], b.astype(F32), MASK_VALUE)
        tabs.append(jnp.transpose(b, (0, 2, 1, 3)).reshape(NA_HEADS, GRID_W, NA_ROWS * GRID_W))
    return jnp.stack(tabs)


def _na_geometry(seqs, g):
    _, start, length = seqs.info(g * NA_TOK)
    row0 = start // GRID_W
    rows = length // GRID_W
    gb0 = row0 // NA_GROUP
    nb = rows // NA_GROUP
    kb = gb0 + jnp.clip(g - gb0 - 1, 0, nb - 3)
    return row0, rows, kb


def _na_kernel(seqs, q_ref, k0, k1, k2, v0, v1, v2, bias_ref, o_ref, kw, vw):
    g = pl.program_id(0)
    row0, rows, kb = _na_geometry(seqs, g)
    for d, (kr, vr) in enumerate(((k0, v0), (k1, v1), (k2, v2))):
        kw[d * NA_TOK:(d + 1) * NA_TOK, :] = kr[...]
        vw[d * NA_TOK:(d + 1) * NA_TOK, :] = vr[...]
    scale = NA_HEAD_DIM ** -0.5

    def row_body(i, carry):
        r = g * NA_GROUP + i - row0
        rs = jnp.clip(r - NA_ROWS // 2, 0, rows - NA_ROWS)
        s_idx = rs - r + NA_ROWS - 1
        ko = pl.multiple_of((rs + row0 - kb * NA_GROUP) * GRID_W, GRID_W)
        qo = pl.multiple_of(i * GRID_W, GRID_W)
        outs = []
        for h in range(NA_HEADS):
            hs = slice(h * NA_HEAD_DIM, (h + 1) * NA_HEAD_DIM)
            qh = q_ref[pl.ds(qo, GRID_W), hs]
            kh = kw[pl.ds(ko, NA_ROWS * GRID_W), hs]
            vh = vw[pl.ds(ko, NA_ROWS * GRID_W), hs]
            s = lax.dot_general(qh, kh, NT_DIMS, preferred_element_type=F32) * scale
            s = s + bias_ref[s_idx, h]
            p = jnp.exp(s - jnp.max(s, axis=-1, keepdims=True))
            l = jnp.sum(p, axis=-1, keepdims=True)
            o = jnp.dot(p.astype(BF16), vh, preferred_element_type=F32)
            outs.append(o / l)
        o_ref[pl.ds(qo, GRID_W), :] = jnp.concatenate(outs, axis=1).astype(o_ref.dtype)
        return carry

    lax.fori_loop(0, NA_GROUP, row_body, 0)


def _na(seqs, proj, bias_tab):
    n = seqs.n

    def kv_spec(slab, d):
        return pl.BlockSpec((None, NA_TOK, NA_WIDTH),
                            lambda g: (slab[0], _na_geometry(seqs, g)[2] + d, slab[1]))

    return pl.pallas_call(
        functools.partial(_na_kernel, seqs),
        grid=(n // NA_TOK,),
        in_specs=[pl.BlockSpec((None, NA_TOK, NA_WIDTH), lambda g: (SLAB_Q[0], g, SLAB_Q[1]))]
        + [kv_spec(SLAB_K, d) for d in range(3)] + [kv_spec(SLAB_V, d) for d in range(3)]
        + [pl.BlockSpec(bias_tab.shape, lambda g: (0, 0, 0, 0))],
        out_specs=pl.BlockSpec((NA_TOK, NA_WIDTH), lambda g: (g, 0)),
        out_shape=jax.ShapeDtypeStruct((n, NA_WIDTH), BF16),
        scratch_shapes=[pltpu.VMEM((3 * NA_TOK, NA_WIDTH), BF16),
                        pltpu.VMEM((3 * NA_TOK, NA_WIDTH), BF16)],
        compiler_params=_params(48, ("arbitrary",)),
        name="natten",
    )(proj, proj, proj, proj, proj, proj, proj, bias_tab)


def _hg_chunk(q, z, v, lb, tri, mask, mid, last, st_ref):
    f = lb + (1.0 - lb) * jax.nn.sigmoid(z)
    lf = jnp.log(f)
    kin = (1.0 - lb) * jax.nn.sigmoid(-z)
    gcum = jnp.dot(tri, lf, precision=HIGHEST, preferred_element_type=F32)
    gm = gcum[mid:mid + 1, :]
    gl = gcum[last:last + 1, :]
    qa = (q * jnp.exp(gcum - gm)).astype(BF16)
    ka = (kin * jnp.exp(gm - gcum)).astype(BF16)
    qe = (q * jnp.exp(gcum)).astype(BF16)
    kd = (kin * jnp.exp(gl - gcum)).astype(BF16)
    eg = jnp.exp(gl)
    vb = v.astype(BF16)
    outs = []
    for h in range(HG_HEADS):
        hs = slice(h * HG_KEY_DIM, (h + 1) * HG_KEY_DIM)
        a = lax.dot_general(qa[:, hs], ka[:, hs], NT_DIMS, preferred_element_type=F32)
        a = jnp.where(mask, a, 0.0)
        st = st_ref[h]
        o = jnp.dot(a.astype(BF16), vb[:, hs], preferred_element_type=F32)
        o = o + lax.dot_general(qe[:, hs], st.astype(BF16), NT_DIMS, preferred_element_type=F32)
        st_ref[h] = st * eg[:, hs] + lax.dot_general(vb[:, hs], kd[:, hs], TN_DIMS,
                                                    preferred_element_type=F32)
        outs.append(o)
    return jnp.concatenate(outs, axis=1)


def _hg_kernel(seqs, qf_ref, zf_ref, vf_ref, qb_ref, zb_ref, vb_ref, lb_ref, of_ref, ob_ref,
               stf, stb):
    i = pl.program_id(0)
    nsteps = pl.num_programs(0)
    tf = i * HG_STEP
    tb = (nsteps - 1 - i) * HG_STEP
    _, start_f, _ = seqs.info(tf)
    _, start_b, len_b = seqs.info(tb)

    @pl.when(tf == start_f)
    def _():
        stf[...] = jnp.zeros_like(stf)

    @pl.when(tb + HG_STEP == start_b + len_b)
    def _():
        stb[...] = jnp.zeros_like(stb)

    lb = lb_ref[...]
    row = lax.broadcasted_iota(jnp.int32, (HG_CHUNK, HG_CHUNK), 0)
    col = lax.broadcasted_iota(jnp.int32, (HG_CHUNK, HG_CHUNK), 1)
    lower = row >= col
    upper = col >= row
    tri_f = lower.astype(F32)
    tri_b = upper.astype(F32)
    nchunk = HG_STEP // HG_CHUNK
    for c in range(nchunk):
        cs = slice(c * HG_CHUNK, (c + 1) * HG_CHUNK)
        of_ref[cs, :] = _hg_chunk(qf_ref[cs, :].astype(F32), zf_ref[cs, :].astype(F32),
                                  vf_ref[cs, :].astype(F32), lb, tri_f, lower,
                                  HG_CHUNK // 2 - 1, HG_CHUNK - 1, stf)
        cb = nchunk - 1 - c
        bs = slice(cb * HG_CHUNK, (cb + 1) * HG_CHUNK)
        ob_ref[bs, :] = _hg_chunk(qb_ref[bs, :].astype(F32), zb_ref[bs, :].astype(F32),
                                  vb_ref[bs, :].astype(F32), lb, tri_b, upper,
                                  HG_CHUNK // 2, 0, stb)


def _hgrn(seqs, proj, lb):
    n = seqs.n
    nsteps = n // HG_STEP

    def spec(slab, rev):
        if rev:
            return pl.BlockSpec((None, HG_STEP, HG_WIDTH), lambda i: (slab[0], nsteps - 1 - i, slab[1]))
        return pl.BlockSpec((None, HG_STEP, HG_WIDTH), lambda i: (slab[0], i, slab[1]))

    return pl.pallas_call(
        functools.partial(_hg_kernel, seqs),
        grid=(nsteps,),
        in_specs=[spec(SLAB_HQ, False), spec(SLAB_FF, False), spec(SLAB_HI, False),
                  spec(SLAB_HQ, True), spec(SLAB_FB, True), spec(SLAB_HI, True),
                  pl.BlockSpec((1, HG_WIDTH), lambda i: (0, 0))],
        out_specs=[pl.BlockSpec((HG_STEP, HG_WIDTH), lambda i: (i, 0)),
                   pl.BlockSpec((HG_STEP, HG_WIDTH), lambda i: (nsteps - 1 - i, 0))],
        out_shape=[jax.ShapeDtypeStruct((n, HG_WIDTH), F32)] * 2,
        scratch_shapes=[pltpu.VMEM((HG_HEADS, HG_KEY_DIM, HG_KEY_DIM), F32)] * 2,
        compiler_params=_params(32, ("arbitrary",)),
        name="hgrn2",
    )(proj, proj, proj, proj, proj, proj, lb.reshape(1, HG_WIDTH))


def _merge_kernel(npt, xp_ref, xs_ref, att_ref, of_ref, ob_ref, hg_ref, ga_ref, gb_ref, mod_ref,
                  hgn_ref, nffn_ref, wa_ref, wb_ref, wo_ref, wr_ref, x1_ref, u2_ref, lg_ref):
    x = jnp.where(pl.program_id(0) < npt, xp_ref[...], xs_ref[...])
    o = of_ref[...] + ob_ref[...]
    parts = []
    for h in range(HG_HEADS):
        hs = slice(h * HG_KEY_DIM, (h + 1) * HG_KEY_DIM)
        parts.append(_rms(o[:, hs]))
    on = jnp.concatenate(parts, axis=1) * hgn_ref[...]
    gate = hg_ref[...].astype(F32)
    hb = (on * (gate * jax.nn.sigmoid(gate))).astype(BF16)
    ya = jnp.dot(att_ref[...], wa_ref[...], preferred_element_type=F32)
    yb = jnp.dot(hb, wb_ref[...], preferred_element_type=F32)
    merged = (jax.nn.sigmoid(ga_ref[...].astype(F32)) * ya
              + jax.nn.sigmoid(gb_ref[...].astype(F32)) * yb)
    x1 = x + mod_ref[2:3, :] * jnp.dot(merged.astype(BF16), wo_ref[...],
                                       preferred_element_type=F32)
    x1_ref[...] = x1
    u2 = _rms(x1) * nffn_ref[...] * (1.0 + mod_ref[4:5, :]) + mod_ref[3:4, :]
    u2_ref[0], u2_ref[1] = _pack_rows(u2)
    lg_ref[...] = lax.dot_general(wr_ref[...], u2, NT_DIMS, precision=HIGHEST,
                                  preferred_element_type=F32)


def _merge(seqs, x_p, x_s, att, o_f, o_b, proj, mod, hg_norm, norm_ffn, wa, wb, wo, wr):
    n = seqs.n
    tm = 256
    tok = lambda i: (i, 0)
    const = lambda i: (0, 0)
    npt, p_spec, s_spec = _two_stream_specs(seqs, tm)
    return pl.pallas_call(
        functools.partial(_merge_kernel, npt),
        grid=(n // tm,),
        in_specs=[p_spec, s_spec,
                  pl.BlockSpec((tm, NA_WIDTH), tok),
                  pl.BlockSpec((tm, HG_WIDTH), tok),
                  pl.BlockSpec((tm, HG_WIDTH), tok),
                  pl.BlockSpec((None, tm, HG_WIDTH), lambda i: (SLAB_HG[0], i, SLAB_HG[1])),
                  pl.BlockSpec((None, tm, D_MODEL), lambda i: (SLAB_GA, i, 0)),
                  pl.BlockSpec((None, tm, D_MODEL), lambda i: (SLAB_GB, i, 0)),
                  pl.BlockSpec((None, N_MOD, D_MODEL), lambda i: (seqs.info(i * tm)[0], 0, 0)),
                  pl.BlockSpec((1, HG_WIDTH), const),
                  pl.BlockSpec((1, D_MODEL), const),
                  pl.BlockSpec((NA_WIDTH, D_MODEL), const),
                  pl.BlockSpec((HG_WIDTH, D_MODEL), const),
                  pl.BlockSpec((D_MODEL, D_MODEL), const),
                  pl.BlockSpec((N_EXPERTS, D_MODEL), const)],
        out_specs=[pl.BlockSpec((tm, D_MODEL), tok),
                   pl.BlockSpec((2, tm, PACK_W), lambda i: (0, i, 0)),
                   pl.BlockSpec((N_EXPERTS, tm), lambda i: (0, i))],
        out_shape=[jax.ShapeDtypeStruct((n, D_MODEL), F32),
                   jax.ShapeDtypeStruct((2, n, PACK_W), jnp.uint32),
                   jax.ShapeDtypeStruct((N_EXPERTS, n), F32)],
        compiler_params=_params(48, ("arbitrary",)),
        name="merge",
    )(x_p, x_s, att, o_f, o_b, proj, proj, proj, mod, hg_norm.reshape(1, HG_WIDTH),
      norm_ffn.reshape(1, D_MODEL), wa, wb, wo, wr)


def _expert_kernel(be_ref, nused_ref, xs_ref, wg_ref, wu_ref, wd_ref, ys_ref, wg_s, wu_s, wd_s):
    i = pl.program_id(0)

    @pl.when(i < nused_ref[0])
    def _():
        @pl.when((i == 0) | (be_ref[i] != be_ref[jnp.maximum(i - 1, 0)]))
        def _():
            wg_s[...] = wg_ref[...].astype(BF16)
            wu_s[...] = wu_ref[...].astype(BF16)
            wd_s[...] = wd_ref[...].astype(BF16)

        xq = _unpack_rows(xs_ref[0], xs_ref[1])
        a = _dot_quarters(xq, wg_s)
        b = _dot_quarters(xq, wu_s)
        h = (a * jax.nn.sigmoid(a) * b).astype(BF16)
        ys_ref[0], ys_ref[1] = _pack_rows(jnp.dot(h, wd_s[...], preferred_element_type=F32))


def _experts(xs, block_e, n_used, wg, wu, wd):
    n_rows = xs.shape[1]
    n_blocks = n_rows // MOE_BLOCK
    grid_spec = pltpu.PrefetchScalarGridSpec(
        num_scalar_prefetch=2,
        grid=(n_blocks,),
        in_specs=[pl.BlockSpec((2, MOE_BLOCK, PACK_W), lambda i, be, nu: (0, i, 0)),
                  pl.BlockSpec((None, D_MODEL, D_EXPERT), lambda i, be, nu: (be[i], 0, 0)),
                  pl.BlockSpec((None, D_MODEL, D_EXPERT), lambda i, be, nu: (be[i], 0, 0)),
                  pl.BlockSpec((None, D_EXPERT, D_MODEL), lambda i, be, nu: (be[i], 0, 0))],
        out_specs=pl.BlockSpec((2, MOE_BLOCK, PACK_W), lambda i, be, nu: (0, i, 0)),
        scratch_shapes=[pltpu.VMEM((D_MODEL, D_EXPERT), BF16),
                        pltpu.VMEM((D_MODEL, D_EXPERT), BF16),
                        pltpu.VMEM((D_EXPERT, D_MODEL), BF16)],
    )
    return pl.pallas_call(
        _expert_kernel,
        grid_spec=grid_spec,
        out_shape=jax.ShapeDtypeStruct((2, n_rows, PACK_W), jnp.uint32),
        compiler_params=_params(32, ("arbitrary",)),
        name="experts",
    )(block_e, n_used, xs, wg, wu, wd)


ROUTE_TILE = 128
ROUTE_STEP = 512
NEG_INF = float("-inf")


def _first_max(x, idx, big):
    m = jnp.max(x, axis=0, keepdims=True)
    first = jnp.min(jnp.where(x == m, idx, big), axis=0, keepdims=True)
    return m, first


def _route_tile(lg, bias, tri, base):
    per_group = N_EXPERTS // N_GROUPS
    scores = jax.nn.sigmoid(lg)
    sel = scores + bias
    lrow = lax.broadcasted_iota(jnp.int32, (per_group, ROUTE_TILE), 0)
    gs = []
    for g in range(N_GROUPS):
        x = sel[g * per_group:(g + 1) * per_group]
        m1, first = _first_max(x, lrow, per_group)
        m2 = jnp.max(jnp.where(lrow == first, NEG_INF, x), axis=0, keepdims=True)
        gs.append(m1 + m2)
    cur = jnp.concatenate(gs, axis=0)
    grow = lax.broadcasted_iota(jnp.int32, (N_GROUPS, ROUTE_TILE), 0)
    chosen = jnp.zeros((N_GROUPS, ROUTE_TILE), jnp.int32)
    for _ in range(TOPK_GROUPS):
        _, first = _first_max(cur, grow, N_GROUPS)
        hit = grow == first
        chosen = jnp.where(hit, 1, chosen)
        cur = jnp.where(hit, NEG_INF, cur)
    cur = jnp.concatenate(
        [jnp.where(chosen[g:g + 1] > 0, sel[g * per_group:(g + 1) * per_group], NEG_INF)
         for g in range(N_GROUPS)], axis=0)
    row = lax.broadcasted_iota(jnp.int32, (N_EXPERTS, ROUTE_TILE), 0)
    member = jnp.zeros((N_EXPERTS, ROUTE_TILE), F32)
    es, ws = [], []
    for _ in range(TOP_K):
        _, first = _first_max(cur, row, N_EXPERTS)
        hit = row == first
        es.append(first)
        ws.append(jnp.sum(jnp.where(hit, scores, 0.0), axis=0, keepdims=True))
        cur = jnp.where(hit, NEG_INF, cur)
        member = jnp.where(hit, 1.0, member)
    e = jnp.concatenate(es, axis=0)
    w = jnp.concatenate(ws, axis=0)
    w = w / jnp.sum(w, axis=0, keepdims=True) * ROUTE_SCALE
    before = jnp.dot(member.astype(BF16), tri, preferred_element_type=F32) + base
    rank = jnp.concatenate(
        [jnp.sum(jnp.where(row == es[k], before, 0.0), axis=0, keepdims=True) for k in range(TOP_K)],
        axis=0)
    return e, w, rank.astype(jnp.int32), member


def _route_kernel(lg_ref, b_ref, e_ref, w_ref, r_ref, cnt_ref, base):
    @pl.when(pl.program_id(0) == 0)
    def _():
        base[...] = jnp.zeros_like(base)

    r_i = lax.broadcasted_iota(jnp.int32, (ROUTE_TILE, ROUTE_TILE), 0)
    c_i = lax.broadcasted_iota(jnp.int32, (ROUTE_TILE, ROUTE_TILE), 1)
    tri = (r_i < c_i).astype(BF16)
    bias = b_ref[...]
    for j in range(ROUTE_STEP // ROUTE_TILE):
        ls = slice(j * ROUTE_TILE, (j + 1) * ROUTE_TILE)
        e, w, rank, member = _route_tile(lg_ref[:, ls], bias, tri, base[...])
        e_ref[:, ls] = e
        w_ref[:, ls] = w
        r_ref[:, ls] = rank
        base[...] = base[...] + jnp.sum(member, axis=1, keepdims=True)
    cnt_ref[...] = base[...]


def _route(logits_t, b_router):
    n = logits_t.shape[1]
    tok = lambda i: (0, i)
    return pl.pallas_call(
        _route_kernel,
        grid=(n // ROUTE_STEP,),
        in_specs=[pl.BlockSpec((N_EXPERTS, ROUTE_STEP), tok),
                  pl.BlockSpec((N_EXPERTS, 1), lambda i: (0, 0))],
        out_specs=[pl.BlockSpec((TOP_K, ROUTE_STEP), tok),
                   pl.BlockSpec((TOP_K, ROUTE_STEP), tok),
                   pl.BlockSpec((TOP_K, ROUTE_STEP), tok),
                   pl.BlockSpec((N_EXPERTS, 1), lambda i: (0, 0))],
        out_shape=[jax.ShapeDtypeStruct((TOP_K, n), jnp.int32),
                   jax.ShapeDtypeStruct((TOP_K, n), F32),
                   jax.ShapeDtypeStruct((TOP_K, n), jnp.int32),
                   jax.ShapeDtypeStruct((N_EXPERTS, 1), F32)],
        scratch_shapes=[pltpu.VMEM((N_EXPERTS, 1), F32)],
        compiler_params=_params(32, ("arbitrary",)),
        name="route",
    )(logits_t, b_router.reshape(N_EXPERTS, 1).astype(F32))


def _pos_kernel(e_ref, r_ref, ps_ref, pos_ref):
    row = lax.broadcasted_iota(jnp.int32, (N_EXPERTS, ROUTE_TILE), 0)
    pstart = ps_ref[...]
    for j in range(ROUTE_STEP // ROUTE_TILE):
        ls = slice(j * ROUTE_TILE, (j + 1) * ROUTE_TILE)
        e = e_ref[:, ls]
        off = jnp.concatenate(
            [jnp.sum(jnp.where(row == e[k:k + 1], pstart, 0.0), axis=0, keepdims=True)
             for k in range(TOP_K)], axis=0)
        pos_ref[:, ls] = off.astype(jnp.int32) + r_ref[:, ls]


def _positions(eidx, rank, pstarts):
    n = eidx.shape[1]
    tok = lambda i: (0, i)
    return pl.pallas_call(
        _pos_kernel,
        grid=(n // ROUTE_STEP,),
        in_specs=[pl.BlockSpec((TOP_K, ROUTE_STEP), tok),
                  pl.BlockSpec((TOP_K, ROUTE_STEP), tok),
                  pl.BlockSpec((N_EXPERTS, 1), lambda i: (0, 0))],
        out_specs=pl.BlockSpec((TOP_K, ROUTE_STEP), tok),
        out_shape=jax.ShapeDtypeStruct((TOP_K, n), jnp.int32),
        compiler_params=_params(32, ("arbitrary",)),
        name="positions",
    )(eidx, rank, pstarts.reshape(N_EXPERTS, 1).astype(F32))


def _block_tables(counts, n_blocks):
    counts = counts.reshape(N_EXPERTS).astype(jnp.int32)
    padded = (counts + MOE_BLOCK - 1) // MOE_BLOCK * MOE_BLOCK
    pend = jnp.cumsum(padded)
    pstarts = pend - padded
    first_row = jnp.arange(n_blocks, dtype=jnp.int32) * MOE_BLOCK
    block_e = jnp.sum((pend[None, :] <= first_row[:, None]).astype(jnp.int32), axis=1)
    block_e = jnp.minimum(block_e, N_EXPERTS - 1).astype(jnp.int32)
    n_used = (pend[-1] // MOE_BLOCK).astype(jnp.int32).reshape(1)
    return pstarts, block_e, n_used


SC_WINDOW = 128


def _sc_mesh():
    return plsc.VectorSubcoreMesh(core_axis_name="core", subcore_axis_name="subcore")


def _both_halves(pos, n_rows):
    return jnp.concatenate([pos, pos + n_rows]).reshape(1, -1)


def _sc_dispatch(rows, pos, n_rows):
    _, n, w = rows.shape
    tiles = n // SC_WINDOW
    steps_per_half = pos.shape[0] // SC_WINDOW

    @functools.partial(pl.kernel, out_type=jax.ShapeDtypeStruct((2 * n_rows, w), rows.dtype),
                       mesh=_sc_mesh(), scratch_types=[])
    def scatter_kernel(x_hbm, i_hbm, o_hbm):
        def body(x_vmem, i_vmem):
            pltpu.sync_copy(x_vmem, o_hbm.at[i_vmem.at[0]])

        pltpu.emit_pipeline(
            body,
            grid=(2 * steps_per_half,),
            in_specs=[pl.BlockSpec((SC_WINDOW, w),
                                   lambda i: ((i // steps_per_half) * tiles + i % tiles, 0)),
                      pl.BlockSpec((1, SC_WINDOW), lambda i: (0, i))],
            out_specs=[],
            core_axis_name=("core", "subcore"),
            dimension_semantics=(pltpu.PARALLEL,),
        )(x_hbm, i_hbm)

    out = scatter_kernel(rows.reshape(2 * n, w), _both_halves(pos, n_rows))
    return out.reshape(2, n_rows, w)


def _sc_gather(table, pos):
    _, n_rows, w = table.shape
    m = pos.shape[0]

    @functools.partial(pl.kernel, out_type=jax.ShapeDtypeStruct((2 * m, w), table.dtype),
                       mesh=_sc_mesh(), scratch_types=[])
    def gather_kernel(t_hbm, i_hbm, o_hbm):
        def body(i_vmem, o_vmem):
            pltpu.sync_copy(t_hbm.at[i_vmem.at[0]], o_vmem)

        pltpu.emit_pipeline(
            body,
            grid=(2 * m // SC_WINDOW,),
            in_specs=[pl.BlockSpec((1, SC_WINDOW), lambda i: (0, i))],
            out_specs=[pl.BlockSpec((SC_WINDOW, w), lambda i: (i, 0))],
            core_axis_name=("core", "subcore"),
            dimension_semantics=(pltpu.PARALLEL,),
        )(i_hbm, o_hbm)

    out = gather_kernel(table.reshape(2 * n_rows, w), _both_halves(pos, n_rows))
    return out.reshape(2, m, w)


def _final_kernel(npt, x1_ref, u2_ref, yg_ref, w_ref, mod_ref, nf_ref, wsg_ref, wsu_ref, wsd_ref,
                  op_ref, os_ref):
    uq = _unpack_rows(u2_ref[0], u2_ref[1])
    a = _dot_quarters(uq, wsg_ref)
    b = _dot_quarters(uq, wsu_ref)
    f = jnp.dot((a * jax.nn.sigmoid(a) * b).astype(BF16), wsd_ref[...], preferred_element_type=F32)
    w = w_ref[...]
    fq = [f[:, q * PACK_W:(q + 1) * PACK_W] for q in range(4)]
    for k in range(TOP_K):
        yq = _unpack_rows(yg_ref[0, k], yg_ref[1, k])
        fq = [fq[q] + w[:, k:k + 1] * yq[q].astype(F32) for q in range(4)]
    x2 = x1_ref[...] + mod_ref[5:6, :] * jnp.concatenate(fq, axis=1)
    out = _rms(x2) * nf_ref[...]
    i = pl.program_id(0)

    @pl.when(i < npt)
    def _():
        op_ref[...] = out

    @pl.when(i >= npt)
    def _():
        os_ref[...] = out


def _final(seqs, x1, u2, yg, w, mod, norm_final, wsg, wsu, wsd):
    n = seqs.n
    tm = 256
    tok = lambda i: (i, 0)
    const = lambda i: (0, 0)
    npt, p_spec, s_spec = _two_stream_specs(seqs, tm)
    return pl.pallas_call(
        functools.partial(_final_kernel, npt),
        grid=(n // tm,),
        in_specs=[pl.BlockSpec((tm, D_MODEL), tok),
                  pl.BlockSpec((2, tm, PACK_W), lambda i: (0, i, 0)),
                  pl.BlockSpec((2, TOP_K, tm, PACK_W), lambda i: (0, 0, i, 0)),
                  pl.BlockSpec((tm, TOP_K), tok),
                  pl.BlockSpec((None, N_MOD, D_MODEL), lambda i: (seqs.info(i * tm)[0], 0, 0)),
                  pl.BlockSpec((1, D_MODEL), const),
                  pl.BlockSpec((D_MODEL, D_EXPERT), const),
                  pl.BlockSpec((D_MODEL, D_EXPERT), const),
                  pl.BlockSpec((D_EXPERT, D_MODEL), const)],
        out_specs=[p_spec, s_spec],
        out_shape=[jax.ShapeDtypeStruct((seqs.np_, D_MODEL), F32),
                   jax.ShapeDtypeStruct((n - seqs.np_, D_MODEL), F32)],
        compiler_params=_params(48, ("arbitrary",)),
        name="final",
    )(x1, u2, yg, w, mod, norm_final.reshape(1, D_MODEL), wsg, wsu, wsd)


def _layer(seqs, x_p, x_s, c, w_ada, b_ada, norm_mix, w_in, na_rpb, hg_lb, hg_norm, w_branch_a,
           w_branch_b, w_out, norm_ffn, w_router, b_router, w_exp_gate, w_exp_up, w_exp_down,
           w_sh_gate, w_sh_up, w_sh_down, norm_final):
    n = seqs.n
    c_rows = -(-seqs.nseq // 8) * 8
    c_pad = jnp.zeros((c_rows, D_MODEL), F32).at[:seqs.nseq].set(c)
    mod = _ada(c_pad, w_ada[0], b_ada[0])[:seqs.nseq].reshape(seqs.nseq, N_MOD, D_MODEL)
    lb = jnp.cumsum(jax.nn.softmax(hg_lb.astype(F32), axis=0), axis=0)[0]

    proj = _inproj(seqs, x_p, x_s, mod, norm_mix[0], w_in[0].astype(BF16))
    att = _na(seqs, proj, _na_bias_table(na_rpb[0]))
    o_f, o_b = _hgrn(seqs, proj, lb)
    x1, u2, logits = _merge(seqs, x_p, x_s, att, o_f, o_b, proj, mod, hg_norm[0], norm_ffn[0],
                            w_branch_a[0].astype(BF16), w_branch_b[0].astype(BF16),
                            w_out[0].astype(BF16), w_router[0].T)

    eidx, w, rank, counts = _route(logits, b_router[0])
    n_rows = n * TOP_K + N_EXPERTS * MOE_BLOCK
    pstarts, block_e, n_used = _block_tables(counts, n_rows // MOE_BLOCK)
    pos = _positions(eidx, rank, pstarts).reshape(-1)
    xs = _sc_dispatch(u2, pos, n_rows)
    ys = _experts(xs, block_e, n_used, w_exp_gate[0], w_exp_up[0], w_exp_down[0])
    yg = _sc_gather(ys, pos).reshape(2, TOP_K, n, PACK_W)
    return _final(seqs, x1, u2, yg, w.T, mod, norm_final, w_sh_gate[0].astype(BF16),
                  w_sh_up[0].astype(BF16), w_sh_down[0].astype(BF16))


def kernel(x_prompt, x_sample, c_prompt, c_sample, w_ada, b_ada, norm_mix, w_in, na_rpb, hg_lb, hg_norm, w_branch_a, w_branch_b, w_out, norm_ffn, w_router, b_router, w_exp_gate, w_exp_up, w_exp_down, w_sh_gate, w_sh_up, w_sh_down, norm_final):
    bp, tp, _ = x_prompt.shape
    bs, ts, _ = x_sample.shape
    seqs = _Seqs(bp, tp, bs, ts)
    c = jnp.concatenate([c_prompt, c_sample])
    y_p, y_s = _layer(seqs, x_prompt.reshape(bp * tp, D_MODEL), x_sample.reshape(bs * ts, D_MODEL),
                      c, w_ada, b_ada, norm_mix, w_in, na_rpb, hg_lb, hg_norm, w_branch_a,
                      w_branch_b, w_out, norm_ffn, w_router, b_router, w_exp_gate, w_exp_up,
                      w_exp_down, w_sh_gate, w_sh_up, w_sh_down, norm_final)
    return (y_p.reshape(bp, tp, D_MODEL), y_s.reshape(bs, ts, D_MODEL))
```

```python
import functools

import jax
import jax.numpy as jnp
import numpy as np
from jax import lax
from jax.experimental import pallas as pl
from jax.experimental.pallas import tpu as pltpu
from jax.experimental.pallas import tpu_sc as plsc

D_MODEL = 1024
GRID_W = 64
NA_HEADS = 8
NA_HEAD_DIM = 64
NA_WIDTH = NA_HEADS * NA_HEAD_DIM
NA_ROWS = 8
NA_COLS = 16
HG_HEADS = 4
HG_KEY_DIM = 128
HG_WIDTH = HG_HEADS * HG_KEY_DIM
HG_CHUNK = 64
N_EXPERTS = 256
TOP_K = 8
N_GROUPS = 8
TOPK_GROUPS = 4
D_EXPERT = 256
ROUTE_SCALE = 2.5
N_MOD = 6
RMS_EPS = 1e-6

MOE_BLOCK = 512
NA_GROUP = 8
NA_TOK = NA_GROUP * GRID_W
HG_STEP = 256
MASK_VALUE = -1e30

F32 = jnp.float32
BF16 = jnp.bfloat16
HIGHEST = lax.Precision.HIGHEST
NT_DIMS = (((1,), (1,)), ((), ()))
TN_DIMS = (((0,), (0,)), ((), ()))

SLAB_Q, SLAB_K, SLAB_V, SLAB_HQ = (0, 0), (0, 1), (1, 0), (1, 1)
SLAB_FF, SLAB_FB, SLAB_HI, SLAB_HG = (2, 0), (2, 1), (3, 0), (3, 1)
SLAB_GA, SLAB_GB = 4, 5


def _params(vmem_mb, sem=None):
    kw = dict(vmem_limit_bytes=vmem_mb * 1024 * 1024)
    if sem is not None:
        kw["dimension_semantics"] = sem
    return pltpu.CompilerParams(**kw)


class _Seqs:
    def __init__(self, bp, tp, bs, ts):
        self.bp, self.tp, self.bs, self.ts = bp, tp, bs, ts
        self.np_ = bp * tp
        self.n = bp * tp + bs * ts
        self.nseq = bp + bs

    def info(self, t0):
        in_p = t0 < self.np_
        rel = jnp.maximum(t0 - self.np_, 0)
        sid = jnp.where(in_p, t0 // self.tp, self.bp + rel // self.ts)
        start = jnp.where(in_p, (t0 // self.tp) * self.tp, self.np_ + (rel // self.ts) * self.ts)
        length = jnp.where(in_p, self.tp, self.ts)
        return sid, start, length


def _ada_kernel(c_ref, w_ref, b_ref, o_ref):
    c = c_ref[...]
    a = c * jax.nn.sigmoid(c)
    o_ref[...] = jnp.dot(a, w_ref[...], precision=HIGHEST, preferred_element_type=F32) + b_ref[...]


def _ada(c_pad, w_ada, b_ada):
    rows = c_pad.shape[0]
    n_out = w_ada.shape[1]
    tn = 1024
    return pl.pallas_call(
        _ada_kernel,
        grid=(n_out // tn,),
        in_specs=[pl.BlockSpec((rows, D_MODEL), lambda j: (0, 0)),
                  pl.BlockSpec((D_MODEL, tn), lambda j: (0, j)),
                  pl.BlockSpec((1, tn), lambda j: (0, j))],
        out_specs=pl.BlockSpec((rows, tn), lambda j: (0, j)),
        out_shape=jax.ShapeDtypeStruct((rows, n_out), F32),
        compiler_params=_params(32),
        name="ada",
    )(c_pad, w_ada, b_ada.reshape(1, n_out))


def _rms(x):
    return x * lax.rsqrt(jnp.mean(x * x, axis=-1, keepdims=True) + RMS_EPS)


PACK_W = D_MODEL // 4


def _pack_rows(x):
    out = []
    for h in range(2):
        lo = x[:, (2 * h) * PACK_W:(2 * h + 1) * PACK_W].astype(BF16).astype(F32)
        hi = x[:, (2 * h + 1) * PACK_W:(2 * h + 2) * PACK_W].astype(BF16).astype(F32)
        out.append(lax.bitcast_convert_type(hi, jnp.uint32)
                   | (lax.bitcast_convert_type(lo, jnp.uint32) >> 16))
    return out


def _unpack_rows(p0, p1):
    quarters = []
    for p in (p0, p1):
        quarters.append(lax.bitcast_convert_type(p << 16, F32).astype(BF16))
        quarters.append(lax.bitcast_convert_type(p & jnp.uint32(0xFFFF0000), F32).astype(BF16))
    return quarters


def _dot_quarters(quarters, w_ref):
    acc = None
    for q, xq in enumerate(quarters):
        part = jnp.dot(xq, w_ref[q * PACK_W:(q + 1) * PACK_W, :], preferred_element_type=F32)
        acc = part if acc is None else acc + part
    return acc


def _two_stream_specs(seqs, tm, grid_rank=1):
    npt = seqs.np_ // tm
    nst = (seqs.n - seqs.np_) // tm
    if grid_rank == 1:
        p_map = lambda i: (jnp.minimum(i, npt - 1), 0)
        s_map = lambda i: (jnp.clip(i - npt, 0, nst - 1), 0)
    else:
        p_map = lambda i, j: (jnp.minimum(i, npt - 1), 0)
        s_map = lambda i, j: (jnp.clip(i - npt, 0, nst - 1), 0)
    return npt, pl.BlockSpec((tm, D_MODEL), p_map), pl.BlockSpec((tm, D_MODEL), s_map)


def _inproj_kernel(npt, xp_ref, xs_ref, mod_ref, g_ref, w_ref, o_ref, u_scr):
    @pl.when(pl.program_id(1) == 0)
    def _():
        x = jnp.where(pl.program_id(0) < npt, xp_ref[...], xs_ref[...])
        y = _rms(x) * g_ref[...]
        u = y * (1.0 + mod_ref[1:2, :]) + mod_ref[0:1, :]
        u_scr[...] = u.astype(BF16)

    o_ref[...] = jnp.dot(u_scr[...], w_ref[...], preferred_element_type=F32).astype(o_ref.dtype)


def _inproj(seqs, x_p, x_s, mod, norm_mix, w_in_bf):
    n = seqs.n
    tm = min(1024, seqs.tp, seqs.ts)
    tn = 1024
    n_slab = w_in_bf.shape[1] // tn
    npt, p_spec, s_spec = _two_stream_specs(seqs, tm, grid_rank=2)
    return pl.pallas_call(
        functools.partial(_inproj_kernel, npt),
        grid=(n // tm, n_slab),
        in_specs=[p_spec, s_spec,
                  pl.BlockSpec((None, N_MOD, D_MODEL), lambda i, j: (seqs.info(i * tm)[0], 0, 0)),
                  pl.BlockSpec((1, D_MODEL), lambda i, j: (0, 0)),
                  pl.BlockSpec((D_MODEL, tn), lambda i, j: (0, j))],
        out_specs=pl.BlockSpec((None, tm, tn), lambda i, j: (j, i, 0)),
        out_shape=jax.ShapeDtypeStruct((n_slab, n, tn), BF16),
        scratch_shapes=[pltpu.VMEM((tm, D_MODEL), BF16)],
        compiler_params=_params(40, ("arbitrary", "arbitrary")),
        name="inproj",
    )(x_p, x_s, mod, norm_mix.reshape(1, D_MODEL), w_in_bf)


def _na_bias_table(rpb):
    col = np.arange(GRID_W)
    cs = np.clip(col - NA_COLS // 2, 0, GRID_W - NA_COLS)
    valid = (col[None, :] >= cs[:, None]) & (col[None, :] < cs[:, None] + NA_COLS)
    coff = col[None, :] - col[:, None] + NA_COLS - 1
    onehot = (coff[None] == np.arange(2 * NA_COLS - 1)[:, None, None]) & valid[None]
    toep = jnp.einsum("hrc,cqk->hrqk", rpb.astype(F32), jnp.asarray(onehot, F32),
                      precision=HIGHEST)
    toep = jnp.where(valid[None, None], toep, MASK_VALUE)
    tabs = []
    for s in range(NA_ROWS):
        b = toep[:, s:s + NA_ROWS]
        tabs.append(jnp.transpose(b, (0, 2, 1, 3)).reshape(NA_HEADS, GRID_W, NA_ROWS * GRID_W))
    return jnp.stack(tabs)


def _na_geometry(seqs, g):
    _, start, length = seqs.info(g * NA_TOK)
    row0 = start // GRID_W
    rows = length // GRID_W
    gb0 = row0 // NA_GROUP
    nb = rows // NA_GROUP
    kb = gb0 + jnp.clip(g - gb0 - 1, 0, nb - 3)
    return row0, rows, kb


def _na_kernel(seqs, q_ref, k0, k1, k2, v0, v1, v2, bias_ref, o_ref, kw, vw):
    g = pl.program_id(0)
    row0, rows, kb = _na_geometry(seqs, g)
    for d, (kr, vr) in enumerate(((k0, v0), (k1, v1), (k2, v2))):
        kw[d * NA_TOK:(d + 1) * NA_TOK, :] = kr[...]
        vw[d * NA_TOK:(d + 1) * NA_TOK, :] = vr[...]
    scale = NA_HEAD_DIM ** -0.5

    def row_body(i, carry):
        r = g * NA_GROUP + i - row0
        rs = jnp.clip(r - NA_ROWS // 2, 0, rows - NA_ROWS)
        s_idx = rs - r + NA_ROWS - 1
        ko = pl.multiple_of((rs + row0 - kb * NA_GROUP) * GRID_W, GRID_W)
        qo = pl.multiple_of(i * GRID_W, GRID_W)
        outs = []
        for h in range(NA_HEADS):
            hs = slice(h * NA_HEAD_DIM, (h + 1) * NA_HEAD_DIM)
            qh = q_ref[pl.ds(qo, GRID_W), hs]
            kh = kw[pl.ds(ko, NA_ROWS * GRID_W), hs]
            vh = vw[pl.ds(ko, NA_ROWS * GRID_W), hs]
            s = lax.dot_general(qh, kh, NT_DIMS, preferred_element_type=F32) * scale
            s = s + bias_ref[s_idx, h]
            p = jnp.exp(s - jnp.max(s, axis=-1, keepdims=True))
            l = jnp.sum(p, axis=-1, keepdims=True)
            o = jnp.dot(p.astype(BF16), vh, preferred_element_type=F32)
            outs.append(o / l)
        o_ref[pl.ds(qo, GRID_W), :] = jnp.concatenate(outs, axis=1).astype(o_ref.dtype)
        return carry

    lax.fori_loop(0, NA_GROUP, row_body, 0)


def _na(seqs, proj, bias_tab):
    n = seqs.n

    def kv_spec(slab, d):
        return pl.BlockSpec((None, NA_TOK, NA_WIDTH),
                            lambda g: (slab[0], _na_geometry(seqs, g)[2] + d, slab[1]))

    return pl.pallas_call(
        functools.partial(_na_kernel, seqs),
        grid=(n // NA_TOK,),
        in_specs=[pl.BlockSpec((None, NA_TOK, NA_WIDTH), lambda g: (SLAB_Q[0], g, SLAB_Q[1]))]
        + [kv_spec(SLAB_K, d) for d in range(3)] + [kv_spec(SLAB_V, d) for d in range(3)]
        + [pl.BlockSpec(bias_tab.shape, lambda g: (0, 0, 0, 0))],
        out_specs=pl.BlockSpec((NA_TOK, NA_WIDTH), lambda g: (g, 0)),
        out_shape=jax.ShapeDtypeStruct((n, NA_WIDTH), BF16),
        scratch_shapes=[pltpu.VMEM((3 * NA_TOK, NA_WIDTH), BF16),
                        pltpu.VMEM((3 * NA_TOK, NA_WIDTH), BF16)],
        compiler_params=_params(48, ("arbitrary",)),
        name="natten",
    )(proj, proj, proj, proj, proj, proj, proj, bias_tab)


def _hg_chunk(q, z, v, lb, tri, mask, mid, last, st_ref):
    sig = jax.nn.sigmoid(z)
    f = lb + (1.0 - lb) * sig
    lf = jnp.log(f)
    kin = (1.0 - lb) * (1.0 - sig)
    hi = lf.astype(BF16)
    r1 = lf - hi.astype(F32)
    md = r1.astype(BF16)
    lo = (r1 - md.astype(F32)).astype(BF16)
    g3 = jnp.dot(tri, jnp.concatenate([lo, md, hi], axis=1), preferred_element_type=F32)
    gcum = (g3[:, :HG_WIDTH] + g3[:, HG_WIDTH:2 * HG_WIDTH]) + g3[:, 2 * HG_WIDTH:]
    gm = gcum[mid:mid + 1, :]
    gl = gcum[last:last + 1, :]
    up = jnp.exp(gcum - gm)
    dn = jnp.exp(gm - gcum)
    qa = (q * up).astype(BF16)
    ka = (kin * dn).astype(BF16)
    qe = (q * (up * jnp.exp(gm))).astype(BF16)
    kd = (kin * (dn * jnp.exp(gl - gm))).astype(BF16)
    eg = jnp.exp(gl)
    vb = v.astype(BF16)
    outs = []
    for h in range(HG_HEADS):
        hs = slice(h * HG_KEY_DIM, (h + 1) * HG_KEY_DIM)
        a = lax.dot_general(qa[:, hs], ka[:, hs], NT_DIMS, preferred_element_type=F32)
        a = jnp.where(mask, a, 0.0)
        st = st_ref[h]
        o = jnp.dot(a.astype(BF16), vb[:, hs], preferred_element_type=F32)
        o = o + lax.dot_general(qe[:, hs], st.astype(BF16), NT_DIMS, preferred_element_type=F32)
        st_ref[h] = st * eg[:, hs] + lax.dot_general(vb[:, hs], kd[:, hs], TN_DIMS,
                                                    preferred_element_type=F32)
        outs.append(o)
    return jnp.concatenate(outs, axis=1)


def _hg_kernel(seqs, qf_ref, zf_ref, vf_ref, qb_ref, zb_ref, vb_ref, lb_ref, of_ref, ob_ref,
               stf, stb):
    i = pl.program_id(0)
    nsteps = pl.num_programs(0)
    tf = i * HG_STEP
    tb = (nsteps - 1 - i) * HG_STEP
    _, start_f, _ = seqs.info(tf)
    _, start_b, len_b = seqs.info(tb)

    @pl.when(tf == start_f)
    def _():
        stf[...] = jnp.zeros_like(stf)

    @pl.when(tb + HG_STEP == start_b + len_b)
    def _():
        stb[...] = jnp.zeros_like(stb)

    lb = lb_ref[...]
    row = lax.broadcasted_iota(jnp.int32, (HG_CHUNK, HG_CHUNK), 0)
    col = lax.broadcasted_iota(jnp.int32, (HG_CHUNK, HG_CHUNK), 1)
    lower = row >= col
    upper = col >= row
    tri_f = lower.astype(BF16)
    tri_b = upper.astype(BF16)
    nchunk = HG_STEP // HG_CHUNK
    for c in range(nchunk):
        cs = slice(c * HG_CHUNK, (c + 1) * HG_CHUNK)
        of_ref[cs, :] = _hg_chunk(qf_ref[cs, :].astype(F32), zf_ref[cs, :].astype(F32),
                                  vf_ref[cs, :].astype(F32), lb, tri_f, lower,
                                  HG_CHUNK // 2 - 1, HG_CHUNK - 1, stf)
        cb = nchunk - 1 - c
        bs = slice(cb * HG_CHUNK, (cb + 1) * HG_CHUNK)
        ob_ref[bs, :] = _hg_chunk(qb_ref[bs, :].astype(F32), zb_ref[bs, :].astype(F32),
                                  vb_ref[bs, :].astype(F32), lb, tri_b, upper,
                                  HG_CHUNK // 2, 0, stb)


def _hgrn(seqs, proj, lb):
    n = seqs.n
    nsteps = n // HG_STEP

    def spec(slab, rev):
        if rev:
            return pl.BlockSpec((None, HG_STEP, HG_WIDTH), lambda i: (slab[0], nsteps - 1 - i, slab[1]))
        return pl.BlockSpec((None, HG_STEP, HG_WIDTH), lambda i: (slab[0], i, slab[1]))

    return pl.pallas_call(
        functools.partial(_hg_kernel, seqs),
        grid=(nsteps,),
        in_specs=[spec(SLAB_HQ, False), spec(SLAB_FF, False), spec(SLAB_HI, False),
                  spec(SLAB_HQ, True), spec(SLAB_FB, True), spec(SLAB_HI, True),
                  pl.BlockSpec((1, HG_WIDTH), lambda i: (0, 0))],
        out_specs=[pl.BlockSpec((HG_STEP, HG_WIDTH), lambda i: (i, 0)),
                   pl.BlockSpec((HG_STEP, HG_WIDTH), lambda i: (nsteps - 1 - i, 0))],
        out_shape=[jax.ShapeDtypeStruct((n, HG_WIDTH), F32)] * 2,
        scratch_shapes=[pltpu.VMEM((HG_HEADS, HG_KEY_DIM, HG_KEY_DIM), F32)] * 2,
        compiler_params=_params(32, ("arbitrary",)),
        name="hgrn2",
    )(proj, proj, proj, proj, proj, proj, lb.reshape(1, HG_WIDTH))


def _merge_kernel(npt, xp_ref, xs_ref, att_ref, of_ref, ob_ref, hg_ref, ga_ref, gb_ref, mod_ref,
                  hgn_ref, nffn_ref, wa_ref, wb_ref, wo_ref, wr_ref, x1_ref, u2_ref, lg_ref):
    x = jnp.where(pl.program_id(0) < npt, xp_ref[...], xs_ref[...])
    o = of_ref[...] + ob_ref[...]
    parts = []
    for h in range(HG_HEADS):
        hs = slice(h * HG_KEY_DIM, (h + 1) * HG_KEY_DIM)
        parts.append(_rms(o[:, hs]))
    on = jnp.concatenate(parts, axis=1) * hgn_ref[...]
    gate = hg_ref[...].astype(F32)
    hb = (on * (gate * jax.nn.sigmoid(gate))).astype(BF16)
    ya = jnp.dot(att_ref[...], wa_ref[...], preferred_element_type=F32)
    yb = jnp.dot(hb, wb_ref[...], preferred_element_type=F32)
    merged = (jax.nn.sigmoid(ga_ref[...].astype(F32)) * ya
              + jax.nn.sigmoid(gb_ref[...].astype(F32)) * yb)
    x1 = x + mod_ref[2:3, :] * jnp.dot(merged.astype(BF16), wo_ref[...],
                                       preferred_element_type=F32)
    x1_ref[...] = x1
    u2 = _rms(x1) * nffn_ref[...] * (1.0 + mod_ref[4:5, :]) + mod_ref[3:4, :]
    u2_ref[0], u2_ref[1] = _pack_rows(u2)
    lg_ref[...] = lax.dot_general(wr_ref[...], u2, NT_DIMS, precision=HIGHEST,
                                  preferred_element_type=F32)


def _merge(seqs, x_p, x_s, att, o_f, o_b, proj, mod, hg_norm, norm_ffn, wa, wb, wo, wr):
    n = seqs.n
    tm = 256
    tok = lambda i: (i, 0)
    const = lambda i: (0, 0)
    npt, p_spec, s_spec = _two_stream_specs(seqs, tm)
    return pl.pallas_call(
        functools.partial(_merge_kernel, npt),
        grid=(n // tm,),
        in_specs=[p_spec, s_spec,
                  pl.BlockSpec((tm, NA_WIDTH), tok),
                  pl.BlockSpec((tm, HG_WIDTH), tok),
                  pl.BlockSpec((tm, HG_WIDTH), tok),
                  pl.BlockSpec((None, tm, HG_WIDTH), lambda i: (SLAB_HG[0], i, SLAB_HG[1])),
                  pl.BlockSpec((None, tm, D_MODEL), lambda i: (SLAB_GA, i, 0)),
                  pl.BlockSpec((None, tm, D_MODEL), lambda i: (SLAB_GB, i, 0)),
                  pl.BlockSpec((None, N_MOD, D_MODEL), lambda i: (seqs.info(i * tm)[0], 0, 0)),
                  pl.BlockSpec((1, HG_WIDTH), const),
                  pl.BlockSpec((1, D_MODEL), const),
                  pl.BlockSpec((NA_WIDTH, D_MODEL), const),
                  pl.BlockSpec((HG_WIDTH, D_MODEL), const),
                  pl.BlockSpec((D_MODEL, D_MODEL), const),
                  pl.BlockSpec((N_EXPERTS, D_MODEL), const)],
        out_specs=[pl.BlockSpec((tm, D_MODEL), tok),
                   pl.BlockSpec((2, tm, PACK_W), lambda i: (0, i, 0)),
                   pl.BlockSpec((N_EXPERTS, tm), lambda i: (0, i))],
        out_shape=[jax.ShapeDtypeStruct((n, D_MODEL), F32),
                   jax.ShapeDtypeStruct((2, n, PACK_W), jnp.uint32),
                   jax.ShapeDtypeStruct((N_EXPERTS, n), F32)],
        compiler_params=_params(48, ("arbitrary",)),
        name="merge",
    )(x_p, x_s, att, o_f, o_b, proj, proj, proj, mod, hg_norm.reshape(1, HG_WIDTH),
      norm_ffn.reshape(1, D_MODEL), wa, wb, wo, wr)


def _expert_kernel(be_ref, nused_ref, xs_ref, wg_ref, wu_ref, wd_ref, ys_ref, wg_s, wu_s, wd_s):
    i = pl.program_id(0)

    @pl.when(i < nused_ref[0])
    def _():
        @pl.when((i == 0) | (be_ref[i] != be_ref[jnp.maximum(i - 1, 0)]))
        def _():
            wg_s[...] = wg_ref[...].astype(BF16)
            wu_s[...] = wu_ref[...].astype(BF16)
            wd_s[...] = wd_ref[...].astype(BF16)

        xq = _unpack_rows(xs_ref[0], xs_ref[1])
        a = _dot_quarters(xq, wg_s)
        b = _dot_quarters(xq, wu_s)
        h = (a * jax.nn.sigmoid(a) * b).astype(BF16)
        ys_ref[0], ys_ref[1] = _pack_rows(jnp.dot(h, wd_s[...], preferred_element_type=F32))


def _experts(xs, block_e, n_used, wg, wu, wd):
    n_rows = xs.shape[1]
    n_blocks = n_rows // MOE_BLOCK
    grid_spec = pltpu.PrefetchScalarGridSpec(
        num_scalar_prefetch=2,
        grid=(n_blocks,),
        in_specs=[pl.BlockSpec((2, MOE_BLOCK, PACK_W), lambda i, be, nu: (0, i, 0)),
                  pl.BlockSpec((None, D_MODEL, D_EXPERT), lambda i, be, nu: (be[i], 0, 0)),
                  pl.BlockSpec((None, D_MODEL, D_EXPERT), lambda i, be, nu: (be[i], 0, 0)),
                  pl.BlockSpec((None, D_EXPERT, D_MODEL), lambda i, be, nu: (be[i], 0, 0))],
        out_specs=pl.BlockSpec((2, MOE_BLOCK, PACK_W), lambda i, be, nu: (0, i, 0)),
        scratch_shapes=[pltpu.VMEM((D_MODEL, D_EXPERT), BF16),
                        pltpu.VMEM((D_MODEL, D_EXPERT), BF16),
                        pltpu.VMEM((D_EXPERT, D_MODEL), BF16)],
    )
    return pl.pallas_call(
        _expert_kernel,
        grid_spec=grid_spec,
        out_shape=jax.ShapeDtypeStruct((2, n_rows, PACK_W), jnp.uint32),
        compiler_params=_params(32, ("arbitrary",)),
        name="experts",
    )(block_e, n_used, xs, wg, wu, wd)


ROUTE_TILE = 128
ROUTE_STEP = 512
NEG_INF = float("-inf")


def _first_max(x, idx, big):
    m = jnp.max(x, axis=0, keepdims=True)
    first = jnp.min(jnp.where(x == m, idx, big), axis=0, keepdims=True)
    return m, first


def _route_tile(lg, bias, tri, base):
    per_group = N_EXPERTS // N_GROUPS
    scores = jax.nn.sigmoid(lg)
    sel = scores + bias
    lrow = lax.broadcasted_iota(jnp.int32, (per_group, ROUTE_TILE), 0)
    gs = []
    for g in range(N_GROUPS):
        x = sel[g * per_group:(g + 1) * per_group]
        m1, first = _first_max(x, lrow, per_group)
        m2 = jnp.max(jnp.where(lrow == first, NEG_INF, x), axis=0, keepdims=True)
        gs.append(m1 + m2)
    cur = jnp.concatenate(gs, axis=0)
    grow = lax.broadcasted_iota(jnp.int32, (N_GROUPS, ROUTE_TILE), 0)
    chosen = jnp.zeros((N_GROUPS, ROUTE_TILE), jnp.int32)
    for _ in range(TOPK_GROUPS):
        _, first = _first_max(cur, grow, N_GROUPS)
        hit = grow == first
        chosen = jnp.where(hit, 1, chosen)
        cur = jnp.where(hit, NEG_INF, cur)
    cur = jnp.concatenate(
        [jnp.where(chosen[g:g + 1] > 0, sel[g * per_group:(g + 1) * per_group], NEG_INF)
         for g in range(N_GROUPS)], axis=0)
    row = lax.broadcasted_iota(jnp.int32, (N_EXPERTS, ROUTE_TILE), 0)
    member = jnp.zeros((N_EXPERTS, ROUTE_TILE), F32)
    es, ws = [], []
    for _ in range(TOP_K):
        _, first = _first_max(cur, row, N_EXPERTS)
        hit = row == first
        es.append(first)
        ws.append(jnp.sum(jnp.where(hit, scores, 0.0), axis=0, keepdims=True))
        cur = jnp.where(hit, NEG_INF, cur)
        member = jnp.where(hit, 1.0, member)
    e = jnp.concatenate(es, axis=0)
    w = jnp.concatenate(ws, axis=0)
    w = w / jnp.sum(w, axis=0, keepdims=True) * ROUTE_SCALE
    before = jnp.dot(member.astype(BF16), tri, preferred_element_type=F32) + base
    rank = jnp.concatenate(
        [jnp.sum(jnp.where(row == es[k], before, 0.0), axis=0, keepdims=True) for k in range(TOP_K)],
        axis=0)
    return e, w, rank.astype(jnp.int32), member


def _route_kernel(lg_ref, b_ref, e_ref, w_ref, r_ref, cnt_ref, base):
    @pl.when(pl.program_id(0) == 0)
    def _():
        base[...] = jnp.zeros_like(base)

    r_i = lax.broadcasted_iota(jnp.int32, (ROUTE_TILE, ROUTE_TILE), 0)
    c_i = lax.broadcasted_iota(jnp.int32, (ROUTE_TILE, ROUTE_TILE), 1)
    tri = (r_i < c_i).astype(BF16)
    bias = b_ref[...]
    for j in range(ROUTE_STEP // ROUTE_TILE):
        ls = slice(j * ROUTE_TILE, (j + 1) * ROUTE_TILE)
        e, w, rank, member = _route_tile(lg_ref[:, ls], bias, tri, base[...])
        e_ref[:, ls] = e
        w_ref[:, ls] = w
        r_ref[:, ls] = rank
        base[...] = base[...] + jnp.sum(member, axis=1, keepdims=True)
    cnt_ref[...] = base[...]


def _route(logits_t, b_router):
    n = logits_t.shape[1]
    tok = lambda i: (0, i)
    return pl.pallas_call(
        _route_kernel,
        grid=(n // ROUTE_STEP,),
        in_specs=[pl.BlockSpec((N_EXPERTS, ROUTE_STEP), tok),
                  pl.BlockSpec((N_EXPERTS, 1), lambda i: (0, 0))],
        out_specs=[pl.BlockSpec((TOP_K, ROUTE_STEP), tok),
                   pl.BlockSpec((TOP_K, ROUTE_STEP), tok),
                   pl.BlockSpec((TOP_K, ROUTE_STEP), tok),
                   pl.BlockSpec((N_EXPERTS, 1), lambda i: (0, 0))],
        out_shape=[jax.ShapeDtypeStruct((TOP_K, n), jnp.int32),
                   jax.ShapeDtypeStruct((TOP_K, n), F32),
                   jax.ShapeDtypeStruct((TOP_K, n), jnp.int32),
                   jax.ShapeDtypeStruct((N_EXPERTS, 1), F32)],
        scratch_shapes=[pltpu.VMEM((N_EXPERTS, 1), F32)],
        compiler_params=_params(32, ("arbitrary",)),
        name="route",
    )(logits_t, b_router.reshape(N_EXPERTS, 1).astype(F32))


def _pos_kernel(e_ref, r_ref, ps_ref, pos_ref):
    row = lax.broadcasted_iota(jnp.int32, (N_EXPERTS, ROUTE_TILE), 0)
    pstart = ps_ref[...]
    for j in range(ROUTE_STEP // ROUTE_TILE):
        ls = slice(j * ROUTE_TILE, (j + 1) * ROUTE_TILE)
        e = e_ref[:, ls]
        off = jnp.concatenate(
            [jnp.sum(jnp.where(row == e[k:k + 1], pstart, 0.0), axis=0, keepdims=True)
             for k in range(TOP_K)], axis=0)
        pos_ref[:, ls] = off.astype(jnp.int32) + r_ref[:, ls]


def _positions(eidx, rank, pstarts):
    n = eidx.shape[1]
    tok = lambda i: (0, i)
    return pl.pallas_call(
        _pos_kernel,
        grid=(n // ROUTE_STEP,),
        in_specs=[pl.BlockSpec((TOP_K, ROUTE_STEP), tok),
                  pl.BlockSpec((TOP_K, ROUTE_STEP), tok),
                  pl.BlockSpec((N_EXPERTS, 1), lambda i: (0, 0))],
        out_specs=pl.BlockSpec((TOP_K, ROUTE_STEP), tok),
        out_shape=jax.ShapeDtypeStruct((TOP_K, n), jnp.int32),
        compiler_params=_params(32, ("arbitrary",)),
        name="positions",
    )(eidx, rank, pstarts.reshape(N_EXPERTS, 1).astype(F32))


def _block_tables(counts, n_blocks):
    counts = counts.reshape(N_EXPERTS).astype(jnp.int32)
    padded = (counts + MOE_BLOCK - 1) // MOE_BLOCK * MOE_BLOCK
    pend = jnp.cumsum(padded)
    pstarts = pend - padded
    first_row = jnp.arange(n_blocks, dtype=jnp.int32) * MOE_BLOCK
    block_e = jnp.sum((pend[None, :] <= first_row[:, None]).astype(jnp.int32), axis=1)
    block_e = jnp.minimum(block_e, N_EXPERTS - 1).astype(jnp.int32)
    n_used = (pend[-1] // MOE_BLOCK).astype(jnp.int32).reshape(1)
    return pstarts, block_e, n_used


SC_WINDOW = 128


def _sc_mesh():
    return plsc.VectorSubcoreMesh(core_axis_name="core", subcore_axis_name="subcore")


def _both_halves(pos, n_rows):
    return jnp.concatenate([pos, pos + n_rows]).reshape(1, -1)


def _sc_dispatch(rows, pos, n_rows):
    _, n, w = rows.shape
    tiles = n // SC_WINDOW
    steps_per_half = pos.shape[0] // SC_WINDOW

    @functools.partial(pl.kernel, out_type=jax.ShapeDtypeStruct((2 * n_rows, w), rows.dtype),
                       mesh=_sc_mesh(), scratch_types=[])
    def scatter_kernel(x_hbm, i_hbm, o_hbm):
        def body(x_vmem, i_vmem):
            pltpu.sync_copy(x_vmem, o_hbm.at[i_vmem.at[0]])

        pltpu.emit_pipeline(
            body,
            grid=(2 * steps_per_half,),
            in_specs=[pl.BlockSpec((SC_WINDOW, w),
                                   lambda i: ((i // steps_per_half) * tiles + i % tiles, 0)),
                      pl.BlockSpec((1, SC_WINDOW), lambda i: (0, i))],
            out_specs=[],
            core_axis_name=("core", "subcore"),
            dimension_semantics=(pltpu.PARALLEL,),
        )(x_hbm, i_hbm)

    out = scatter_kernel(rows.reshape(2 * n, w), _both_halves(pos, n_rows))
    return out.reshape(2, n_rows, w)


def _sc_gather(table, pos):
    _, n_rows, w = table.shape
    m = pos.shape[0]

    @functools.partial(pl.kernel, out_type=jax.ShapeDtypeStruct((2 * m, w), table.dtype),
                       mesh=_sc_mesh(), scratch_types=[])
    def gather_kernel(t_hbm, i_hbm, o_hbm):
        def body(i_vmem, o_vmem):
            pltpu.sync_copy(t_hbm.at[i_vmem.at[0]], o_vmem)

        pltpu.emit_pipeline(
            body,
            grid=(2 * m // SC_WINDOW,),
            in_specs=[pl.BlockSpec((1, SC_WINDOW), lambda i: (0, i))],
            out_specs=[pl.BlockSpec((SC_WINDOW, w), lambda i: (i, 0))],
            core_axis_name=("core", "subcore"),
            dimension_semantics=(pltpu.PARALLEL,),
        )(i_hbm, o_hbm)

    out = gather_kernel(table.reshape(2 * n_rows, w), _both_halves(pos, n_rows))
    return out.reshape(2, m, w)


def _final_kernel(npt, x1_ref, u2_ref, yg_ref, w_ref, mod_ref, nf_ref, wsg_ref, wsu_ref, wsd_ref,
                  op_ref, os_ref):
    uq = _unpack_rows(u2_ref[0], u2_ref[1])
    a = _dot_quarters(uq, wsg_ref)
    b = _dot_quarters(uq, wsu_ref)
    f = jnp.dot((a * jax.nn.sigmoid(a) * b).astype(BF16), wsd_ref[...], preferred_element_type=F32)
    w = w_ref[...]
    fq = [f[:, q * PACK_W:(q + 1) * PACK_W] for q in range(4)]
    for k in range(TOP_K):
        yq = _unpack_rows(yg_ref[0, k], yg_ref[1, k])
        fq = [fq[q] + w[:, k:k + 1] * yq[q].astype(F32) for q in range(4)]
    x2 = x1_ref[...] + mod_ref[5:6, :] * jnp.concatenate(fq, axis=1)
    out = _rms(x2) * nf_ref[...]
    i = pl.program_id(0)

    @pl.when(i < npt)
    def _():
        op_ref[...] = out

    @pl.when(i >= npt)
    def _():
        os_ref[...] = out


def _final(seqs, x1, u2, yg, w, mod, norm_final, wsg, wsu, wsd):
    n = seqs.n
    tm = 256
    tok = lambda i: (i, 0)
    const = lambda i: (0, 0)
    npt, p_spec, s_spec = _two_stream_specs(seqs, tm)
    return pl.pallas_call(
        functools.partial(_final_kernel, npt),
        grid=(n // tm,),
        in_specs=[pl.BlockSpec((tm, D_MODEL), tok),
                  pl.BlockSpec((2, tm, PACK_W), lambda i: (0, i, 0)),
                  pl.BlockSpec((2, TOP_K, tm, PACK_W), lambda i: (0, 0, i, 0)),
                  pl.BlockSpec((tm, TOP_K), tok),
                  pl.BlockSpec((None, N_MOD, D_MODEL), lambda i: (seqs.info(i * tm)[0], 0, 0)),
                  pl.BlockSpec((1, D_MODEL), const),
                  pl.BlockSpec((D_MODEL, D_EXPERT), const),
                  pl.BlockSpec((D_MODEL, D_EXPERT), const),
                  pl.BlockSpec((D_EXPERT, D_MODEL), const)],
        out_specs=[p_spec, s_spec],
        out_shape=[jax.ShapeDtypeStruct((seqs.np_, D_MODEL), F32),
                   jax.ShapeDtypeStruct((n - seqs.np_, D_MODEL), F32)],
        compiler_params=_params(48, ("arbitrary",)),
        name="final",
    )(x1, u2, yg, w, mod, norm_final.reshape(1, D_MODEL), wsg, wsu, wsd)


def _layer(seqs, x_p, x_s, c, w_ada, b_ada, norm_mix, w_in, na_rpb, hg_lb, hg_norm, w_branch_a,
           w_branch_b, w_out, norm_ffn, w_router, b_router, w_exp_gate, w_exp_up, w_exp_down,
           w_sh_gate, w_sh_up, w_sh_down, norm_final):
    n = seqs.n
    c_rows = -(-seqs.nseq // 8) * 8
    c_pad = jnp.zeros((c_rows, D_MODEL), F32).at[:seqs.nseq].set(c)
    mod = _ada(c_pad, w_ada[0], b_ada[0])[:seqs.nseq].reshape(seqs.nseq, N_MOD, D_MODEL)
    lb = jnp.cumsum(jax.nn.softmax(hg_lb.astype(F32), axis=0), axis=0)[0]

    proj = _inproj(seqs, x_p, x_s, mod, norm_mix[0], w_in[0].astype(BF16))
    att = _na(seqs, proj, _na_bias_table(na_rpb[0]))
    o_f, o_b = _hgrn(seqs, proj, lb)
    x1, u2, logits = _merge(seqs, x_p, x_s, att, o_f, o_b, proj, mod, hg_norm[0], norm_ffn[0],
                            w_branch_a[0].astype(BF16), w_branch_b[0].astype(BF16),
                            w_out[0].astype(BF16), w_router[0].T)

    eidx, w, rank, counts = _route(logits, b_router[0])
    n_rows = n * TOP_K + N_EXPERTS * MOE_BLOCK
    pstarts, block_e, n_used = _block_tables(counts, n_rows // MOE_BLOCK)
    pos = _positions(eidx, rank, pstarts).reshape(-1)
    xs = _sc_dispatch(u2, pos, n_rows)
    ys = _experts(xs, block_e, n_used, w_exp_gate[0], w_exp_up[0], w_exp_down[0])
    yg = _sc_gather(ys, pos).reshape(2, TOP_K, n, PACK_W)
    return _final(seqs, x1, u2, yg, w.T, mod, norm_final, w_sh_gate[0].astype(BF16),
                  w_sh_up[0].astype(BF16), w_sh_down[0].astype(BF16))


def kernel(x_prompt, x_sample, c_prompt, c_sample, w_ada, b_ada, norm_mix, w_in, na_rpb, hg_lb, hg_norm, w_branch_a, w_branch_b, w_out, norm_ffn, w_router, b_router, w_exp_gate, w_exp_up, w_exp_down, w_sh_gate, w_sh_up, w_sh_down, norm_final):
    bp, tp, _ = x_prompt.shape
    bs, ts, _ = x_sample.shape
    seqs = _Seqs(bp, tp, bs, ts)
    c = jnp.concatenate([c_prompt, c_sample])
    y_p, y_s = _layer(seqs, x_prompt.reshape(bp * tp, D_MODEL), x_sample.reshape(bs * ts, D_MODEL),
                      c, w_ada, b_ada, norm_mix, w_in, na_rpb, hg_lb, hg_norm, w_branch_a,
                      w_branch_b, w_out, norm_ffn, w_router, b_router, w_exp_gate, w_exp_up,
                      w_exp_down, w_sh_gate, w_sh_up, w_sh_down, norm_final)
    return (y_p.reshape(bp, tp, D_MODEL), y_s.reshape(bs, ts, D_MODEL))
```

```python
import functools

import jax
import jax.numpy as jnp
import numpy as np
from jax import lax
from jax.experimental import pallas as pl
from jax.experimental.pallas import tpu as pltpu
from jax.experimental.pallas import tpu_sc as plsc

D_MODEL = 1024
GRID_W = 64
NA_HEADS = 8
NA_HEAD_DIM = 64
NA_WIDTH = NA_HEADS * NA_HEAD_DIM
NA_ROWS = 8
NA_COLS = 16
HG_HEADS = 4
HG_KEY_DIM = 128
HG_WIDTH = HG_HEADS * HG_KEY_DIM
HG_CHUNK = 64
N_EXPERTS = 256
TOP_K = 8
N_GROUPS = 8
TOPK_GROUPS = 4
D_EXPERT = 256
ROUTE_SCALE = 2.5
N_MOD = 6
RMS_EPS = 1e-6

MOE_BLOCK = 512
NA_GROUP = 4
NA_WIN = 3 * NA_GROUP
NA_TOK = NA_GROUP * GRID_W
LOG2E = 1.4426950408889634
NA_Q_SCALE = NA_HEAD_DIM ** -0.5 * LOG2E
HG_STEP = 256
MASK_VALUE = -1e30

F32 = jnp.float32
BF16 = jnp.bfloat16
HIGHEST = lax.Precision.HIGHEST
NT_DIMS = (((1,), (1,)), ((), ()))
TN_DIMS = (((0,), (0,)), ((), ()))

SLAB_Q, SLAB_K, SLAB_V, SLAB_HQ = (0, 0), (0, 1), (1, 0), (1, 1)
SLAB_FF, SLAB_FB, SLAB_HI, SLAB_HG = (2, 0), (2, 1), (3, 0), (3, 1)
SLAB_GA, SLAB_GB = 4, 5


def _params(vmem_mb, sem=None):
    kw = dict(vmem_limit_bytes=vmem_mb * 1024 * 1024)
    if sem is not None:
        kw["dimension_semantics"] = sem
    return pltpu.CompilerParams(**kw)


class _Seqs:
    def __init__(self, bp, tp, bs, ts):
        self.bp, self.tp, self.bs, self.ts = bp, tp, bs, ts
        self.np_ = bp * tp
        self.n = bp * tp + bs * ts
        self.nseq = bp + bs

    def info(self, t0):
        in_p = t0 < self.np_
        rel = jnp.maximum(t0 - self.np_, 0)
        sid = jnp.where(in_p, t0 // self.tp, self.bp + rel // self.ts)
        start = jnp.where(in_p, (t0 // self.tp) * self.tp, self.np_ + (rel // self.ts) * self.ts)
        length = jnp.where(in_p, self.tp, self.ts)
        return sid, start, length


def _ada_kernel(c_ref, w_ref, b_ref, o_ref):
    c = c_ref[...]
    a = c * jax.nn.sigmoid(c)
    o_ref[...] = jnp.dot(a, w_ref[...], precision=HIGHEST, preferred_element_type=F32) + b_ref[...]


def _ada(c_pad, w_ada, b_ada):
    rows = c_pad.shape[0]
    n_out = w_ada.shape[1]
    tn = 1024
    return pl.pallas_call(
        _ada_kernel,
        grid=(n_out // tn,),
        in_specs=[pl.BlockSpec((rows, D_MODEL), lambda j: (0, 0)),
                  pl.BlockSpec((D_MODEL, tn), lambda j: (0, j)),
                  pl.BlockSpec((1, tn), lambda j: (0, j))],
        out_specs=pl.BlockSpec((rows, tn), lambda j: (0, j)),
        out_shape=jax.ShapeDtypeStruct((rows, n_out), F32),
        compiler_params=_params(32),
        name="ada",
    )(c_pad, w_ada, b_ada.reshape(1, n_out))


def _rms(x):
    return x * lax.rsqrt(jnp.mean(x * x, axis=-1, keepdims=True) + RMS_EPS)


PACK_W = D_MODEL // 4


def _pack_rows(x):
    out = []
    for h in range(2):
        lo = x[:, (2 * h) * PACK_W:(2 * h + 1) * PACK_W].astype(BF16).astype(F32)
        hi = x[:, (2 * h + 1) * PACK_W:(2 * h + 2) * PACK_W].astype(BF16).astype(F32)
        out.append(lax.bitcast_convert_type(hi, jnp.uint32)
                   | (lax.bitcast_convert_type(lo, jnp.uint32) >> 16))
    return out


def _unpack_rows(p0, p1):
    quarters = []
    for p in (p0, p1):
        quarters.append(lax.bitcast_convert_type(p << 16, F32).astype(BF16))
        quarters.append(lax.bitcast_convert_type(p & jnp.uint32(0xFFFF0000), F32).astype(BF16))
    return quarters


def _dot_quarters(quarters, w_ref):
    acc = None
    for q, xq in enumerate(quarters):
        part = jnp.dot(xq, w_ref[q * PACK_W:(q + 1) * PACK_W, :], preferred_element_type=F32)
        acc = part if acc is None else acc + part
    return acc


def _two_stream_specs(seqs, tm, grid_rank=1):
    npt = seqs.np_ // tm
    nst = (seqs.n - seqs.np_) // tm
    if grid_rank == 1:
        p_map = lambda i: (jnp.minimum(i, npt - 1), 0)
        s_map = lambda i: (jnp.clip(i - npt, 0, nst - 1), 0)
    else:
        p_map = lambda i, j: (jnp.minimum(i, npt - 1), 0)
        s_map = lambda i, j: (jnp.clip(i - npt, 0, nst - 1), 0)
    return npt, pl.BlockSpec((tm, D_MODEL), p_map), pl.BlockSpec((tm, D_MODEL), s_map)


def _inproj_kernel(npt, xp_ref, xs_ref, mod_ref, g_ref, w_ref, cs_ref, o_ref, u_scr):
    @pl.when(pl.program_id(1) == 0)
    def _():
        x = jnp.where(pl.program_id(0) < npt, xp_ref[...], xs_ref[...])
        y = _rms(x) * g_ref[...]
        u = y * (1.0 + mod_ref[1:2, :]) + mod_ref[0:1, :]
        u_scr[...] = u.astype(BF16)

    acc = jnp.dot(u_scr[...], w_ref[...], preferred_element_type=F32)
    o_ref[...] = (acc * cs_ref[...]).astype(o_ref.dtype)


def _inproj(seqs, x_p, x_s, mod, norm_mix, w_in_bf):
    n = seqs.n
    tm = min(1024, seqs.tp, seqs.ts)
    tn = 1024
    n_slab = w_in_bf.shape[1] // tn
    npt, p_spec, s_spec = _two_stream_specs(seqs, tm, grid_rank=2)
    col_scale = jnp.ones((1, w_in_bf.shape[1]), F32).at[:, :NA_WIDTH].set(NA_Q_SCALE)
    return pl.pallas_call(
        functools.partial(_inproj_kernel, npt),
        grid=(n // tm, n_slab),
        in_specs=[p_spec, s_spec,
                  pl.BlockSpec((None, N_MOD, D_MODEL), lambda i, j: (seqs.info(i * tm)[0], 0, 0)),
                  pl.BlockSpec((1, D_MODEL), lambda i, j: (0, 0)),
                  pl.BlockSpec((D_MODEL, tn), lambda i, j: (0, j)),
                  pl.BlockSpec((1, tn), lambda i, j: (0, j))],
        out_specs=pl.BlockSpec((None, tm, tn), lambda i, j: (j, i, 0)),
        out_shape=jax.ShapeDtypeStruct((n_slab, n, tn), BF16),
        scratch_shapes=[pltpu.VMEM((tm, D_MODEL), BF16)],
        compiler_params=_params(40, ("arbitrary", "arbitrary")),
        name="inproj",
    )(x_p, x_s, mod, norm_mix.reshape(1, D_MODEL), w_in_bf, col_scale)


def _na_bias_table(rpb):
    col = np.arange(GRID_W)
    cs = np.clip(col - NA_COLS // 2, 0, GRID_W - NA_COLS)
    valid = (col[None, :] >= cs[:, None]) & (col[None, :] < cs[:, None] + NA_COLS)
    coff = col[None, :] - col[:, None] + NA_COLS - 1
    onehot = (coff[None] == np.arange(2 * NA_COLS - 1)[:, None, None]) & valid[None]
    toep = jnp.einsum("hrc,cqk->hrqk", rpb.astype(F32), jnp.asarray(onehot, F32),
                      precision=HIGHEST)
    toep = jnp.where(valid[None, None], toep * LOG2E, MASK_VALUE)
    masked =jnp.full((NA_HEADS, GRID_W, GRID_W), MASK_VALUE, F32)
    cases = (([0] * NA_GROUP, [NA_ROWS - 1 - i for i in range(NA_GROUP)]),
             (list(range(NA_GROUP)), [NA_ROWS // 2 - 1] * NA_GROUP),
             ([NA_GROUP] * NA_GROUP, [NA_ROWS // 2 - 1 - i for i in range(NA_GROUP)]))
    tabs = []
    for first_row, first_off in cases:
        q_rows = []
        for i in range(NA_GROUP):
            blocks = [toep[:, first_off[i] + w - first_row[i]]
                      if 0 <= w - first_row[i] < NA_ROWS else masked for w in range(NA_WIN)]
            q_rows.append(jnp.concatenate(blocks, axis=2))
        tabs.append(jnp.concatenate(q_rows, axis=1))
    return jnp.stack(tabs)


def _na_geometry(seqs, g):
    _, start, length = seqs.info(g * NA_TOK)
    row0 = start // GRID_W
    rows = length // GRID_W
    r0 = g * NA_GROUP - row0
    wb = jnp.clip(r0 - NA_ROWS // 2, 0, rows - NA_WIN)
    case = jnp.where(r0 == 0, 0, jnp.where(r0 == rows - NA_GROUP, 2, 1))
    return (row0 + wb) // NA_GROUP, case


def _na_kernel(q_ref, k0, k1, k2, v0, v1, v2, bias_ref, o_ref):
    k_refs = (k0, k1, k2)
    v_refs = (v0, v1, v2)
    outs = []
    for h in range(NA_HEADS):
        hs = slice(h * NA_HEAD_DIM, (h + 1) * NA_HEAD_DIM)
        q = q_ref[:, hs]
        s = [lax.dot_general(q, kr[:, hs], NT_DIMS, preferred_element_type=F32)
             + bias_ref[h, :, d * NA_TOK:(d + 1) * NA_TOK] for d, kr in enumerate(k_refs)]
        m = jnp.max(jnp.maximum(jnp.maximum(s[0], s[1]), s[2]), axis=-1, keepdims=True)
        p = [jnp.exp2(sd - m) for sd in s]
        l = jnp.sum((p[0] + p[1]) + p[2], axis=-1, keepdims=True)
        o = sum(jnp.dot(pd.astype(BF16), vr[:, hs], preferred_element_type=F32)
                for pd, vr in zip(p, v_refs))
        outs.append(o / l)
    o_ref[...] = jnp.concatenate(outs, axis=1).astype(o_ref.dtype)


def _na(seqs, proj, bias_tab):
    n = seqs.n

    def kv_spec(slab, d):
        return pl.BlockSpec((None, NA_TOK, NA_WIDTH),
                            lambda g: (slab[0], _na_geometry(seqs, g)[0] + d, slab[1]))

    return pl.pallas_call(
        _na_kernel,
        grid=(n // NA_TOK,),
        in_specs=[pl.BlockSpec((None, NA_TOK, NA_WIDTH), lambda g: (SLAB_Q[0], g, SLAB_Q[1]))]
        + [kv_spec(SLAB_K, d) for d in range(3)] + [kv_spec(SLAB_V, d) for d in range(3)]
        + [pl.BlockSpec((None,) + bias_tab.shape[1:],
                        lambda g: (_na_geometry(seqs, g)[1], 0, 0, 0))],
        out_specs=pl.BlockSpec((NA_TOK, NA_WIDTH), lambda g: (g, 0)),
        out_shape=jax.ShapeDtypeStruct((n, NA_WIDTH), BF16),
        compiler_params=_params(48, ("arbitrary",)),
        name="natten",
    )(proj, proj, proj, proj, proj, proj, proj, bias_tab)


def _hg_chunk(q, z, v, lb, tri, mask, mid, last, st_ref):
    sig = jax.nn.sigmoid(z)
    f = lb + (1.0 - lb) * sig
    lf = jnp.log(f)
    kin = (1.0 - lb) * (1.0 - sig)
    hi = lf.astype(BF16)
    r1 = lf - hi.astype(F32)
    md = r1.astype(BF16)
    lo = (r1 - md.astype(F32)).astype(BF16)
    g3 = jnp.dot(tri, jnp.concatenate([lo, md, hi], axis=1), preferred_element_type=F32)
    gcum = (g3[:, :HG_WIDTH] + g3[:, HG_WIDTH:2 * HG_WIDTH]) + g3[:, 2 * HG_WIDTH:]
    gm = gcum[mid:mid + 1, :]
    gl = gcum[last:last + 1, :]
    up = jnp.exp(gcum - gm)
    dn = jnp.exp(gm - gcum)
    qa = (q * up).astype(BF16)
    ka = (kin * dn).astype(BF16)
    qe = (q * (up * jnp.exp(gm))).astype(BF16)
    kd = (kin * (dn * jnp.exp(gl - gm))).astype(BF16)
    eg = jnp.exp(gl)
    vb = v.astype(BF16)
    outs = []
    for h in range(HG_HEADS):
        hs = slice(h * HG_KEY_DIM, (h + 1) * HG_KEY_DIM)
        a = lax.dot_general(qa[:, hs], ka[:, hs], NT_DIMS, preferred_element_type=F32)
        a = jnp.where(mask, a, 0.0)
        st = st_ref[h]
        o = jnp.dot(a.astype(BF16), vb[:, hs], preferred_element_type=F32)
        o = o + lax.dot_general(qe[:, hs], st.astype(BF16), NT_DIMS, preferred_element_type=F32)
        st_ref[h] = st * eg[:, hs] + lax.dot_general(vb[:, hs], kd[:, hs], TN_DIMS,
                                                    preferred_element_type=F32)
        outs.append(o)
    return jnp.concatenate(outs, axis=1)


def _hg_kernel(seqs, qf_ref, zf_ref, vf_ref, qb_ref, zb_ref, vb_ref, lb_ref, of_ref, ob_ref,
               stf, stb):
    i = pl.program_id(0)
    nsteps = pl.num_programs(0)
    tf = i * HG_STEP
    tb = (nsteps - 1 - i) * HG_STEP
    _, start_f, _ = seqs.info(tf)
    _, start_b, len_b = seqs.info(tb)

    @pl.when(tf == start_f)
    def _():
        stf[...] = jnp.zeros_like(stf)

    @pl.when(tb + HG_STEP == start_b + len_b)
    def _():
        stb[...] = jnp.zeros_like(stb)

    lb = lb_ref[...]
    row = lax.broadcasted_iota(jnp.int32, (HG_CHUNK, HG_CHUNK), 0)
    col = lax.broadcasted_iota(jnp.int32, (HG_CHUNK, HG_CHUNK), 1)
    lower = row >= col
    upper = col >= row
    tri_f = lower.astype(BF16)
    tri_b = upper.astype(BF16)
    nchunk = HG_STEP // HG_CHUNK
    for c in range(nchunk):
        cs = slice(c * HG_CHUNK, (c + 1) * HG_CHUNK)
        of_ref[cs, :] = _hg_chunk(qf_ref[cs, :].astype(F32), zf_ref[cs, :].astype(F32),
                                  vf_ref[cs, :].astype(F32), lb, tri_f, lower,
                                  HG_CHUNK // 2 - 1, HG_CHUNK - 1, stf)
        cb = nchunk - 1 - c
        bs = slice(cb * HG_CHUNK, (cb + 1) * HG_CHUNK)
        ob_ref[bs, :] = _hg_chunk(qb_ref[bs, :].astype(F32), zb_ref[bs, :].astype(F32),
                                  vb_ref[bs, :].astype(F32), lb, tri_b, upper,
                                  HG_CHUNK // 2, 0, stb)


def _hgrn(seqs, proj, lb):
    n = seqs.n
    nsteps = n // HG_STEP

    def spec(slab, rev):
        if rev:
            return pl.BlockSpec((None, HG_STEP, HG_WIDTH), lambda i: (slab[0], nsteps - 1 - i, slab[1]))
        return pl.BlockSpec((None, HG_STEP, HG_WIDTH), lambda i: (slab[0], i, slab[1]))

    return pl.pallas_call(
        functools.partial(_hg_kernel, seqs),
        grid=(nsteps,),
        in_specs=[spec(SLAB_HQ, False), spec(SLAB_FF, False), spec(SLAB_HI, False),
                  spec(SLAB_HQ, True), spec(SLAB_FB, True), spec(SLAB_HI, True),
                  pl.BlockSpec((1, HG_WIDTH), lambda i: (0, 0))],
        out_specs=[pl.BlockSpec((HG_STEP, HG_WIDTH), lambda i: (i, 0)),
                   pl.BlockSpec((HG_STEP, HG_WIDTH), lambda i: (nsteps - 1 - i, 0))],
        out_shape=[jax.ShapeDtypeStruct((n, HG_WIDTH), F32)] * 2,
        scratch_shapes=[pltpu.VMEM((HG_HEADS, HG_KEY_DIM, HG_KEY_DIM), F32)] * 2,
        compiler_params=_params(32, ("arbitrary",)),
        name="hgrn2",
    )(proj, proj, proj, proj, proj, proj, lb.reshape(1, HG_WIDTH))


def _merge_kernel(npt, xp_ref, xs_ref, att_ref, of_ref, ob_ref, hg_ref, ga_ref, gb_ref, mod_ref,
                  hgn_ref, nffn_ref, wa_ref, wb_ref, wo_ref, wr_ref, x1_ref, u2_ref, lg_ref):
    x = jnp.where(pl.program_id(0) < npt, xp_ref[...], xs_ref[...])
    o = of_ref[...] + ob_ref[...]
    parts = []
    for h in range(HG_HEADS):
        hs = slice(h * HG_KEY_DIM, (h + 1) * HG_KEY_DIM)
        parts.append(_rms(o[:, hs]))
    on = jnp.concatenate(parts, axis=1) * hgn_ref[...]
    gate = hg_ref[...].astype(F32)
    hb = (on * (gate * jax.nn.sigmoid(gate))).astype(BF16)
    ya = jnp.dot(att_ref[...], wa_ref[...], preferred_element_type=F32)
    yb = jnp.dot(hb, wb_ref[...], preferred_element_type=F32)
    merged = (jax.nn.sigmoid(ga_ref[...].astype(F32)) * ya
              + jax.nn.sigmoid(gb_ref[...].astype(F32)) * yb)
    x1 = x + mod_ref[2:3, :] * jnp.dot(merged.astype(BF16), wo_ref[...],
                                       preferred_element_type=F32)
    x1_ref[...] = x1
    u2 = _rms(x1) * nffn_ref[...] * (1.0 + mod_ref[4:5, :]) + mod_ref[3:4, :]
    u2_ref[0], u2_ref[1] = _pack_rows(u2)
    w_hi = wr_ref[0]
    u_hi = u2.astype(BF16)
    u_lo = (u2 - u_hi.astype(F32)).astype(BF16)
    lg_ref[...] = (lax.dot_general(w_hi, u_hi, NT_DIMS, preferred_element_type=F32)
                   + (lax.dot_general(w_hi, u_lo, NT_DIMS, preferred_element_type=F32)
                      + lax.dot_general(wr_ref[1], u_hi, NT_DIMS, preferred_element_type=F32)))


def _split_hi_lo(w):
    hi = w.astype(BF16)
    return jnp.stack([hi, (w - hi.astype(F32)).astype(BF16)])


def _merge(seqs, x_p, x_s, att, o_f, o_b, proj, mod, hg_norm, norm_ffn, wa, wb, wo, wr):
    n = seqs.n
    tm = 256
    tok = lambda i: (i, 0)
    const = lambda i: (0, 0)
    npt, p_spec, s_spec = _two_stream_specs(seqs, tm)
    return pl.pallas_call(
        functools.partial(_merge_kernel, npt),
        grid=(n // tm,),
        in_specs=[p_spec, s_spec,
                  pl.BlockSpec((tm, NA_WIDTH), tok),
                  pl.BlockSpec((tm, HG_WIDTH), tok),
                  pl.BlockSpec((tm, HG_WIDTH), tok),
                  pl.BlockSpec((None, tm, HG_WIDTH), lambda i: (SLAB_HG[0], i, SLAB_HG[1])),
                  pl.BlockSpec((None, tm, D_MODEL), lambda i: (SLAB_GA, i, 0)),
                  pl.BlockSpec((None, tm, D_MODEL), lambda i: (SLAB_GB, i, 0)),
                  pl.BlockSpec((None, N_MOD, D_MODEL), lambda i: (seqs.info(i * tm)[0], 0, 0)),
                  pl.BlockSpec((1, HG_WIDTH), const),
                  pl.BlockSpec((1, D_MODEL), const),
                  pl.BlockSpec((NA_WIDTH, D_MODEL), const),
                  pl.BlockSpec((HG_WIDTH, D_MODEL), const),
                  pl.BlockSpec((D_MODEL, D_MODEL), const),
                  pl.BlockSpec((2, N_EXPERTS, D_MODEL), lambda i: (0, 0, 0))],
        out_specs=[pl.BlockSpec((tm, D_MODEL), tok),
                   pl.BlockSpec((2, tm, PACK_W), lambda i: (0, i, 0)),
                   pl.BlockSpec((N_EXPERTS, tm), lambda i: (0, i))],
        out_shape=[jax.ShapeDtypeStruct((n, D_MODEL), F32),
                   jax.ShapeDtypeStruct((2, n, PACK_W), jnp.uint32),
                   jax.ShapeDtypeStruct((N_EXPERTS, n), F32)],
        compiler_params=_params(48, ("arbitrary",)),
        name="merge",
    )(x_p, x_s, att, o_f, o_b, proj, proj, proj, mod, hg_norm.reshape(1, HG_WIDTH),
      norm_ffn.reshape(1, D_MODEL), wa, wb, wo, wr)


def _expert_kernel(be_ref, nused_ref, xs_ref, wg_ref, wu_ref, wd_ref, ys_ref, wg_s, wu_s, wd_s):
    i = pl.program_id(0)

    @pl.when(i < nused_ref[0])
    def _():
        @pl.when((i == 0) | (be_ref[i] != be_ref[jnp.maximum(i - 1, 0)]))
        def _():
            wg_s[...] = wg_ref[...].astype(BF16)
            wu_s[...] = wu_ref[...].astype(BF16)
            wd_s[...] = wd_ref[...].astype(BF16)

        xq = _unpack_rows(xs_ref[0], xs_ref[1])
        a = _dot_quarters(xq, wg_s)
        b = _dot_quarters(xq, wu_s)
        h = (a * jax.nn.sigmoid(a) * b).astype(BF16)
        ys_ref[0], ys_ref[1] = _pack_rows(jnp.dot(h, wd_s[...], preferred_element_type=F32))


def _experts(xs, block_e, n_used, wg, wu, wd):
    n_rows = xs.shape[1]
    n_blocks = n_rows // MOE_BLOCK
    grid_spec = pltpu.PrefetchScalarGridSpec(
        num_scalar_prefetch=2,
        grid=(n_blocks,),
        in_specs=[pl.BlockSpec((2, MOE_BLOCK, PACK_W), lambda i, be, nu: (0, i, 0)),
                  pl.BlockSpec((None, D_MODEL, D_EXPERT), lambda i, be, nu: (be[i], 0, 0)),
                  pl.BlockSpec((None, D_MODEL, D_EXPERT), lambda i, be, nu: (be[i], 0, 0)),
                  pl.BlockSpec((None, D_EXPERT, D_MODEL), lambda i, be, nu: (be[i], 0, 0))],
        out_specs=pl.BlockSpec((2, MOE_BLOCK, PACK_W), lambda i, be, nu: (0, i, 0)),
        scratch_shapes=[pltpu.VMEM((D_MODEL, D_EXPERT), BF16),
                        pltpu.VMEM((D_MODEL, D_EXPERT), BF16),
                        pltpu.VMEM((D_EXPERT, D_MODEL), BF16)],
    )
    return pl.pallas_call(
        _expert_kernel,
        grid_spec=grid_spec,
        out_shape=jax.ShapeDtypeStruct((2, n_rows, PACK_W), jnp.uint32),
        compiler_params=_params(32, ("arbitrary",)),
        name="experts",
    )(block_e, n_used, xs, wg, wu, wd)


ROUTE_TILE = 128
ROUTE_STEP = 512
NEG_INF = float("-inf")


def _first_max(x, idx, big):
    m = jnp.max(x, axis=0, keepdims=True)
    first = jnp.min(jnp.where(x == m, idx, big), axis=0, keepdims=True)
    return m, first


def _route_tile(lg, bias, tri, base):
    per_group = N_EXPERTS // N_GROUPS
    scores = jax.nn.sigmoid(lg)
    sel = scores + bias
    lrow = lax.broadcasted_iota(jnp.int32, (per_group, ROUTE_TILE), 0)
    gs = []
    for g in range(N_GROUPS):
        x = sel[g * per_group:(g + 1) * per_group]
        m1, first = _first_max(x, lrow, per_group)
        m2 = jnp.max(jnp.where(lrow == first, NEG_INF, x), axis=0, keepdims=True)
        gs.append(m1 + m2)
    cur = jnp.concatenate(gs, axis=0)
    grow = lax.broadcasted_iota(jnp.int32, (N_GROUPS, ROUTE_TILE), 0)
    chosen = jnp.zeros((N_GROUPS, ROUTE_TILE), jnp.int32)
    for _ in range(TOPK_GROUPS):
        _, first = _first_max(cur, grow, N_GROUPS)
        hit = grow == first
        chosen = jnp.where(hit, 1, chosen)
        cur = jnp.where(hit, NEG_INF, cur)
    cur = jnp.concatenate(
        [jnp.where(chosen[g:g + 1] > 0, sel[g * per_group:(g + 1) * per_group], NEG_INF)
         for g in range(N_GROUPS)], axis=0)
    row = lax.broadcasted_iota(jnp.int32, (N_EXPERTS, ROUTE_TILE), 0)
    member = jnp.zeros((N_EXPERTS, ROUTE_TILE), F32)
    es, ws = [], []
    for _ in range(TOP_K):
        _, first = _first_max(cur, row, N_EXPERTS)
        hit = row == first
        es.append(first)
        ws.append(jnp.sum(jnp.where(hit, scores, 0.0), axis=0, keepdims=True))
        cur = jnp.where(hit, NEG_INF, cur)
        member = jnp.where(hit, 1.0, member)
    e = jnp.concatenate(es, axis=0)
    w = jnp.concatenate(ws, axis=0)
    w = w / jnp.sum(w, axis=0, keepdims=True) * ROUTE_SCALE
    before = jnp.dot(member.astype(BF16), tri, preferred_element_type=F32) + base
    rank = jnp.concatenate(
        [jnp.sum(jnp.where(row == es[k], before, 0.0), axis=0, keepdims=True) for k in range(TOP_K)],
        axis=0)
    return e, w, rank.astype(jnp.int32), member


def _route_kernel(lg_ref, b_ref, e_ref, w_ref, r_ref, cnt_ref, base):
    @pl.when(pl.program_id(0) == 0)
    def _():
        base[...] = jnp.zeros_like(base)

    r_i = lax.broadcasted_iota(jnp.int32, (ROUTE_TILE, ROUTE_TILE), 0)
    c_i = lax.broadcasted_iota(jnp.int32, (ROUTE_TILE, ROUTE_TILE), 1)
    tri = (r_i < c_i).astype(BF16)
    bias = b_ref[...]
    for j in range(ROUTE_STEP // ROUTE_TILE):
        ls = slice(j * ROUTE_TILE, (j + 1) * ROUTE_TILE)
        e, w, rank, member = _route_tile(lg_ref[:, ls], bias, tri, base[...])
        e_ref[:, ls] = e
        w_ref[:, ls] = w
        r_ref[:, ls] = rank
        base[...] = base[...] + jnp.sum(member, axis=1, keepdims=True)
    cnt_ref[...] = base[...]


def _route(logits_t, b_router):
    n = logits_t.shape[1]
    tok = lambda i: (0, i)
    return pl.pallas_call(
        _route_kernel,
        grid=(n // ROUTE_STEP,),
        in_specs=[pl.BlockSpec((N_EXPERTS, ROUTE_STEP), tok),
                  pl.BlockSpec((N_EXPERTS, 1), lambda i: (0, 0))],
        out_specs=[pl.BlockSpec((TOP_K, ROUTE_STEP), tok),
                   pl.BlockSpec((TOP_K, ROUTE_STEP), tok),
                   pl.BlockSpec((TOP_K, ROUTE_STEP), tok),
                   pl.BlockSpec((N_EXPERTS, 1), lambda i: (0, 0))],
        out_shape=[jax.ShapeDtypeStruct((TOP_K, n), jnp.int32),
                   jax.ShapeDtypeStruct((TOP_K, n), F32),
                   jax.ShapeDtypeStruct((TOP_K, n), jnp.int32),
                   jax.ShapeDtypeStruct((N_EXPERTS, 1), F32)],
        scratch_shapes=[pltpu.VMEM((N_EXPERTS, 1), F32)],
        compiler_params=_params(32, ("arbitrary",)),
        name="route",
    )(logits_t, b_router.reshape(N_EXPERTS, 1).astype(F32))


def _pos_kernel(e_ref, r_ref, ps_ref, pos_ref):
    row = lax.broadcasted_iota(jnp.int32, (N_EXPERTS, ROUTE_TILE), 0)
    pstart = ps_ref[...]
    for j in range(ROUTE_STEP // ROUTE_TILE):
        ls = slice(j * ROUTE_TILE, (j + 1) * ROUTE_TILE)
        e = e_ref[:, ls]
        off = jnp.concatenate(
            [jnp.sum(jnp.where(row == e[k:k + 1], pstart, 0.0), axis=0, keepdims=True)
             for k in range(TOP_K)], axis=0)
        pos_ref[:, ls] = off.astype(jnp.int32) + r_ref[:, ls]


def _positions(eidx, rank, pstarts):
    n = eidx.shape[1]
    tok = lambda i: (0, i)
    return pl.pallas_call(
        _pos_kernel,
        grid=(n // ROUTE_STEP,),
        in_specs=[pl.BlockSpec((TOP_K, ROUTE_STEP), tok),
                  pl.BlockSpec((TOP_K, ROUTE_STEP), tok),
                  pl.BlockSpec((N_EXPERTS, 1), lambda i: (0, 0))],
        out_specs=pl.BlockSpec((TOP_K, ROUTE_STEP), tok),
        out_shape=jax.ShapeDtypeStruct((TOP_K, n), jnp.int32),
        compiler_params=_params(32, ("arbitrary",)),
        name="positions",
    )(eidx, rank, pstarts.reshape(N_EXPERTS, 1).astype(F32))


def _block_tables(counts, n_blocks):
    counts = counts.reshape(N_EXPERTS).astype(jnp.int32)
    padded = (counts + MOE_BLOCK - 1) // MOE_BLOCK * MOE_BLOCK
    pend = jnp.cumsum(padded)
    pstarts = pend - padded
    first_row = jnp.arange(n_blocks, dtype=jnp.int32) * MOE_BLOCK
    block_e = jnp.sum((pend[None, :] <= first_row[:, None]).astype(jnp.int32), axis=1)
    block_e = jnp.minimum(block_e, N_EXPERTS - 1).astype(jnp.int32)
    n_used = (pend[-1] // MOE_BLOCK).astype(jnp.int32).reshape(1)
    return pstarts, block_e, n_used


SC_WINDOW = 128


def _sc_mesh():
    return plsc.VectorSubcoreMesh(core_axis_name="core", subcore_axis_name="subcore")


def _both_halves(pos, n_rows):
    return jnp.concatenate([pos, pos + n_rows]).reshape(1, -1)


def _sc_dispatch(rows, pos, n_rows):
    _, n, w = rows.shape
    tiles = n // SC_WINDOW
    steps_per_half = pos.shape[0] // SC_WINDOW

    @functools.partial(pl.kernel, out_type=jax.ShapeDtypeStruct((2 * n_rows, w), rows.dtype),
                       mesh=_sc_mesh(), scratch_types=[])
    def scatter_kernel(x_hbm, i_hbm, o_hbm):
        def body(x_vmem, i_vmem):
            pltpu.sync_copy(x_vmem, o_hbm.at[i_vmem.at[0]])

        pltpu.emit_pipeline(
            body,
            grid=(2 * steps_per_half,),
            in_specs=[pl.BlockSpec((SC_WINDOW, w),
                                   lambda i: ((i // steps_per_half) * tiles + i % tiles, 0)),
                      pl.BlockSpec((1, SC_WINDOW), lambda i: (0, i))],
            out_specs=[],
            core_axis_name=("core", "subcore"),
            dimension_semantics=(pltpu.PARALLEL,),
        )(x_hbm, i_hbm)

    out = scatter_kernel(rows.reshape(2 * n, w), _both_halves(pos, n_rows))
    return out.reshape(2, n_rows, w)


def _sc_gather(table, pos):
    _, n_rows, w = table.shape
    m = pos.shape[0]

    @functools.partial(pl.kernel, out_type=jax.ShapeDtypeStruct((2 * m, w), table.dtype),
                       mesh=_sc_mesh(), scratch_types=[])
    def gather_kernel(t_hbm, i_hbm, o_hbm):
        def body(i_vmem, o_vmem):
            pltpu.sync_copy(t_hbm.at[i_vmem.at[0]], o_vmem)

        pltpu.emit_pipeline(
            body,
            grid=(2 * m // SC_WINDOW,),
            in_specs=[pl.BlockSpec((1, SC_WINDOW), lambda i: (0, i))],
            out_specs=[pl.BlockSpec((SC_WINDOW, w), lambda i: (i, 0))],
            core_axis_name=("core", "subcore"),
            dimension_semantics=(pltpu.PARALLEL,),
        )(i_hbm, o_hbm)

    out = gather_kernel(table.reshape(2 * n_rows, w), _both_halves(pos, n_rows))
    return out.reshape(2, m, w)


def _final_kernel(npt, x1_ref, u2_ref, yg_ref, w_ref, mod_ref, nf_ref, wsg_ref, wsu_ref, wsd_ref,
                  op_ref, os_ref):
    uq = _unpack_rows(u2_ref[0], u2_ref[1])
    a = _dot_quarters(uq, wsg_ref)
    b = _dot_quarters(uq, wsu_ref)
    f = jnp.dot((a * jax.nn.sigmoid(a) * b).astype(BF16), wsd_ref[...], preferred_element_type=F32)
    w = w_ref[...]
    fq = [f[:, q * PACK_W:(q + 1) * PACK_W] for q in range(4)]
    for k in range(TOP_K):
        yq = _unpack_rows(yg_ref[0, k], yg_ref[1, k])
        fq = [fq[q] + w[:, k:k + 1] * yq[q].astype(F32) for q in range(4)]
    x2 = x1_ref[...] + mod_ref[5:6, :] * jnp.concatenate(fq, axis=1)
    out = _rms(x2) * nf_ref[...]
    i = pl.program_id(0)

    @pl.when(i < npt)
    def _():
        op_ref[...] = out

    @pl.when(i >= npt)
    def _():
        os_ref[...] = out


def _final(seqs, x1, u2, yg, w, mod, norm_final, wsg, wsu, wsd):
    n = seqs.n
    tm = 256
    tok = lambda i: (i, 0)
    const = lambda i: (0, 0)
    npt, p_spec, s_spec = _two_stream_specs(seqs, tm)
    return pl.pallas_call(
        functools.partial(_final_kernel, npt),
        grid=(n // tm,),
        in_specs=[pl.BlockSpec((tm, D_MODEL), tok),
                  pl.BlockSpec((2, tm, PACK_W), lambda i: (0, i, 0)),
                  pl.BlockSpec((2, TOP_K, tm, PACK_W), lambda i: (0, 0, i, 0)),
                  pl.BlockSpec((tm, TOP_K), tok),
                  pl.BlockSpec((None, N_MOD, D_MODEL), lambda i: (seqs.info(i * tm)[0], 0, 0)),
                  pl.BlockSpec((1, D_MODEL), const),
                  pl.BlockSpec((D_MODEL, D_EXPERT), const),
                  pl.BlockSpec((D_MODEL, D_EXPERT), const),
                  pl.BlockSpec((D_EXPERT, D_MODEL), const)],
        out_specs=[p_spec, s_spec],
        out_shape=[jax.ShapeDtypeStruct((seqs.np_, D_MODEL), F32),
                   jax.ShapeDtypeStruct((n - seqs.np_, D_MODEL), F32)],
        compiler_params=_params(48, ("arbitrary",)),
        name="final",
    )(x1, u2, yg, w, mod, norm_final.reshape(1, D_MODEL), wsg, wsu, wsd)


def _layer(seqs, x_p, x_s, c, w_ada, b_ada, norm_mix, w_in, na_rpb, hg_lb, hg_norm, w_branch_a,
           w_branch_b, w_out, norm_ffn, w_router, b_router, w_exp_gate, w_exp_up, w_exp_down,
           w_sh_gate, w_sh_up, w_sh_down, norm_final):
    n = seqs.n
    c_rows = -(-seqs.nseq // 8) * 8
    c_pad = jnp.zeros((c_rows, D_MODEL), F32).at[:seqs.nseq].set(c)
    mod = _ada(c_pad, w_ada[0], b_ada[0])[:seqs.nseq].reshape(seqs.nseq, N_MOD, D_MODEL)
    lb = jnp.cumsum(jax.nn.softmax(hg_lb.astype(F32), axis=0), axis=0)[0]

    proj = _inproj(seqs, x_p, x_s, mod, norm_mix[0], w_in[0].astype(BF16))
    att = _na(seqs, proj, _na_bias_table(na_rpb[0]))
    o_f, o_b = _hgrn(seqs, proj, lb)
    x1, u2, logits = _merge(seqs, x_p, x_s, att, o_f, o_b, proj, mod, hg_norm[0], norm_ffn[0],
                            w_branch_a[0].astype(BF16), w_branch_b[0].astype(BF16),
                            w_out[0].astype(BF16), _split_hi_lo(w_router[0].T))

    eidx, w, rank, counts = _route(logits, b_router[0])
    n_rows = n * TOP_K + N_EXPERTS * MOE_BLOCK
    pstarts, block_e, n_used = _block_tables(counts, n_rows // MOE_BLOCK)
    pos = _positions(eidx, rank, pstarts).reshape(-1)
    xs = _sc_dispatch(u2, pos, n_rows)
    ys = _experts(xs, block_e, n_used, w_exp_gate[0], w_exp_up[0], w_exp_down[0])
    yg = _sc_gather(ys, pos).reshape(2, TOP_K, n, PACK_W)
    return _final(seqs, x1, u2, yg, w.T, mod, norm_final, w_sh_gate[0].astype(BF16),
                  w_sh_up[0].astype(BF16), w_sh_down[0].astype(BF16))


def kernel(x_prompt, x_sample, c_prompt, c_sample, w_ada, b_ada, norm_mix, w_in, na_rpb, hg_lb, hg_norm, w_branch_a, w_branch_b, w_out, norm_ffn, w_router, b_router, w_exp_gate, w_exp_up, w_exp_down, w_sh_gate, w_sh_up, w_sh_down, norm_final):
    bp, tp, _ = x_prompt.shape
    bs, ts, _ = x_sample.shape
    seqs = _Seqs(bp, tp, bs, ts)
    c = jnp.concatenate([c_prompt, c_sample])
    y_p, y_s = _layer(seqs, x_prompt.reshape(bp * tp, D_MODEL), x_sample.reshape(bs * ts, D_MODEL),
                      c, w_ada, b_ada, norm_mix, w_in, na_rpb, hg_lb, hg_norm, w_branch_a,
                      w_branch_b, w_out, norm_ffn, w_router, b_router, w_exp_gate, w_exp_up,
                      w_exp_down, w_sh_gate, w_sh_up, w_sh_down, norm_final)
    return (y_p.reshape(bp, tp, D_MODEL), y_s.reshape(bs, ts, D_MODEL))
```

```python
import functools

import jax
import jax.numpy as jnp
import numpy as np
from jax import lax
from jax.experimental import pallas as pl
from jax.experimental.pallas import tpu as pltpu
from jax.experimental.pallas import tpu_sc as plsc

D_MODEL = 1024
GRID_W = 64
NA_HEADS = 8
NA_HEAD_DIM = 64
NA_WIDTH = NA_HEADS * NA_HEAD_DIM
NA_ROWS = 8
NA_COLS = 16
HG_HEADS = 4
HG_KEY_DIM = 128
HG_WIDTH = HG_HEADS * HG_KEY_DIM
HG_CHUNK = 128
HG_EXP_LIMIT = 80.0
N_EXPERTS = 256
TOP_K = 8
N_GROUPS = 8
TOPK_GROUPS = 4
D_EXPERT = 256
ROUTE_SCALE = 2.5
N_MOD = 6
RMS_EPS = 1e-6

MOE_BLOCK = 512
NA_GROUP = 4
NA_WIN = 3 * NA_GROUP
NA_TOK = NA_GROUP * GRID_W
LOG2E = 1.4426950408889634
NA_Q_SCALE = NA_HEAD_DIM ** -0.5 * LOG2E
HG_STEP = 256
MASK_VALUE = -1e30

F32 = jnp.float32
BF16 = jnp.bfloat16
HIGHEST = lax.Precision.HIGHEST
NT_DIMS = (((1,), (1,)), ((), ()))
TN_DIMS = (((0,), (0,)), ((), ()))

SLAB_Q, SLAB_K, SLAB_V, SLAB_HQ = (0, 0), (0, 1), (1, 0), (1, 1)
SLAB_FF, SLAB_FB, SLAB_HI, SLAB_HG = (2, 0), (2, 1), (3, 0), (3, 1)
SLAB_GA, SLAB_GB = 4, 5


def _params(vmem_mb, sem=None):
    kw = dict(vmem_limit_bytes=vmem_mb * 1024 * 1024)
    if sem is not None:
        kw["dimension_semantics"] = sem
    return pltpu.CompilerParams(**kw)


class _Seqs:
    def __init__(self, bp, tp, bs, ts):
        self.bp, self.tp, self.bs, self.ts = bp, tp, bs, ts
        self.np_ = bp * tp
        self.n = bp * tp + bs * ts
        self.nseq = bp + bs

    def info(self, t0):
        in_p = t0 < self.np_
        rel = jnp.maximum(t0 - self.np_, 0)
        sid = jnp.where(in_p, t0 // self.tp, self.bp + rel // self.ts)
        start = jnp.where(in_p, (t0 // self.tp) * self.tp, self.np_ + (rel // self.ts) * self.ts)
        length = jnp.where(in_p, self.tp, self.ts)
        return sid, start, length


def _ada_kernel(c_ref, w_ref, b_ref, o_ref):
    c = c_ref[...]
    a = c * jax.nn.sigmoid(c)
    o_ref[...] = jnp.dot(a, w_ref[...], precision=HIGHEST, preferred_element_type=F32) + b_ref[...]


def _ada(c_pad, w_ada, b_ada):
    rows = c_pad.shape[0]
    n_out = w_ada.shape[1]
    tn = 1024
    return pl.pallas_call(
        _ada_kernel,
        grid=(n_out // tn,),
        in_specs=[pl.BlockSpec((rows, D_MODEL), lambda j: (0, 0)),
                  pl.BlockSpec((D_MODEL, tn), lambda j: (0, j)),
                  pl.BlockSpec((1, tn), lambda j: (0, j))],
        out_specs=pl.BlockSpec((rows, tn), lambda j: (0, j)),
        out_shape=jax.ShapeDtypeStruct((rows, n_out), F32),
        compiler_params=_params(32),
        name="ada",
    )(c_pad, w_ada, b_ada.reshape(1, n_out))


def _rms(x):
    return x * lax.rsqrt(jnp.mean(x * x, axis=-1, keepdims=True) + RMS_EPS)


PACK_W = D_MODEL // 4


def _pack_rows(x):
    out = []
    for h in range(2):
        lo = x[:, (2 * h) * PACK_W:(2 * h + 1) * PACK_W].astype(BF16).astype(F32)
        hi = x[:, (2 * h + 1) * PACK_W:(2 * h + 2) * PACK_W].astype(BF16).astype(F32)
        out.append(lax.bitcast_convert_type(hi, jnp.uint32)
                   | (lax.bitcast_convert_type(lo, jnp.uint32) >> 16))
    return out


def _unpack_rows(p0, p1, dtype=BF16):
    quarters = []
    for p in (p0, p1):
        quarters.append(lax.bitcast_convert_type(p << 16, F32).astype(dtype))
        quarters.append(lax.bitcast_convert_type(p & jnp.uint32(0xFFFF0000), F32).astype(dtype))
    return quarters


def _dot_quarters(quarters, w_ref):
    acc = None
    for q, xq in enumerate(quarters):
        part = jnp.dot(xq, w_ref[q * PACK_W:(q + 1) * PACK_W, :], preferred_element_type=F32)
        acc = part if acc is None else acc + part
    return acc


def _two_stream_specs(seqs, tm, grid_rank=1):
    npt = seqs.np_ // tm
    nst = (seqs.n - seqs.np_) // tm
    if grid_rank == 1:
        p_map = lambda i: (jnp.minimum(i, npt - 1), 0)
        s_map = lambda i: (jnp.clip(i - npt, 0, nst - 1), 0)
    else:
        p_map = lambda i, j: (jnp.minimum(i, npt - 1), 0)
        s_map = lambda i, j: (jnp.clip(i - npt, 0, nst - 1), 0)
    return npt, pl.BlockSpec((tm, D_MODEL), p_map), pl.BlockSpec((tm, D_MODEL), s_map)


def _inproj_kernel(npt, xp_ref, xs_ref, mod_ref, g_ref, w_ref, cs_ref, o_ref, u_scr):
    @pl.when(pl.program_id(1) == 0)
    def _():
        x = jnp.where(pl.program_id(0) < npt, xp_ref[...], xs_ref[...])
        y = _rms(x) * g_ref[...]
        u = y * (1.0 + mod_ref[1:2, :]) + mod_ref[0:1, :]
        u_scr[...] = u.astype(BF16)

    acc = jnp.dot(u_scr[...], w_ref[...], preferred_element_type=F32)
    o_ref[...] = (acc * cs_ref[...]).astype(o_ref.dtype)


def _inproj(seqs, x_p, x_s, mod, norm_mix, w_in_bf):
    n = seqs.n
    tm = min(1024, seqs.tp, seqs.ts)
    tn = 1024
    n_slab = w_in_bf.shape[1] // tn
    npt, p_spec, s_spec = _two_stream_specs(seqs, tm, grid_rank=2)
    col_scale = jnp.ones((1, w_in_bf.shape[1]), F32).at[:, :NA_WIDTH].set(NA_Q_SCALE)
    return pl.pallas_call(
        functools.partial(_inproj_kernel, npt),
        grid=(n // tm, n_slab),
        in_specs=[p_spec, s_spec,
                  pl.BlockSpec((None, N_MOD, D_MODEL), lambda i, j: (seqs.info(i * tm)[0], 0, 0)),
                  pl.BlockSpec((1, D_MODEL), lambda i, j: (0, 0)),
                  pl.BlockSpec((D_MODEL, tn), lambda i, j: (0, j)),
                  pl.BlockSpec((1, tn), lambda i, j: (0, j))],
        out_specs=pl.BlockSpec((None, tm, tn), lambda i, j: (j, i, 0)),
        out_shape=jax.ShapeDtypeStruct((n_slab, n, tn), BF16),
        scratch_shapes=[pltpu.VMEM((tm, D_MODEL), BF16)],
        compiler_params=_params(40, ("arbitrary", "arbitrary")),
        name="inproj",
    )(x_p, x_s, mod, norm_mix.reshape(1, D_MODEL), w_in_bf, col_scale)


def _na_bias_table(rpb):
    col = np.arange(GRID_W)
    cs = np.clip(col - NA_COLS // 2, 0, GRID_W - NA_COLS)
    valid = (col[None, :] >= cs[:, None]) & (col[None, :] < cs[:, None] + NA_COLS)
    coff = col[None, :] - col[:, None] + NA_COLS - 1
    onehot = (coff[None] == np.arange(2 * NA_COLS - 1)[:, None, None]) & valid[None]
    toep = jnp.einsum("hrc,cqk->hrqk", rpb.astype(F32), jnp.asarray(onehot, F32),
                      precision=HIGHEST)
    toep = jnp.where(valid[None, None], toep * LOG2E, MASK_VALUE)
    masked =jnp.full((NA_HEADS, GRID_W, GRID_W), MASK_VALUE, F32)
    cases = (([0] * NA_GROUP, [NA_ROWS - 1 - i for i in range(NA_GROUP)]),
             (list(range(NA_GROUP)), [NA_ROWS // 2 - 1] * NA_GROUP),
             ([NA_GROUP] * NA_GROUP, [NA_ROWS // 2 - 1 - i for i in range(NA_GROUP)]))
    tabs = []
    for first_row, first_off in cases:
        q_rows = []
        for i in range(NA_GROUP):
            blocks = [toep[:, first_off[i] + w - first_row[i]]
                      if 0 <= w - first_row[i] < NA_ROWS else masked for w in range(NA_WIN)]
            q_rows.append(jnp.concatenate(blocks, axis=2))
        tabs.append(jnp.concatenate(q_rows, axis=1))
    return jnp.stack(tabs)


def _na_geometry(seqs, g):
    _, start, length = seqs.info(g * NA_TOK)
    row0 = start // GRID_W
    rows = length // GRID_W
    r0 = g * NA_GROUP - row0
    wb = jnp.clip(r0 - NA_ROWS // 2, 0, rows - NA_WIN)
    case = jnp.where(r0 == 0, 0, jnp.where(r0 == rows - NA_GROUP, 2, 1))
    return (row0 + wb) // NA_GROUP, case


def _na_kernel(q_ref, k0, k1, k2, v0, v1, v2, bias_ref, o_ref):
    k_refs = (k0, k1, k2)
    v_refs = (v0, v1, v2)
    outs = []
    for h in range(NA_HEADS):
        hs = slice(h * NA_HEAD_DIM, (h + 1) * NA_HEAD_DIM)
        q = q_ref[:, hs]
        s = [lax.dot_general(q, kr[:, hs], NT_DIMS, preferred_element_type=F32)
             + bias_ref[h, :, d * NA_TOK:(d + 1) * NA_TOK] for d, kr in enumerate(k_refs)]
        m = jnp.max(jnp.maximum(jnp.maximum(s[0], s[1]), s[2]), axis=-1, keepdims=True)
        p = [jnp.exp2(sd - m) for sd in s]
        l = jnp.sum((p[0] + p[1]) + p[2], axis=-1, keepdims=True)
        o = sum(jnp.dot(pd.astype(BF16), vr[:, hs], preferred_element_type=F32)
                for pd, vr in zip(p, v_refs))
        outs.append(o / l)
    o_ref[...] = jnp.concatenate(outs, axis=1).astype(o_ref.dtype)


def _na(seqs, proj, bias_tab):
    n = seqs.n

    def kv_spec(slab, d):
        return pl.BlockSpec((None, NA_TOK, NA_WIDTH),
                            lambda g: (slab[0], _na_geometry(seqs, g)[0] + d, slab[1]))

    return pl.pallas_call(
        _na_kernel,
        grid=(n // NA_TOK,),
        in_specs=[pl.BlockSpec((None, NA_TOK, NA_WIDTH), lambda g: (SLAB_Q[0], g, SLAB_Q[1]))]
        + [kv_spec(SLAB_K, d) for d in range(3)] + [kv_spec(SLAB_V, d) for d in range(3)]
        + [pl.BlockSpec((None,) + bias_tab.shape[1:],
                        lambda g: (_na_geometry(seqs, g)[1], 0, 0, 0))],
        out_specs=pl.BlockSpec((NA_TOK, NA_WIDTH), lambda g: (g, 0)),
        out_shape=jax.ShapeDtypeStruct((n, NA_WIDTH), BF16),
        compiler_params=_params(48, ("arbitrary",)),
        name="natten",
    )(proj, proj, proj, proj, proj, proj, proj, bias_tab)


def _hg_chunk(q, z, v, lb, tri, mask, mid, last, st_ref):
    sig = jax.nn.sigmoid(z)
    f = lb + (1.0 - lb) * sig
    lf = jnp.log(f)
    kin = (1.0 - lb) * (1.0 - sig)
    hi = lf.astype(BF16)
    lo = (lf - hi.astype(F32)).astype(BF16)
    g2 = jnp.dot(tri, jnp.concatenate([lo, hi], axis=1), preferred_element_type=F32)
    gcum = g2[:, :HG_WIDTH] + g2[:, HG_WIDTH:]
    gm = gcum[mid:mid + 1, :]
    gl = gcum[last:last + 1, :]
    up = jnp.exp(gcum - gm)
    dn = jnp.exp(gm - gcum)
    qa = (q * up).astype(BF16)
    ka = (kin * dn).astype(BF16)
    qe = (q * (up * jnp.exp(gm))).astype(BF16)
    kd = (kin * (dn * jnp.exp(gl - gm))).astype(BF16)
    eg = jnp.exp(gl)
    vb = v.astype(BF16)
    outs = []
    for h in range(HG_HEADS):
        hs = slice(h * HG_KEY_DIM, (h + 1) * HG_KEY_DIM)
        a = lax.dot_general(qa[:, hs], ka[:, hs], NT_DIMS, preferred_element_type=F32)
        a = jnp.where(mask, a, 0.0)
        st = st_ref[h]
        o = jnp.dot(a.astype(BF16), vb[:, hs], preferred_element_type=F32)
        o = o + lax.dot_general(qe[:, hs], st.astype(BF16), NT_DIMS, preferred_element_type=F32)
        st_ref[h] = st * eg[:, hs] + lax.dot_general(vb[:, hs], kd[:, hs], TN_DIMS,
                                                    preferred_element_type=F32)
        outs.append(o)
    return jnp.concatenate(outs, axis=1)


def _hg_exact(q_ref, z_ref, v_ref, lb, reverse, st_ref, o_ref, qs, fs, ks, vs):
    sig = jax.nn.sigmoid(z_ref[...].astype(F32))
    qs[...] = q_ref[...].astype(F32)
    fs[...] = lb + (1.0 - lb) * sig
    ks[...] = (1.0 - lb) * (1.0 - sig)
    vs[...] = v_ref[...].astype(F32)
    eye = (lax.broadcasted_iota(jnp.int32, (HG_KEY_DIM, HG_KEY_DIM), 0)
           == lax.broadcasted_iota(jnp.int32, (HG_KEY_DIM, HG_KEY_DIM), 1)).astype(F32)

    def body(i, carry):
        t = HG_STEP - 1 - i if reverse else i
        q_t, f_t, k_t, v_t = (r[pl.ds(t, 1), :] for r in (qs, fs, ks, vs))
        outs = []
        for h in range(HG_HEADS):
            hs = slice(h * HG_KEY_DIM, (h + 1) * HG_KEY_DIM)
            v_col = jnp.sum(eye * v_t[:, hs], axis=1, keepdims=True)
            st = st_ref[h] * f_t[:, hs] + v_col * k_t[:, hs]
            st_ref[h] = st
            o_col = jnp.sum(st * q_t[:, hs], axis=1, keepdims=True)
            outs.append(jnp.sum(eye * o_col, axis=0, keepdims=True))
        o_ref[pl.ds(t, 1), :] = jnp.concatenate(outs, axis=1)
        return carry

    lax.fori_loop(0, HG_STEP, body, 0)


def _hg_kernel(seqs, safe_ref, qf_ref, zf_ref, vf_ref, qb_ref, zb_ref, vb_ref, lb_ref, of_ref,
               ob_ref, stf, stb, qs, fs, ks, vs):
    i = pl.program_id(0)
    nsteps = pl.num_programs(0)
    tf = i * HG_STEP
    tb = (nsteps - 1 - i) * HG_STEP
    _, start_f, _ = seqs.info(tf)
    _, start_b, len_b = seqs.info(tb)

    @pl.when(tf == start_f)
    def _():
        stf[...] = jnp.zeros_like(stf)

    @pl.when(tb + HG_STEP == start_b + len_b)
    def _():
        stb[...] = jnp.zeros_like(stb)

    lb = lb_ref[...]

    @pl.when(safe_ref[0] > 0)
    def _():
        row = lax.broadcasted_iota(jnp.int32, (HG_CHUNK, HG_CHUNK), 0)
        col = lax.broadcasted_iota(jnp.int32, (HG_CHUNK, HG_CHUNK), 1)
        lower = row >= col
        upper = col >= row
        tri_f = lower.astype(BF16)
        tri_b = upper.astype(BF16)
        nchunk = HG_STEP // HG_CHUNK
        for c in range(nchunk):
            cs = slice(c * HG_CHUNK, (c + 1) * HG_CHUNK)
            of_ref[cs, :] = _hg_chunk(qf_ref[cs, :].astype(F32), zf_ref[cs, :].astype(F32),
                                      vf_ref[cs, :].astype(F32), lb, tri_f, lower,
                                      HG_CHUNK // 2 - 1, HG_CHUNK - 1, stf)
            cb = nchunk - 1 - c
            bs = slice(cb * HG_CHUNK, (cb + 1) * HG_CHUNK)
            ob_ref[bs, :] = _hg_chunk(qb_ref[bs, :].astype(F32), zb_ref[bs, :].astype(F32),
                                      vb_ref[bs, :].astype(F32), lb, tri_b, upper,
                                      HG_CHUNK // 2, 0, stb)

    @pl.when(safe_ref[0] == 0)
    def _():
        _hg_exact(qf_ref, zf_ref, vf_ref, lb, False, stf, of_ref, qs, fs, ks, vs)
        _hg_exact(qb_ref, zb_ref, vb_ref, lb, True, stb, ob_ref, qs, fs, ks, vs)


def _hgrn(seqs, proj, lb):
    n = seqs.n
    nsteps = n // HG_STEP

    def spec(slab, rev):
        if rev:
            return pl.BlockSpec((None, HG_STEP, HG_WIDTH), lambda i: (slab[0], nsteps - 1 - i, slab[1]))
        return pl.BlockSpec((None, HG_STEP, HG_WIDTH), lambda i: (slab[0], i, slab[1]))

    safe = (jnp.max(-jnp.log(lb)) * (HG_CHUNK // 2) < HG_EXP_LIMIT).astype(jnp.int32).reshape(1)
    step_scratch = pltpu.VMEM((HG_STEP, HG_WIDTH), F32)
    return pl.pallas_call(
        functools.partial(_hg_kernel, seqs),
        grid=(nsteps,),
        in_specs=[pl.BlockSpec(memory_space=pltpu.SMEM),
                  spec(SLAB_HQ, False), spec(SLAB_FF, False), spec(SLAB_HI, False),
                  spec(SLAB_HQ, True), spec(SLAB_FB, True), spec(SLAB_HI, True),
                  pl.BlockSpec((1, HG_WIDTH), lambda i: (0, 0))],
        out_specs=[pl.BlockSpec((HG_STEP, HG_WIDTH), lambda i: (i, 0)),
                   pl.BlockSpec((HG_STEP, HG_WIDTH), lambda i: (nsteps - 1 - i, 0))],
        out_shape=[jax.ShapeDtypeStruct((n, HG_WIDTH), F32)] * 2,
        scratch_shapes=[pltpu.VMEM((HG_HEADS, HG_KEY_DIM, HG_KEY_DIM), F32)] * 2
        + [step_scratch] * 4,
        compiler_params=_params(32, ("arbitrary",)),
        name="hgrn2",
    )(safe, proj, proj, proj, proj, proj, proj, lb.reshape(1, HG_WIDTH))


def _merge_kernel(npt, xp_ref, xs_ref, att_ref, of_ref, ob_ref, hg_ref, ga_ref, gb_ref, mod_ref,
                  hgn_ref, nffn_ref, wa_ref, wb_ref, wo_ref, wr_ref, x1_ref, u2_ref, lg_ref):
    x = jnp.where(pl.program_id(0) < npt, xp_ref[...], xs_ref[...])
    o = of_ref[...] + ob_ref[...]
    parts = []
    for h in range(HG_HEADS):
        hs = slice(h * HG_KEY_DIM, (h + 1) * HG_KEY_DIM)
        parts.append(_rms(o[:, hs]))
    on = jnp.concatenate(parts, axis=1) * hgn_ref[...]
    gate = hg_ref[...].astype(F32)
    hb = (on * (gate * jax.nn.sigmoid(gate))).astype(BF16)
    ya = jnp.dot(att_ref[...], wa_ref[...], preferred_element_type=F32)
    yb = jnp.dot(hb, wb_ref[...], preferred_element_type=F32)
    merged = (jax.nn.sigmoid(ga_ref[...].astype(F32)) * ya
              + jax.nn.sigmoid(gb_ref[...].astype(F32)) * yb)
    x1 = x + mod_ref[2:3, :] * jnp.dot(merged.astype(BF16), wo_ref[...],
                                       preferred_element_type=F32)
    x1_ref[...] = x1
    u2 = _rms(x1) * nffn_ref[...] * (1.0 + mod_ref[4:5, :]) + mod_ref[3:4, :]
    u2_ref[0], u2_ref[1] = _pack_rows(u2)
    w_hi = wr_ref[0]
    u_hi = u2.astype(BF16)
    u_lo = (u2 - u_hi.astype(F32)).astype(BF16)
    lg_ref[...] = (lax.dot_general(w_hi, u_hi, NT_DIMS, preferred_element_type=F32)
                   + (lax.dot_general(w_hi, u_lo, NT_DIMS, preferred_element_type=F32)
                      + lax.dot_general(wr_ref[1], u_hi, NT_DIMS, preferred_element_type=F32)))


def _split_hi_lo(w):
    hi = w.astype(BF16)
    return jnp.stack([hi, (w - hi.astype(F32)).astype(BF16)])


def _merge(seqs, x_p, x_s, att, o_f, o_b, proj, mod, hg_norm, norm_ffn, wa, wb, wo, wr):
    n = seqs.n
    tm = 256
    tok = lambda i: (i, 0)
    const = lambda i: (0, 0)
    npt, p_spec, s_spec = _two_stream_specs(seqs, tm)
    return pl.pallas_call(
        functools.partial(_merge_kernel, npt),
        grid=(n // tm,),
        in_specs=[p_spec, s_spec,
                  pl.BlockSpec((tm, NA_WIDTH), tok),
                  pl.BlockSpec((tm, HG_WIDTH), tok),
                  pl.BlockSpec((tm, HG_WIDTH), tok),
                  pl.BlockSpec((None, tm, HG_WIDTH), lambda i: (SLAB_HG[0], i, SLAB_HG[1])),
                  pl.BlockSpec((None, tm, D_MODEL), lambda i: (SLAB_GA, i, 0)),
                  pl.BlockSpec((None, tm, D_MODEL), lambda i: (SLAB_GB, i, 0)),
                  pl.BlockSpec((None, N_MOD, D_MODEL), lambda i: (seqs.info(i * tm)[0], 0, 0)),
                  pl.BlockSpec((1, HG_WIDTH), const),
                  pl.BlockSpec((1, D_MODEL), const),
                  pl.BlockSpec((NA_WIDTH, D_MODEL), const),
                  pl.BlockSpec((HG_WIDTH, D_MODEL), const),
                  pl.BlockSpec((D_MODEL, D_MODEL), const),
                  pl.BlockSpec((2, N_EXPERTS, D_MODEL), lambda i: (0, 0, 0))],
        out_specs=[pl.BlockSpec((tm, D_MODEL), tok),
                   pl.BlockSpec((2, tm, PACK_W), lambda i: (0, i, 0)),
                   pl.BlockSpec((N_EXPERTS, tm), lambda i: (0, i))],
        out_shape=[jax.ShapeDtypeStruct((n, D_MODEL), F32),
                   jax.ShapeDtypeStruct((2, n, PACK_W), jnp.uint32),
                   jax.ShapeDtypeStruct((N_EXPERTS, n), F32)],
        compiler_params=_params(48, ("arbitrary",)),
        name="merge",
    )(x_p, x_s, att, o_f, o_b, proj, proj, proj, mod, hg_norm.reshape(1, HG_WIDTH),
      norm_ffn.reshape(1, D_MODEL), wa, wb, wo, wr)


def _expert_kernel(be_ref, nused_ref, xs_ref, wg_ref, wu_ref, wd_ref, ys_ref, wg_s, wu_s, wd_s):
    i = pl.program_id(0)

    @pl.when(i < nused_ref[0])
    def _():
        @pl.when((i == 0) | (be_ref[i] != be_ref[jnp.maximum(i - 1, 0)]))
        def _():
            wg_s[...] = wg_ref[...].astype(BF16)
            wu_s[...] = wu_ref[...].astype(BF16)
            wd_s[...] = wd_ref[...].astype(BF16)

        xq = _unpack_rows(xs_ref[0], xs_ref[1])
        a = _dot_quarters(xq, wg_s)
        b = _dot_quarters(xq, wu_s)
        h = (a * jax.nn.sigmoid(a) * b).astype(BF16)
        ys_ref[0], ys_ref[1] = _pack_rows(jnp.dot(h, wd_s[...], preferred_element_type=F32))


def _experts(xs, block_e, n_used, wg, wu, wd):
    n_rows = xs.shape[1]
    n_blocks = n_rows // MOE_BLOCK
    grid_spec = pltpu.PrefetchScalarGridSpec(
        num_scalar_prefetch=2,
        grid=(n_blocks,),
        in_specs=[pl.BlockSpec((2, MOE_BLOCK, PACK_W), lambda i, be, nu: (0, jnp.minimum(i, nu[0] - 1), 0)),
                  pl.BlockSpec((None, D_MODEL, D_EXPERT), lambda i, be, nu: (be[jnp.minimum(i, nu[0] - 1)], 0, 0)),
                  pl.BlockSpec((None, D_MODEL, D_EXPERT), lambda i, be, nu: (be[jnp.minimum(i, nu[0] - 1)], 0, 0)),
                  pl.BlockSpec((None, D_EXPERT, D_MODEL), lambda i, be, nu: (be[jnp.minimum(i, nu[0] - 1)], 0, 0))],
        out_specs=pl.BlockSpec((2, MOE_BLOCK, PACK_W), lambda i, be, nu: (0, jnp.minimum(i, nu[0] - 1), 0)),
        scratch_shapes=[pltpu.VMEM((D_MODEL, D_EXPERT), BF16),
                        pltpu.VMEM((D_MODEL, D_EXPERT), BF16),
                        pltpu.VMEM((D_EXPERT, D_MODEL), BF16)],
    )
    return pl.pallas_call(
        _expert_kernel,
        grid_spec=grid_spec,
        out_shape=jax.ShapeDtypeStruct((2, n_rows, PACK_W), jnp.uint32),
        compiler_params=_params(32, ("arbitrary",)),
        name="experts",
    )(block_e, n_used, xs, wg, wu, wd)


ROUTE_TILE = 128
ROUTE_STEP = 512
NEG_INF = float("-inf")


def _first_max(x, idx, big):
    m = jnp.max(x, axis=0, keepdims=True)
    first = jnp.min(jnp.where(x == m, idx, big), axis=0, keepdims=True)
    return m, first


def _route_tile(lg, bias, tri, base):
    per_group = N_EXPERTS // N_GROUPS
    scores = jax.nn.sigmoid(lg)
    sel = scores + bias
    lrow = lax.broadcasted_iota(jnp.int32, (per_group, ROUTE_TILE), 0)
    gs = []
    for g in range(N_GROUPS):
        x = sel[g * per_group:(g + 1) * per_group]
        m1, first = _first_max(x, lrow, per_group)
        m2 = jnp.max(jnp.where(lrow == first, NEG_INF, x), axis=0, keepdims=True)
        gs.append(m1 + m2)
    cur = jnp.concatenate(gs, axis=0)
    grow = lax.broadcasted_iota(jnp.int32, (N_GROUPS, ROUTE_TILE), 0)
    chosen = jnp.zeros((N_GROUPS, ROUTE_TILE), jnp.int32)
    for _ in range(TOPK_GROUPS):
        _, first = _first_max(cur, grow, N_GROUPS)
        hit = grow == first
        chosen = jnp.where(hit, 1, chosen)
        cur = jnp.where(hit, NEG_INF, cur)
    cur = jnp.concatenate(
        [jnp.where(chosen[g:g + 1] > 0, sel[g * per_group:(g + 1) * per_group], NEG_INF)
         for g in range(N_GROUPS)], axis=0)
    row = lax.broadcasted_iota(jnp.int32, (N_EXPERTS, ROUTE_TILE), 0)
    member = jnp.zeros((N_EXPERTS, ROUTE_TILE), F32)
    es, ws = [], []
    for _ in range(TOP_K):
        _, first = _first_max(cur, row, N_EXPERTS)
        hit = row == first
        es.append(first)
        ws.append(jnp.sum(jnp.where(hit, scores, 0.0), axis=0, keepdims=True))
        cur = jnp.where(hit, NEG_INF, cur)
        member = jnp.where(hit, 1.0, member)
    e = jnp.concatenate(es, axis=0)
    w = jnp.concatenate(ws, axis=0)
    w = w / jnp.sum(w, axis=0, keepdims=True) * ROUTE_SCALE
    before = jnp.dot(member.astype(BF16), tri, preferred_element_type=F32) + base
    rank = jnp.concatenate(
        [jnp.sum(jnp.where(row == es[k], before, 0.0), axis=0, keepdims=True) for k in range(TOP_K)],
        axis=0)
    return e, w, rank.astype(jnp.int32), member


def _route_kernel(lg_ref, b_ref, e_ref, w_ref, r_ref, cnt_ref, base):
    @pl.when(pl.program_id(0) == 0)
    def _():
        base[...] = jnp.zeros_like(base)

    r_i = lax.broadcasted_iota(jnp.int32, (ROUTE_TILE, ROUTE_TILE), 0)
    c_i = lax.broadcasted_iota(jnp.int32, (ROUTE_TILE, ROUTE_TILE), 1)
    tri = (r_i < c_i).astype(BF16)
    bias = b_ref[...]
    for j in range(ROUTE_STEP // ROUTE_TILE):
        ls = slice(j * ROUTE_TILE, (j + 1) * ROUTE_TILE)
        e, w, rank, member = _route_tile(lg_ref[:, ls], bias, tri, base[...])
        e_ref[:, ls] = e
        w_ref[:, ls] = w
        r_ref[:, ls] = rank
        base[...] = base[...] + jnp.sum(member, axis=1, keepdims=True)
    cnt_ref[...] = base[...]


def _route(logits_t, b_router):
    n = logits_t.shape[1]
    tok = lambda i: (0, i)
    return pl.pallas_call(
        _route_kernel,
        grid=(n // ROUTE_STEP,),
        in_specs=[pl.BlockSpec((N_EXPERTS, ROUTE_STEP), tok),
                  pl.BlockSpec((N_EXPERTS, 1), lambda i: (0, 0))],
        out_specs=[pl.BlockSpec((TOP_K, ROUTE_STEP), tok),
                   pl.BlockSpec((TOP_K, ROUTE_STEP), tok),
                   pl.BlockSpec((TOP_K, ROUTE_STEP), tok),
                   pl.BlockSpec((N_EXPERTS, 1), lambda i: (0, 0))],
        out_shape=[jax.ShapeDtypeStruct((TOP_K, n), jnp.int32),
                   jax.ShapeDtypeStruct((TOP_K, n), F32),
                   jax.ShapeDtypeStruct((TOP_K, n), jnp.int32),
                   jax.ShapeDtypeStruct((N_EXPERTS, 1), F32)],
        scratch_shapes=[pltpu.VMEM((N_EXPERTS, 1), F32)],
        compiler_params=_params(32, ("arbitrary",)),
        name="route",
    )(logits_t, b_router.reshape(N_EXPERTS, 1).astype(F32))


def _pos_kernel(e_ref, r_ref, ps_ref, pos_ref):
    row = lax.broadcasted_iota(jnp.int32, (N_EXPERTS, ROUTE_TILE), 0)
    pstart = ps_ref[...]
    for j in range(ROUTE_STEP // ROUTE_TILE):
        ls = slice(j * ROUTE_TILE, (j + 1) * ROUTE_TILE)
        e = e_ref[:, ls]
        off = jnp.concatenate(
            [jnp.sum(jnp.where(row == e[k:k + 1], pstart, 0.0), axis=0, keepdims=True)
             for k in range(TOP_K)], axis=0)
        pos_ref[:, ls] = off.astype(jnp.int32) + r_ref[:, ls]


def _positions(eidx, rank, pstarts):
    n = eidx.shape[1]
    tok = lambda i: (0, i)
    return pl.pallas_call(
        _pos_kernel,
        grid=(n // ROUTE_STEP,),
        in_specs=[pl.BlockSpec((TOP_K, ROUTE_STEP), tok),
                  pl.BlockSpec((TOP_K, ROUTE_STEP), tok),
                  pl.BlockSpec((N_EXPERTS, 1), lambda i: (0, 0))],
        out_specs=pl.BlockSpec((TOP_K, ROUTE_STEP), tok),
        out_shape=jax.ShapeDtypeStruct((TOP_K, n), jnp.int32),
        compiler_params=_params(32, ("arbitrary",)),
        name="positions",
    )(eidx, rank, pstarts.reshape(N_EXPERTS, 1).astype(F32))


def _block_tables(counts, n_blocks):
    counts = counts.reshape(N_EXPERTS).astype(jnp.int32)
    padded = (counts + MOE_BLOCK - 1) // MOE_BLOCK * MOE_BLOCK
    pend = jnp.cumsum(padded)
    pstarts = pend - padded
    first_row = jnp.arange(n_blocks, dtype=jnp.int32) * MOE_BLOCK
    block_e = jnp.sum((pend[None, :] <= first_row[:, None]).astype(jnp.int32), axis=1)
    block_e = jnp.minimum(block_e, N_EXPERTS - 1).astype(jnp.int32)
    n_used = (pend[-1] // MOE_BLOCK).astype(jnp.int32).reshape(1)
    return pstarts, block_e, n_used


SC_WINDOW = 128


def _sc_mesh():
    return plsc.VectorSubcoreMesh(core_axis_name="core", subcore_axis_name="subcore")


def _both_halves(pos, n_rows):
    return jnp.concatenate([pos, pos + n_rows]).reshape(1, -1)


def _sc_dispatch(rows, pos, n_rows):
    _, n, w = rows.shape
    tiles = n // SC_WINDOW
    steps_per_half = pos.shape[0] // SC_WINDOW

    @functools.partial(pl.kernel, out_type=jax.ShapeDtypeStruct((2 * n_rows, w), rows.dtype),
                       mesh=_sc_mesh(), scratch_types=[])
    def scatter_kernel(x_hbm, i_hbm, o_hbm):
        def body(x_vmem, i_vmem):
            pltpu.sync_copy(x_vmem, o_hbm.at[i_vmem.at[0]])

        pltpu.emit_pipeline(
            body,
            grid=(2 * steps_per_half,),
            in_specs=[pl.BlockSpec((SC_WINDOW, w),
                                   lambda i: ((i // steps_per_half) * tiles + i % tiles, 0)),
                      pl.BlockSpec((1, SC_WINDOW), lambda i: (0, i))],
            out_specs=[],
            core_axis_name=("core", "subcore"),
            dimension_semantics=(pltpu.PARALLEL,),
        )(x_hbm, i_hbm)

    out = scatter_kernel(rows.reshape(2 * n, w), _both_halves(pos, n_rows))
    return out.reshape(2, n_rows, w)


def _sc_gather(table, pos):
    _, n_rows, w = table.shape
    m = pos.shape[0]

    @functools.partial(pl.kernel, out_type=jax.ShapeDtypeStruct((2 * m, w), table.dtype),
                       mesh=_sc_mesh(), scratch_types=[])
    def gather_kernel(t_hbm, i_hbm, o_hbm):
        def body(i_vmem, o_vmem):
            pltpu.sync_copy(t_hbm.at[i_vmem.at[0]], o_vmem)

        pltpu.emit_pipeline(
            body,
            grid=(2 * m // SC_WINDOW,),
            in_specs=[pl.BlockSpec((1, SC_WINDOW), lambda i: (0, i))],
            out_specs=[pl.BlockSpec((SC_WINDOW, w), lambda i: (i, 0))],
            core_axis_name=("core", "subcore"),
            dimension_semantics=(pltpu.PARALLEL,),
        )(i_hbm, o_hbm)

    out = gather_kernel(table.reshape(2 * n_rows, w), _both_halves(pos, n_rows))
    return out.reshape(2, m, w)


def _final_kernel(npt, x1_ref, u2_ref, yg_ref, w_ref, mod_ref, nf_ref, wsg_ref, wsu_ref, wsd_ref,
                  op_ref, os_ref):
    uq = _unpack_rows(u2_ref[0], u2_ref[1])
    a = _dot_quarters(uq, wsg_ref)
    b = _dot_quarters(uq, wsu_ref)
    f = jnp.dot((a * jax.nn.sigmoid(a) * b).astype(BF16), wsd_ref[...], preferred_element_type=F32)
    w = w_ref[...]
    fq = [f[:, q * PACK_W:(q + 1) * PACK_W] for q in range(4)]
    for k in range(TOP_K):
        yq = _unpack_rows(yg_ref[0, k], yg_ref[1, k], F32)
        fq = [fq[q] + w[:, k:k + 1] * yq[q] for q in range(4)]
    x2 = x1_ref[...] + mod_ref[5:6, :] * jnp.concatenate(fq, axis=1)
    out = _rms(x2) * nf_ref[...]
    i = pl.program_id(0)

    @pl.when(i < npt)
    def _():
        op_ref[...] = out

    @pl.when(i >= npt)
    def _():
        os_ref[...] = out


def _final(seqs, x1, u2, yg, w, mod, norm_final, wsg, wsu, wsd):
    n = seqs.n
    tm = 256
    tok = lambda i: (i, 0)
    const = lambda i: (0, 0)
    npt, p_spec, s_spec = _two_stream_specs(seqs, tm)
    return pl.pallas_call(
        functools.partial(_final_kernel, npt),
        grid=(n // tm,),
        in_specs=[pl.BlockSpec((tm, D_MODEL), tok),
                  pl.BlockSpec((2, tm, PACK_W), lambda i: (0, i, 0)),
                  pl.BlockSpec((2, TOP_K, tm, PACK_W), lambda i: (0, 0, i, 0)),
                  pl.BlockSpec((tm, TOP_K), tok),
                  pl.BlockSpec((None, N_MOD, D_MODEL), lambda i: (seqs.info(i * tm)[0], 0, 0)),
                  pl.BlockSpec((1, D_MODEL), const),
                  pl.BlockSpec((D_MODEL, D_EXPERT), const),
                  pl.BlockSpec((D_MODEL, D_EXPERT), const),
                  pl.BlockSpec((D_EXPERT, D_MODEL), const)],
        out_specs=[p_spec, s_spec],
        out_shape=[jax.ShapeDtypeStruct((seqs.np_, D_MODEL), F32),
                   jax.ShapeDtypeStruct((n - seqs.np_, D_MODEL), F32)],
        compiler_params=_params(48, ("arbitrary",)),
        name="final",
    )(x1, u2, yg, w, mod, norm_final.reshape(1, D_MODEL), wsg, wsu, wsd)


def _layer(seqs, x_p, x_s, c, w_ada, b_ada, norm_mix, w_in, na_rpb, hg_lb, hg_norm, w_branch_a,
           w_branch_b, w_out, norm_ffn, w_router, b_router, w_exp_gate, w_exp_up, w_exp_down,
           w_sh_gate, w_sh_up, w_sh_down, norm_final):
    n = seqs.n
    c_rows = -(-seqs.nseq // 8) * 8
    c_pad = jnp.zeros((c_rows, D_MODEL), F32).at[:seqs.nseq].set(c)
    mod = _ada(c_pad, w_ada[0], b_ada[0])[:seqs.nseq].reshape(seqs.nseq, N_MOD, D_MODEL)
    lb = jnp.cumsum(jax.nn.softmax(hg_lb.astype(F32), axis=0), axis=0)[0]

    proj = _inproj(seqs, x_p, x_s, mod, norm_mix[0], w_in[0].astype(BF16))
    att = _na(seqs, proj, _na_bias_table(na_rpb[0]))
    o_f, o_b = _hgrn(seqs, proj, lb)
    x1, u2, logits = _merge(seqs, x_p, x_s, att, o_f, o_b, proj, mod, hg_norm[0], norm_ffn[0],
                            w_branch_a[0].astype(BF16), w_branch_b[0].astype(BF16),
                            w_out[0].astype(BF16), _split_hi_lo(w_router[0].T))

    eidx, w, rank, counts = _route(logits, b_router[0])
    n_rows = n * TOP_K + N_EXPERTS * MOE_BLOCK
    pstarts, block_e, n_used = _block_tables(counts, n_rows // MOE_BLOCK)
    pos = _positions(eidx, rank, pstarts).reshape(-1)
    xs = _sc_dispatch(u2, pos, n_rows)
    ys = _experts(xs, block_e, n_used, w_exp_gate[0], w_exp_up[0], w_exp_down[0])
    yg = _sc_gather(ys, pos).reshape(2, TOP_K, n, PACK_W)
    return _final(seqs, x1, u2, yg, w.T, mod, norm_final, w_sh_gate[0].astype(BF16),
                  w_sh_up[0].astype(BF16), w_sh_down[0].astype(BF16))


def kernel(x_prompt, x_sample, c_prompt, c_sample, w_ada, b_ada, norm_mix, w_in, na_rpb, hg_lb, hg_norm, w_branch_a, w_branch_b, w_out, norm_ffn, w_router, b_router, w_exp_gate, w_exp_up, w_exp_down, w_sh_gate, w_sh_up, w_sh_down, norm_final):
    bp, tp, _ = x_prompt.shape
    bs, ts, _ = x_sample.shape
    seqs = _Seqs(bp, tp, bs, ts)
    c = jnp.concatenate([c_prompt, c_sample])
    y_p, y_s = _layer(seqs, x_prompt.reshape(bp * tp, D_MODEL), x_sample.reshape(bs * ts, D_MODEL),
                      c, w_ada, b_ada, norm_mix, w_in, na_rpb, hg_lb, hg_norm, w_branch_a,
                      w_branch_b, w_out, norm_ffn, w_router, b_router, w_exp_gate, w_exp_up,
                      w_exp_down, w_sh_gate, w_sh_up, w_sh_down, norm_final)
    return (y_p.reshape(bp, tp, D_MODEL), y_s.reshape(bs, ts, D_MODEL))
```

```python
import functools

import jax
import jax.numpy as jnp
import numpy as np
from jax import lax
from jax.experimental import pallas as pl
from jax.experimental.pallas import tpu as pltpu
from jax.experimental.pallas import tpu_sc as plsc

D_MODEL = 1024
GRID_W = 64
NA_HEADS = 8
NA_HEAD_DIM = 64
NA_WIDTH = NA_HEADS * NA_HEAD_DIM
NA_ROWS = 8
NA_COLS = 16
HG_HEADS = 4
HG_KEY_DIM = 128
HG_WIDTH = HG_HEADS * HG_KEY_DIM
HG_CHUNK = 128
HG_EXP_LIMIT = 80.0
N_EXPERTS = 256
TOP_K = 8
N_GROUPS = 8
TOPK_GROUPS = 4
D_EXPERT = 256
ROUTE_SCALE = 2.5
N_MOD = 6
RMS_EPS = 1e-6

MOE_BLOCK = 512
NA_GROUP = 4
NA_WIN = 3 * NA_GROUP
NA_TOK = NA_GROUP * GRID_W
LOG2E = 1.4426950408889634
NA_Q_SCALE = NA_HEAD_DIM ** -0.5 * LOG2E
HG_STEP = 256
MASK_VALUE = -1e30

F32 = jnp.float32
BF16 = jnp.bfloat16
HIGHEST = lax.Precision.HIGHEST
NT_DIMS = (((1,), (1,)), ((), ()))
TN_DIMS = (((0,), (0,)), ((), ()))

SLAB_Q, SLAB_K, SLAB_V, SLAB_HQ = (0, 0), (0, 1), (1, 0), (1, 1)
SLAB_FF, SLAB_FB, SLAB_HI, SLAB_HG = (2, 0), (2, 1), (3, 0), (3, 1)
SLAB_GA, SLAB_GB = 4, 5


def _params(vmem_mb, sem=None):
    kw = dict(vmem_limit_bytes=vmem_mb * 1024 * 1024)
    if sem is not None:
        kw["dimension_semantics"] = sem
    return pltpu.CompilerParams(**kw)


class _Seqs:
    def __init__(self, bp, tp, bs, ts):
        self.bp, self.tp, self.bs, self.ts = bp, tp, bs, ts
        self.np_ = bp * tp
        self.n = bp * tp + bs * ts
        self.nseq = bp + bs

    def info(self, t0):
        in_p = t0 < self.np_
        rel = jnp.maximum(t0 - self.np_, 0)
        sid = jnp.where(in_p, t0 // self.tp, self.bp + rel // self.ts)
        start = jnp.where(in_p, (t0 // self.tp) * self.tp, self.np_ + (rel // self.ts) * self.ts)
        length = jnp.where(in_p, self.tp, self.ts)
        return sid, start, length


def _ada_kernel(c_ref, w_ref, b_ref, o_ref):
    c = c_ref[...]
    a = c * jax.nn.sigmoid(c)
    o_ref[...] = jnp.dot(a, w_ref[...], precision=HIGHEST, preferred_element_type=F32) + b_ref[...]


def _ada(c_pad, w_ada, b_ada):
    rows = c_pad.shape[0]
    n_out = w_ada.shape[1]
    tn = 1024
    return pl.pallas_call(
        _ada_kernel,
        grid=(n_out // tn,),
        in_specs=[pl.BlockSpec((rows, D_MODEL), lambda j: (0, 0)),
                  pl.BlockSpec((D_MODEL, tn), lambda j: (0, j)),
                  pl.BlockSpec((1, tn), lambda j: (0, j))],
        out_specs=pl.BlockSpec((rows, tn), lambda j: (0, j)),
        out_shape=jax.ShapeDtypeStruct((rows, n_out), F32),
        compiler_params=_params(32),
        name="ada",
    )(c_pad, w_ada, b_ada.reshape(1, n_out))


def _rms(x):
    return x * lax.rsqrt(jnp.mean(x * x, axis=-1, keepdims=True) + RMS_EPS)


PACK_W = D_MODEL // 4


def _pack_rows(x):
    out = []
    for h in range(2):
        lo = x[:, (2 * h) * PACK_W:(2 * h + 1) * PACK_W].astype(BF16).astype(F32)
        hi = x[:, (2 * h + 1) * PACK_W:(2 * h + 2) * PACK_W].astype(BF16).astype(F32)
        out.append(lax.bitcast_convert_type(hi, jnp.uint32)
                   | (lax.bitcast_convert_type(lo, jnp.uint32) >> 16))
    return out


def _unpack_rows(p0, p1, dtype=BF16):
    quarters = []
    for p in (p0, p1):
        quarters.append(lax.bitcast_convert_type(p << 16, F32).astype(dtype))
        quarters.append(lax.bitcast_convert_type(p & jnp.uint32(0xFFFF0000), F32).astype(dtype))
    return quarters


def _dot_quarters(quarters, w_ref):
    acc = None
    for q, xq in enumerate(quarters):
        part = jnp.dot(xq, w_ref[q * PACK_W:(q + 1) * PACK_W, :], preferred_element_type=F32)
        acc = part if acc is None else acc + part
    return acc


def _two_stream_specs(seqs, tm, grid_rank=1):
    npt = seqs.np_ // tm
    nst = (seqs.n - seqs.np_) // tm
    if grid_rank == 1:
        p_map = lambda i: (jnp.minimum(i, npt - 1), 0)
        s_map = lambda i: (jnp.clip(i - npt, 0, nst - 1), 0)
    else:
        p_map = lambda i, j: (jnp.minimum(i, npt - 1), 0)
        s_map = lambda i, j: (jnp.clip(i - npt, 0, nst - 1), 0)
    return npt, pl.BlockSpec((tm, D_MODEL), p_map), pl.BlockSpec((tm, D_MODEL), s_map)


def _inproj_kernel(npt, xp_ref, xs_ref, mod_ref, g_ref, w_ref, cs_ref, o_ref, u_scr):
    @pl.when(pl.program_id(1) == 0)
    def _():
        x = jnp.where(pl.program_id(0) < npt, xp_ref[...], xs_ref[...])
        y = _rms(x) * g_ref[...]
        u = y * (1.0 + mod_ref[1:2, :]) + mod_ref[0:1, :]
        u_scr[...] = u.astype(BF16)

    acc = jnp.dot(u_scr[...], w_ref[...], preferred_element_type=F32)
    o_ref[...] = (acc * cs_ref[...]).astype(o_ref.dtype)


def _inproj(seqs, x_p, x_s, mod, norm_mix, w_in_bf):
    n = seqs.n
    tm = min(1024, seqs.tp, seqs.ts)
    tn = 1024
    n_slab = w_in_bf.shape[1] // tn
    npt, p_spec, s_spec = _two_stream_specs(seqs, tm, grid_rank=2)
    col_scale = jnp.ones((1, w_in_bf.shape[1]), F32).at[:, :NA_WIDTH].set(NA_Q_SCALE)
    return pl.pallas_call(
        functools.partial(_inproj_kernel, npt),
        grid=(n // tm, n_slab),
        in_specs=[p_spec, s_spec,
                  pl.BlockSpec((None, N_MOD, D_MODEL), lambda i, j: (seqs.info(i * tm)[0], 0, 0)),
                  pl.BlockSpec((1, D_MODEL), lambda i, j: (0, 0)),
                  pl.BlockSpec((D_MODEL, tn), lambda i, j: (0, j)),
                  pl.BlockSpec((1, tn), lambda i, j: (0, j))],
        out_specs=pl.BlockSpec((None, tm, tn), lambda i, j: (j, i, 0)),
        out_shape=jax.ShapeDtypeStruct((n_slab, n, tn), BF16),
        scratch_shapes=[pltpu.VMEM((tm, D_MODEL), BF16)],
        compiler_params=_params(40, ("arbitrary", "arbitrary")),
        name="inproj",
    )(x_p, x_s, mod, norm_mix.reshape(1, D_MODEL), w_in_bf, col_scale)


def _na_bias_table(rpb):
    col = np.arange(GRID_W)
    cs = np.clip(col - NA_COLS // 2, 0, GRID_W - NA_COLS)
    valid = (col[None, :] >= cs[:, None]) & (col[None, :] < cs[:, None] + NA_COLS)
    coff = col[None, :] - col[:, None] + NA_COLS - 1
    onehot = (coff[None] == np.arange(2 * NA_COLS - 1)[:, None, None]) & valid[None]
    toep = jnp.einsum("hrc,cqk->hrqk", rpb.astype(F32), jnp.asarray(onehot, F32),
                      precision=HIGHEST)
    toep = jnp.where(valid[None, None], toep * LOG2E, MASK_VALUE)
    masked =jnp.full((NA_HEADS, GRID_W, GRID_W), MASK_VALUE, F32)
    cases = (([0] * NA_GROUP, [NA_ROWS - 1 - i for i in range(NA_GROUP)]),
             (list(range(NA_GROUP)), [NA_ROWS // 2 - 1] * NA_GROUP),
             ([NA_GROUP] * NA_GROUP, [NA_ROWS // 2 - 1 - i for i in range(NA_GROUP)]))
    tabs = []
    for first_row, first_off in cases:
        q_rows = []
        for i in range(NA_GROUP):
            blocks = [toep[:, first_off[i] + w - first_row[i]]
                      if 0 <= w - first_row[i] < NA_ROWS else masked for w in range(NA_WIN)]
            q_rows.append(jnp.concatenate(blocks, axis=2))
        tabs.append(jnp.concatenate(q_rows, axis=1))
    return jnp.stack(tabs)


def _na_geometry(seqs, g):
    _, start, length = seqs.info(g * NA_TOK)
    row0 = start // GRID_W
    rows = length // GRID_W
    r0 = g * NA_GROUP - row0
    wb = jnp.clip(r0 - NA_ROWS // 2, 0, rows - NA_WIN)
    case = jnp.where(r0 == 0, 0, jnp.where(r0 == rows - NA_GROUP, 2, 1))
    return (row0 + wb) // NA_GROUP, case


def _na_kernel(q_ref, k0, k1, k2, v0, v1, v2, bias_ref, o_ref):
    k_refs = (k0, k1, k2)
    v_refs = (v0, v1, v2)
    outs = []
    for h in range(NA_HEADS):
        hs = slice(h * NA_HEAD_DIM, (h + 1) * NA_HEAD_DIM)
        q = q_ref[:, hs]
        s = [lax.dot_general(q, kr[:, hs], NT_DIMS, preferred_element_type=F32)
             + bias_ref[h, :, d * NA_TOK:(d + 1) * NA_TOK] for d, kr in enumerate(k_refs)]
        m = jnp.max(jnp.maximum(jnp.maximum(s[0], s[1]), s[2]), axis=-1, keepdims=True)
        p = [jnp.exp2(sd - m) for sd in s]
        l = jnp.sum((p[0] + p[1]) + p[2], axis=-1, keepdims=True)
        o = sum(jnp.dot(pd.astype(BF16), vr[:, hs], preferred_element_type=F32)
                for pd, vr in zip(p, v_refs))
        outs.append(o / l)
    o_ref[...] = jnp.concatenate(outs, axis=1).astype(o_ref.dtype)


def _na(seqs, proj, bias_tab):
    n = seqs.n

    def kv_spec(slab, d):
        return pl.BlockSpec((None, NA_TOK, NA_WIDTH),
                            lambda g: (slab[0], _na_geometry(seqs, g)[0] + d, slab[1]))

    return pl.pallas_call(
        _na_kernel,
        grid=(n // NA_TOK,),
        in_specs=[pl.BlockSpec((None, NA_TOK, NA_WIDTH), lambda g: (SLAB_Q[0], g, SLAB_Q[1]))]
        + [kv_spec(SLAB_K, d) for d in range(3)] + [kv_spec(SLAB_V, d) for d in range(3)]
        + [pl.BlockSpec((None,) + bias_tab.shape[1:],
                        lambda g: (_na_geometry(seqs, g)[1], 0, 0, 0))],
        out_specs=pl.BlockSpec((NA_TOK, NA_WIDTH), lambda g: (g, 0)),
        out_shape=jax.ShapeDtypeStruct((n, NA_WIDTH), BF16),
        compiler_params=_params(48, ("arbitrary",)),
        name="natten",
    )(proj, proj, proj, proj, proj, proj, proj, bias_tab)


def _hg_chunk(q, z, v, lb, tri, mask, mid, last, st_ref):
    sig = jax.nn.sigmoid(z)
    f = lb + (1.0 - lb) * sig
    lf = jnp.log(f)
    kin = (1.0 - lb) * (1.0 - sig)
    hi = lf.astype(BF16)
    lo = (lf - hi.astype(F32)).astype(BF16)
    g2 = jnp.dot(tri, jnp.concatenate([lo, hi], axis=1), preferred_element_type=F32)
    gcum = g2[:, :HG_WIDTH] + g2[:, HG_WIDTH:]
    gm = gcum[mid:mid + 1, :]
    gl = gcum[last:last + 1, :]
    up = jnp.exp(gcum - gm)
    dn = jnp.exp(gm - gcum)
    qa = (q * up).astype(BF16)
    ka = (kin * dn).astype(BF16)
    qe = (q * (up * jnp.exp(gm))).astype(BF16)
    kd = (kin * (dn * jnp.exp(gl - gm))).astype(BF16)
    eg = jnp.exp(gl)
    vb = v.astype(BF16)
    outs = []
    for h in range(HG_HEADS):
        hs = slice(h * HG_KEY_DIM, (h + 1) * HG_KEY_DIM)
        a = lax.dot_general(qa[:, hs], ka[:, hs], NT_DIMS, preferred_element_type=F32)
        a = jnp.where(mask, a, 0.0)
        st = st_ref[h]
        o = jnp.dot(a.astype(BF16), vb[:, hs], preferred_element_type=F32)
        o = o + lax.dot_general(qe[:, hs], st.astype(BF16), NT_DIMS, preferred_element_type=F32)
        st_ref[h] = st * eg[:, hs] + lax.dot_general(vb[:, hs], kd[:, hs], TN_DIMS,
                                                    preferred_element_type=F32)
        outs.append(o)
    return jnp.concatenate(outs, axis=1)


def _hg_exact(q_ref, z_ref, v_ref, lb, reverse, st_ref, o_ref, qs, fs, ks, vs):
    sig = jax.nn.sigmoid(z_ref[...].astype(F32))
    qs[...] = q_ref[...].astype(F32)
    fs[...] = lb + (1.0 - lb) * sig
    ks[...] = (1.0 - lb) * (1.0 - sig)
    vs[...] = v_ref[...].astype(F32)
    eye = (lax.broadcasted_iota(jnp.int32, (HG_KEY_DIM, HG_KEY_DIM), 0)
           == lax.broadcasted_iota(jnp.int32, (HG_KEY_DIM, HG_KEY_DIM), 1)).astype(F32)

    def body(i, carry):
        t = HG_STEP - 1 - i if reverse else i
        q_t, f_t, k_t, v_t = (r[pl.ds(t, 1), :] for r in (qs, fs, ks, vs))
        outs = []
        for h in range(HG_HEADS):
            hs = slice(h * HG_KEY_DIM, (h + 1) * HG_KEY_DIM)
            v_col = jnp.sum(eye * v_t[:, hs], axis=1, keepdims=True)
            st = st_ref[h] * f_t[:, hs] + v_col * k_t[:, hs]
            st_ref[h] = st
            o_col = jnp.sum(st * q_t[:, hs], axis=1, keepdims=True)
            outs.append(jnp.sum(eye * o_col, axis=0, keepdims=True))
        o_ref[pl.ds(t, 1), :] = jnp.concatenate(outs, axis=1)
        return carry

    lax.fori_loop(0, HG_STEP, body, 0)


def _hg_kernel(seqs, safe_ref, qf_ref, zf_ref, vf_ref, qb_ref, zb_ref, vb_ref, lb_ref, of_ref,
               ob_ref, stf, stb, qs, fs, ks, vs):
    i = pl.program_id(0)
    nsteps = pl.num_programs(0)
    tf = i * HG_STEP
    tb = (nsteps - 1 - i) * HG_STEP
    _, start_f, _ = seqs.info(tf)
    _, start_b, len_b = seqs.info(tb)

    @pl.when(tf == start_f)
    def _():
        stf[...] = jnp.zeros_like(stf)

    @pl.when(tb + HG_STEP == start_b + len_b)
    def _():
        stb[...] = jnp.zeros_like(stb)

    lb = lb_ref[...]

    @pl.when(safe_ref[0] > 0)
    def _():
        row = lax.broadcasted_iota(jnp.int32, (HG_CHUNK, HG_CHUNK), 0)
        col = lax.broadcasted_iota(jnp.int32, (HG_CHUNK, HG_CHUNK), 1)
        lower = row >= col
        upper = col >= row
        tri_f = lower.astype(BF16)
        tri_b = upper.astype(BF16)
        nchunk = HG_STEP // HG_CHUNK
        for c in range(nchunk):
            cs = slice(c * HG_CHUNK, (c + 1) * HG_CHUNK)
            of_ref[cs, :] = _hg_chunk(qf_ref[cs, :].astype(F32), zf_ref[cs, :].astype(F32),
                                      vf_ref[cs, :].astype(F32), lb, tri_f, lower,
                                      HG_CHUNK // 2 - 1, HG_CHUNK - 1, stf)
            cb = nchunk - 1 - c
            bs = slice(cb * HG_CHUNK, (cb + 1) * HG_CHUNK)
            ob_ref[bs, :] = _hg_chunk(qb_ref[bs, :].astype(F32), zb_ref[bs, :].astype(F32),
                                      vb_ref[bs, :].astype(F32), lb, tri_b, upper,
                                      HG_CHUNK // 2, 0, stb)

    @pl.when(safe_ref[0] == 0)
    def _():
        _hg_exact(qf_ref, zf_ref, vf_ref, lb, False, stf, of_ref, qs, fs, ks, vs)
        _hg_exact(qb_ref, zb_ref, vb_ref, lb, True, stb, ob_ref, qs, fs, ks, vs)


def _hgrn(seqs, proj, lb):
    n = seqs.n
    nsteps = n // HG_STEP

    def spec(slab, rev):
        if rev:
            return pl.BlockSpec((None, HG_STEP, HG_WIDTH), lambda i: (slab[0], nsteps - 1 - i, slab[1]))
        return pl.BlockSpec((None, HG_STEP, HG_WIDTH), lambda i: (slab[0], i, slab[1]))

    safe = (jnp.max(-jnp.log(lb)) * (HG_CHUNK // 2) < HG_EXP_LIMIT).astype(jnp.int32).reshape(1)
    step_scratch = pltpu.VMEM((HG_STEP, HG_WIDTH), F32)
    return pl.pallas_call(
        functools.partial(_hg_kernel, seqs),
        grid=(nsteps,),
        in_specs=[pl.BlockSpec(memory_space=pltpu.SMEM),
                  spec(SLAB_HQ, False), spec(SLAB_FF, False), spec(SLAB_HI, False),
                  spec(SLAB_HQ, True), spec(SLAB_FB, True), spec(SLAB_HI, True),
                  pl.BlockSpec((1, HG_WIDTH), lambda i: (0, 0))],
        out_specs=[pl.BlockSpec((HG_STEP, HG_WIDTH), lambda i: (i, 0)),
                   pl.BlockSpec((HG_STEP, HG_WIDTH), lambda i: (nsteps - 1 - i, 0))],
        out_shape=[jax.ShapeDtypeStruct((n, HG_WIDTH), F32)] * 2,
        scratch_shapes=[pltpu.VMEM((HG_HEADS, HG_KEY_DIM, HG_KEY_DIM), F32)] * 2
        + [step_scratch] * 4,
        compiler_params=_params(32, ("arbitrary",)),
        name="hgrn2",
    )(safe, proj, proj, proj, proj, proj, proj, lb.reshape(1, HG_WIDTH))


def _merge_kernel(npt, xp_ref, xs_ref, att_ref, of_ref, ob_ref, hg_ref, ga_ref, gb_ref, mod_ref,
                  hgn_ref, nffn_ref, wa_ref, wb_ref, wo_ref, wr_ref, x1_ref, u2_ref, lg_ref):
    x = jnp.where(pl.program_id(0) < npt, xp_ref[...], xs_ref[...])
    o = of_ref[...] + ob_ref[...]
    parts = []
    for h in range(HG_HEADS):
        hs = slice(h * HG_KEY_DIM, (h + 1) * HG_KEY_DIM)
        parts.append(_rms(o[:, hs]))
    on = jnp.concatenate(parts, axis=1) * hgn_ref[...]
    gate = hg_ref[...].astype(F32)
    hb = (on * (gate * jax.nn.sigmoid(gate))).astype(BF16)
    ya = jnp.dot(att_ref[...], wa_ref[...], preferred_element_type=F32)
    yb = jnp.dot(hb, wb_ref[...], preferred_element_type=F32)
    merged = (jax.nn.sigmoid(ga_ref[...].astype(F32)) * ya
              + jax.nn.sigmoid(gb_ref[...].astype(F32)) * yb)
    x1 = x + mod_ref[2:3, :] * jnp.dot(merged.astype(BF16), wo_ref[...],
                                       preferred_element_type=F32)
    x1_ref[...] = x1
    u2 = _rms(x1) * nffn_ref[...] * (1.0 + mod_ref[4:5, :]) + mod_ref[3:4, :]
    u2_ref[0], u2_ref[1] = _pack_rows(u2)
    w_hi = wr_ref[0]
    u_hi = u2.astype(BF16)
    u_lo = (u2 - u_hi.astype(F32)).astype(BF16)
    lg_ref[...] = (lax.dot_general(w_hi, u_hi, NT_DIMS, preferred_element_type=F32)
                   + (lax.dot_general(w_hi, u_lo, NT_DIMS, preferred_element_type=F32)
                      + lax.dot_general(wr_ref[1], u_hi, NT_DIMS, preferred_element_type=F32)))


def _split_hi_lo(w):
    hi = w.astype(BF16)
    return jnp.stack([hi, (w - hi.astype(F32)).astype(BF16)])


def _merge(seqs, x_p, x_s, att, o_f, o_b, proj, mod, hg_norm, norm_ffn, wa, wb, wo, wr):
    n = seqs.n
    tm = 256
    tok = lambda i: (i, 0)
    const = lambda i: (0, 0)
    npt, p_spec, s_spec = _two_stream_specs(seqs, tm)
    return pl.pallas_call(
        functools.partial(_merge_kernel, npt),
        grid=(n // tm,),
        in_specs=[p_spec, s_spec,
                  pl.BlockSpec((tm, NA_WIDTH), tok),
                  pl.BlockSpec((tm, HG_WIDTH), tok),
                  pl.BlockSpec((tm, HG_WIDTH), tok),
                  pl.BlockSpec((None, tm, HG_WIDTH), lambda i: (SLAB_HG[0], i, SLAB_HG[1])),
                  pl.BlockSpec((None, tm, D_MODEL), lambda i: (SLAB_GA, i, 0)),
                  pl.BlockSpec((None, tm, D_MODEL), lambda i: (SLAB_GB, i, 0)),
                  pl.BlockSpec((None, N_MOD, D_MODEL), lambda i: (seqs.info(i * tm)[0], 0, 0)),
                  pl.BlockSpec((1, HG_WIDTH), const),
                  pl.BlockSpec((1, D_MODEL), const),
                  pl.BlockSpec((NA_WIDTH, D_MODEL), const),
                  pl.BlockSpec((HG_WIDTH, D_MODEL), const),
                  pl.BlockSpec((D_MODEL, D_MODEL), const),
                  pl.BlockSpec((2, N_EXPERTS, D_MODEL), lambda i: (0, 0, 0))],
        out_specs=[pl.BlockSpec((tm, D_MODEL), tok),
                   pl.BlockSpec((2, tm, PACK_W), lambda i: (0, i, 0)),
                   pl.BlockSpec((N_EXPERTS, tm), lambda i: (0, i))],
        out_shape=[jax.ShapeDtypeStruct((n, D_MODEL), F32),
                   jax.ShapeDtypeStruct((2, n, PACK_W), jnp.uint32),
                   jax.ShapeDtypeStruct((N_EXPERTS, n), F32)],
        compiler_params=_params(48, ("arbitrary",)),
        name="merge",
    )(x_p, x_s, att, o_f, o_b, proj, proj, proj, mod, hg_norm.reshape(1, HG_WIDTH),
      norm_ffn.reshape(1, D_MODEL), wa, wb, wo, wr)


def _expert_kernel(be_ref, nused_ref, next_ref, slot_ref, xs_ref, wg_hbm, wu_hbm, wd_hbm, ys_ref,
                   wg_s, wu_s, wd_s, wg_st, wu_st, wd_st, sems):
    i = pl.program_id(0)

    def weight_copies(e, slot):
        pairs = ((wg_hbm, wg_st), (wu_hbm, wu_st), (wd_hbm, wd_st))
        return [pltpu.make_async_copy(hbm.at[e], stage.at[slot], sems.at[slot, j])
                for j, (hbm, stage) in enumerate(pairs)]

    @pl.when(i < nused_ref[0])
    def _():
        e = be_ref[i]

        @pl.when((i == 0) | (e != be_ref[jnp.maximum(i - 1, 0)]))
        def _():
            slot = slot_ref[i]

            @pl.when(i == 0)
            def _():
                for c in weight_copies(e, slot):
                    c.start()

            for c in weight_copies(e, slot):
                c.wait()
            nxt = next_ref[i]

            @pl.when(nxt >= 0)
            def _():
                for c in weight_copies(nxt, 1 - slot):
                    c.start()

            wg_s[...] = wg_st[slot].astype(BF16)
            wu_s[...] = wu_st[slot].astype(BF16)
            wd_s[...] = wd_st[slot].astype(BF16)

        xq = _unpack_rows(xs_ref[0], xs_ref[1])
        a = _dot_quarters(xq, wg_s)
        b = _dot_quarters(xq, wu_s)
        h = (a * jax.nn.sigmoid(a) * b).astype(BF16)
        ys_ref[0], ys_ref[1] = _pack_rows(jnp.dot(h, wd_s[...], preferred_element_type=F32))


def _experts(xs, plan, wg, wu, wd):
    n_rows = xs.shape[1]
    n_blocks = n_rows // MOE_BLOCK
    rows_map = lambda i, be, nu, nx, sl: (0, jnp.minimum(i, nu[0] - 1), 0)
    whole = pl.BlockSpec(memory_space=pl.ANY)
    grid_spec = pltpu.PrefetchScalarGridSpec(
        num_scalar_prefetch=4,
        grid=(n_blocks,),
        in_specs=[pl.BlockSpec((2, MOE_BLOCK, PACK_W), rows_map), whole, whole, whole],
        out_specs=pl.BlockSpec((2, MOE_BLOCK, PACK_W), rows_map),
        scratch_shapes=[pltpu.VMEM((D_MODEL, D_EXPERT), BF16),
                        pltpu.VMEM((D_MODEL, D_EXPERT), BF16),
                        pltpu.VMEM((D_EXPERT, D_MODEL), BF16),
                        pltpu.VMEM((2, D_MODEL, D_EXPERT), F32),
                        pltpu.VMEM((2, D_MODEL, D_EXPERT), F32),
                        pltpu.VMEM((2, D_EXPERT, D_MODEL), F32),
                        pltpu.SemaphoreType.DMA((2, 3))],
    )
    return pl.pallas_call(
        _expert_kernel,
        grid_spec=grid_spec,
        out_shape=jax.ShapeDtypeStruct((2, n_rows, PACK_W), jnp.uint32),
        compiler_params=_params(32, ("arbitrary",)),
        name="experts",
    )(*plan, xs, wg, wu, wd)


ROUTE_TILE = 128
ROUTE_STEP = 512
NEG_INF = float("-inf")


def _first_max(x, idx, big):
    m = jnp.max(x, axis=0, keepdims=True)
    first = jnp.min(jnp.where(x == m, idx, big), axis=0, keepdims=True)
    return m, first


def _route_tile(lg, bias, tri, base):
    per_group = N_EXPERTS // N_GROUPS
    scores = jax.nn.sigmoid(lg)
    sel = scores + bias
    lrow = lax.broadcasted_iota(jnp.int32, (per_group, ROUTE_TILE), 0)
    gs = []
    for g in range(N_GROUPS):
        x = sel[g * per_group:(g + 1) * per_group]
        m1, first = _first_max(x, lrow, per_group)
        m2 = jnp.max(jnp.where(lrow == first, NEG_INF, x), axis=0, keepdims=True)
        gs.append(m1 + m2)
    cur = jnp.concatenate(gs, axis=0)
    grow = lax.broadcasted_iota(jnp.int32, (N_GROUPS, ROUTE_TILE), 0)
    chosen = jnp.zeros((N_GROUPS, ROUTE_TILE), jnp.int32)
    for _ in range(TOPK_GROUPS):
        _, first = _first_max(cur, grow, N_GROUPS)
        hit = grow == first
        chosen = jnp.where(hit, 1, chosen)
        cur = jnp.where(hit, NEG_INF, cur)
    cur = jnp.concatenate(
        [jnp.where(chosen[g:g + 1] > 0, sel[g * per_group:(g + 1) * per_group], NEG_INF)
         for g in range(N_GROUPS)], axis=0)
    row = lax.broadcasted_iota(jnp.int32, (N_EXPERTS, ROUTE_TILE), 0)
    member = jnp.zeros((N_EXPERTS, ROUTE_TILE), F32)
    es, ws = [], []
    for _ in range(TOP_K):
        _, first = _first_max(cur, row, N_EXPERTS)
        hit = row == first
        es.append(first)
        ws.append(jnp.sum(jnp.where(hit, scores, 0.0), axis=0, keepdims=True))
        cur = jnp.where(hit, NEG_INF, cur)
        member = jnp.where(hit, 1.0, member)
    e = jnp.concatenate(es, axis=0)
    w = jnp.concatenate(ws, axis=0)
    w = w / jnp.sum(w, axis=0, keepdims=True) * ROUTE_SCALE
    before = jnp.dot(member.astype(BF16), tri, preferred_element_type=F32) + base
    rank = jnp.concatenate(
        [jnp.sum(jnp.where(row == es[k], before, 0.0), axis=0, keepdims=True) for k in range(TOP_K)],
        axis=0)
    return e, w, rank.astype(jnp.int32), member


def _route_kernel(lg_ref, b_ref, e_ref, w_ref, r_ref, cnt_ref, base):
    @pl.when(pl.program_id(0) == 0)
    def _():
        base[...] = jnp.zeros_like(base)

    r_i = lax.broadcasted_iota(jnp.int32, (ROUTE_TILE, ROUTE_TILE), 0)
    c_i = lax.broadcasted_iota(jnp.int32, (ROUTE_TILE, ROUTE_TILE), 1)
    tri = (r_i < c_i).astype(BF16)
    bias = b_ref[...]
    for j in range(ROUTE_STEP // ROUTE_TILE):
        ls = slice(j * ROUTE_TILE, (j + 1) * ROUTE_TILE)
        e, w, rank, member = _route_tile(lg_ref[:, ls], bias, tri, base[...])
        e_ref[:, ls] = e
        w_ref[:, ls] = w
        r_ref[:, ls] = rank
        base[...] = base[...] + jnp.sum(member, axis=1, keepdims=True)
    cnt_ref[...] = base[...]


def _route(logits_t, b_router):
    n = logits_t.shape[1]
    tok = lambda i: (0, i)
    return pl.pallas_call(
        _route_kernel,
        grid=(n // ROUTE_STEP,),
        in_specs=[pl.BlockSpec((N_EXPERTS, ROUTE_STEP), tok),
                  pl.BlockSpec((N_EXPERTS, 1), lambda i: (0, 0))],
        out_specs=[pl.BlockSpec((TOP_K, ROUTE_STEP), tok),
                   pl.BlockSpec((TOP_K, ROUTE_STEP), tok),
                   pl.BlockSpec((TOP_K, ROUTE_STEP), tok),
                   pl.BlockSpec((N_EXPERTS, 1), lambda i: (0, 0))],
        out_shape=[jax.ShapeDtypeStruct((TOP_K, n), jnp.int32),
                   jax.ShapeDtypeStruct((TOP_K, n), F32),
                   jax.ShapeDtypeStruct((TOP_K, n), jnp.int32),
                   jax.ShapeDtypeStruct((N_EXPERTS, 1), F32)],
        scratch_shapes=[pltpu.VMEM((N_EXPERTS, 1), F32)],
        compiler_params=_params(32, ("arbitrary",)),
        name="route",
    )(logits_t, b_router.reshape(N_EXPERTS, 1).astype(F32))


def _pos_kernel(e_ref, r_ref, ps_ref, pos_ref):
    row = lax.broadcasted_iota(jnp.int32, (N_EXPERTS, ROUTE_TILE), 0)
    pstart = ps_ref[...]
    for j in range(ROUTE_STEP // ROUTE_TILE):
        ls = slice(j * ROUTE_TILE, (j + 1) * ROUTE_TILE)
        e = e_ref[:, ls]
        off = jnp.concatenate(
            [jnp.sum(jnp.where(row == e[k:k + 1], pstart, 0.0), axis=0, keepdims=True)
             for k in range(TOP_K)], axis=0)
        pos_ref[:, ls] = off.astype(jnp.int32) + r_ref[:, ls]


def _positions(eidx, rank, pstarts):
    n = eidx.shape[1]
    tok = lambda i: (0, i)
    return pl.pallas_call(
        _pos_kernel,
        grid=(n // ROUTE_STEP,),
        in_specs=[pl.BlockSpec((TOP_K, ROUTE_STEP), tok),
                  pl.BlockSpec((TOP_K, ROUTE_STEP), tok),
                  pl.BlockSpec((N_EXPERTS, 1), lambda i: (0, 0))],
        out_specs=pl.BlockSpec((TOP_K, ROUTE_STEP), tok),
        out_shape=jax.ShapeDtypeStruct((TOP_K, n), jnp.int32),
        compiler_params=_params(32, ("arbitrary",)),
        name="positions",
    )(eidx, rank, pstarts.reshape(N_EXPERTS, 1).astype(F32))


def _block_tables(counts, n_blocks):
    counts = counts.reshape(N_EXPERTS).astype(jnp.int32)
    padded = (counts + MOE_BLOCK - 1) // MOE_BLOCK * MOE_BLOCK
    pend = jnp.cumsum(padded)
    pstarts = pend - padded
    first_row = jnp.arange(n_blocks, dtype=jnp.int32) * MOE_BLOCK
    block_e = jnp.sum((pend[None, :] <= first_row[:, None]).astype(jnp.int32), axis=1)
    block_e = jnp.minimum(block_e, N_EXPERTS - 1).astype(jnp.int32)
    n_used = (pend[-1] // MOE_BLOCK).astype(jnp.int32).reshape(1)
    ids = jnp.arange(N_EXPERTS, dtype=jnp.int32)
    used = padded > 0
    ordinal = jnp.cumsum(used.astype(jnp.int32)) - 1
    from_here = jnp.flip(lax.cummin(jnp.flip(jnp.where(used, ids, N_EXPERTS))))
    after = jnp.concatenate([from_here[1:], jnp.full((1,), N_EXPERTS, jnp.int32)])
    after = jnp.where(after >= N_EXPERTS, -1, after)
    mine = block_e[:, None] == ids[None, :]
    next_e = jnp.sum(jnp.where(mine, after[None, :], 0), axis=1).astype(jnp.int32)
    slot = jnp.sum(jnp.where(mine, ordinal[None, :] % 2, 0), axis=1).astype(jnp.int32)
    return pstarts, (block_e, n_used, next_e, slot)


SC_WINDOW = 128


def _sc_mesh():
    return plsc.VectorSubcoreMesh(core_axis_name="core", subcore_axis_name="subcore")


def _both_halves(pos, n_rows):
    return jnp.concatenate([pos, pos + n_rows]).reshape(1, -1)


def _sc_dispatch(rows, pos, n_rows):
    _, n, w = rows.shape
    tiles = n // SC_WINDOW
    steps_per_half = pos.shape[0] // SC_WINDOW

    @functools.partial(pl.kernel, out_type=jax.ShapeDtypeStruct((2 * n_rows, w), rows.dtype),
                       mesh=_sc_mesh(), scratch_types=[])
    def scatter_kernel(x_hbm, i_hbm, o_hbm):
        def body(x_vmem, i_vmem):
            pltpu.sync_copy(x_vmem, o_hbm.at[i_vmem.at[0]])

        pltpu.emit_pipeline(
            body,
            grid=(2 * steps_per_half,),
            in_specs=[pl.BlockSpec((SC_WINDOW, w),
                                   lambda i: ((i // steps_per_half) * tiles + i % tiles, 0)),
                      pl.BlockSpec((1, SC_WINDOW), lambda i: (0, i))],
            out_specs=[],
            core_axis_name=("core", "subcore"),
            dimension_semantics=(pltpu.PARALLEL,),
        )(x_hbm, i_hbm)

    out = scatter_kernel(rows.reshape(2 * n, w), _both_halves(pos, n_rows))
    return out.reshape(2, n_rows, w)


def _sc_gather(table, pos):
    _, n_rows, w = table.shape
    m = pos.shape[0]

    @functools.partial(pl.kernel, out_type=jax.ShapeDtypeStruct((2 * m, w), table.dtype),
                       mesh=_sc_mesh(), scratch_types=[])
    def gather_kernel(t_hbm, i_hbm, o_hbm):
        def body(i_vmem, o_vmem):
            pltpu.sync_copy(t_hbm.at[i_vmem.at[0]], o_vmem)

        pltpu.emit_pipeline(
            body,
            grid=(2 * m // SC_WINDOW,),
            in_specs=[pl.BlockSpec((1, SC_WINDOW), lambda i: (0, i))],
            out_specs=[pl.BlockSpec((SC_WINDOW, w), lambda i: (i, 0))],
            core_axis_name=("core", "subcore"),
            dimension_semantics=(pltpu.PARALLEL,),
        )(i_hbm, o_hbm)

    out = gather_kernel(table.reshape(2 * n_rows, w), _both_halves(pos, n_rows))
    return out.reshape(2, m, w)


def _final_kernel(npt, x1_ref, u2_ref, yg_ref, w_ref, mod_ref, nf_ref, wsg_ref, wsu_ref, wsd_ref,
                  op_ref, os_ref):
    uq = _unpack_rows(u2_ref[0], u2_ref[1])
    a = _dot_quarters(uq, wsg_ref)
    b = _dot_quarters(uq, wsu_ref)
    f = jnp.dot((a * jax.nn.sigmoid(a) * b).astype(BF16), wsd_ref[...], preferred_element_type=F32)
    w = w_ref[...]
    fq = [f[:, q * PACK_W:(q + 1) * PACK_W] for q in range(4)]
    for k in range(TOP_K):
        yq = _unpack_rows(yg_ref[0, k], yg_ref[1, k], F32)
        fq = [fq[q] + w[:, k:k + 1] * yq[q] for q in range(4)]
    x2 = x1_ref[...] + mod_ref[5:6, :] * jnp.concatenate(fq, axis=1)
    out = _rms(x2) * nf_ref[...]
    i = pl.program_id(0)

    @pl.when(i < npt)
    def _():
        op_ref[...] = out

    @pl.when(i >= npt)
    def _():
        os_ref[...] = out


def _final(seqs, x1, u2, yg, w, mod, norm_final, wsg, wsu, wsd):
    n = seqs.n
    tm = 256
    tok = lambda i: (i, 0)
    const = lambda i: (0, 0)
    npt, p_spec, s_spec = _two_stream_specs(seqs, tm)
    return pl.pallas_call(
        functools.partial(_final_kernel, npt),
        grid=(n // tm,),
        in_specs=[pl.BlockSpec((tm, D_MODEL), tok),
                  pl.BlockSpec((2, tm, PACK_W), lambda i: (0, i, 0)),
                  pl.BlockSpec((2, TOP_K, tm, PACK_W), lambda i: (0, 0, i, 0)),
                  pl.BlockSpec((tm, TOP_K), tok),
                  pl.BlockSpec((None, N_MOD, D_MODEL), lambda i: (seqs.info(i * tm)[0], 0, 0)),
                  pl.BlockSpec((1, D_MODEL), const),
                  pl.BlockSpec((D_MODEL, D_EXPERT), const),
                  pl.BlockSpec((D_MODEL, D_EXPERT), const),
                  pl.BlockSpec((D_EXPERT, D_MODEL), const)],
        out_specs=[p_spec, s_spec],
        out_shape=[jax.ShapeDtypeStruct((seqs.np_, D_MODEL), F32),
                   jax.ShapeDtypeStruct((n - seqs.np_, D_MODEL), F32)],
        compiler_params=_params(48, ("arbitrary",)),
        name="final",
    )(x1, u2, yg, w, mod, norm_final.reshape(1, D_MODEL), wsg, wsu, wsd)


def _layer(seqs, x_p, x_s, c, w_ada, b_ada, norm_mix, w_in, na_rpb, hg_lb, hg_norm, w_branch_a,
           w_branch_b, w_out, norm_ffn, w_router, b_router, w_exp_gate, w_exp_up, w_exp_down,
           w_sh_gate, w_sh_up, w_sh_down, norm_final):
    n = seqs.n
    c_rows = -(-seqs.nseq // 8) * 8
    c_pad = jnp.zeros((c_rows, D_MODEL), F32).at[:seqs.nseq].set(c)
    mod = _ada(c_pad, w_ada[0], b_ada[0])[:seqs.nseq].reshape(seqs.nseq, N_MOD, D_MODEL)
    lb = jnp.cumsum(jax.nn.softmax(hg_lb.astype(F32), axis=0), axis=0)[0]

    proj = _inproj(seqs, x_p, x_s, mod, norm_mix[0], w_in[0].astype(BF16))
    att = _na(seqs, proj, _na_bias_table(na_rpb[0]))
    o_f, o_b = _hgrn(seqs, proj, lb)
    x1, u2, logits = _merge(seqs, x_p, x_s, att, o_f, o_b, proj, mod, hg_norm[0], norm_ffn[0],
                            w_branch_a[0].astype(BF16), w_branch_b[0].astype(BF16),
                            w_out[0].astype(BF16), _split_hi_lo(w_router[0].T))

    eidx, w, rank, counts = _route(logits, b_router[0])
    n_rows = n * TOP_K + N_EXPERTS * MOE_BLOCK
    pstarts, plan = _block_tables(counts, n_rows // MOE_BLOCK)
    pos = _positions(eidx, rank, pstarts).reshape(-1)
    xs = _sc_dispatch(u2, pos, n_rows)
    ys = _experts(xs, plan, w_exp_gate[0], w_exp_up[0], w_exp_down[0])
    yg = _sc_gather(ys, pos).reshape(2, TOP_K, n, PACK_W)
    return _final(seqs, x1, u2, yg, w.T, mod, norm_final, w_sh_gate[0].astype(BF16),
                  w_sh_up[0].astype(BF16), w_sh_down[0].astype(BF16))


def kernel(x_prompt, x_sample, c_prompt, c_sample, w_ada, b_ada, norm_mix, w_in, na_rpb, hg_lb, hg_norm, w_branch_a, w_branch_b, w_out, norm_ffn, w_router, b_router, w_exp_gate, w_exp_up, w_exp_down, w_sh_gate, w_sh_up, w_sh_down, norm_final):
    bp, tp, _ = x_prompt.shape
    bs, ts, _ = x_sample.shape
    seqs = _Seqs(bp, tp, bs, ts)
    c = jnp.concatenate([c_prompt, c_sample])
    y_p, y_s = _layer(seqs, x_prompt.reshape(bp * tp, D_MODEL), x_sample.reshape(bs * ts, D_MODEL),
                      c, w_ada, b_ada, norm_mix, w_in, na_rpb, hg_lb, hg_norm, w_branch_a,
                      w_branch_b, w_out, norm_ffn, w_router, b_router, w_exp_gate, w_exp_up,
                      w_exp_down, w_sh_gate, w_sh_up, w_sh_down, norm_final)
    return (y_p.reshape(bp, tp, D_MODEL), y_s.reshape(bs, ts, D_MODEL))
```

```python
import functools

import jax
import jax.numpy as jnp
import numpy as np
from jax import lax
from jax.experimental import pallas as pl
from jax.experimental.pallas import tpu as pltpu
from jax.experimental.pallas import tpu_sc as plsc

D_MODEL = 1024
GRID_W = 64
NA_HEADS = 8
NA_HEAD_DIM = 64
NA_WIDTH = NA_HEADS * NA_HEAD_DIM
NA_ROWS = 8
NA_COLS = 16
HG_HEADS = 4
HG_KEY_DIM = 128
HG_WIDTH = HG_HEADS * HG_KEY_DIM
HG_CHUNK = 128
HG_EXP_LIMIT = 80.0
N_EXPERTS = 256
TOP_K = 8
N_GROUPS = 8
TOPK_GROUPS = 4
D_EXPERT = 256
ROUTE_SCALE = 2.5
N_MOD = 6
RMS_EPS = 1e-6

MOE_BLOCK = 512
NA_GROUP = 4
NA_WIN = 3 * NA_GROUP
NA_TOK = NA_GROUP * GRID_W
LOG2E = 1.4426950408889634
NA_Q_SCALE = NA_HEAD_DIM ** -0.5 * LOG2E
HG_STEP = 256
MASK_VALUE = -1e30

F32 = jnp.float32
BF16 = jnp.bfloat16
HIGHEST = lax.Precision.HIGHEST
NT_DIMS = (((1,), (1,)), ((), ()))
TN_DIMS = (((0,), (0,)), ((), ()))

SLAB_Q, SLAB_K, SLAB_V, SLAB_HQ = (0, 0), (0, 1), (1, 0), (1, 1)
SLAB_FF, SLAB_FB, SLAB_HI, SLAB_HG = (2, 0), (2, 1), (3, 0), (3, 1)
SLAB_GA, SLAB_GB = 4, 5


def _params(vmem_mb, sem=None):
    kw = dict(vmem_limit_bytes=vmem_mb * 1024 * 1024)
    if sem is not None:
        kw["dimension_semantics"] = sem
    return pltpu.CompilerParams(**kw)


class _Seqs:
    def __init__(self, bp, tp, bs, ts):
        self.bp, self.tp, self.bs, self.ts = bp, tp, bs, ts
        self.np_ = bp * tp
        self.n = bp * tp + bs * ts
        self.nseq = bp + bs

    def info(self, t0):
        in_p = t0 < self.np_
        rel = jnp.maximum(t0 - self.np_, 0)
        sid = jnp.where(in_p, t0 // self.tp, self.bp + rel // self.ts)
        start = jnp.where(in_p, (t0 // self.tp) * self.tp, self.np_ + (rel // self.ts) * self.ts)
        length = jnp.where(in_p, self.tp, self.ts)
        return sid, start, length


def _ada_kernel(c_ref, w_ref, b_ref, o_ref):
    c = c_ref[...]
    a = c * jax.nn.sigmoid(c)
    o_ref[...] = jnp.dot(a, w_ref[...], precision=HIGHEST, preferred_element_type=F32) + b_ref[...]


def _ada(c_pad, w_ada, b_ada):
    rows = c_pad.shape[0]
    n_out = w_ada.shape[1]
    tn = 1024
    return pl.pallas_call(
        _ada_kernel,
        grid=(n_out // tn,),
        in_specs=[pl.BlockSpec((rows, D_MODEL), lambda j: (0, 0)),
                  pl.BlockSpec((D_MODEL, tn), lambda j: (0, j)),
                  pl.BlockSpec((1, tn), lambda j: (0, j))],
        out_specs=pl.BlockSpec((rows, tn), lambda j: (0, j)),
        out_shape=jax.ShapeDtypeStruct((rows, n_out), F32),
        compiler_params=_params(32),
        name="ada",
    )(c_pad, w_ada, b_ada.reshape(1, n_out))


def _rms(x):
    return x * lax.rsqrt(jnp.mean(x * x, axis=-1, keepdims=True) + RMS_EPS)


def _sigmoid(x):
    return 0.5 * jnp.tanh(0.5 * x) + 0.5


PACK_W = D_MODEL // 4


def _pack_rows(x):
    out = []
    for h in range(2):
        lo = x[:, (2 * h) * PACK_W:(2 * h + 1) * PACK_W].astype(BF16).astype(F32)
        hi = x[:, (2 * h + 1) * PACK_W:(2 * h + 2) * PACK_W].astype(BF16).astype(F32)
        out.append(lax.bitcast_convert_type(hi, jnp.uint32)
                   | (lax.bitcast_convert_type(lo, jnp.uint32) >> 16))
    return out


def _unpack_rows(p0, p1, dtype=BF16):
    quarters = []
    for p in (p0, p1):
        quarters.append(lax.bitcast_convert_type(p << 16, F32).astype(dtype))
        quarters.append(lax.bitcast_convert_type(p & jnp.uint32(0xFFFF0000), F32).astype(dtype))
    return quarters


def _dot_quarters(quarters, w_ref):
    acc = None
    for q, xq in enumerate(quarters):
        part = jnp.dot(xq, w_ref[q * PACK_W:(q + 1) * PACK_W, :], preferred_element_type=F32)
        acc = part if acc is None else acc + part
    return acc


def _two_stream_specs(seqs, tm, grid_rank=1):
    npt = seqs.np_ // tm
    nst = (seqs.n - seqs.np_) // tm
    if grid_rank == 1:
        p_map = lambda i: (jnp.minimum(i, npt - 1), 0)
        s_map = lambda i: (jnp.clip(i - npt, 0, nst - 1), 0)
    else:
        p_map = lambda i, j: (jnp.minimum(i, npt - 1), 0)
        s_map = lambda i, j: (jnp.clip(i - npt, 0, nst - 1), 0)
    return npt, pl.BlockSpec((tm, D_MODEL), p_map), pl.BlockSpec((tm, D_MODEL), s_map)


def _inproj_kernel(npt, xp_ref, xs_ref, mod_ref, g_ref, w_ref, cs_ref, o_ref, u_scr):
    @pl.when(pl.program_id(1) == 0)
    def _():
        x = jnp.where(pl.program_id(0) < npt, xp_ref[...], xs_ref[...])
        y = _rms(x) * g_ref[...]
        u = y * (1.0 + mod_ref[1:2, :]) + mod_ref[0:1, :]
        u_scr[...] = u.astype(BF16)

    acc = jnp.dot(u_scr[...], w_ref[...], preferred_element_type=F32)
    o_ref[...] = (acc * cs_ref[...]).astype(o_ref.dtype)


def _inproj(seqs, x_p, x_s, mod, norm_mix, w_in_bf):
    n = seqs.n
    tm = min(1024, seqs.tp, seqs.ts)
    tn = 1024
    n_slab = w_in_bf.shape[1] // tn
    npt, p_spec, s_spec = _two_stream_specs(seqs, tm, grid_rank=2)
    col_scale = jnp.ones((1, w_in_bf.shape[1]), F32).at[:, :NA_WIDTH].set(NA_Q_SCALE)
    return pl.pallas_call(
        functools.partial(_inproj_kernel, npt),
        grid=(n // tm, n_slab),
        in_specs=[p_spec, s_spec,
                  pl.BlockSpec((None, N_MOD, D_MODEL), lambda i, j: (seqs.info(i * tm)[0], 0, 0)),
                  pl.BlockSpec((1, D_MODEL), lambda i, j: (0, 0)),
                  pl.BlockSpec((D_MODEL, tn), lambda i, j: (0, j)),
                  pl.BlockSpec((1, tn), lambda i, j: (0, j))],
        out_specs=pl.BlockSpec((None, tm, tn), lambda i, j: (j, i, 0)),
        out_shape=jax.ShapeDtypeStruct((n_slab, n, tn), BF16),
        scratch_shapes=[pltpu.VMEM((tm, D_MODEL), BF16)],
        compiler_params=_params(40, ("arbitrary", "arbitrary")),
        name="inproj",
    )(x_p, x_s, mod, norm_mix.reshape(1, D_MODEL), w_in_bf, col_scale)


def _na_bias_table(rpb):
    col = np.arange(GRID_W)
    cs = np.clip(col - NA_COLS // 2, 0, GRID_W - NA_COLS)
    valid = (col[None, :] >= cs[:, None]) & (col[None, :] < cs[:, None] + NA_COLS)
    coff = col[None, :] - col[:, None] + NA_COLS - 1
    onehot = (coff[None] == np.arange(2 * NA_COLS - 1)[:, None, None]) & valid[None]
    toep = jnp.einsum("hrc,cqk->hrqk", rpb.astype(F32), jnp.asarray(onehot, F32),
                      precision=HIGHEST)
    toep = jnp.where(valid[None, None], toep * LOG2E, MASK_VALUE)
    masked =jnp.full((NA_HEADS, GRID_W, GRID_W), MASK_VALUE, F32)
    cases = (([0] * NA_GROUP, [NA_ROWS - 1 - i for i in range(NA_GROUP)]),
             (list(range(NA_GROUP)), [NA_ROWS // 2 - 1] * NA_GROUP),
             ([NA_GROUP] * NA_GROUP, [NA_ROWS // 2 - 1 - i for i in range(NA_GROUP)]))
    tabs = []
    for first_row, first_off in cases:
        q_rows = []
        for i in range(NA_GROUP):
            blocks = [toep[:, first_off[i] + w - first_row[i]]
                      if 0 <= w - first_row[i] < NA_ROWS else masked for w in range(NA_WIN)]
            q_rows.append(jnp.concatenate(blocks, axis=2))
        tabs.append(jnp.concatenate(q_rows, axis=1))
    return jnp.stack(tabs)


def _na_geometry(seqs, g):
    _, start, length = seqs.info(g * NA_TOK)
    row0 = start // GRID_W
    rows = length // GRID_W
    r0 = g * NA_GROUP - row0
    wb = jnp.clip(r0 - NA_ROWS // 2, 0, rows - NA_WIN)
    case = jnp.where(r0 == 0, 0, jnp.where(r0 == rows - NA_GROUP, 2, 1))
    return (row0 + wb) // NA_GROUP, case


def _na_kernel(q_ref, k0, k1, k2, v0, v1, v2, bias_ref, o_ref):
    k_refs = (k0, k1, k2)
    v_refs = (v0, v1, v2)
    outs = []
    for h in range(NA_HEADS):
        hs = slice(h * NA_HEAD_DIM, (h + 1) * NA_HEAD_DIM)
        q = q_ref[:, hs]
        s = [lax.dot_general(q, kr[:, hs], NT_DIMS, preferred_element_type=F32)
             + bias_ref[h, :, d * NA_TOK:(d + 1) * NA_TOK] for d, kr in enumerate(k_refs)]
        m = jnp.max(jnp.maximum(jnp.maximum(s[0], s[1]), s[2]), axis=-1, keepdims=True)
        p = [jnp.exp2(sd - m) for sd in s]
        l = jnp.sum((p[0] + p[1]) + p[2], axis=-1, keepdims=True)
        o = sum(jnp.dot(pd.astype(BF16), vr[:, hs], preferred_element_type=F32)
                for pd, vr in zip(p, v_refs))
        outs.append(o / l)
    o_ref[...] = jnp.concatenate(outs, axis=1).astype(o_ref.dtype)


def _na(seqs, proj, bias_tab):
    n = seqs.n

    def kv_spec(slab, d):
        return pl.BlockSpec((None, NA_TOK, NA_WIDTH),
                            lambda g: (slab[0], _na_geometry(seqs, g)[0] + d, slab[1]))

    return pl.pallas_call(
        _na_kernel,
        grid=(n // NA_TOK,),
        in_specs=[pl.BlockSpec((None, NA_TOK, NA_WIDTH), lambda g: (SLAB_Q[0], g, SLAB_Q[1]))]
        + [kv_spec(SLAB_K, d) for d in range(3)] + [kv_spec(SLAB_V, d) for d in range(3)]
        + [pl.BlockSpec((None,) + bias_tab.shape[1:],
                        lambda g: (_na_geometry(seqs, g)[1], 0, 0, 0))],
        out_specs=pl.BlockSpec((NA_TOK, NA_WIDTH), lambda g: (g, 0)),
        out_shape=jax.ShapeDtypeStruct((n, NA_WIDTH), BF16),
        compiler_params=_params(48, ("arbitrary",)),
        name="natten",
    )(proj, proj, proj, proj, proj, proj, proj, bias_tab)


def _hg_chunk(q, z, v, lb, tri, mask, mid, last, st_ref):
    sig = _sigmoid(z)
    f = lb + (1.0 - lb) * sig
    lf = jnp.log(f)
    kin = (1.0 - lb) * (1.0 - sig)
    hi = lf.astype(BF16)
    lo = (lf - hi.astype(F32)).astype(BF16)
    g2 = jnp.dot(tri, jnp.concatenate([lo, hi], axis=1), preferred_element_type=F32)
    gcum = g2[:, :HG_WIDTH] + g2[:, HG_WIDTH:]
    gm = gcum[mid:mid + 1, :]
    gl = gcum[last:last + 1, :]
    up = jnp.exp(gcum - gm)
    dn = jnp.exp(gm - gcum)
    qa = (q * up).astype(BF16)
    ka = (kin * dn).astype(BF16)
    qe = (q * (up * jnp.exp(gm))).astype(BF16)
    kd = (kin * (dn * jnp.exp(gl - gm))).astype(BF16)
    eg = jnp.exp(gl)
    vb = v.astype(BF16)
    outs = []
    for h in range(HG_HEADS):
        hs = slice(h * HG_KEY_DIM, (h + 1) * HG_KEY_DIM)
        a = lax.dot_general(qa[:, hs], ka[:, hs], NT_DIMS, preferred_element_type=F32)
        a = jnp.where(mask, a, 0.0)
        st = st_ref[h]
        o = jnp.dot(a.astype(BF16), vb[:, hs], preferred_element_type=F32)
        o = o + lax.dot_general(qe[:, hs], st.astype(BF16), NT_DIMS, preferred_element_type=F32)
        st_ref[h] = st * eg[:, hs] + lax.dot_general(vb[:, hs], kd[:, hs], TN_DIMS,
                                                    preferred_element_type=F32)
        outs.append(o)
    return jnp.concatenate(outs, axis=1)


def _hg_exact(q_ref, z_ref, v_ref, lb, reverse, st_ref, o_ref, qs, fs, ks, vs):
    sig = jax.nn.sigmoid(z_ref[...].astype(F32))
    qs[...] = q_ref[...].astype(F32)
    fs[...] = lb + (1.0 - lb) * sig
    ks[...] = (1.0 - lb) * (1.0 - sig)
    vs[...] = v_ref[...].astype(F32)
    eye = (lax.broadcasted_iota(jnp.int32, (HG_KEY_DIM, HG_KEY_DIM), 0)
           == lax.broadcasted_iota(jnp.int32, (HG_KEY_DIM, HG_KEY_DIM), 1)).astype(F32)

    def body(i, carry):
        t = HG_STEP - 1 - i if reverse else i
        q_t, f_t, k_t, v_t = (r[pl.ds(t, 1), :] for r in (qs, fs, ks, vs))
        outs = []
        for h in range(HG_HEADS):
            hs = slice(h * HG_KEY_DIM, (h + 1) * HG_KEY_DIM)
            v_col = jnp.sum(eye * v_t[:, hs], axis=1, keepdims=True)
            st = st_ref[h] * f_t[:, hs] + v_col * k_t[:, hs]
            st_ref[h] = st
            o_col = jnp.sum(st * q_t[:, hs], axis=1, keepdims=True)
            outs.append(jnp.sum(eye * o_col, axis=0, keepdims=True))
        o_ref[pl.ds(t, 1), :] = jnp.concatenate(outs, axis=1)
        return carry

    lax.fori_loop(0, HG_STEP, body, 0)


def _hg_kernel(seqs, safe_ref, qf_ref, zf_ref, vf_ref, qb_ref, zb_ref, vb_ref, lb_ref, of_ref,
               ob_ref, stf, stb, qs, fs, ks, vs):
    i = pl.program_id(0)
    nsteps = pl.num_programs(0)
    tf = i * HG_STEP
    tb = (nsteps - 1 - i) * HG_STEP
    _, start_f, _ = seqs.info(tf)
    _, start_b, len_b = seqs.info(tb)

    @pl.when(tf == start_f)
    def _():
        stf[...] = jnp.zeros_like(stf)

    @pl.when(tb + HG_STEP == start_b + len_b)
    def _():
        stb[...] = jnp.zeros_like(stb)

    lb = lb_ref[...]

    @pl.when(safe_ref[0] > 0)
    def _():
        row = lax.broadcasted_iota(jnp.int32, (HG_CHUNK, HG_CHUNK), 0)
        col = lax.broadcasted_iota(jnp.int32, (HG_CHUNK, HG_CHUNK), 1)
        lower = row >= col
        upper = col >= row
        tri_f = lower.astype(BF16)
        tri_b = upper.astype(BF16)
        nchunk = HG_STEP // HG_CHUNK
        for c in range(nchunk):
            cs = slice(c * HG_CHUNK, (c + 1) * HG_CHUNK)
            of_ref[cs, :] = _hg_chunk(qf_ref[cs, :].astype(F32), zf_ref[cs, :].astype(F32),
                                      vf_ref[cs, :].astype(F32), lb, tri_f, lower,
                                      HG_CHUNK // 2 - 1, HG_CHUNK - 1, stf)
            cb = nchunk - 1 - c
            bs = slice(cb * HG_CHUNK, (cb + 1) * HG_CHUNK)
            ob_ref[bs, :] = _hg_chunk(qb_ref[bs, :].astype(F32), zb_ref[bs, :].astype(F32),
                                      vb_ref[bs, :].astype(F32), lb, tri_b, upper,
                                      HG_CHUNK // 2, 0, stb)

    @pl.when(safe_ref[0] == 0)
    def _():
        _hg_exact(qf_ref, zf_ref, vf_ref, lb, False, stf, of_ref, qs, fs, ks, vs)
        _hg_exact(qb_ref, zb_ref, vb_ref, lb, True, stb, ob_ref, qs, fs, ks, vs)


def _hgrn(seqs, proj, lb):
    n = seqs.n
    nsteps = n // HG_STEP

    def spec(slab, rev):
        if rev:
            return pl.BlockSpec((None, HG_STEP, HG_WIDTH), lambda i: (slab[0], nsteps - 1 - i, slab[1]))
        return pl.BlockSpec((None, HG_STEP, HG_WIDTH), lambda i: (slab[0], i, slab[1]))

    safe = (jnp.max(-jnp.log(lb)) * (HG_CHUNK // 2) < HG_EXP_LIMIT).astype(jnp.int32).reshape(1)
    step_scratch = pltpu.VMEM((HG_STEP, HG_WIDTH), F32)
    return pl.pallas_call(
        functools.partial(_hg_kernel, seqs),
        grid=(nsteps,),
        in_specs=[pl.BlockSpec(memory_space=pltpu.SMEM),
                  spec(SLAB_HQ, False), spec(SLAB_FF, False), spec(SLAB_HI, False),
                  spec(SLAB_HQ, True), spec(SLAB_FB, True), spec(SLAB_HI, True),
                  pl.BlockSpec((1, HG_WIDTH), lambda i: (0, 0))],
        out_specs=[pl.BlockSpec((HG_STEP, HG_WIDTH), lambda i: (i, 0)),
                   pl.BlockSpec((HG_STEP, HG_WIDTH), lambda i: (nsteps - 1 - i, 0))],
        out_shape=[jax.ShapeDtypeStruct((n, HG_WIDTH), F32)] * 2,
        scratch_shapes=[pltpu.VMEM((HG_HEADS, HG_KEY_DIM, HG_KEY_DIM), F32)] * 2
        + [step_scratch] * 4,
        compiler_params=_params(32, ("arbitrary",)),
        name="hgrn2",
    )(safe, proj, proj, proj, proj, proj, proj, lb.reshape(1, HG_WIDTH))


def _merge_kernel(npt, xp_ref, xs_ref, att_ref, of_ref, ob_ref, hg_ref, ga_ref, gb_ref, mod_ref,
                  hgn_ref, nffn_ref, wa_ref, wb_ref, wo_ref, wr_ref, x1_ref, u2_ref, lg_ref):
    x = jnp.where(pl.program_id(0) < npt, xp_ref[...], xs_ref[...])
    o = of_ref[...] + ob_ref[...]
    parts = []
    for h in range(HG_HEADS):
        hs = slice(h * HG_KEY_DIM, (h + 1) * HG_KEY_DIM)
        parts.append(_rms(o[:, hs]))
    on = jnp.concatenate(parts, axis=1) * hgn_ref[...]
    gate = hg_ref[...].astype(F32)
    hb = (on * (gate * _sigmoid(gate))).astype(BF16)
    ya = jnp.dot(att_ref[...], wa_ref[...], preferred_element_type=F32)
    yb = jnp.dot(hb, wb_ref[...], preferred_element_type=F32)
    merged = (_sigmoid(ga_ref[...].astype(F32)) * ya
              + _sigmoid(gb_ref[...].astype(F32)) * yb)
    x1 = x + mod_ref[2:3, :] * jnp.dot(merged.astype(BF16), wo_ref[...],
                                       preferred_element_type=F32)
    x1_ref[...] = x1
    u2 = _rms(x1) * nffn_ref[...] * (1.0 + mod_ref[4:5, :]) + mod_ref[3:4, :]
    u2_ref[0], u2_ref[1] = _pack_rows(u2)
    w_hi = wr_ref[0]
    u_hi = u2.astype(BF16)
    u_lo = (u2 - u_hi.astype(F32)).astype(BF16)
    lg_ref[...] = (lax.dot_general(w_hi, u_hi, NT_DIMS, preferred_element_type=F32)
                   + (lax.dot_general(w_hi, u_lo, NT_DIMS, preferred_element_type=F32)
                      + lax.dot_general(wr_ref[1], u_hi, NT_DIMS, preferred_element_type=F32)))


def _split_hi_lo(w):
    hi = w.astype(BF16)
    return jnp.stack([hi, (w - hi.astype(F32)).astype(BF16)])


def _merge(seqs, x_p, x_s, att, o_f, o_b, proj, mod, hg_norm, norm_ffn, wa, wb, wo, wr):
    n = seqs.n
    tm = 256
    tok = lambda i: (i, 0)
    const = lambda i: (0, 0)
    npt, p_spec, s_spec = _two_stream_specs(seqs, tm)
    return pl.pallas_call(
        functools.partial(_merge_kernel, npt),
        grid=(n // tm,),
        in_specs=[p_spec, s_spec,
                  pl.BlockSpec((tm, NA_WIDTH), tok),
                  pl.BlockSpec((tm, HG_WIDTH), tok),
                  pl.BlockSpec((tm, HG_WIDTH), tok),
                  pl.BlockSpec((None, tm, HG_WIDTH), lambda i: (SLAB_HG[0], i, SLAB_HG[1])),
                  pl.BlockSpec((None, tm, D_MODEL), lambda i: (SLAB_GA, i, 0)),
                  pl.BlockSpec((None, tm, D_MODEL), lambda i: (SLAB_GB, i, 0)),
                  pl.BlockSpec((None, N_MOD, D_MODEL), lambda i: (seqs.info(i * tm)[0], 0, 0)),
                  pl.BlockSpec((1, HG_WIDTH), const),
                  pl.BlockSpec((1, D_MODEL), const),
                  pl.BlockSpec((NA_WIDTH, D_MODEL), const),
                  pl.BlockSpec((HG_WIDTH, D_MODEL), const),
                  pl.BlockSpec((D_MODEL, D_MODEL), const),
                  pl.BlockSpec((2, N_EXPERTS, D_MODEL), lambda i: (0, 0, 0))],
        out_specs=[pl.BlockSpec((tm, D_MODEL), tok),
                   pl.BlockSpec((2, tm, PACK_W), lambda i: (0, i, 0)),
                   pl.BlockSpec((N_EXPERTS, tm), lambda i: (0, i))],
        out_shape=[jax.ShapeDtypeStruct((n, D_MODEL), F32),
                   jax.ShapeDtypeStruct((2, n, PACK_W), jnp.uint32),
                   jax.ShapeDtypeStruct((N_EXPERTS, n), F32)],
        compiler_params=_params(48, ("arbitrary",)),
        name="merge",
    )(x_p, x_s, att, o_f, o_b, proj, proj, proj, mod, hg_norm.reshape(1, HG_WIDTH),
      norm_ffn.reshape(1, D_MODEL), wa, wb, wo, wr)


def _expert_kernel(be_ref, nused_ref, next_ref, slot_ref, xs_ref, wg_hbm, wu_hbm, wd_hbm, ys_ref,
                   wg_s, wu_s, wd_s, wg_st, wu_st, wd_st, sems):
    i = pl.program_id(0)

    def weight_copies(e, slot):
        pairs = ((wg_hbm, wg_st), (wu_hbm, wu_st), (wd_hbm, wd_st))
        return [pltpu.make_async_copy(hbm.at[e], stage.at[slot], sems.at[slot, j])
                for j, (hbm, stage) in enumerate(pairs)]

    @pl.when(i < nused_ref[0])
    def _():
        e = be_ref[i]

        @pl.when((i == 0) | (e != be_ref[jnp.maximum(i - 1, 0)]))
        def _():
            slot = slot_ref[i]

            @pl.when(i == 0)
            def _():
                for c in weight_copies(e, slot):
                    c.start()

            for c in weight_copies(e, slot):
                c.wait()
            nxt = next_ref[i]

            @pl.when(nxt >= 0)
            def _():
                for c in weight_copies(nxt, 1 - slot):
                    c.start()

            wg_s[...] = wg_st[slot].astype(BF16)
            wu_s[...] = wu_st[slot].astype(BF16)
            wd_s[...] = wd_st[slot].astype(BF16)

        xq = _unpack_rows(xs_ref[0], xs_ref[1])
        a = _dot_quarters(xq, wg_s)
        b = _dot_quarters(xq, wu_s)
        h = (a * _sigmoid(a) * b).astype(BF16)
        ys_ref[0], ys_ref[1] = _pack_rows(jnp.dot(h, wd_s[...], preferred_element_type=F32))


def _experts(xs, plan, wg, wu, wd):
    n_rows = xs.shape[1]
    n_blocks = n_rows // MOE_BLOCK
    rows_map = lambda i, be, nu, nx, sl: (0, jnp.minimum(i, nu[0] - 1), 0)
    whole = pl.BlockSpec(memory_space=pl.ANY)
    grid_spec = pltpu.PrefetchScalarGridSpec(
        num_scalar_prefetch=4,
        grid=(n_blocks,),
        in_specs=[pl.BlockSpec((2, MOE_BLOCK, PACK_W), rows_map), whole, whole, whole],
        out_specs=pl.BlockSpec((2, MOE_BLOCK, PACK_W), rows_map),
        scratch_shapes=[pltpu.VMEM((D_MODEL, D_EXPERT), BF16),
                        pltpu.VMEM((D_MODEL, D_EXPERT), BF16),
                        pltpu.VMEM((D_EXPERT, D_MODEL), BF16),
                        pltpu.VMEM((2, D_MODEL, D_EXPERT), F32),
                        pltpu.VMEM((2, D_MODEL, D_EXPERT), F32),
                        pltpu.VMEM((2, D_EXPERT, D_MODEL), F32),
                        pltpu.SemaphoreType.DMA((2, 3))],
    )
    return pl.pallas_call(
        _expert_kernel,
        grid_spec=grid_spec,
        out_shape=jax.ShapeDtypeStruct((2, n_rows, PACK_W), jnp.uint32),
        compiler_params=_params(32, ("arbitrary",)),
        name="experts",
    )(*plan, xs, wg, wu, wd)


ROUTE_TILE = 128
ROUTE_STEP = 512
NEG_INF = float("-inf")


def _first_max(x, idx, big):
    m = jnp.max(x, axis=0, keepdims=True)
    first = jnp.min(jnp.where(x == m, idx, big), axis=0, keepdims=True)
    return m, first


def _route_tile(lg, bias, tri, base):
    per_group = N_EXPERTS // N_GROUPS
    scores = jax.nn.sigmoid(lg)
    sel = scores + bias
    lrow = lax.broadcasted_iota(jnp.int32, (per_group, ROUTE_TILE), 0)
    gs = []
    for g in range(N_GROUPS):
        x = sel[g * per_group:(g + 1) * per_group]
        m1, first = _first_max(x, lrow, per_group)
        m2 = jnp.max(jnp.where(lrow == first, NEG_INF, x), axis=0, keepdims=True)
        gs.append(m1 + m2)
    cur = jnp.concatenate(gs, axis=0)
    grow = lax.broadcasted_iota(jnp.int32, (N_GROUPS, ROUTE_TILE), 0)
    chosen = jnp.zeros((N_GROUPS, ROUTE_TILE), jnp.int32)
    for _ in range(TOPK_GROUPS):
        _, first = _first_max(cur, grow, N_GROUPS)
        hit = grow == first
        chosen = jnp.where(hit, 1, chosen)
        cur = jnp.where(hit, NEG_INF, cur)
    cur = jnp.concatenate(
        [jnp.where(chosen[g:g + 1] > 0, sel[g * per_group:(g + 1) * per_group], NEG_INF)
         for g in range(N_GROUPS)], axis=0)
    row = lax.broadcasted_iota(jnp.int32, (N_EXPERTS, ROUTE_TILE), 0)
    member = jnp.zeros((N_EXPERTS, ROUTE_TILE), F32)
    es, ws = [], []
    for _ in range(TOP_K):
        _, first = _first_max(cur, row, N_EXPERTS)
        hit = row == first
        es.append(first)
        ws.append(jnp.sum(jnp.where(hit, scores, 0.0), axis=0, keepdims=True))
        cur = jnp.where(hit, NEG_INF, cur)
        member = jnp.where(hit, 1.0, member)
    e = jnp.concatenate(es, axis=0)
    w = jnp.concatenate(ws, axis=0)
    w = w / jnp.sum(w, axis=0, keepdims=True) * ROUTE_SCALE
    before = jnp.dot(member.astype(BF16), tri, preferred_element_type=F32) + base
    rank = jnp.concatenate(
        [jnp.sum(jnp.where(row == es[k], before, 0.0), axis=0, keepdims=True) for k in range(TOP_K)],
        axis=0)
    return e, w, rank.astype(jnp.int32), member


def _route_kernel(lg_ref, b_ref, e_ref, w_ref, r_ref, cnt_ref, base):
    @pl.when(pl.program_id(0) == 0)
    def _():
        base[...] = jnp.zeros_like(base)

    r_i = lax.broadcasted_iota(jnp.int32, (ROUTE_TILE, ROUTE_TILE), 0)
    c_i = lax.broadcasted_iota(jnp.int32, (ROUTE_TILE, ROUTE_TILE), 1)
    tri = (r_i < c_i).astype(BF16)
    bias = b_ref[...]
    for j in range(ROUTE_STEP // ROUTE_TILE):
        ls = slice(j * ROUTE_TILE, (j + 1) * ROUTE_TILE)
        e, w, rank, member = _route_tile(lg_ref[:, ls], bias, tri, base[...])
        e_ref[:, ls] = e
        w_ref[:, ls] = w
        r_ref[:, ls] = rank
        base[...] = base[...] + jnp.sum(member, axis=1, keepdims=True)
    cnt_ref[...] = base[...]


def _route(logits_t, b_router):
    n = logits_t.shape[1]
    tok = lambda i: (0, i)
    return pl.pallas_call(
        _route_kernel,
        grid=(n // ROUTE_STEP,),
        in_specs=[pl.BlockSpec((N_EXPERTS, ROUTE_STEP), tok),
                  pl.BlockSpec((N_EXPERTS, 1), lambda i: (0, 0))],
        out_specs=[pl.BlockSpec((TOP_K, ROUTE_STEP), tok),
                   pl.BlockSpec((TOP_K, ROUTE_STEP), tok),
                   pl.BlockSpec((TOP_K, ROUTE_STEP), tok),
                   pl.BlockSpec((N_EXPERTS, 1), lambda i: (0, 0))],
        out_shape=[jax.ShapeDtypeStruct((TOP_K, n), jnp.int32),
                   jax.ShapeDtypeStruct((TOP_K, n), F32),
                   jax.ShapeDtypeStruct((TOP_K, n), jnp.int32),
                   jax.ShapeDtypeStruct((N_EXPERTS, 1), F32)],
        scratch_shapes=[pltpu.VMEM((N_EXPERTS, 1), F32)],
        compiler_params=_params(32, ("arbitrary",)),
        name="route",
    )(logits_t, b_router.reshape(N_EXPERTS, 1).astype(F32))


def _pos_kernel(e_ref, r_ref, ps_ref, pos_ref):
    row = lax.broadcasted_iota(jnp.int32, (N_EXPERTS, ROUTE_TILE), 0)
    pstart = ps_ref[...]
    for j in range(ROUTE_STEP // ROUTE_TILE):
        ls = slice(j * ROUTE_TILE, (j + 1) * ROUTE_TILE)
        e = e_ref[:, ls]
        off = jnp.concatenate(
            [jnp.sum(jnp.where(row == e[k:k + 1], pstart, 0.0), axis=0, keepdims=True)
             for k in range(TOP_K)], axis=0)
        pos_ref[:, ls] = off.astype(jnp.int32) + r_ref[:, ls]


def _positions(eidx, rank, pstarts):
    n = eidx.shape[1]
    tok = lambda i: (0, i)
    return pl.pallas_call(
        _pos_kernel,
        grid=(n // ROUTE_STEP,),
        in_specs=[pl.BlockSpec((TOP_K, ROUTE_STEP), tok),
                  pl.BlockSpec((TOP_K, ROUTE_STEP), tok),
                  pl.BlockSpec((N_EXPERTS, 1), lambda i: (0, 0))],
        out_specs=pl.BlockSpec((TOP_K, ROUTE_STEP), tok),
        out_shape=jax.ShapeDtypeStruct((TOP_K, n), jnp.int32),
        compiler_params=_params(32, ("arbitrary",)),
        name="positions",
    )(eidx, rank, pstarts.reshape(N_EXPERTS, 1).astype(F32))


def _block_tables(counts, n_blocks):
    counts = counts.reshape(N_EXPERTS).astype(jnp.int32)
    padded = (counts + MOE_BLOCK - 1) // MOE_BLOCK * MOE_BLOCK
    pend = jnp.cumsum(padded)
    pstarts = pend - padded
    first_row = jnp.arange(n_blocks, dtype=jnp.int32) * MOE_BLOCK
    block_e = jnp.sum((pend[None, :] <= first_row[:, None]).astype(jnp.int32), axis=1)
    block_e = jnp.minimum(block_e, N_EXPERTS - 1).astype(jnp.int32)
    n_used = (pend[-1] // MOE_BLOCK).astype(jnp.int32).reshape(1)
    ids = jnp.arange(N_EXPERTS, dtype=jnp.int32)
    used = padded > 0
    ordinal = jnp.cumsum(used.astype(jnp.int32)) - 1
    from_here = jnp.flip(lax.cummin(jnp.flip(jnp.where(used, ids, N_EXPERTS))))
    after = jnp.concatenate([from_here[1:], jnp.full((1,), N_EXPERTS, jnp.int32)])
    after = jnp.where(after >= N_EXPERTS, -1, after)
    mine = block_e[:, None] == ids[None, :]
    next_e = jnp.sum(jnp.where(mine, after[None, :], 0), axis=1).astype(jnp.int32)
    slot = jnp.sum(jnp.where(mine, ordinal[None, :] % 2, 0), axis=1).astype(jnp.int32)
    return pstarts, (block_e, n_used, next_e, slot)


SC_WINDOW = 128


def _sc_mesh():
    return plsc.VectorSubcoreMesh(core_axis_name="core", subcore_axis_name="subcore")


def _both_halves(pos, n_rows):
    return jnp.concatenate([pos, pos + n_rows]).reshape(1, -1)


def _sc_dispatch(rows, pos, n_rows):
    _, n, w = rows.shape
    tiles = n // SC_WINDOW
    top_k = pos.shape[0] // n
    idx = jnp.transpose(pos.reshape(top_k, tiles, SC_WINDOW), (1, 0, 2))
    idx = jnp.stack([idx, idx + n_rows]).reshape(1, -1)

    @functools.partial(pl.kernel, out_type=jax.ShapeDtypeStruct((2 * n_rows, w), rows.dtype),
                       mesh=_sc_mesh(), scratch_types=[])
    def scatter_kernel(x_hbm, i_hbm, o_hbm):
        def body(x_vmem, i_vmem):
            pltpu.sync_copy(x_vmem, o_hbm.at[i_vmem.at[0]])

        pltpu.emit_pipeline(
            body,
            grid=(2 * tiles * top_k,),
            in_specs=[pl.BlockSpec((SC_WINDOW, w), lambda i: (i // top_k, 0)),
                      pl.BlockSpec((1, SC_WINDOW), lambda i: (0, i))],
            out_specs=[],
            core_axis_name=("core", "subcore"),
            dimension_semantics=(pltpu.PARALLEL,),
        )(x_hbm, i_hbm)

    out = scatter_kernel(rows.reshape(2 * n, w), idx)
    return out.reshape(2, n_rows, w)


def _sc_gather(table, pos):
    _, n_rows, w = table.shape
    m = pos.shape[0]

    @functools.partial(pl.kernel, out_type=jax.ShapeDtypeStruct((2 * m, w), table.dtype),
                       mesh=_sc_mesh(), scratch_types=[])
    def gather_kernel(t_hbm, i_hbm, o_hbm):
        def body(i_vmem, o_vmem):
            pltpu.sync_copy(t_hbm.at[i_vmem.at[0]], o_vmem)

        pltpu.emit_pipeline(
            body,
            grid=(2 * m // SC_WINDOW,),
            in_specs=[pl.BlockSpec((1, SC_WINDOW), lambda i: (0, i))],
            out_specs=[pl.BlockSpec((SC_WINDOW, w), lambda i: (i, 0))],
            core_axis_name=("core", "subcore"),
            dimension_semantics=(pltpu.PARALLEL,),
        )(i_hbm, o_hbm)

    out = gather_kernel(table.reshape(2 * n_rows, w), _both_halves(pos, n_rows))
    return out.reshape(2, m, w)


def _final_kernel(npt, x1_ref, u2_ref, yg_ref, w_ref, mod_ref, nf_ref, wsg_ref, wsu_ref, wsd_ref,
                  op_ref, os_ref):
    uq = _unpack_rows(u2_ref[0], u2_ref[1])
    a = _dot_quarters(uq, wsg_ref)
    b = _dot_quarters(uq, wsu_ref)
    f = jnp.dot((a * _sigmoid(a) * b).astype(BF16), wsd_ref[...], preferred_element_type=F32)
    w = w_ref[...]
    fq = [f[:, q * PACK_W:(q + 1) * PACK_W] for q in range(4)]
    for k in range(TOP_K):
        yq = _unpack_rows(yg_ref[0, k], yg_ref[1, k], F32)
        fq = [fq[q] + w[:, k:k + 1] * yq[q] for q in range(4)]
    x2 = x1_ref[...] + mod_ref[5:6, :] * jnp.concatenate(fq, axis=1)
    out = _rms(x2) * nf_ref[...]
    i = pl.program_id(0)

    @pl.when(i < npt)
    def _():
        op_ref[...] = out

    @pl.when(i >= npt)
    def _():
        os_ref[...] = out


def _final(seqs, x1, u2, yg, w, mod, norm_final, wsg, wsu, wsd):
    n = seqs.n
    tm = 256
    tok = lambda i: (i, 0)
    const = lambda i: (0, 0)
    npt, p_spec, s_spec = _two_stream_specs(seqs, tm)
    return pl.pallas_call(
        functools.partial(_final_kernel, npt),
        grid=(n // tm,),
        in_specs=[pl.BlockSpec((tm, D_MODEL), tok),
                  pl.BlockSpec((2, tm, PACK_W), lambda i: (0, i, 0)),
                  pl.BlockSpec((2, TOP_K, tm, PACK_W), lambda i: (0, 0, i, 0)),
                  pl.BlockSpec((tm, TOP_K), tok),
                  pl.BlockSpec((None, N_MOD, D_MODEL), lambda i: (seqs.info(i * tm)[0], 0, 0)),
                  pl.BlockSpec((1, D_MODEL), const),
                  pl.BlockSpec((D_MODEL, D_EXPERT), const),
                  pl.BlockSpec((D_MODEL, D_EXPERT), const),
                  pl.BlockSpec((D_EXPERT, D_MODEL), const)],
        out_specs=[p_spec, s_spec],
        out_shape=[jax.ShapeDtypeStruct((seqs.np_, D_MODEL), F32),
                   jax.ShapeDtypeStruct((n - seqs.np_, D_MODEL), F32)],
        compiler_params=_params(48, ("arbitrary",)),
        name="final",
    )(x1, u2, yg, w, mod, norm_final.reshape(1, D_MODEL), wsg, wsu, wsd)


def _layer(seqs, x_p, x_s, c, w_ada, b_ada, norm_mix, w_in, na_rpb, hg_lb, hg_norm, w_branch_a,
           w_branch_b, w_out, norm_ffn, w_router, b_router, w_exp_gate, w_exp_up, w_exp_down,
           w_sh_gate, w_sh_up, w_sh_down, norm_final):
    n = seqs.n
    c_rows = -(-seqs.nseq // 8) * 8
    c_pad = jnp.zeros((c_rows, D_MODEL), F32).at[:seqs.nseq].set(c)
    mod = _ada(c_pad, w_ada[0], b_ada[0])[:seqs.nseq].reshape(seqs.nseq, N_MOD, D_MODEL)
    lb = jnp.cumsum(jax.nn.softmax(hg_lb.astype(F32), axis=0), axis=0)[0]

    proj = _inproj(seqs, x_p, x_s, mod, norm_mix[0], w_in[0].astype(BF16))
    att = _na(seqs, proj, _na_bias_table(na_rpb[0]))
    o_f, o_b = _hgrn(seqs, proj, lb)
    x1, u2, logits = _merge(seqs, x_p, x_s, att, o_f, o_b, proj, mod, hg_norm[0], norm_ffn[0],
                            w_branch_a[0].astype(BF16), w_branch_b[0].astype(BF16),
                            w_out[0].astype(BF16), _split_hi_lo(w_router[0].T))

    eidx, w, rank, counts = _route(logits, b_router[0])
    n_rows = n * TOP_K + N_EXPERTS * MOE_BLOCK
    pstarts, plan = _block_tables(counts, n_rows // MOE_BLOCK)
    pos = _positions(eidx, rank, pstarts).reshape(-1)
    xs = _sc_dispatch(u2, pos, n_rows)
    ys = _experts(xs, plan, w_exp_gate[0], w_exp_up[0], w_exp_down[0])
    yg = _sc_gather(ys, pos).reshape(2, TOP_K, n, PACK_W)
    return _final(seqs, x1, u2, yg, w.T, mod, norm_final, w_sh_gate[0].astype(BF16),
                  w_sh_up[0].astype(BF16), w_sh_down[0].astype(BF16))


def kernel(x_prompt, x_sample, c_prompt, c_sample, w_ada, b_ada, norm_mix, w_in, na_rpb, hg_lb, hg_norm, w_branch_a, w_branch_b, w_out, norm_ffn, w_router, b_router, w_exp_gate, w_exp_up, w_exp_down, w_sh_gate, w_sh_up, w_sh_down, norm_final):
    bp, tp, _ = x_prompt.shape
    bs, ts, _ = x_sample.shape
    seqs = _Seqs(bp, tp, bs, ts)
    c = jnp.concatenate([c_prompt, c_sample])
    y_p, y_s = _layer(seqs, x_prompt.reshape(bp * tp, D_MODEL), x_sample.reshape(bs * ts, D_MODEL),
                      c, w_ada, b_ada, norm_mix, w_in, na_rpb, hg_lb, hg_norm, w_branch_a,
                      w_branch_b, w_out, norm_ffn, w_router, b_router, w_exp_gate, w_exp_up,
                      w_exp_down, w_sh_gate, w_sh_up, w_sh_down, norm_final)
    return (y_p.reshape(bp, tp, D_MODEL), y_s.reshape(bs, ts, D_MODEL))
```

```python
import functools

import jax
import jax.numpy as jnp
import numpy as np
from jax import lax
from jax.experimental import pallas as pl
from jax.experimental.pallas import tpu as pltpu
from jax.experimental.pallas import tpu_sc as plsc

D_MODEL = 1024
GRID_W = 64
NA_HEADS = 8
NA_HEAD_DIM = 64
NA_WIDTH = NA_HEADS * NA_HEAD_DIM
NA_ROWS = 8
NA_COLS = 16
HG_HEADS = 4
HG_KEY_DIM = 128
HG_WIDTH = HG_HEADS * HG_KEY_DIM
HG_CHUNK = 128
HG_EXP_LIMIT = 80.0
N_EXPERTS = 256
TOP_K = 8
N_GROUPS = 8
TOPK_GROUPS = 4
D_EXPERT = 256
ROUTE_SCALE = 2.5
N_MOD = 6
RMS_EPS = 1e-6

MOE_BLOCK = 512
NA_GROUP = 4
NA_WIN = 3 * NA_GROUP
NA_TOK = NA_GROUP * GRID_W
LOG2E = 1.4426950408889634
NA_Q_SCALE = NA_HEAD_DIM ** -0.5 * LOG2E
HG_STEP = 256
MASK_VALUE = -1e30

F32 = jnp.float32
BF16 = jnp.bfloat16
HIGHEST = lax.Precision.HIGHEST
NT_DIMS = (((1,), (1,)), ((), ()))
TN_DIMS = (((0,), (0,)), ((), ()))

PROJ_TN = 2048
SLAB_Q, SLAB_K, SLAB_V, SLAB_HQ = (0, 0), (0, 1), (0, 2), (0, 3)
SLAB_FF, SLAB_FB, SLAB_HI, SLAB_HG = (1, 0), (1, 1), (1, 2), (1, 3)
SLAB_GA, SLAB_GB = (2, 0), (2, 1)


def _params(vmem_mb, sem=None):
    kw = dict(vmem_limit_bytes=vmem_mb * 1024 * 1024)
    if sem is not None:
        kw["dimension_semantics"] = sem
    return pltpu.CompilerParams(**kw)


class _Seqs:
    def __init__(self, bp, tp, bs, ts):
        self.bp, self.tp, self.bs, self.ts = bp, tp, bs, ts
        self.np_ = bp * tp
        self.n = bp * tp + bs * ts
        self.nseq = bp + bs

    def info(self, t0):
        in_p = t0 < self.np_
        rel = jnp.maximum(t0 - self.np_, 0)
        sid = jnp.where(in_p, t0 // self.tp, self.bp + rel // self.ts)
        start = jnp.where(in_p, (t0 // self.tp) * self.tp, self.np_ + (rel // self.ts) * self.ts)
        length = jnp.where(in_p, self.tp, self.ts)
        return sid, start, length


def _ada_kernel(c_ref, w_ref, b_ref, o_ref):
    c = c_ref[...]
    a = c * jax.nn.sigmoid(c)
    o_ref[...] = jnp.dot(a, w_ref[...], precision=HIGHEST, preferred_element_type=F32) + b_ref[...]


def _ada(c_pad, w_ada, b_ada):
    rows = c_pad.shape[0]
    n_out = w_ada.shape[1]
    tn = 1024
    return pl.pallas_call(
        _ada_kernel,
        grid=(n_out // tn,),
        in_specs=[pl.BlockSpec((rows, D_MODEL), lambda j: (0, 0)),
                  pl.BlockSpec((D_MODEL, tn), lambda j: (0, j)),
                  pl.BlockSpec((1, tn), lambda j: (0, j))],
        out_specs=pl.BlockSpec((rows, tn), lambda j: (0, j)),
        out_shape=jax.ShapeDtypeStruct((rows, n_out), F32),
        compiler_params=_params(32),
        name="ada",
    )(c_pad, w_ada, b_ada.reshape(1, n_out))


def _rms(x):
    return x * lax.rsqrt(jnp.mean(x * x, axis=-1, keepdims=True) + RMS_EPS)


def _sigmoid(x):
    return 0.5 * jnp.tanh(0.5 * x) + 0.5


PACK_W = D_MODEL // 4


def _pack_rows(x):
    out = []
    for h in range(2):
        lo = x[:, (2 * h) * PACK_W:(2 * h + 1) * PACK_W].astype(BF16).astype(F32)
        hi = x[:, (2 * h + 1) * PACK_W:(2 * h + 2) * PACK_W].astype(BF16).astype(F32)
        out.append(lax.bitcast_convert_type(hi, jnp.uint32)
                   | (lax.bitcast_convert_type(lo, jnp.uint32) >> 16))
    return out


def _unpack_rows(p0, p1, dtype=BF16):
    quarters = []
    for p in (p0, p1):
        quarters.append(lax.bitcast_convert_type(p << 16, F32).astype(dtype))
        quarters.append(lax.bitcast_convert_type(p & jnp.uint32(0xFFFF0000), F32).astype(dtype))
    return quarters


def _dot_quarters(quarters, w_ref):
    acc = None
    for q, xq in enumerate(quarters):
        part = jnp.dot(xq, w_ref[q * PACK_W:(q + 1) * PACK_W, :], preferred_element_type=F32)
        acc = part if acc is None else acc + part
    return acc


def _two_stream_specs(seqs, tm, grid_rank=1):
    npt = seqs.np_ // tm
    nst = (seqs.n - seqs.np_) // tm
    if grid_rank == 1:
        p_map = lambda i: (jnp.minimum(i, npt - 1), 0)
        s_map = lambda i: (jnp.clip(i - npt, 0, nst - 1), 0)
    else:
        p_map = lambda i, j: (jnp.minimum(i, npt - 1), 0)
        s_map = lambda i, j: (jnp.clip(i - npt, 0, nst - 1), 0)
    return npt, pl.BlockSpec((tm, D_MODEL), p_map), pl.BlockSpec((tm, D_MODEL), s_map)


def _inproj_kernel(npt, xp_ref, xs_ref, mod_ref, g_ref, w_ref, cs_ref, o_ref, u_scr):
    @pl.when(pl.program_id(1) == 0)
    def _():
        x = jnp.where(pl.program_id(0) < npt, xp_ref[...], xs_ref[...])
        y = _rms(x) * g_ref[...]
        u = y * (1.0 + mod_ref[1:2, :]) + mod_ref[0:1, :]
        u_scr[...] = u.astype(BF16)

    acc = jnp.dot(u_scr[...], w_ref[...], preferred_element_type=F32)
    o_ref[...] = (acc * cs_ref[...]).astype(o_ref.dtype)


def _inproj(seqs, x_p, x_s, mod, norm_mix, w_in_bf):
    n = seqs.n
    tm = min(1024, seqs.tp, seqs.ts)
    tn = PROJ_TN
    n_slab = w_in_bf.shape[1] // tn
    npt, p_spec, s_spec = _two_stream_specs(seqs, tm, grid_rank=2)
    col_scale = jnp.ones((1, w_in_bf.shape[1]), F32).at[:, :NA_WIDTH].set(NA_Q_SCALE)
    return pl.pallas_call(
        functools.partial(_inproj_kernel, npt),
        grid=(n // tm, n_slab),
        in_specs=[p_spec, s_spec,
                  pl.BlockSpec((None, N_MOD, D_MODEL), lambda i, j: (seqs.info(i * tm)[0], 0, 0)),
                  pl.BlockSpec((1, D_MODEL), lambda i, j: (0, 0)),
                  pl.BlockSpec((D_MODEL, tn), lambda i, j: (0, j)),
                  pl.BlockSpec((1, tn), lambda i, j: (0, j))],
        out_specs=pl.BlockSpec((None, tm, tn), lambda i, j: (j, i, 0)),
        out_shape=jax.ShapeDtypeStruct((n_slab, n, tn), BF16),
        scratch_shapes=[pltpu.VMEM((tm, D_MODEL), BF16)],
        compiler_params=_params(56, ("arbitrary", "arbitrary")),
        name="inproj",
    )(x_p, x_s, mod, norm_mix.reshape(1, D_MODEL), w_in_bf, col_scale)


def _na_bias_table(rpb):
    col = np.arange(GRID_W)
    cs = np.clip(col - NA_COLS // 2, 0, GRID_W - NA_COLS)
    valid = (col[None, :] >= cs[:, None]) & (col[None, :] < cs[:, None] + NA_COLS)
    coff = col[None, :] - col[:, None] + NA_COLS - 1
    onehot = (coff[None] == np.arange(2 * NA_COLS - 1)[:, None, None]) & valid[None]
    toep = jnp.einsum("hrc,cqk->hrqk", rpb.astype(F32), jnp.asarray(onehot, F32),
                      precision=HIGHEST)
    toep = jnp.where(valid[None, None], toep * LOG2E, MASK_VALUE)
    masked =jnp.full((NA_HEADS, GRID_W, GRID_W), MASK_VALUE, F32)
    cases = (([0] * NA_GROUP, [NA_ROWS - 1 - i for i in range(NA_GROUP)]),
             (list(range(NA_GROUP)), [NA_ROWS // 2 - 1] * NA_GROUP),
             ([NA_GROUP] * NA_GROUP, [NA_ROWS // 2 - 1 - i for i in range(NA_GROUP)]))
    tabs = []
    for first_row, first_off in cases:
        q_rows = []
        for i in range(NA_GROUP):
            blocks = [toep[:, first_off[i] + w - first_row[i]]
                      if 0 <= w - first_row[i] < NA_ROWS else masked for w in range(NA_WIN)]
            q_rows.append(jnp.concatenate(blocks, axis=2))
        tabs.append(jnp.concatenate(q_rows, axis=1))
    return jnp.stack(tabs)


def _na_geometry(seqs, g):
    _, start, length = seqs.info(g * NA_TOK)
    row0 = start // GRID_W
    rows = length // GRID_W
    r0 = g * NA_GROUP - row0
    wb = jnp.clip(r0 - NA_ROWS // 2, 0, rows - NA_WIN)
    case = jnp.where(r0 == 0, 0, jnp.where(r0 == rows - NA_GROUP, 2, 1))
    return (row0 + wb) // NA_GROUP, case


def _na_kernel(q_ref, k0, k1, k2, v0, v1, v2, bias_ref, o_ref):
    k_refs = (k0, k1, k2)
    v_refs = (v0, v1, v2)
    outs = []
    for h in range(NA_HEADS):
        hs = slice(h * NA_HEAD_DIM, (h + 1) * NA_HEAD_DIM)
        q = q_ref[:, hs]
        s = [lax.dot_general(q, kr[:, hs], NT_DIMS, preferred_element_type=F32)
             + bias_ref[h, :, d * NA_TOK:(d + 1) * NA_TOK] for d, kr in enumerate(k_refs)]
        m = jnp.max(jnp.maximum(jnp.maximum(s[0], s[1]), s[2]), axis=-1, keepdims=True)
        p = [jnp.exp2(sd - m) for sd in s]
        l = jnp.sum((p[0] + p[1]) + p[2], axis=-1, keepdims=True)
        o = sum(jnp.dot(pd.astype(BF16), vr[:, hs], preferred_element_type=F32)
                for pd, vr in zip(p, v_refs))
        outs.append(o / l)
    o_ref[...] = jnp.concatenate(outs, axis=1).astype(o_ref.dtype)


def _na(seqs, proj, bias_tab):
    n = seqs.n

    def kv_spec(slab, d):
        return pl.BlockSpec((None, NA_TOK, NA_WIDTH),
                            lambda g: (slab[0], _na_geometry(seqs, g)[0] + d, slab[1]))

    return pl.pallas_call(
        _na_kernel,
        grid=(n // NA_TOK,),
        in_specs=[pl.BlockSpec((None, NA_TOK, NA_WIDTH), lambda g: (SLAB_Q[0], g, SLAB_Q[1]))]
        + [kv_spec(SLAB_K, d) for d in range(3)] + [kv_spec(SLAB_V, d) for d in range(3)]
        + [pl.BlockSpec((None,) + bias_tab.shape[1:],
                        lambda g: (_na_geometry(seqs, g)[1], 0, 0, 0))],
        out_specs=pl.BlockSpec((NA_TOK, NA_WIDTH), lambda g: (g, 0)),
        out_shape=jax.ShapeDtypeStruct((n, NA_WIDTH), BF16),
        compiler_params=_params(48, ("arbitrary",)),
        name="natten",
    )(proj, proj, proj, proj, proj, proj, proj, bias_tab)


def _hg_chunk(q, z, v, lb, tri, mask, mid, last, st_ref):
    sig = _sigmoid(z)
    f = lb + (1.0 - lb) * sig
    lf = jnp.log(f)
    kin = (1.0 - lb) * (1.0 - sig)
    hi = lf.astype(BF16)
    lo = (lf - hi.astype(F32)).astype(BF16)
    g2 = jnp.dot(tri, jnp.concatenate([lo, hi], axis=1), preferred_element_type=F32)
    gcum = g2[:, :HG_WIDTH] + g2[:, HG_WIDTH:]
    gm = gcum[mid:mid + 1, :]
    gl = gcum[last:last + 1, :]
    up = jnp.exp(gcum - gm)
    dn = jnp.exp(gm - gcum)
    qa = (q * up).astype(BF16)
    ka = (kin * dn).astype(BF16)
    qe = (q * (up * jnp.exp(gm))).astype(BF16)
    kd = (kin * (dn * jnp.exp(gl - gm))).astype(BF16)
    eg = jnp.exp(gl)
    vb = v.astype(BF16)
    outs = []
    for h in range(HG_HEADS):
        hs = slice(h * HG_KEY_DIM, (h + 1) * HG_KEY_DIM)
        a = lax.dot_general(qa[:, hs], ka[:, hs], NT_DIMS, preferred_element_type=F32)
        a = jnp.where(mask, a, 0.0)
        st = st_ref[h]
        o = jnp.dot(a.astype(BF16), vb[:, hs], preferred_element_type=F32)
        o = o + lax.dot_general(qe[:, hs], st.astype(BF16), NT_DIMS, preferred_element_type=F32)
        st_ref[h] = st * eg[:, hs] + lax.dot_general(vb[:, hs], kd[:, hs], TN_DIMS,
                                                    preferred_element_type=F32)
        outs.append(o)
    return jnp.concatenate(outs, axis=1)


def _hg_exact(q_ref, z_ref, v_ref, lb, reverse, st_ref, o_ref, qs, fs, ks, vs):
    sig = jax.nn.sigmoid(z_ref[...].astype(F32))
    qs[...] = q_ref[...].astype(F32)
    fs[...] = lb + (1.0 - lb) * sig
    ks[...] = (1.0 - lb) * (1.0 - sig)
    vs[...] = v_ref[...].astype(F32)
    eye = (lax.broadcasted_iota(jnp.int32, (HG_KEY_DIM, HG_KEY_DIM), 0)
           == lax.broadcasted_iota(jnp.int32, (HG_KEY_DIM, HG_KEY_DIM), 1)).astype(F32)

    def body(i, carry):
        t = HG_STEP - 1 - i if reverse else i
        q_t, f_t, k_t, v_t = (r[pl.ds(t, 1), :] for r in (qs, fs, ks, vs))
        outs = []
        for h in range(HG_HEADS):
            hs = slice(h * HG_KEY_DIM, (h + 1) * HG_KEY_DIM)
            v_col = jnp.sum(eye * v_t[:, hs], axis=1, keepdims=True)
            st = st_ref[h] * f_t[:, hs] + v_col * k_t[:, hs]
            st_ref[h] = st
            o_col = jnp.sum(st * q_t[:, hs], axis=1, keepdims=True)
            outs.append(jnp.sum(eye * o_col, axis=0, keepdims=True))
        o_ref[pl.ds(t, 1), :] = jnp.concatenate(outs, axis=1)
        return carry

    lax.fori_loop(0, HG_STEP, body, 0)


def _hg_kernel(seqs, safe_ref, qf_ref, zf_ref, vf_ref, qb_ref, zb_ref, vb_ref, lb_ref, of_ref,
               ob_ref, stf, stb, qs, fs, ks, vs):
    i = pl.program_id(0)
    nsteps = pl.num_programs(0)
    tf = i * HG_STEP
    tb = (nsteps - 1 - i) * HG_STEP
    _, start_f, _ = seqs.info(tf)
    _, start_b, len_b = seqs.info(tb)

    @pl.when(tf == start_f)
    def _():
        stf[...] = jnp.zeros_like(stf)

    @pl.when(tb + HG_STEP == start_b + len_b)
    def _():
        stb[...] = jnp.zeros_like(stb)

    lb = lb_ref[...]

    @pl.when(safe_ref[0] > 0)
    def _():
        row = lax.broadcasted_iota(jnp.int32, (HG_CHUNK, HG_CHUNK), 0)
        col = lax.broadcasted_iota(jnp.int32, (HG_CHUNK, HG_CHUNK), 1)
        lower = row >= col
        upper = col >= row
        tri_f = lower.astype(BF16)
        tri_b = upper.astype(BF16)
        nchunk = HG_STEP // HG_CHUNK
        for c in range(nchunk):
            cs = slice(c * HG_CHUNK, (c + 1) * HG_CHUNK)
            of_ref[cs, :] = _hg_chunk(qf_ref[cs, :].astype(F32), zf_ref[cs, :].astype(F32),
                                      vf_ref[cs, :].astype(F32), lb, tri_f, lower,
                                      HG_CHUNK // 2 - 1, HG_CHUNK - 1, stf)
            cb = nchunk - 1 - c
            bs = slice(cb * HG_CHUNK, (cb + 1) * HG_CHUNK)
            ob_ref[bs, :] = _hg_chunk(qb_ref[bs, :].astype(F32), zb_ref[bs, :].astype(F32),
                                      vb_ref[bs, :].astype(F32), lb, tri_b, upper,
                                      HG_CHUNK // 2, 0, stb)

    @pl.when(safe_ref[0] == 0)
    def _():
        _hg_exact(qf_ref, zf_ref, vf_ref, lb, False, stf, of_ref, qs, fs, ks, vs)
        _hg_exact(qb_ref, zb_ref, vb_ref, lb, True, stb, ob_ref, qs, fs, ks, vs)


def _hgrn(seqs, proj, lb):
    n = seqs.n
    nsteps = n // HG_STEP

    def spec(slab, rev):
        if rev:
            return pl.BlockSpec((None, HG_STEP, HG_WIDTH), lambda i: (slab[0], nsteps - 1 - i, slab[1]))
        return pl.BlockSpec((None, HG_STEP, HG_WIDTH), lambda i: (slab[0], i, slab[1]))

    safe = (jnp.max(-jnp.log(lb)) * (HG_CHUNK // 2) < HG_EXP_LIMIT).astype(jnp.int32).reshape(1)
    step_scratch = pltpu.VMEM((HG_STEP, HG_WIDTH), F32)
    return pl.pallas_call(
        functools.partial(_hg_kernel, seqs),
        grid=(nsteps,),
        in_specs=[pl.BlockSpec(memory_space=pltpu.SMEM),
                  spec(SLAB_HQ, False), spec(SLAB_FF, False), spec(SLAB_HI, False),
                  spec(SLAB_HQ, True), spec(SLAB_FB, True), spec(SLAB_HI, True),
                  pl.BlockSpec((1, HG_WIDTH), lambda i: (0, 0))],
        out_specs=[pl.BlockSpec((HG_STEP, HG_WIDTH), lambda i: (i, 0)),
                   pl.BlockSpec((HG_STEP, HG_WIDTH), lambda i: (nsteps - 1 - i, 0))],
        out_shape=[jax.ShapeDtypeStruct((n, HG_WIDTH), F32)] * 2,
        scratch_shapes=[pltpu.VMEM((HG_HEADS, HG_KEY_DIM, HG_KEY_DIM), F32)] * 2
        + [step_scratch] * 4,
        compiler_params=_params(32, ("arbitrary",)),
        name="hgrn2",
    )(safe, proj, proj, proj, proj, proj, proj, lb.reshape(1, HG_WIDTH))


def _merge_kernel(npt, xp_ref, xs_ref, att_ref, of_ref, ob_ref, hg_ref, ga_ref, gb_ref, mod_ref,
                  hgn_ref, nffn_ref, wa_ref, wb_ref, wo_ref, wr_ref, x1_ref, u2_ref, lg_ref):
    x = jnp.where(pl.program_id(0) < npt, xp_ref[...], xs_ref[...])
    o = of_ref[...] + ob_ref[...]
    parts = []
    for h in range(HG_HEADS):
        hs = slice(h * HG_KEY_DIM, (h + 1) * HG_KEY_DIM)
        parts.append(_rms(o[:, hs]))
    on = jnp.concatenate(parts, axis=1) * hgn_ref[...]
    gate = hg_ref[...].astype(F32)
    hb = (on * (gate * _sigmoid(gate))).astype(BF16)
    ya = jnp.dot(att_ref[...], wa_ref[...], preferred_element_type=F32)
    yb = jnp.dot(hb, wb_ref[...], preferred_element_type=F32)
    merged = (_sigmoid(ga_ref[...].astype(F32)) * ya
              + _sigmoid(gb_ref[...].astype(F32)) * yb)
    x1 = x + mod_ref[2:3, :] * jnp.dot(merged.astype(BF16), wo_ref[...],
                                       preferred_element_type=F32)
    x1_ref[...] = x1
    u2 = _rms(x1) * nffn_ref[...] * (1.0 + mod_ref[4:5, :]) + mod_ref[3:4, :]
    u2_ref[0], u2_ref[1] = _pack_rows(u2)
    w_hi = wr_ref[0]
    u_hi = u2.astype(BF16)
    u_lo = (u2 - u_hi.astype(F32)).astype(BF16)
    lg_ref[...] = (lax.dot_general(w_hi, u_hi, NT_DIMS, preferred_element_type=F32)
                   + (lax.dot_general(w_hi, u_lo, NT_DIMS, preferred_element_type=F32)
                      + lax.dot_general(wr_ref[1], u_hi, NT_DIMS, preferred_element_type=F32)))


def _split_hi_lo(w):
    hi = w.astype(BF16)
    return jnp.stack([hi, (w - hi.astype(F32)).astype(BF16)])


def _merge(seqs, x_p, x_s, att, o_f, o_b, proj, mod, hg_norm, norm_ffn, wa, wb, wo, wr):
    n = seqs.n
    tm = 512
    tok = lambda i: (i, 0)
    const = lambda i: (0, 0)
    npt, p_spec, s_spec = _two_stream_specs(seqs, tm)
    return pl.pallas_call(
        functools.partial(_merge_kernel, npt),
        grid=(n // tm,),
        in_specs=[p_spec, s_spec,
                  pl.BlockSpec((tm, NA_WIDTH), tok),
                  pl.BlockSpec((tm, HG_WIDTH), tok),
                  pl.BlockSpec((tm, HG_WIDTH), tok),
                  pl.BlockSpec((None, tm, HG_WIDTH), lambda i: (SLAB_HG[0], i, SLAB_HG[1])),
                  pl.BlockSpec((None, tm, D_MODEL), lambda i: (SLAB_GA[0], i, SLAB_GA[1])),
                  pl.BlockSpec((None, tm, D_MODEL), lambda i: (SLAB_GB[0], i, SLAB_GB[1])),
                  pl.BlockSpec((None, N_MOD, D_MODEL), lambda i: (seqs.info(i * tm)[0], 0, 0)),
                  pl.BlockSpec((1, HG_WIDTH), const),
                  pl.BlockSpec((1, D_MODEL), const),
                  pl.BlockSpec((NA_WIDTH, D_MODEL), const),
                  pl.BlockSpec((HG_WIDTH, D_MODEL), const),
                  pl.BlockSpec((D_MODEL, D_MODEL), const),
                  pl.BlockSpec((2, N_EXPERTS, D_MODEL), lambda i: (0, 0, 0))],
        out_specs=[pl.BlockSpec((tm, D_MODEL), tok),
                   pl.BlockSpec((2, tm, PACK_W), lambda i: (0, i, 0)),
                   pl.BlockSpec((N_EXPERTS, tm), lambda i: (0, i))],
        out_shape=[jax.ShapeDtypeStruct((n, D_MODEL), F32),
                   jax.ShapeDtypeStruct((2, n, PACK_W), jnp.uint32),
                   jax.ShapeDtypeStruct((N_EXPERTS, n), F32)],
        compiler_params=_params(56, ("arbitrary",)),
        name="merge",
    )(x_p, x_s, att, o_f, o_b, proj, proj, proj, mod, hg_norm.reshape(1, HG_WIDTH),
      norm_ffn.reshape(1, D_MODEL), wa, wb, wo, wr)


def _expert_kernel(be_ref, nused_ref, next_ref, slot_ref, xs_ref, wg_hbm, wu_hbm, wd_hbm, ys_ref,
                   wg_s, wu_s, wd_s, wg_st, wu_st, wd_st, sems):
    i = pl.program_id(0)

    def weight_copies(e, slot):
        pairs = ((wg_hbm, wg_st), (wu_hbm, wu_st), (wd_hbm, wd_st))
        return [pltpu.make_async_copy(hbm.at[e], stage.at[slot], sems.at[slot, j])
                for j, (hbm, stage) in enumerate(pairs)]

    @pl.when(i < nused_ref[0])
    def _():
        e = be_ref[i]
        slot = slot_ref[i]
        nxt = next_ref[i]
        last = nused_ref[0] - 1
        first = (i == 0) | (e != be_ref[jnp.maximum(i - 1, 0)])
        only = (i == last) | (be_ref[jnp.minimum(i + 1, last)] != e)
        second = (i >= 1) & jnp.logical_not(first) & (
            (i == 1) | (be_ref[jnp.maximum(i - 2, 0)] != e))

        @pl.when(first)
        def _():
            @pl.when(i == 0)
            def _():
                for c in weight_copies(e, slot):
                    c.start()

            for c in weight_copies(e, slot):
                c.wait()

            @pl.when(nxt >= 0)
            def _():
                ahead = weight_copies(nxt, 1 - slot)
                ahead[0].start()
                ahead[1].start()

                @pl.when(only)
                def _():
                    ahead[2].start()

            wg_s[...] = wg_st[slot].astype(BF16)
            wu_s[...] = wu_st[slot].astype(BF16)
            wd_s[...] = wd_st[slot].astype(BF16)

        @pl.when(second & (nxt >= 0))
        def _():
            weight_copies(nxt, 1 - slot)[2].start()

        xq = _unpack_rows(xs_ref[0], xs_ref[1])
        a = _dot_quarters(xq, wg_s)
        b = _dot_quarters(xq, wu_s)
        h = (a * _sigmoid(a) * b).astype(BF16)
        ys_ref[0], ys_ref[1] = _pack_rows(jnp.dot(h, wd_s[...], preferred_element_type=F32))


def _experts(xs, plan, wg, wu, wd):
    n_rows = xs.shape[1]
    n_blocks = n_rows // MOE_BLOCK
    rows_map = lambda i, be, nu, nx, sl: (0, jnp.minimum(i, nu[0] - 1), 0)
    whole = pl.BlockSpec(memory_space=pl.ANY)
    grid_spec = pltpu.PrefetchScalarGridSpec(
        num_scalar_prefetch=4,
        grid=(n_blocks,),
        in_specs=[pl.BlockSpec((2, MOE_BLOCK, PACK_W), rows_map), whole, whole, whole],
        out_specs=pl.BlockSpec((2, MOE_BLOCK, PACK_W), rows_map),
        scratch_shapes=[pltpu.VMEM((D_MODEL, D_EXPERT), BF16),
                        pltpu.VMEM((D_MODEL, D_EXPERT), BF16),
                        pltpu.VMEM((D_EXPERT, D_MODEL), BF16),
                        pltpu.VMEM((2, D_MODEL, D_EXPERT), F32),
                        pltpu.VMEM((2, D_MODEL, D_EXPERT), F32),
                        pltpu.VMEM((2, D_EXPERT, D_MODEL), F32),
                        pltpu.SemaphoreType.DMA((2, 3))],
    )
    return pl.pallas_call(
        _expert_kernel,
        grid_spec=grid_spec,
        out_shape=jax.ShapeDtypeStruct((2, n_rows, PACK_W), jnp.uint32),
        compiler_params=_params(32, ("arbitrary",)),
        name="experts",
    )(*plan, xs, wg, wu, wd)


ROUTE_TILE = 128
ROUTE_STEP = 512
NEG_INF = float("-inf")


def _first_max(x, idx, big):
    m = jnp.max(x, axis=0, keepdims=True)
    first = jnp.min(jnp.where(x == m, idx, big), axis=0, keepdims=True)
    return m, first


def _route_tile(lg, bias, tri, base):
    per_group = N_EXPERTS // N_GROUPS
    scores = jax.nn.sigmoid(lg)
    sel = scores + bias
    lrow = lax.broadcasted_iota(jnp.int32, (per_group, ROUTE_TILE), 0)
    gs = []
    for g in range(N_GROUPS):
        x = sel[g * per_group:(g + 1) * per_group]
        m1, first = _first_max(x, lrow, per_group)
        m2 = jnp.max(jnp.where(lrow == first, NEG_INF, x), axis=0, keepdims=True)
        gs.append(m1 + m2)
    cur = jnp.concatenate(gs, axis=0)
    grow = lax.broadcasted_iota(jnp.int32, (N_GROUPS, ROUTE_TILE), 0)
    chosen = jnp.zeros((N_GROUPS, ROUTE_TILE), jnp.int32)
    for _ in range(TOPK_GROUPS):
        _, first = _first_max(cur, grow, N_GROUPS)
        hit = grow == first
        chosen = jnp.where(hit, 1, chosen)
        cur = jnp.where(hit, NEG_INF, cur)
    cur = jnp.concatenate(
        [jnp.where(chosen[g:g + 1] > 0, sel[g * per_group:(g + 1) * per_group], NEG_INF)
         for g in range(N_GROUPS)], axis=0)
    row = lax.broadcasted_iota(jnp.int32, (N_EXPERTS, ROUTE_TILE), 0)
    member = jnp.zeros((N_EXPERTS, ROUTE_TILE), F32)
    es, ws = [], []
    for _ in range(TOP_K):
        _, first = _first_max(cur, row, N_EXPERTS)
        hit = row == first
        es.append(first)
        ws.append(jnp.sum(jnp.where(hit, scores, 0.0), axis=0, keepdims=True))
        cur = jnp.where(hit, NEG_INF, cur)
        member = jnp.where(hit, 1.0, member)
    e = jnp.concatenate(es, axis=0)
    w = jnp.concatenate(ws, axis=0)
    w = w / jnp.sum(w, axis=0, keepdims=True) * ROUTE_SCALE
    before = jnp.dot(member.astype(BF16), tri, preferred_element_type=F32) + base
    rank = jnp.concatenate(
        [jnp.sum(jnp.where(row == es[k], before, 0.0), axis=0, keepdims=True) for k in range(TOP_K)],
        axis=0)
    return e, w, rank.astype(jnp.int32), member


def _route_kernel(lg_ref, b_ref, e_ref, w_ref, r_ref, cnt_ref, base):
    @pl.when(pl.program_id(0) == 0)
    def _():
        base[...] = jnp.zeros_like(base)

    r_i = lax.broadcasted_iota(jnp.int32, (ROUTE_TILE, ROUTE_TILE), 0)
    c_i = lax.broadcasted_iota(jnp.int32, (ROUTE_TILE, ROUTE_TILE), 1)
    tri = (r_i < c_i).astype(BF16)
    bias = b_ref[...]
    for j in range(ROUTE_STEP // ROUTE_TILE):
        ls = slice(j * ROUTE_TILE, (j + 1) * ROUTE_TILE)
        e, w, rank, member = _route_tile(lg_ref[:, ls], bias, tri, base[...])
        e_ref[:, ls] = e
        w_ref[:, ls] = w
        r_ref[:, ls] = rank
        base[...] = base[...] + jnp.sum(member, axis=1, keepdims=True)
    cnt_ref[...] = base[...]


def _route(logits_t, b_router):
    n = logits_t.shape[1]
    tok = lambda i: (0, i)
    return pl.pallas_call(
        _route_kernel,
        grid=(n // ROUTE_STEP,),
        in_specs=[pl.BlockSpec((N_EXPERTS, ROUTE_STEP), tok),
                  pl.BlockSpec((N_EXPERTS, 1), lambda i: (0, 0))],
        out_specs=[pl.BlockSpec((TOP_K, ROUTE_STEP), tok),
                   pl.BlockSpec((TOP_K, ROUTE_STEP), tok),
                   pl.BlockSpec((TOP_K, ROUTE_STEP), tok),
                   pl.BlockSpec((N_EXPERTS, 1), lambda i: (0, 0))],
        out_shape=[jax.ShapeDtypeStruct((TOP_K, n), jnp.int32),
                   jax.ShapeDtypeStruct((TOP_K, n), F32),
                   jax.ShapeDtypeStruct((TOP_K, n), jnp.int32),
                   jax.ShapeDtypeStruct((N_EXPERTS, 1), F32)],
        scratch_shapes=[pltpu.VMEM((N_EXPERTS, 1), F32)],
        compiler_params=_params(32, ("arbitrary",)),
        name="route",
    )(logits_t, b_router.reshape(N_EXPERTS, 1).astype(F32))


def _pos_kernel(e_ref, r_ref, ps_ref, pos_ref):
    row = lax.broadcasted_iota(jnp.int32, (N_EXPERTS, ROUTE_TILE), 0)
    pstart = ps_ref[...]
    for j in range(ROUTE_STEP // ROUTE_TILE):
        ls = slice(j * ROUTE_TILE, (j + 1) * ROUTE_TILE)
        e = e_ref[:, ls]
        off = jnp.concatenate(
            [jnp.sum(jnp.where(row == e[k:k + 1], pstart, 0.0), axis=0, keepdims=True)
             for k in range(TOP_K)], axis=0)
        pos_ref[:, ls] = off.astype(jnp.int32) + r_ref[:, ls]


def _positions(eidx, rank, pstarts):
    n = eidx.shape[1]
    tok = lambda i: (0, i)
    return pl.pallas_call(
        _pos_kernel,
        grid=(n // ROUTE_STEP,),
        in_specs=[pl.BlockSpec((TOP_K, ROUTE_STEP), tok),
                  pl.BlockSpec((TOP_K, ROUTE_STEP), tok),
                  pl.BlockSpec((N_EXPERTS, 1), lambda i: (0, 0))],
        out_specs=pl.BlockSpec((TOP_K, ROUTE_STEP), tok),
        out_shape=jax.ShapeDtypeStruct((TOP_K, n), jnp.int32),
        compiler_params=_params(32, ("arbitrary",)),
        name="positions",
    )(eidx, rank, pstarts.reshape(N_EXPERTS, 1).astype(F32))


def _block_tables(counts, n_blocks):
    counts = counts.reshape(N_EXPERTS).astype(jnp.int32)
    padded = (counts + MOE_BLOCK - 1) // MOE_BLOCK * MOE_BLOCK
    pend = jnp.cumsum(padded)
    pstarts = pend - padded
    first_row = jnp.arange(n_blocks, dtype=jnp.int32) * MOE_BLOCK
    block_e = jnp.sum((pend[None, :] <= first_row[:, None]).astype(jnp.int32), axis=1)
    block_e = jnp.minimum(block_e, N_EXPERTS - 1).astype(jnp.int32)
    n_used = (pend[-1] // MOE_BLOCK).astype(jnp.int32).reshape(1)
    ids = jnp.arange(N_EXPERTS, dtype=jnp.int32)
    used = padded > 0
    ordinal = jnp.cumsum(used.astype(jnp.int32)) - 1
    from_here = jnp.flip(lax.cummin(jnp.flip(jnp.where(used, ids, N_EXPERTS))))
    after = jnp.concatenate([from_here[1:], jnp.full((1,), N_EXPERTS, jnp.int32)])
    after = jnp.where(after >= N_EXPERTS, -1, after)
    mine = block_e[:, None] == ids[None, :]
    next_e = jnp.sum(jnp.where(mine, after[None, :], 0), axis=1).astype(jnp.int32)
    slot = jnp.sum(jnp.where(mine, ordinal[None, :] % 2, 0), axis=1).astype(jnp.int32)
    return pstarts, (block_e, n_used, next_e, slot)


SC_WINDOW = 128


def _sc_mesh():
    return plsc.VectorSubcoreMesh(core_axis_name="core", subcore_axis_name="subcore")


def _both_halves(pos, n_rows):
    return jnp.concatenate([pos, pos + n_rows]).reshape(1, -1)


def _sc_dispatch(rows, pos, n_rows):
    _, n, w = rows.shape
    tiles = n // SC_WINDOW
    top_k = pos.shape[0] // n
    idx = jnp.transpose(pos.reshape(top_k, tiles, SC_WINDOW), (1, 0, 2))
    idx = jnp.stack([idx, idx + n_rows]).reshape(1, -1)

    @functools.partial(pl.kernel, out_type=jax.ShapeDtypeStruct((2 * n_rows, w), rows.dtype),
                       mesh=_sc_mesh(), scratch_types=[])
    def scatter_kernel(x_hbm, i_hbm, o_hbm):
        def body(x_vmem, i_vmem):
            pltpu.sync_copy(x_vmem, o_hbm.at[i_vmem.at[0]])

        pltpu.emit_pipeline(
            body,
            grid=(2 * tiles * top_k,),
            in_specs=[pl.BlockSpec((SC_WINDOW, w), lambda i: (i // top_k, 0)),
                      pl.BlockSpec((1, SC_WINDOW), lambda i: (0, i))],
            out_specs=[],
            core_axis_name=("core", "subcore"),
            dimension_semantics=(pltpu.PARALLEL,),
        )(x_hbm, i_hbm)

    out = scatter_kernel(rows.reshape(2 * n, w), idx)
    return out.reshape(2, n_rows, w)


def _sc_gather(table, pos):
    _, n_rows, w = table.shape
    m = pos.shape[0]

    @functools.partial(pl.kernel, out_type=jax.ShapeDtypeStruct((2 * m, w), table.dtype),
                       mesh=_sc_mesh(), scratch_types=[])
    def gather_kernel(t_hbm, i_hbm, o_hbm):
        def body(i_vmem, o_vmem):
            pltpu.sync_copy(t_hbm.at[i_vmem.at[0]], o_vmem)

        pltpu.emit_pipeline(
            body,
            grid=(2 * m // SC_WINDOW,),
            in_specs=[pl.BlockSpec((1, SC_WINDOW), lambda i: (0, i))],
            out_specs=[pl.BlockSpec((SC_WINDOW, w), lambda i: (i, 0))],
            core_axis_name=("core", "subcore"),
            dimension_semantics=(pltpu.PARALLEL,),
        )(i_hbm, o_hbm)

    out = gather_kernel(table.reshape(2 * n_rows, w), _both_halves(pos, n_rows))
    return out.reshape(2, m, w)


def _final_kernel(npt, x1_ref, u2_ref, yg_ref, w_ref, mod_ref, nf_ref, wsg_ref, wsu_ref, wsd_ref,
                  op_ref, os_ref):
    uq = _unpack_rows(u2_ref[0], u2_ref[1])
    a = _dot_quarters(uq, wsg_ref)
    b = _dot_quarters(uq, wsu_ref)
    f = jnp.dot((a * _sigmoid(a) * b).astype(BF16), wsd_ref[...], preferred_element_type=F32)
    w = w_ref[...]
    fq = [f[:, q * PACK_W:(q + 1) * PACK_W] for q in range(4)]
    for k in range(TOP_K):
        yq = _unpack_rows(yg_ref[0, k], yg_ref[1, k], F32)
        fq = [fq[q] + w[:, k:k + 1] * yq[q] for q in range(4)]
    x2 = x1_ref[...] + mod_ref[5:6, :] * jnp.concatenate(fq, axis=1)
    out = _rms(x2) * nf_ref[...]
    i = pl.program_id(0)

    @pl.when(i < npt)
    def _():
        op_ref[...] = out

    @pl.when(i >= npt)
    def _():
        os_ref[...] = out


def _final(seqs, x1, u2, yg, w, mod, norm_final, wsg, wsu, wsd):
    n = seqs.n
    tm = 256
    tok = lambda i: (i, 0)
    const = lambda i: (0, 0)
    npt, p_spec, s_spec = _two_stream_specs(seqs, tm)
    return pl.pallas_call(
        functools.partial(_final_kernel, npt),
        grid=(n // tm,),
        in_specs=[pl.BlockSpec((tm, D_MODEL), tok),
                  pl.BlockSpec((2, tm, PACK_W), lambda i: (0, i, 0)),
                  pl.BlockSpec((2, TOP_K, tm, PACK_W), lambda i: (0, 0, i, 0)),
                  pl.BlockSpec((tm, TOP_K), tok),
                  pl.BlockSpec((None, N_MOD, D_MODEL), lambda i: (seqs.info(i * tm)[0], 0, 0)),
                  pl.BlockSpec((1, D_MODEL), const),
                  pl.BlockSpec((D_MODEL, D_EXPERT), const),
                  pl.BlockSpec((D_MODEL, D_EXPERT), const),
                  pl.BlockSpec((D_EXPERT, D_MODEL), const)],
        out_specs=[p_spec, s_spec],
        out_shape=[jax.ShapeDtypeStruct((seqs.np_, D_MODEL), F32),
                   jax.ShapeDtypeStruct((n - seqs.np_, D_MODEL), F32)],
        compiler_params=_params(48, ("arbitrary",)),
        name="final",
    )(x1, u2, yg, w, mod, norm_final.reshape(1, D_MODEL), wsg, wsu, wsd)


def _layer(seqs, x_p, x_s, c, w_ada, b_ada, norm_mix, w_in, na_rpb, hg_lb, hg_norm, w_branch_a,
           w_branch_b, w_out, norm_ffn, w_router, b_router, w_exp_gate, w_exp_up, w_exp_down,
           w_sh_gate, w_sh_up, w_sh_down, norm_final):
    n = seqs.n
    c_rows = -(-seqs.nseq // 8) * 8
    c_pad = jnp.zeros((c_rows, D_MODEL), F32).at[:seqs.nseq].set(c)
    mod = _ada(c_pad, w_ada[0], b_ada[0])[:seqs.nseq].reshape(seqs.nseq, N_MOD, D_MODEL)
    lb = jnp.cumsum(jax.nn.softmax(hg_lb.astype(F32), axis=0), axis=0)[0]

    proj = _inproj(seqs, x_p, x_s, mod, norm_mix[0], w_in[0].astype(BF16))
    att = _na(seqs, proj, _na_bias_table(na_rpb[0]))
    o_f, o_b = _hgrn(seqs, proj, lb)
    x1, u2, logits = _merge(seqs, x_p, x_s, att, o_f, o_b, proj, mod, hg_norm[0], norm_ffn[0],
                            w_branch_a[0].astype(BF16), w_branch_b[0].astype(BF16),
                            w_out[0].astype(BF16), _split_hi_lo(w_router[0].T))

    eidx, w, rank, counts = _route(logits, b_router[0])
    n_rows = n * TOP_K + N_EXPERTS * MOE_BLOCK
    pstarts, plan = _block_tables(counts, n_rows // MOE_BLOCK)
    pos = _positions(eidx, rank, pstarts).reshape(-1)
    xs = _sc_dispatch(u2, pos, n_rows)
    ys = _experts(xs, plan, w_exp_gate[0], w_exp_up[0], w_exp_down[0])
    yg = _sc_gather(ys, pos).reshape(2, TOP_K, n, PACK_W)
    return _final(seqs, x1, u2, yg, w.T, mod, norm_final, w_sh_gate[0].astype(BF16),
                  w_sh_up[0].astype(BF16), w_sh_down[0].astype(BF16))


def kernel(x_prompt, x_sample, c_prompt, c_sample, w_ada, b_ada, norm_mix, w_in, na_rpb, hg_lb, hg_norm, w_branch_a, w_branch_b, w_out, norm_ffn, w_router, b_router, w_exp_gate, w_exp_up, w_exp_down, w_sh_gate, w_sh_up, w_sh_down, norm_final):
    bp, tp, _ = x_prompt.shape
    bs, ts, _ = x_sample.shape
    seqs = _Seqs(bp, tp, bs, ts)
    c = jnp.concatenate([c_prompt, c_sample])
    y_p, y_s = _layer(seqs, x_prompt.reshape(bp * tp, D_MODEL), x_sample.reshape(bs * ts, D_MODEL),
                      c, w_ada, b_ada, norm_mix, w_in, na_rpb, hg_lb, hg_norm, w_branch_a,
                      w_branch_b, w_out, norm_ffn, w_router, b_router, w_exp_gate, w_exp_up,
                      w_exp_down, w_sh_gate, w_sh_up, w_sh_down, norm_final)
    return (y_p.reshape(bp, tp, D_MODEL), y_s.reshape(bs, ts, D_MODEL))
```

```python
import functools

import jax
import jax.numpy as jnp
import numpy as np
from jax import lax
from jax.experimental import pallas as pl
from jax.experimental.pallas import tpu as pltpu
from jax.experimental.pallas import tpu_sc as plsc

D_MODEL = 1024
GRID_W = 64
NA_HEADS = 8
NA_HEAD_DIM = 64
NA_WIDTH = NA_HEADS * NA_HEAD_DIM
NA_ROWS = 8
NA_COLS = 16
HG_HEADS = 4
HG_KEY_DIM = 128
HG_WIDTH = HG_HEADS * HG_KEY_DIM
HG_CHUNK = 128
HG_EXP_LIMIT = 80.0
N_EXPERTS = 256
TOP_K = 8
N_GROUPS = 8
TOPK_GROUPS = 4
D_EXPERT = 256
ROUTE_SCALE = 2.5
N_MOD = 6
RMS_EPS = 1e-6

MOE_BLOCK = 512
MOE_SUB = 4
NA_GROUP = 4
NA_WIN = 3 * NA_GROUP
NA_TOK = NA_GROUP * GRID_W
LOG2E = 1.4426950408889634
NA_Q_SCALE = NA_HEAD_DIM ** -0.5 * LOG2E
HG_STEP = 512
MASK_VALUE = -1e30

F32 = jnp.float32
BF16 = jnp.bfloat16
HIGHEST = lax.Precision.HIGHEST
NT_DIMS = (((1,), (1,)), ((), ()))
TN_DIMS = (((0,), (0,)), ((), ()))

PROJ_TN = 2048
SLAB_Q, SLAB_K, SLAB_V, SLAB_HQ = (0, 0), (0, 1), (0, 2), (0, 3)
SLAB_FF, SLAB_FB, SLAB_HI, SLAB_HG = (1, 0), (1, 1), (1, 2), (1, 3)
SLAB_GA, SLAB_GB = (2, 0), (2, 1)


def _params(vmem_mb, sem=None):
    kw = dict(vmem_limit_bytes=vmem_mb * 1024 * 1024)
    if sem is not None:
        kw["dimension_semantics"] = sem
    return pltpu.CompilerParams(**kw)


class _Seqs:
    def __init__(self, bp, tp, bs, ts):
        self.bp, self.tp, self.bs, self.ts = bp, tp, bs, ts
        self.np_ = bp * tp
        self.n = bp * tp + bs * ts
        self.nseq = bp + bs

    def info(self, t0):
        in_p = t0 < self.np_
        rel = jnp.maximum(t0 - self.np_, 0)
        sid = jnp.where(in_p, t0 // self.tp, self.bp + rel // self.ts)
        start = jnp.where(in_p, (t0 // self.tp) * self.tp, self.np_ + (rel // self.ts) * self.ts)
        length = jnp.where(in_p, self.tp, self.ts)
        return sid, start, length


def _ada_kernel(c_ref, w_ref, b_ref, o_ref):
    c = c_ref[...]
    a = c * jax.nn.sigmoid(c)
    o_ref[...] = jnp.dot(a, w_ref[...], precision=HIGHEST, preferred_element_type=F32) + b_ref[...]


def _ada(c_pad, w_ada, b_ada):
    rows = c_pad.shape[0]
    n_out = w_ada.shape[1]
    tn = 1024
    return pl.pallas_call(
        _ada_kernel,
        grid=(n_out // tn,),
        in_specs=[pl.BlockSpec((rows, D_MODEL), lambda j: (0, 0)),
                  pl.BlockSpec((D_MODEL, tn), lambda j: (0, j)),
                  pl.BlockSpec((1, tn), lambda j: (0, j))],
        out_specs=pl.BlockSpec((rows, tn), lambda j: (0, j)),
        out_shape=jax.ShapeDtypeStruct((rows, n_out), F32),
        compiler_params=_params(32),
        name="ada",
    )(c_pad, w_ada, b_ada.reshape(1, n_out))


def _rms(x):
    return x * lax.rsqrt(jnp.mean(x * x, axis=-1, keepdims=True) + RMS_EPS)


def _sigmoid(x):
    return 0.5 * jnp.tanh(0.5 * x) + 0.5


PACK_W = D_MODEL // 4


def _pack_rows(x):
    out = []
    for h in range(2):
        lo = x[:, (2 * h) * PACK_W:(2 * h + 1) * PACK_W].astype(BF16).astype(F32)
        hi = x[:, (2 * h + 1) * PACK_W:(2 * h + 2) * PACK_W].astype(BF16).astype(F32)
        out.append(lax.bitcast_convert_type(hi, jnp.uint32)
                   | (lax.bitcast_convert_type(lo, jnp.uint32) >> 16))
    return out


def _unpack_rows(p0, p1, dtype=BF16):
    quarters = []
    for p in (p0, p1):
        quarters.append(lax.bitcast_convert_type(p << 16, F32).astype(dtype))
        quarters.append(lax.bitcast_convert_type(p & jnp.uint32(0xFFFF0000), F32).astype(dtype))
    return quarters


def _dot_quarters(quarters, w_ref):
    acc = None
    for q, xq in enumerate(quarters):
        part = jnp.dot(xq, w_ref[q * PACK_W:(q + 1) * PACK_W, :], preferred_element_type=F32)
        acc = part if acc is None else acc + part
    return acc


def _two_stream_specs(seqs, tm, grid_rank=1):
    npt = seqs.np_ // tm
    nst = (seqs.n - seqs.np_) // tm
    if grid_rank == 1:
        p_map = lambda i: (jnp.minimum(i, npt - 1), 0)
        s_map = lambda i: (jnp.clip(i - npt, 0, nst - 1), 0)
    else:
        p_map = lambda i, j: (jnp.minimum(i, npt - 1), 0)
        s_map = lambda i, j: (jnp.clip(i - npt, 0, nst - 1), 0)
    return npt, pl.BlockSpec((tm, D_MODEL), p_map), pl.BlockSpec((tm, D_MODEL), s_map)


def _inproj_kernel(npt, xp_ref, xs_ref, mod_ref, g_ref, w_ref, cs_ref, o_ref, u_scr):
    @pl.when(pl.program_id(1) == 0)
    def _():
        x = jnp.where(pl.program_id(0) < npt, xp_ref[...], xs_ref[...])
        y = _rms(x) * g_ref[...]
        u = y * (1.0 + mod_ref[1:2, :]) + mod_ref[0:1, :]
        u_scr[...] = u.astype(BF16)

    acc = jnp.dot(u_scr[...], w_ref[...], preferred_element_type=F32)
    o_ref[...] = (acc * cs_ref[...]).astype(o_ref.dtype)


def _inproj(seqs, x_p, x_s, mod, norm_mix, w_in_bf):
    n = seqs.n
    tm = min(1024, seqs.tp, seqs.ts)
    tn = PROJ_TN
    n_slab = w_in_bf.shape[1] // tn
    npt, p_spec, s_spec = _two_stream_specs(seqs, tm, grid_rank=2)
    col_scale = jnp.ones((1, w_in_bf.shape[1]), F32).at[:, :NA_WIDTH].set(NA_Q_SCALE)
    return pl.pallas_call(
        functools.partial(_inproj_kernel, npt),
        grid=(n // tm, n_slab),
        in_specs=[p_spec, s_spec,
                  pl.BlockSpec((None, N_MOD, D_MODEL), lambda i, j: (seqs.info(i * tm)[0], 0, 0)),
                  pl.BlockSpec((1, D_MODEL), lambda i, j: (0, 0)),
                  pl.BlockSpec((D_MODEL, tn), lambda i, j: (0, j)),
                  pl.BlockSpec((1, tn), lambda i, j: (0, j))],
        out_specs=pl.BlockSpec((None, tm, tn), lambda i, j: (j, i, 0)),
        out_shape=jax.ShapeDtypeStruct((n_slab, n, tn), BF16),
        scratch_shapes=[pltpu.VMEM((tm, D_MODEL), BF16)],
        compiler_params=_params(56, ("arbitrary", "arbitrary")),
        name="inproj",
    )(x_p, x_s, mod, norm_mix.reshape(1, D_MODEL), w_in_bf, col_scale)


def _na_bias_table(rpb):
    col = np.arange(GRID_W)
    cs = np.clip(col - NA_COLS // 2, 0, GRID_W - NA_COLS)
    valid = (col[None, :] >= cs[:, None]) & (col[None, :] < cs[:, None] + NA_COLS)
    coff = col[None, :] - col[:, None] + NA_COLS - 1
    onehot = (coff[None] == np.arange(2 * NA_COLS - 1)[:, None, None]) & valid[None]
    toep = jnp.einsum("hrc,cqk->hrqk", rpb.astype(F32), jnp.asarray(onehot, F32),
                      precision=HIGHEST)
    toep = jnp.where(valid[None, None], toep * LOG2E, MASK_VALUE)
    masked =jnp.full((NA_HEADS, GRID_W, GRID_W), MASK_VALUE, F32)
    cases = (([0] * NA_GROUP, [NA_ROWS - 1 - i for i in range(NA_GROUP)]),
             (list(range(NA_GROUP)), [NA_ROWS // 2 - 1] * NA_GROUP),
             ([NA_GROUP] * NA_GROUP, [NA_ROWS // 2 - 1 - i for i in range(NA_GROUP)]))
    tabs = []
    for first_row, first_off in cases:
        q_rows = []
        for i in range(NA_GROUP):
            blocks = [toep[:, first_off[i] + w - first_row[i]]
                      if 0 <= w - first_row[i] < NA_ROWS else masked for w in range(NA_WIN)]
            q_rows.append(jnp.concatenate(blocks, axis=2))
        tabs.append(jnp.concatenate(q_rows, axis=1))
    return jnp.stack(tabs)


def _na_geometry(seqs, g):
    _, start, length = seqs.info(g * NA_TOK)
    row0 = start // GRID_W
    rows = length // GRID_W
    r0 = g * NA_GROUP - row0
    wb = jnp.clip(r0 - NA_ROWS // 2, 0, rows - NA_WIN)
    case = jnp.where(r0 == 0, 0, jnp.where(r0 == rows - NA_GROUP, 2, 1))
    return (row0 + wb) // NA_GROUP, case


def _na_kernel(q_ref, k0, k1, k2, v0, v1, v2, bias_ref, o_ref):
    k_refs = (k0, k1, k2)
    v_refs = (v0, v1, v2)
    outs = []
    for h in range(NA_HEADS):
        hs = slice(h * NA_HEAD_DIM, (h + 1) * NA_HEAD_DIM)
        q = q_ref[:, hs]
        s = [lax.dot_general(q, kr[:, hs], NT_DIMS, preferred_element_type=F32)
             + bias_ref[h, :, d * NA_TOK:(d + 1) * NA_TOK] for d, kr in enumerate(k_refs)]
        m = jnp.max(jnp.maximum(jnp.maximum(s[0], s[1]), s[2]), axis=-1, keepdims=True)
        p = [jnp.exp2(sd - m) for sd in s]
        l = jnp.sum((p[0] + p[1]) + p[2], axis=-1, keepdims=True)
        o = sum(jnp.dot(pd.astype(BF16), vr[:, hs], preferred_element_type=F32)
                for pd, vr in zip(p, v_refs))
        outs.append(o / l)
    o_ref[...] = jnp.concatenate(outs, axis=1).astype(o_ref.dtype)


def _na(seqs, proj, bias_tab):
    n = seqs.n

    def kv_spec(slab, d):
        return pl.BlockSpec((None, NA_TOK, NA_WIDTH),
                            lambda g: (slab[0], _na_geometry(seqs, g)[0] + d, slab[1]))

    return pl.pallas_call(
        _na_kernel,
        grid=(n // NA_TOK,),
        in_specs=[pl.BlockSpec((None, NA_TOK, NA_WIDTH), lambda g: (SLAB_Q[0], g, SLAB_Q[1]))]
        + [kv_spec(SLAB_K, d) for d in range(3)] + [kv_spec(SLAB_V, d) for d in range(3)]
        + [pl.BlockSpec((None,) + bias_tab.shape[1:],
                        lambda g: (_na_geometry(seqs, g)[1], 0, 0, 0))],
        out_specs=pl.BlockSpec((NA_TOK, NA_WIDTH), lambda g: (g, 0)),
        out_shape=jax.ShapeDtypeStruct((n, NA_WIDTH), BF16),
        compiler_params=_params(48, ("arbitrary",)),
        name="natten",
    )(proj, proj, proj, proj, proj, proj, proj, bias_tab)


def _hg_chunk(q, z, v, lb, tri, mask, mid, last, st_ref):
    sig = _sigmoid(z)
    f = lb + (1.0 - lb) * sig
    lf = jnp.log(f)
    kin = (1.0 - lb) * (1.0 - sig)
    hi = lf.astype(BF16)
    lo = (lf - hi.astype(F32)).astype(BF16)
    g2 = jnp.dot(tri, jnp.concatenate([lo, hi], axis=1), preferred_element_type=F32)
    gcum = g2[:, :HG_WIDTH] + g2[:, HG_WIDTH:]
    gm = gcum[mid:mid + 1, :]
    gl = gcum[last:last + 1, :]
    up = jnp.exp(gcum - gm)
    dn = jnp.exp(gm - gcum)
    qa = (q * up).astype(BF16)
    ka = (kin * dn).astype(BF16)
    qe = (q * (up * jnp.exp(gm))).astype(BF16)
    kd = (kin * (dn * jnp.exp(gl - gm))).astype(BF16)
    eg = jnp.exp(gl)
    vb = v.astype(BF16)
    outs = []
    for h in range(HG_HEADS):
        hs = slice(h * HG_KEY_DIM, (h + 1) * HG_KEY_DIM)
        a = lax.dot_general(qa[:, hs], ka[:, hs], NT_DIMS, preferred_element_type=F32)
        a = jnp.where(mask, a, 0.0)
        st = st_ref[h]
        o = jnp.dot(a.astype(BF16), vb[:, hs], preferred_element_type=F32)
        o = o + lax.dot_general(qe[:, hs], st.astype(BF16), NT_DIMS, preferred_element_type=F32)
        st_ref[h] = st * eg[:, hs] + lax.dot_general(vb[:, hs], kd[:, hs], TN_DIMS,
                                                    preferred_element_type=F32)
        outs.append(o)
    return jnp.concatenate(outs, axis=1)


def _hg_exact(q_ref, z_ref, v_ref, lb, reverse, st_ref, o_ref, qs, fs, ks, vs):
    sig = jax.nn.sigmoid(z_ref[...].astype(F32))
    qs[...] = q_ref[...].astype(F32)
    fs[...] = lb + (1.0 - lb) * sig
    ks[...] = (1.0 - lb) * (1.0 - sig)
    vs[...] = v_ref[...].astype(F32)
    eye = (lax.broadcasted_iota(jnp.int32, (HG_KEY_DIM, HG_KEY_DIM), 0)
           == lax.broadcasted_iota(jnp.int32, (HG_KEY_DIM, HG_KEY_DIM), 1)).astype(F32)

    def body(i, carry):
        t = HG_STEP - 1 - i if reverse else i
        q_t, f_t, k_t, v_t = (r[pl.ds(t, 1), :] for r in (qs, fs, ks, vs))
        outs = []
        for h in range(HG_HEADS):
            hs = slice(h * HG_KEY_DIM, (h + 1) * HG_KEY_DIM)
            v_col = jnp.sum(eye * v_t[:, hs], axis=1, keepdims=True)
            st = st_ref[h] * f_t[:, hs] + v_col * k_t[:, hs]
            st_ref[h] = st
            o_col = jnp.sum(st * q_t[:, hs], axis=1, keepdims=True)
            outs.append(jnp.sum(eye * o_col, axis=0, keepdims=True))
        o_ref[pl.ds(t, 1), :] = jnp.concatenate(outs, axis=1)
        return carry

    lax.fori_loop(0, HG_STEP, body, 0)


def _hg_kernel(seqs, safe_ref, qf_ref, zf_ref, vf_ref, qb_ref, zb_ref, vb_ref, lb_ref, of_ref,
               ob_ref, stf, stb, qs, fs, ks, vs):
    i = pl.program_id(0)
    nsteps = pl.num_programs(0)
    tf = i * HG_STEP
    tb = (nsteps - 1 - i) * HG_STEP
    _, start_f, _ = seqs.info(tf)
    _, start_b, len_b = seqs.info(tb)

    @pl.when(tf == start_f)
    def _():
        stf[...] = jnp.zeros_like(stf)

    @pl.when(tb + HG_STEP == start_b + len_b)
    def _():
        stb[...] = jnp.zeros_like(stb)

    lb = lb_ref[...]

    @pl.when(safe_ref[0] > 0)
    def _():
        row = lax.broadcasted_iota(jnp.int32, (HG_CHUNK, HG_CHUNK), 0)
        col = lax.broadcasted_iota(jnp.int32, (HG_CHUNK, HG_CHUNK), 1)
        lower = row >= col
        upper = col >= row
        tri_f = lower.astype(BF16)
        tri_b = upper.astype(BF16)
        nchunk = HG_STEP // HG_CHUNK
        for c in range(nchunk):
            cs = slice(c * HG_CHUNK, (c + 1) * HG_CHUNK)
            of_ref[cs, :] = _hg_chunk(qf_ref[cs, :].astype(F32), zf_ref[cs, :].astype(F32),
                                      vf_ref[cs, :].astype(F32), lb, tri_f, lower,
                                      HG_CHUNK // 2 - 1, HG_CHUNK - 1, stf)
            cb = nchunk - 1 - c
            bs = slice(cb * HG_CHUNK, (cb + 1) * HG_CHUNK)
            ob_ref[bs, :] = _hg_chunk(qb_ref[bs, :].astype(F32), zb_ref[bs, :].astype(F32),
                                      vb_ref[bs, :].astype(F32), lb, tri_b, upper,
                                      HG_CHUNK // 2, 0, stb)

    @pl.when(safe_ref[0] == 0)
    def _():
        _hg_exact(qf_ref, zf_ref, vf_ref, lb, False, stf, of_ref, qs, fs, ks, vs)
        _hg_exact(qb_ref, zb_ref, vb_ref, lb, True, stb, ob_ref, qs, fs, ks, vs)


def _hgrn(seqs, proj, lb):
    n = seqs.n
    nsteps = n // HG_STEP

    def spec(slab, rev):
        if rev:
            return pl.BlockSpec((None, HG_STEP, HG_WIDTH), lambda i: (slab[0], nsteps - 1 - i, slab[1]))
        return pl.BlockSpec((None, HG_STEP, HG_WIDTH), lambda i: (slab[0], i, slab[1]))

    safe = (jnp.max(-jnp.log(lb)) * (HG_CHUNK // 2) < HG_EXP_LIMIT).astype(jnp.int32).reshape(1)
    step_scratch = pltpu.VMEM((HG_STEP, HG_WIDTH), F32)
    return pl.pallas_call(
        functools.partial(_hg_kernel, seqs),
        grid=(nsteps,),
        in_specs=[pl.BlockSpec(memory_space=pltpu.SMEM),
                  spec(SLAB_HQ, False), spec(SLAB_FF, False), spec(SLAB_HI, False),
                  spec(SLAB_HQ, True), spec(SLAB_FB, True), spec(SLAB_HI, True),
                  pl.BlockSpec((1, HG_WIDTH), lambda i: (0, 0))],
        out_specs=[pl.BlockSpec((HG_STEP, HG_WIDTH), lambda i: (i, 0)),
                   pl.BlockSpec((HG_STEP, HG_WIDTH), lambda i: (nsteps - 1 - i, 0))],
        out_shape=[jax.ShapeDtypeStruct((n, HG_WIDTH), F32)] * 2,
        scratch_shapes=[pltpu.VMEM((HG_HEADS, HG_KEY_DIM, HG_KEY_DIM), F32)] * 2
        + [step_scratch] * 4,
        compiler_params=_params(32, ("arbitrary",)),
        name="hgrn2",
    )(safe, proj, proj, proj, proj, proj, proj, lb.reshape(1, HG_WIDTH))


def _merge_kernel(npt, xp_ref, xs_ref, att_ref, of_ref, ob_ref, hg_ref, ga_ref, gb_ref, mod_ref,
                  hgn_ref, nffn_ref, wa_ref, wb_ref, wo_ref, wr_ref, x1_ref, u2_ref, lg_ref):
    x = jnp.where(pl.program_id(0) < npt, xp_ref[...], xs_ref[...])
    o = of_ref[...] + ob_ref[...]
    parts = []
    for h in range(HG_HEADS):
        hs = slice(h * HG_KEY_DIM, (h + 1) * HG_KEY_DIM)
        parts.append(_rms(o[:, hs]))
    on = jnp.concatenate(parts, axis=1) * hgn_ref[...]
    gate = hg_ref[...].astype(F32)
    hb = (on * (gate * _sigmoid(gate))).astype(BF16)
    ya = jnp.dot(att_ref[...], wa_ref[...], preferred_element_type=F32)
    yb = jnp.dot(hb, wb_ref[...], preferred_element_type=F32)
    merged = (_sigmoid(ga_ref[...].astype(F32)) * ya
              + _sigmoid(gb_ref[...].astype(F32)) * yb)
    x1 = x + mod_ref[2:3, :] * jnp.dot(merged.astype(BF16), wo_ref[...],
                                       preferred_element_type=F32)
    x1_ref[...] = x1
    u2 = _rms(x1) * nffn_ref[...] * (1.0 + mod_ref[4:5, :]) + mod_ref[3:4, :]
    u2_ref[0], u2_ref[1] = _pack_rows(u2)
    w_hi = wr_ref[0]
    u_hi = u2.astype(BF16)
    u_lo = (u2 - u_hi.astype(F32)).astype(BF16)
    lg_ref[...] = (lax.dot_general(w_hi, u_hi, NT_DIMS, preferred_element_type=F32)
                   + (lax.dot_general(w_hi, u_lo, NT_DIMS, preferred_element_type=F32)
                      + lax.dot_general(wr_ref[1], u_hi, NT_DIMS, preferred_element_type=F32)))


def _split_hi_lo(w):
    hi = w.astype(BF16)
    return jnp.stack([hi, (w - hi.astype(F32)).astype(BF16)])


def _merge(seqs, x_p, x_s, att, o_f, o_b, proj, mod, hg_norm, norm_ffn, wa, wb, wo, wr):
    n = seqs.n
    tm = 512
    tok = lambda i: (i, 0)
    const = lambda i: (0, 0)
    npt, p_spec, s_spec = _two_stream_specs(seqs, tm)
    return pl.pallas_call(
        functools.partial(_merge_kernel, npt),
        grid=(n // tm,),
        in_specs=[p_spec, s_spec,
                  pl.BlockSpec((tm, NA_WIDTH), tok),
                  pl.BlockSpec((tm, HG_WIDTH), tok),
                  pl.BlockSpec((tm, HG_WIDTH), tok),
                  pl.BlockSpec((None, tm, HG_WIDTH), lambda i: (SLAB_HG[0], i, SLAB_HG[1])),
                  pl.BlockSpec((None, tm, D_MODEL), lambda i: (SLAB_GA[0], i, SLAB_GA[1])),
                  pl.BlockSpec((None, tm, D_MODEL), lambda i: (SLAB_GB[0], i, SLAB_GB[1])),
                  pl.BlockSpec((None, N_MOD, D_MODEL), lambda i: (seqs.info(i * tm)[0], 0, 0)),
                  pl.BlockSpec((1, HG_WIDTH), const),
                  pl.BlockSpec((1, D_MODEL), const),
                  pl.BlockSpec((NA_WIDTH, D_MODEL), const),
                  pl.BlockSpec((HG_WIDTH, D_MODEL), const),
                  pl.BlockSpec((D_MODEL, D_MODEL), const),
                  pl.BlockSpec((2, N_EXPERTS, D_MODEL), lambda i: (0, 0, 0))],
        out_specs=[pl.BlockSpec((tm, D_MODEL), tok),
                   pl.BlockSpec((2, tm, PACK_W), lambda i: (0, i, 0)),
                   pl.BlockSpec((N_EXPERTS, tm), lambda i: (0, i))],
        out_shape=[jax.ShapeDtypeStruct((n, D_MODEL), F32),
                   jax.ShapeDtypeStruct((2, n, PACK_W), jnp.uint32),
                   jax.ShapeDtypeStruct((N_EXPERTS, n), F32)],
        compiler_params=_params(56, ("arbitrary",)),
        name="merge",
    )(x_p, x_s, att, o_f, o_b, proj, proj, proj, mod, hg_norm.reshape(1, HG_WIDTH),
      norm_ffn.reshape(1, D_MODEL), wa, wb, wo, wr)


def _expert_kernel(be_ref, nused_ref, next_ref, slot_ref, xs_ref, wg_hbm, wu_hbm, wd_hbm, ys_ref,
                   wg_s, wu_s, wd_s, wg_st, wu_st, wd_st, sems):
    def weight_copies(e, slot):
        pairs = ((wg_hbm, wg_st), (wu_hbm, wu_st), (wd_hbm, wd_st))
        return [pltpu.make_async_copy(hbm.at[e], stage.at[slot], sems.at[slot, j])
                for j, (hbm, stage) in enumerate(pairs)]

    def block(i, rows):
        e = be_ref[i]
        slot = slot_ref[i]
        nxt = next_ref[i]
        last = nused_ref[0] - 1
        first = (i == 0) | (e != be_ref[jnp.maximum(i - 1, 0)])
        only = (i == last) | (be_ref[jnp.minimum(i + 1, last)] != e)
        second = (i >= 1) & jnp.logical_not(first) & (
            (i == 1) | (be_ref[jnp.maximum(i - 2, 0)] != e))

        @pl.when(first)
        def _():
            @pl.when(i == 0)
            def _():
                for c in weight_copies(e, slot):
                    c.start()

            for c in weight_copies(e, slot):
                c.wait()

            @pl.when(nxt >= 0)
            def _():
                ahead = weight_copies(nxt, 1 - slot)
                ahead[0].start()
                ahead[1].start()

                @pl.when(only)
                def _():
                    ahead[2].start()

            wg_s[...] = wg_st[slot].astype(BF16)
            wu_s[...] = wu_st[slot].astype(BF16)
            wd_s[...] = wd_st[slot].astype(BF16)

        @pl.when(second & (nxt >= 0))
        def _():
            weight_copies(nxt, 1 - slot)[2].start()

        xq = _unpack_rows(xs_ref[0, rows, :], xs_ref[1, rows, :])
        a = _dot_quarters(xq, wg_s)
        b = _dot_quarters(xq, wu_s)
        h = (a * _sigmoid(a) * b).astype(BF16)
        ys_ref[0, rows, :], ys_ref[1, rows, :] = _pack_rows(
            jnp.dot(h, wd_s[...], preferred_element_type=F32))

    for j in range(MOE_SUB):
        i = pl.program_id(0) * MOE_SUB + j
        pl.when(i < nused_ref[0])(functools.partial(block, i, slice(j * MOE_BLOCK, (j + 1) * MOE_BLOCK)))


def _experts(xs, plan, wg, wu, wd):
    n_rows = xs.shape[1]
    step_rows = MOE_SUB * MOE_BLOCK
    rows_map = lambda i, be, nu, nx, sl: (0, jnp.minimum(i, (nu[0] - 1) // MOE_SUB), 0)
    whole = pl.BlockSpec(memory_space=pl.ANY)
    grid_spec = pltpu.PrefetchScalarGridSpec(
        num_scalar_prefetch=4,
        grid=(n_rows // step_rows,),
        in_specs=[pl.BlockSpec((2, step_rows, PACK_W), rows_map), whole, whole, whole],
        out_specs=pl.BlockSpec((2, step_rows, PACK_W), rows_map),
        scratch_shapes=[pltpu.VMEM((D_MODEL, D_EXPERT), BF16),
                        pltpu.VMEM((D_MODEL, D_EXPERT), BF16),
                        pltpu.VMEM((D_EXPERT, D_MODEL), BF16),
                        pltpu.VMEM((2, D_MODEL, D_EXPERT), F32),
                        pltpu.VMEM((2, D_MODEL, D_EXPERT), F32),
                        pltpu.VMEM((2, D_EXPERT, D_MODEL), F32),
                        pltpu.SemaphoreType.DMA((2, 3))],
    )
    return pl.pallas_call(
        _expert_kernel,
        grid_spec=grid_spec,
        out_shape=jax.ShapeDtypeStruct((2, n_rows, PACK_W), jnp.uint32),
        compiler_params=_params(48, ("arbitrary",)),
        name="experts",
    )(*plan, xs, wg, wu, wd)


ROUTE_TILE = 128
ROUTE_STEP = 1024
NEG_INF = float("-inf")


def _first_max(x, idx, big):
    m = jnp.max(x, axis=0, keepdims=True)
    first = jnp.min(jnp.where(x == m, idx, big), axis=0, keepdims=True)
    return m, first


def _route_tile(lg, bias, tri, base):
    per_group = N_EXPERTS // N_GROUPS
    scores = jax.nn.sigmoid(lg)
    sel = scores + bias
    lrow = lax.broadcasted_iota(jnp.int32, (per_group, ROUTE_TILE), 0)
    gs = []
    for g in range(N_GROUPS):
        x = sel[g * per_group:(g + 1) * per_group]
        m1, first = _first_max(x, lrow, per_group)
        m2 = jnp.max(jnp.where(lrow == first, NEG_INF, x), axis=0, keepdims=True)
        gs.append(m1 + m2)
    cur = jnp.concatenate(gs, axis=0)
    grow = lax.broadcasted_iota(jnp.int32, (N_GROUPS, ROUTE_TILE), 0)
    chosen = jnp.zeros((N_GROUPS, ROUTE_TILE), jnp.int32)
    for _ in range(TOPK_GROUPS):
        _, first = _first_max(cur, grow, N_GROUPS)
        hit = grow == first
        chosen = jnp.where(hit, 1, chosen)
        cur = jnp.where(hit, NEG_INF, cur)
    cur = jnp.concatenate(
        [jnp.where(chosen[g:g + 1] > 0, sel[g * per_group:(g + 1) * per_group], NEG_INF)
         for g in range(N_GROUPS)], axis=0)
    row = lax.broadcasted_iota(jnp.int32, (N_EXPERTS, ROUTE_TILE), 0)
    member = jnp.zeros((N_EXPERTS, ROUTE_TILE), F32)
    es, ws = [], []
    for _ in range(TOP_K):
        _, first = _first_max(cur, row, N_EXPERTS)
        hit = row == first
        es.append(first)
        ws.append(jnp.sum(jnp.where(hit, scores, 0.0), axis=0, keepdims=True))
        cur = jnp.where(hit, NEG_INF, cur)
        member = jnp.where(hit, 1.0, member)
    e = jnp.concatenate(es, axis=0)
    w = jnp.concatenate(ws, axis=0)
    w = w / jnp.sum(w, axis=0, keepdims=True) * ROUTE_SCALE
    before = jnp.dot(member.astype(BF16), tri, preferred_element_type=F32) + base
    rank = jnp.concatenate(
        [jnp.sum(jnp.where(row == es[k], before, 0.0), axis=0, keepdims=True) for k in range(TOP_K)],
        axis=0)
    return e, w, rank.astype(jnp.int32), member


def _route_kernel(lg_ref, b_ref, e_ref, w_ref, r_ref, cnt_ref, base):
    @pl.when(pl.program_id(0) == 0)
    def _():
        base[...] = jnp.zeros_like(base)

    r_i = lax.broadcasted_iota(jnp.int32, (ROUTE_TILE, ROUTE_TILE), 0)
    c_i = lax.broadcasted_iota(jnp.int32, (ROUTE_TILE, ROUTE_TILE), 1)
    tri = (r_i < c_i).astype(BF16)
    bias = b_ref[...]
    for j in range(ROUTE_STEP // ROUTE_TILE):
        ls = slice(j * ROUTE_TILE, (j + 1) * ROUTE_TILE)
        e, w, rank, member = _route_tile(lg_ref[:, ls], bias, tri, base[...])
        e_ref[:, ls] = e
        w_ref[:, ls] = w
        r_ref[:, ls] = rank
        base[...] = base[...] + jnp.sum(member, axis=1, keepdims=True)
    cnt_ref[...] = base[...]


def _route(logits_t, b_router):
    n = logits_t.shape[1]
    tok = lambda i: (0, i)
    return pl.pallas_call(
        _route_kernel,
        grid=(n // ROUTE_STEP,),
        in_specs=[pl.BlockSpec((N_EXPERTS, ROUTE_STEP), tok),
                  pl.BlockSpec((N_EXPERTS, 1), lambda i: (0, 0))],
        out_specs=[pl.BlockSpec((TOP_K, ROUTE_STEP), tok),
                   pl.BlockSpec((TOP_K, ROUTE_STEP), tok),
                   pl.BlockSpec((TOP_K, ROUTE_STEP), tok),
                   pl.BlockSpec((N_EXPERTS, 1), lambda i: (0, 0))],
        out_shape=[jax.ShapeDtypeStruct((TOP_K, n), jnp.int32),
                   jax.ShapeDtypeStruct((TOP_K, n), F32),
                   jax.ShapeDtypeStruct((TOP_K, n), jnp.int32),
                   jax.ShapeDtypeStruct((N_EXPERTS, 1), F32)],
        scratch_shapes=[pltpu.VMEM((N_EXPERTS, 1), F32)],
        compiler_params=_params(32, ("arbitrary",)),
        name="route",
    )(logits_t, b_router.reshape(N_EXPERTS, 1).astype(F32))


def _pos_kernel(e_ref, r_ref, ps_ref, pos_ref):
    row = lax.broadcasted_iota(jnp.int32, (N_EXPERTS, ROUTE_TILE), 0)
    pstart = ps_ref[...]
    for j in range(ROUTE_STEP // ROUTE_TILE):
        ls = slice(j * ROUTE_TILE, (j + 1) * ROUTE_TILE)
        e = e_ref[:, ls]
        off = jnp.concatenate(
            [jnp.sum(jnp.where(row == e[k:k + 1], pstart, 0.0), axis=0, keepdims=True)
             for k in range(TOP_K)], axis=0)
        pos_ref[:, ls] = off.astype(jnp.int32) + r_ref[:, ls]


def _positions(eidx, rank, pstarts):
    n = eidx.shape[1]
    tok = lambda i: (0, i)
    return pl.pallas_call(
        _pos_kernel,
        grid=(n // ROUTE_STEP,),
        in_specs=[pl.BlockSpec((TOP_K, ROUTE_STEP), tok),
                  pl.BlockSpec((TOP_K, ROUTE_STEP), tok),
                  pl.BlockSpec((N_EXPERTS, 1), lambda i: (0, 0))],
        out_specs=pl.BlockSpec((TOP_K, ROUTE_STEP), tok),
        out_shape=jax.ShapeDtypeStruct((TOP_K, n), jnp.int32),
        compiler_params=_params(32, ("arbitrary",)),
        name="positions",
    )(eidx, rank, pstarts.reshape(N_EXPERTS, 1).astype(F32))


def _block_tables(counts, n_blocks):
    counts = counts.reshape(N_EXPERTS).astype(jnp.int32)
    padded = (counts + MOE_BLOCK - 1) // MOE_BLOCK * MOE_BLOCK
    pend = jnp.cumsum(padded)
    pstarts = pend - padded
    first_row = jnp.arange(n_blocks, dtype=jnp.int32) * MOE_BLOCK
    block_e = jnp.sum((pend[None, :] <= first_row[:, None]).astype(jnp.int32), axis=1)
    block_e = jnp.minimum(block_e, N_EXPERTS - 1).astype(jnp.int32)
    n_used = (pend[-1] // MOE_BLOCK).astype(jnp.int32).reshape(1)
    ids = jnp.arange(N_EXPERTS, dtype=jnp.int32)
    used = padded > 0
    ordinal = jnp.cumsum(used.astype(jnp.int32)) - 1
    from_here = jnp.flip(lax.cummin(jnp.flip(jnp.where(used, ids, N_EXPERTS))))
    after = jnp.concatenate([from_here[1:], jnp.full((1,), N_EXPERTS, jnp.int32)])
    after = jnp.where(after >= N_EXPERTS, -1, after)
    mine = block_e[:, None] == ids[None, :]
    next_e = jnp.sum(jnp.where(mine, after[None, :], 0), axis=1).astype(jnp.int32)
    slot = jnp.sum(jnp.where(mine, ordinal[None, :] % 2, 0), axis=1).astype(jnp.int32)
    return pstarts, (block_e, n_used, next_e, slot)


SC_WINDOW = 128


def _sc_mesh():
    return plsc.VectorSubcoreMesh(core_axis_name="core", subcore_axis_name="subcore")


def _both_halves(pos, n_rows):
    return jnp.concatenate([pos, pos + n_rows]).reshape(1, -1)


def _sc_dispatch(rows, pos, n_rows):
    _, n, w = rows.shape
    tiles = n // SC_WINDOW
    top_k = pos.shape[0] // n
    idx = jnp.transpose(pos.reshape(top_k, tiles, SC_WINDOW), (1, 0, 2))
    idx = jnp.stack([idx, idx + n_rows]).reshape(1, -1)

    @functools.partial(pl.kernel, out_type=jax.ShapeDtypeStruct((2 * n_rows, w), rows.dtype),
                       mesh=_sc_mesh(), scratch_types=[])
    def scatter_kernel(x_hbm, i_hbm, o_hbm):
        def body(x_vmem, i_vmem):
            pltpu.sync_copy(x_vmem, o_hbm.at[i_vmem.at[0]])

        pltpu.emit_pipeline(
            body,
            grid=(2 * tiles * top_k,),
            in_specs=[pl.BlockSpec((SC_WINDOW, w), lambda i: (i // top_k, 0)),
                      pl.BlockSpec((1, SC_WINDOW), lambda i: (0, i))],
            out_specs=[],
            core_axis_name=("core", "subcore"),
            dimension_semantics=(pltpu.PARALLEL,),
        )(x_hbm, i_hbm)

    out = scatter_kernel(rows.reshape(2 * n, w), idx)
    return out.reshape(2, n_rows, w)


def _sc_gather(table, pos):
    _, n_rows, w = table.shape
    m = pos.shape[0]

    @functools.partial(pl.kernel, out_type=jax.ShapeDtypeStruct((2 * m, w), table.dtype),
                       mesh=_sc_mesh(), scratch_types=[])
    def gather_kernel(t_hbm, i_hbm, o_hbm):
        def body(i_vmem, o_vmem):
            pltpu.sync_copy(t_hbm.at[i_vmem.at[0]], o_vmem)

        pltpu.emit_pipeline(
            body,
            grid=(2 * m // SC_WINDOW,),
            in_specs=[pl.BlockSpec((1, SC_WINDOW), lambda i: (0, i))],
            out_specs=[pl.BlockSpec((SC_WINDOW, w), lambda i: (i, 0))],
            core_axis_name=("core", "subcore"),
            dimension_semantics=(pltpu.PARALLEL,),
        )(i_hbm, o_hbm)

    out = gather_kernel(table.reshape(2 * n_rows, w), _both_halves(pos, n_rows))
    return out.reshape(2, m, w)


def _final_kernel(npt, x1_ref, u2_ref, yg_ref, w_ref, mod_ref, nf_ref, wsg_ref, wsu_ref, wsd_ref,
                  op_ref, os_ref):
    uq = _unpack_rows(u2_ref[0], u2_ref[1])
    a = _dot_quarters(uq, wsg_ref)
    b = _dot_quarters(uq, wsu_ref)
    f = jnp.dot((a * _sigmoid(a) * b).astype(BF16), wsd_ref[...], preferred_element_type=F32)
    w = w_ref[...]
    fq = [f[:, q * PACK_W:(q + 1) * PACK_W] for q in range(4)]
    for k in range(TOP_K):
        yq = _unpack_rows(yg_ref[0, k], yg_ref[1, k], F32)
        fq = [fq[q] + w[:, k:k + 1] * yq[q] for q in range(4)]
    x2 = x1_ref[...] + mod_ref[5:6, :] * jnp.concatenate(fq, axis=1)
    out = _rms(x2) * nf_ref[...]
    i = pl.program_id(0)

    @pl.when(i < npt)
    def _():
        op_ref[...] = out

    @pl.when(i >= npt)
    def _():
        os_ref[...] = out


def _final(seqs, x1, u2, yg, w, mod, norm_final, wsg, wsu, wsd):
    n = seqs.n
    tm = 512
    tok = lambda i: (i, 0)
    const = lambda i: (0, 0)
    npt, p_spec, s_spec = _two_stream_specs(seqs, tm)
    return pl.pallas_call(
        functools.partial(_final_kernel, npt),
        grid=(n // tm,),
        in_specs=[pl.BlockSpec((tm, D_MODEL), tok),
                  pl.BlockSpec((2, tm, PACK_W), lambda i: (0, i, 0)),
                  pl.BlockSpec((2, TOP_K, tm, PACK_W), lambda i: (0, 0, i, 0)),
                  pl.BlockSpec((tm, TOP_K), tok),
                  pl.BlockSpec((None, N_MOD, D_MODEL), lambda i: (seqs.info(i * tm)[0], 0, 0)),
                  pl.BlockSpec((1, D_MODEL), const),
                  pl.BlockSpec((D_MODEL, D_EXPERT), const),
                  pl.BlockSpec((D_MODEL, D_EXPERT), const),
                  pl.BlockSpec((D_EXPERT, D_MODEL), const)],
        out_specs=[p_spec, s_spec],
        out_shape=[jax.ShapeDtypeStruct((seqs.np_, D_MODEL), F32),
                   jax.ShapeDtypeStruct((n - seqs.np_, D_MODEL), F32)],
        compiler_params=_params(48, ("arbitrary",)),
        name="final",
    )(x1, u2, yg, w, mod, norm_final.reshape(1, D_MODEL), wsg, wsu, wsd)


def _layer(seqs, x_p, x_s, c, w_ada, b_ada, norm_mix, w_in, na_rpb, hg_lb, hg_norm, w_branch_a,
           w_branch_b, w_out, norm_ffn, w_router, b_router, w_exp_gate, w_exp_up, w_exp_down,
           w_sh_gate, w_sh_up, w_sh_down, norm_final):
    n = seqs.n
    c_rows = -(-seqs.nseq // 8) * 8
    c_pad = jnp.zeros((c_rows, D_MODEL), F32).at[:seqs.nseq].set(c)
    mod = _ada(c_pad, w_ada[0], b_ada[0])[:seqs.nseq].reshape(seqs.nseq, N_MOD, D_MODEL)
    lb = jnp.cumsum(jax.nn.softmax(hg_lb.astype(F32), axis=0), axis=0)[0]

    proj = _inproj(seqs, x_p, x_s, mod, norm_mix[0], w_in[0].astype(BF16))
    att = _na(seqs, proj, _na_bias_table(na_rpb[0]))
    o_f, o_b = _hgrn(seqs, proj, lb)
    x1, u2, logits = _merge(seqs, x_p, x_s, att, o_f, o_b, proj, mod, hg_norm[0], norm_ffn[0],
                            w_branch_a[0].astype(BF16), w_branch_b[0].astype(BF16),
                            w_out[0].astype(BF16), _split_hi_lo(w_router[0].T))

    eidx, w, rank, counts = _route(logits, b_router[0])
    step_rows = MOE_SUB * MOE_BLOCK
    n_rows = -(-(n * TOP_K + N_EXPERTS * MOE_BLOCK) // step_rows) * step_rows
    pstarts, plan = _block_tables(counts, n_rows // MOE_BLOCK)
    pos = _positions(eidx, rank, pstarts).reshape(-1)
    xs = _sc_dispatch(u2, pos, n_rows)
    ys = _experts(xs, plan, w_exp_gate[0], w_exp_up[0], w_exp_down[0])
    yg = _sc_gather(ys, pos).reshape(2, TOP_K, n, PACK_W)
    return _final(seqs, x1, u2, yg, w.T, mod, norm_final, w_sh_gate[0].astype(BF16),
                  w_sh_up[0].astype(BF16), w_sh_down[0].astype(BF16))


def kernel(x_prompt, x_sample, c_prompt, c_sample, w_ada, b_ada, norm_mix, w_in, na_rpb, hg_lb, hg_norm, w_branch_a, w_branch_b, w_out, norm_ffn, w_router, b_router, w_exp_gate, w_exp_up, w_exp_down, w_sh_gate, w_sh_up, w_sh_down, norm_final):
    bp, tp, _ = x_prompt.shape
    bs, ts, _ = x_sample.shape
    seqs = _Seqs(bp, tp, bs, ts)
    c = jnp.concatenate([c_prompt, c_sample])
    y_p, y_s = _layer(seqs, x_prompt.reshape(bp * tp, D_MODEL), x_sample.reshape(bs * ts, D_MODEL),
                      c, w_ada, b_ada, norm_mix, w_in, na_rpb, hg_lb, hg_norm, w_branch_a,
                      w_branch_b, w_out, norm_ffn, w_router, b_router, w_exp_gate, w_exp_up,
                      w_exp_down, w_sh_gate, w_sh_up, w_sh_down, norm_final)
    return (y_p.reshape(bp, tp, D_MODEL), y_s.reshape(bs, ts, D_MODEL))
```

```python
import functools

import jax
import jax.numpy as jnp
import numpy as np
from jax import lax
from jax.experimental import pallas as pl
from jax.experimental.pallas import tpu as pltpu
from jax.experimental.pallas import tpu_sc as plsc

D_MODEL = 1024
GRID_W = 64
NA_HEADS = 8
NA_HEAD_DIM = 64
NA_WIDTH = NA_HEADS * NA_HEAD_DIM
NA_ROWS = 8
NA_COLS = 16
HG_HEADS = 4
HG_KEY_DIM = 128
HG_WIDTH = HG_HEADS * HG_KEY_DIM
HG_CHUNK = 128
HG_EXP_LIMIT = 80.0
N_EXPERTS = 256
TOP_K = 8
N_GROUPS = 8
TOPK_GROUPS = 4
D_EXPERT = 256
ROUTE_SCALE = 2.5
N_MOD = 6
RMS_EPS = 1e-6

MOE_BLOCK = 512
MOE_SUB = 8
NA_GROUP = 4
NA_WIN = 3 * NA_GROUP
NA_TOK = NA_GROUP * GRID_W
LOG2E = 1.4426950408889634
NA_Q_SCALE = NA_HEAD_DIM ** -0.5 * LOG2E
HG_STEP = 512
MASK_VALUE = -1e30

F32 = jnp.float32
BF16 = jnp.bfloat16
HIGHEST = lax.Precision.HIGHEST
NT_DIMS = (((1,), (1,)), ((), ()))
TN_DIMS = (((0,), (0,)), ((), ()))

PROJ_TN = 2048
SLAB_Q, SLAB_K, SLAB_V, SLAB_HQ = (0, 0), (0, 1), (0, 2), (0, 3)
SLAB_FF, SLAB_FB, SLAB_HI, SLAB_HG = (1, 0), (1, 1), (1, 2), (1, 3)
SLAB_GA, SLAB_GB = (2, 0), (2, 1)


def _params(vmem_mb, sem=None):
    kw = dict(vmem_limit_bytes=vmem_mb * 1024 * 1024)
    if sem is not None:
        kw["dimension_semantics"] = sem
    return pltpu.CompilerParams(**kw)


class _Seqs:
    def __init__(self, bp, tp, bs, ts):
        self.bp, self.tp, self.bs, self.ts = bp, tp, bs, ts
        self.np_ = bp * tp
        self.n = bp * tp + bs * ts
        self.nseq = bp + bs

    def info(self, t0):
        in_p = t0 < self.np_
        rel = jnp.maximum(t0 - self.np_, 0)
        sid = jnp.where(in_p, t0 // self.tp, self.bp + rel // self.ts)
        start = jnp.where(in_p, (t0 // self.tp) * self.tp, self.np_ + (rel // self.ts) * self.ts)
        length = jnp.where(in_p, self.tp, self.ts)
        return sid, start, length


def _ada_kernel(c_ref, w_ref, b_ref, o_ref):
    c = c_ref[...]
    a = c * jax.nn.sigmoid(c)
    o_ref[...] = jnp.dot(a, w_ref[...], precision=HIGHEST, preferred_element_type=F32) + b_ref[...]


def _ada(c_pad, w_ada, b_ada):
    rows = c_pad.shape[0]
    n_out = w_ada.shape[1]
    tn = 1024
    return pl.pallas_call(
        _ada_kernel,
        grid=(n_out // tn,),
        in_specs=[pl.BlockSpec((rows, D_MODEL), lambda j: (0, 0)),
                  pl.BlockSpec((D_MODEL, tn), lambda j: (0, j)),
                  pl.BlockSpec((1, tn), lambda j: (0, j))],
        out_specs=pl.BlockSpec((rows, tn), lambda j: (0, j)),
        out_shape=jax.ShapeDtypeStruct((rows, n_out), F32),
        compiler_params=_params(32),
        name="ada",
    )(c_pad, w_ada, b_ada.reshape(1, n_out))


def _rms(x):
    return x * lax.rsqrt(jnp.mean(x * x, axis=-1, keepdims=True) + RMS_EPS)


def _sigmoid(x):
    return 0.5 * jnp.tanh(0.5 * x) + 0.5


PACK_W = D_MODEL // 4


def _pack_rows(x):
    out = []
    for h in range(2):
        lo = x[:, (2 * h) * PACK_W:(2 * h + 1) * PACK_W].astype(BF16).astype(F32)
        hi = x[:, (2 * h + 1) * PACK_W:(2 * h + 2) * PACK_W].astype(BF16).astype(F32)
        out.append(lax.bitcast_convert_type(hi, jnp.uint32)
                   | (lax.bitcast_convert_type(lo, jnp.uint32) >> 16))
    return out


def _unpack_rows(p0, p1, dtype=BF16):
    quarters = []
    for p in (p0, p1):
        quarters.append(lax.bitcast_convert_type(p << 16, F32).astype(dtype))
        quarters.append(lax.bitcast_convert_type(p & jnp.uint32(0xFFFF0000), F32).astype(dtype))
    return quarters


def _dot_quarters(quarters, w_ref):
    acc = None
    for q, xq in enumerate(quarters):
        part = jnp.dot(xq, w_ref[q * PACK_W:(q + 1) * PACK_W, :], preferred_element_type=F32)
        acc = part if acc is None else acc + part
    return acc


def _two_stream_specs(seqs, tm, grid_rank=1):
    npt = seqs.np_ // tm
    nst = (seqs.n - seqs.np_) // tm
    if grid_rank == 1:
        p_map = lambda i: (jnp.minimum(i, npt - 1), 0)
        s_map = lambda i: (jnp.clip(i - npt, 0, nst - 1), 0)
    else:
        p_map = lambda i, j: (jnp.minimum(i, npt - 1), 0)
        s_map = lambda i, j: (jnp.clip(i - npt, 0, nst - 1), 0)
    return npt, pl.BlockSpec((tm, D_MODEL), p_map), pl.BlockSpec((tm, D_MODEL), s_map)


def _inproj_kernel(npt, xp_ref, xs_ref, mod_ref, g_ref, w_ref, cs_ref, o_ref, u_scr):
    @pl.when(pl.program_id(1) == 0)
    def _():
        x = jnp.where(pl.program_id(0) < npt, xp_ref[...], xs_ref[...])
        y = _rms(x) * g_ref[...]
        u = y * (1.0 + mod_ref[1:2, :]) + mod_ref[0:1, :]
        u_scr[...] = u.astype(BF16)

    acc = jnp.dot(u_scr[...], w_ref[...], preferred_element_type=F32)
    o_ref[...] = (acc * cs_ref[...]).astype(o_ref.dtype)


def _inproj(seqs, x_p, x_s, mod, norm_mix, w_in_bf):
    n = seqs.n
    tm = min(1024, seqs.tp, seqs.ts)
    tn = PROJ_TN
    n_slab = w_in_bf.shape[1] // tn
    npt, p_spec, s_spec = _two_stream_specs(seqs, tm, grid_rank=2)
    col_scale = jnp.ones((1, w_in_bf.shape[1]), F32).at[:, :NA_WIDTH].set(NA_Q_SCALE)
    return pl.pallas_call(
        functools.partial(_inproj_kernel, npt),
        grid=(n // tm, n_slab),
        in_specs=[p_spec, s_spec,
                  pl.BlockSpec((None, N_MOD, D_MODEL), lambda i, j: (seqs.info(i * tm)[0], 0, 0)),
                  pl.BlockSpec((1, D_MODEL), lambda i, j: (0, 0)),
                  pl.BlockSpec((D_MODEL, tn), lambda i, j: (0, j)),
                  pl.BlockSpec((1, tn), lambda i, j: (0, j))],
        out_specs=pl.BlockSpec((None, tm, tn), lambda i, j: (j, i, 0)),
        out_shape=jax.ShapeDtypeStruct((n_slab, n, tn), BF16),
        scratch_shapes=[pltpu.VMEM((tm, D_MODEL), BF16)],
        compiler_params=_params(56, ("arbitrary", "arbitrary")),
        name="inproj",
    )(x_p, x_s, mod, norm_mix.reshape(1, D_MODEL), w_in_bf, col_scale)


def _na_bias_table(rpb):
    col = np.arange(GRID_W)
    cs = np.clip(col - NA_COLS // 2, 0, GRID_W - NA_COLS)
    valid = (col[None, :] >= cs[:, None]) & (col[None, :] < cs[:, None] + NA_COLS)
    coff = col[None, :] - col[:, None] + NA_COLS - 1
    onehot = (coff[None] == np.arange(2 * NA_COLS - 1)[:, None, None]) & valid[None]
    toep = jnp.einsum("hrc,cqk->hrqk", rpb.astype(F32), jnp.asarray(onehot, F32),
                      precision=HIGHEST)
    toep = jnp.where(valid[None, None], toep * LOG2E, MASK_VALUE)
    masked =jnp.full((NA_HEADS, GRID_W, GRID_W), MASK_VALUE, F32)
    cases = (([0] * NA_GROUP, [NA_ROWS - 1 - i for i in range(NA_GROUP)]),
             (list(range(NA_GROUP)), [NA_ROWS // 2 - 1] * NA_GROUP),
             ([NA_GROUP] * NA_GROUP, [NA_ROWS // 2 - 1 - i for i in range(NA_GROUP)]))
    tabs = []
    for first_row, first_off in cases:
        q_rows = []
        for i in range(NA_GROUP):
            blocks = [toep[:, first_off[i] + w - first_row[i]]
                      if 0 <= w - first_row[i] < NA_ROWS else masked for w in range(NA_WIN)]
            q_rows.append(jnp.concatenate(blocks, axis=2))
        tabs.append(jnp.concatenate(q_rows, axis=1))
    return jnp.stack(tabs)


def _na_geometry(seqs, g):
    _, start, length = seqs.info(g * NA_TOK)
    row0 = start // GRID_W
    rows = length // GRID_W
    r0 = g * NA_GROUP - row0
    wb = jnp.clip(r0 - NA_ROWS // 2, 0, rows - NA_WIN)
    case = jnp.where(r0 == 0, 0, jnp.where(r0 == rows - NA_GROUP, 2, 1))
    return (row0 + wb) // NA_GROUP, case


def _na_kernel(q_ref, k0, k1, k2, v0, v1, v2, bias_ref, o_ref):
    k_refs = (k0, k1, k2)
    v_refs = (v0, v1, v2)
    outs = []
    for h in range(NA_HEADS):
        hs = slice(h * NA_HEAD_DIM, (h + 1) * NA_HEAD_DIM)
        q = q_ref[:, hs]
        s = [lax.dot_general(q, kr[:, hs], NT_DIMS, preferred_element_type=F32)
             + bias_ref[h, :, d * NA_TOK:(d + 1) * NA_TOK] for d, kr in enumerate(k_refs)]
        m = jnp.max(jnp.maximum(jnp.maximum(s[0], s[1]), s[2]), axis=-1, keepdims=True)
        p = [jnp.exp2(sd - m) for sd in s]
        l = jnp.sum((p[0] + p[1]) + p[2], axis=-1, keepdims=True)
        o = sum(jnp.dot(pd.astype(BF16), vr[:, hs], preferred_element_type=F32)
                for pd, vr in zip(p, v_refs))
        outs.append(o / l)
    o_ref[...] = jnp.concatenate(outs, axis=1).astype(o_ref.dtype)


def _na(seqs, proj, bias_tab):
    n = seqs.n

    def kv_spec(slab, d):
        return pl.BlockSpec((None, NA_TOK, NA_WIDTH),
                            lambda g: (slab[0], _na_geometry(seqs, g)[0] + d, slab[1]))

    return pl.pallas_call(
        _na_kernel,
        grid=(n // NA_TOK,),
        in_specs=[pl.BlockSpec((None, NA_TOK, NA_WIDTH), lambda g: (SLAB_Q[0], g, SLAB_Q[1]))]
        + [kv_spec(SLAB_K, d) for d in range(3)] + [kv_spec(SLAB_V, d) for d in range(3)]
        + [pl.BlockSpec((None,) + bias_tab.shape[1:],
                        lambda g: (_na_geometry(seqs, g)[1], 0, 0, 0))],
        out_specs=pl.BlockSpec((NA_TOK, NA_WIDTH), lambda g: (g, 0)),
        out_shape=jax.ShapeDtypeStruct((n, NA_WIDTH), BF16),
        compiler_params=_params(48, ("arbitrary",)),
        name="natten",
    )(proj, proj, proj, proj, proj, proj, proj, bias_tab)


def _hg_chunk(q, z, v, lb, tri, mask, mid, last, st_ref):
    sig = _sigmoid(z)
    f = lb + (1.0 - lb) * sig
    lf = jnp.log(f)
    kin = (1.0 - lb) * (1.0 - sig)
    hi = lf.astype(BF16)
    lo = (lf - hi.astype(F32)).astype(BF16)
    g2 = jnp.dot(tri, jnp.concatenate([lo, hi], axis=1), preferred_element_type=F32)
    gcum = g2[:, :HG_WIDTH] + g2[:, HG_WIDTH:]
    gm = gcum[mid:mid + 1, :]
    gl = gcum[last:last + 1, :]
    up = jnp.exp(gcum - gm)
    dn = jnp.exp(gm - gcum)
    qa = (q * up).astype(BF16)
    ka = (kin * dn).astype(BF16)
    qe = (q * (up * jnp.exp(gm))).astype(BF16)
    kd = (kin * (dn * jnp.exp(gl - gm))).astype(BF16)
    eg = jnp.exp(gl)
    vb = v.astype(BF16)
    outs = []
    for h in range(HG_HEADS):
        hs = slice(h * HG_KEY_DIM, (h + 1) * HG_KEY_DIM)
        a = lax.dot_general(qa[:, hs], ka[:, hs], NT_DIMS, preferred_element_type=F32)
        a = jnp.where(mask, a, 0.0)
        st = st_ref[h]
        o = jnp.dot(a.astype(BF16), vb[:, hs], preferred_element_type=F32)
        o = o + lax.dot_general(qe[:, hs], st.astype(BF16), NT_DIMS, preferred_element_type=F32)
        st_ref[h] = st * eg[:, hs] + lax.dot_general(vb[:, hs], kd[:, hs], TN_DIMS,
                                                    preferred_element_type=F32)
        outs.append(o)
    return jnp.concatenate(outs, axis=1)


def _hg_exact(q_ref, z_ref, v_ref, lb, reverse, st_ref, o_ref, qs, fs, ks, vs, os):
    sig = jax.nn.sigmoid(z_ref[...].astype(F32))
    qs[...] = q_ref[...].astype(F32)
    fs[...] = lb + (1.0 - lb) * sig
    ks[...] = (1.0 - lb) * (1.0 - sig)
    vs[...] = v_ref[...].astype(F32)
    eye = (lax.broadcasted_iota(jnp.int32, (HG_KEY_DIM, HG_KEY_DIM), 0)
           == lax.broadcasted_iota(jnp.int32, (HG_KEY_DIM, HG_KEY_DIM), 1)).astype(F32)

    def body(i, carry):
        t = HG_STEP - 1 - i if reverse else i
        q_t, f_t, k_t, v_t = (r[pl.ds(t, 1), :] for r in (qs, fs, ks, vs))
        outs = []
        for h in range(HG_HEADS):
            hs = slice(h * HG_KEY_DIM, (h + 1) * HG_KEY_DIM)
            v_col = jnp.sum(eye * v_t[:, hs], axis=1, keepdims=True)
            st = st_ref[h] * f_t[:, hs] + v_col * k_t[:, hs]
            st_ref[h] = st
            o_col = jnp.sum(st * q_t[:, hs], axis=1, keepdims=True)
            outs.append(jnp.sum(eye * o_col, axis=0, keepdims=True))
        os[pl.ds(t, 1), :] = jnp.concatenate(outs, axis=1)
        return carry

    lax.fori_loop(0, HG_STEP, body, 0)
    o_ref[...] = os[...].astype(o_ref.dtype)


def _hg_kernel(seqs, safe_ref, qf_ref, zf_ref, vf_ref, qb_ref, zb_ref, vb_ref, lb_ref, of_ref,
               ob_ref, stf, stb, qs, fs, ks, vs, os):
    i = pl.program_id(0)
    nsteps = pl.num_programs(0)
    tf = i * HG_STEP
    tb = (nsteps - 1 - i) * HG_STEP
    _, start_f, _ = seqs.info(tf)
    _, start_b, len_b = seqs.info(tb)

    @pl.when(tf == start_f)
    def _():
        stf[...] = jnp.zeros_like(stf)

    @pl.when(tb + HG_STEP == start_b + len_b)
    def _():
        stb[...] = jnp.zeros_like(stb)

    lb = lb_ref[...]

    @pl.when(safe_ref[0] > 0)
    def _():
        row = lax.broadcasted_iota(jnp.int32, (HG_CHUNK, HG_CHUNK), 0)
        col = lax.broadcasted_iota(jnp.int32, (HG_CHUNK, HG_CHUNK), 1)
        lower = row >= col
        upper = col >= row
        tri_f = lower.astype(BF16)
        tri_b = upper.astype(BF16)
        nchunk = HG_STEP // HG_CHUNK
        for c in range(nchunk):
            cs = slice(c * HG_CHUNK, (c + 1) * HG_CHUNK)
            of_ref[cs, :] = _hg_chunk(qf_ref[cs, :].astype(F32), zf_ref[cs, :].astype(F32),
                                      vf_ref[cs, :].astype(F32), lb, tri_f, lower,
                                      HG_CHUNK // 2 - 1, HG_CHUNK - 1, stf).astype(of_ref.dtype)
            cb = nchunk - 1 - c
            bs = slice(cb * HG_CHUNK, (cb + 1) * HG_CHUNK)
            ob_ref[bs, :] = _hg_chunk(qb_ref[bs, :].astype(F32), zb_ref[bs, :].astype(F32),
                                      vb_ref[bs, :].astype(F32), lb, tri_b, upper,
                                      HG_CHUNK // 2, 0, stb).astype(ob_ref.dtype)

    @pl.when(safe_ref[0] == 0)
    def _():
        _hg_exact(qf_ref, zf_ref, vf_ref, lb, False, stf, of_ref, qs, fs, ks, vs, os)
        _hg_exact(qb_ref, zb_ref, vb_ref, lb, True, stb, ob_ref, qs, fs, ks, vs, os)


def _hgrn(seqs, proj, lb):
    n = seqs.n
    nsteps = n // HG_STEP

    def spec(slab, rev):
        if rev:
            return pl.BlockSpec((None, HG_STEP, HG_WIDTH), lambda i: (slab[0], nsteps - 1 - i, slab[1]))
        return pl.BlockSpec((None, HG_STEP, HG_WIDTH), lambda i: (slab[0], i, slab[1]))

    safe = (jnp.max(-jnp.log(lb)) * (HG_CHUNK // 2) < HG_EXP_LIMIT).astype(jnp.int32).reshape(1)
    step_scratch = pltpu.VMEM((HG_STEP, HG_WIDTH), F32)
    return pl.pallas_call(
        functools.partial(_hg_kernel, seqs),
        grid=(nsteps,),
        in_specs=[pl.BlockSpec(memory_space=pltpu.SMEM),
                  spec(SLAB_HQ, False), spec(SLAB_FF, False), spec(SLAB_HI, False),
                  spec(SLAB_HQ, True), spec(SLAB_FB, True), spec(SLAB_HI, True),
                  pl.BlockSpec((1, HG_WIDTH), lambda i: (0, 0))],
        out_specs=[pl.BlockSpec((HG_STEP, HG_WIDTH), lambda i: (i, 0)),
                   pl.BlockSpec((HG_STEP, HG_WIDTH), lambda i: (nsteps - 1 - i, 0))],
        out_shape=[jax.ShapeDtypeStruct((n, HG_WIDTH), BF16)] * 2,
        scratch_shapes=[pltpu.VMEM((HG_HEADS, HG_KEY_DIM, HG_KEY_DIM), F32)] * 2
        + [step_scratch] * 5,
        compiler_params=_params(32, ("arbitrary",)),
        name="hgrn2",
    )(safe, proj, proj, proj, proj, proj, proj, lb.reshape(1, HG_WIDTH))


def _merge_kernel(npt, xp_ref, xs_ref, att_ref, of_ref, ob_ref, hg_ref, ga_ref, gb_ref, mod_ref,
                  hgn_ref, nffn_ref, wa_ref, wb_ref, wo_ref, wr_ref, x1_ref, u2_ref, lg_ref):
    x = jnp.where(pl.program_id(0) < npt, xp_ref[...], xs_ref[...])
    o = of_ref[...].astype(F32) + ob_ref[...].astype(F32)
    parts = []
    for h in range(HG_HEADS):
        hs = slice(h * HG_KEY_DIM, (h + 1) * HG_KEY_DIM)
        parts.append(_rms(o[:, hs]))
    on = jnp.concatenate(parts, axis=1) * hgn_ref[...]
    gate = hg_ref[...].astype(F32)
    hb = (on * (gate * _sigmoid(gate))).astype(BF16)
    ya = jnp.dot(att_ref[...], wa_ref[...], preferred_element_type=F32)
    yb = jnp.dot(hb, wb_ref[...], preferred_element_type=F32)
    merged = (_sigmoid(ga_ref[...].astype(F32)) * ya
              + _sigmoid(gb_ref[...].astype(F32)) * yb)
    x1 = x + mod_ref[2:3, :] * jnp.dot(merged.astype(BF16), wo_ref[...],
                                       preferred_element_type=F32)
    x1_ref[...] = x1
    u2 = _rms(x1) * nffn_ref[...] * (1.0 + mod_ref[4:5, :]) + mod_ref[3:4, :]
    u2_ref[0], u2_ref[1] = _pack_rows(u2)
    w_hi = wr_ref[0]
    u_hi = u2.astype(BF16)
    u_lo = (u2 - u_hi.astype(F32)).astype(BF16)
    lg_ref[...] = (lax.dot_general(w_hi, u_hi, NT_DIMS, preferred_element_type=F32)
                   + (lax.dot_general(w_hi, u_lo, NT_DIMS, preferred_element_type=F32)
                      + lax.dot_general(wr_ref[1], u_hi, NT_DIMS, preferred_element_type=F32)))


def _split_hi_lo(w):
    hi = w.astype(BF16)
    return jnp.stack([hi, (w - hi.astype(F32)).astype(BF16)])


def _merge(seqs, x_p, x_s, att, o_f, o_b, proj, mod, hg_norm, norm_ffn, wa, wb, wo, wr):
    n = seqs.n
    tm = 512
    tok = lambda i: (i, 0)
    const = lambda i: (0, 0)
    npt, p_spec, s_spec = _two_stream_specs(seqs, tm)
    return pl.pallas_call(
        functools.partial(_merge_kernel, npt),
        grid=(n // tm,),
        in_specs=[p_spec, s_spec,
                  pl.BlockSpec((tm, NA_WIDTH), tok),
                  pl.BlockSpec((tm, HG_WIDTH), tok),
                  pl.BlockSpec((tm, HG_WIDTH), tok),
                  pl.BlockSpec((None, tm, HG_WIDTH), lambda i: (SLAB_HG[0], i, SLAB_HG[1])),
                  pl.BlockSpec((None, tm, D_MODEL), lambda i: (SLAB_GA[0], i, SLAB_GA[1])),
                  pl.BlockSpec((None, tm, D_MODEL), lambda i: (SLAB_GB[0], i, SLAB_GB[1])),
                  pl.BlockSpec((None, N_MOD, D_MODEL), lambda i: (seqs.info(i * tm)[0], 0, 0)),
                  pl.BlockSpec((1, HG_WIDTH), const),
                  pl.BlockSpec((1, D_MODEL), const),
                  pl.BlockSpec((NA_WIDTH, D_MODEL), const),
                  pl.BlockSpec((HG_WIDTH, D_MODEL), const),
                  pl.BlockSpec((D_MODEL, D_MODEL), const),
                  pl.BlockSpec((2, N_EXPERTS, D_MODEL), lambda i: (0, 0, 0))],
        out_specs=[pl.BlockSpec((tm, D_MODEL), tok),
                   pl.BlockSpec((2, tm, PACK_W), lambda i: (0, i, 0)),
                   pl.BlockSpec((N_EXPERTS, tm), lambda i: (0, i))],
        out_shape=[jax.ShapeDtypeStruct((n, D_MODEL), F32),
                   jax.ShapeDtypeStruct((2, n, PACK_W), jnp.uint32),
                   jax.ShapeDtypeStruct((N_EXPERTS, n), F32)],
        compiler_params=_params(56, ("arbitrary",)),
        name="merge",
    )(x_p, x_s, att, o_f, o_b, proj, proj, proj, mod, hg_norm.reshape(1, HG_WIDTH),
      norm_ffn.reshape(1, D_MODEL), wa, wb, wo, wr)


def _expert_kernel(be_ref, nused_ref, next_ref, slot_ref, xs_ref, wg_hbm, wu_hbm, wd_hbm, ys_ref,
                   wg_s, wu_s, wd_s, wg_st, wu_st, wd_st, sems):
    def weight_copies(e, slot):
        pairs = ((wg_hbm, wg_st), (wu_hbm, wu_st), (wd_hbm, wd_st))
        return [pltpu.make_async_copy(hbm.at[e], stage.at[slot], sems.at[slot, j])
                for j, (hbm, stage) in enumerate(pairs)]

    def block(i, rows):
        e = be_ref[i]
        slot = slot_ref[i]
        nxt = next_ref[i]
        last = nused_ref[0] - 1
        first = (i == 0) | (e != be_ref[jnp.maximum(i - 1, 0)])
        only = (i == last) | (be_ref[jnp.minimum(i + 1, last)] != e)
        second = (i >= 1) & jnp.logical_not(first) & (
            (i == 1) | (be_ref[jnp.maximum(i - 2, 0)] != e))

        @pl.when(first)
        def _():
            @pl.when(i == 0)
            def _():
                for c in weight_copies(e, slot):
                    c.start()

            for c in weight_copies(e, slot):
                c.wait()

            @pl.when(nxt >= 0)
            def _():
                ahead = weight_copies(nxt, 1 - slot)
                ahead[0].start()
                ahead[1].start()

                @pl.when(only)
                def _():
                    ahead[2].start()

            wg_s[...] = wg_st[slot].astype(BF16)
            wu_s[...] = wu_st[slot].astype(BF16)
            wd_s[...] = wd_st[slot].astype(BF16)

        @pl.when(second & (nxt >= 0))
        def _():
            weight_copies(nxt, 1 - slot)[2].start()

        xq = _unpack_rows(xs_ref[0, rows, :], xs_ref[1, rows, :])
        a = _dot_quarters(xq, wg_s)
        b = _dot_quarters(xq, wu_s)
        h = (a * _sigmoid(a) * b).astype(BF16)
        ys_ref[0, rows, :], ys_ref[1, rows, :] = _pack_rows(
            jnp.dot(h, wd_s[...], preferred_element_type=F32))

    for j in range(MOE_SUB):
        i = pl.program_id(0) * MOE_SUB + j
        pl.when(i < nused_ref[0])(functools.partial(block, i, slice(j * MOE_BLOCK, (j + 1) * MOE_BLOCK)))


def _experts(xs, plan, wg, wu, wd):
    n_rows = xs.shape[1]
    step_rows = MOE_SUB * MOE_BLOCK
    rows_map = lambda i, be, nu, nx, sl: (0, jnp.minimum(i, (nu[0] - 1) // MOE_SUB), 0)
    whole = pl.BlockSpec(memory_space=pl.ANY)
    grid_spec = pltpu.PrefetchScalarGridSpec(
        num_scalar_prefetch=4,
        grid=(n_rows // step_rows,),
        in_specs=[pl.BlockSpec((2, step_rows, PACK_W), rows_map), whole, whole, whole],
        out_specs=pl.BlockSpec((2, step_rows, PACK_W), rows_map),
        scratch_shapes=[pltpu.VMEM((D_MODEL, D_EXPERT), BF16),
                        pltpu.VMEM((D_MODEL, D_EXPERT), BF16),
                        pltpu.VMEM((D_EXPERT, D_MODEL), BF16),
                        pltpu.VMEM((2, D_MODEL, D_EXPERT), F32),
                        pltpu.VMEM((2, D_MODEL, D_EXPERT), F32),
                        pltpu.VMEM((2, D_EXPERT, D_MODEL), F32),
                        pltpu.SemaphoreType.DMA((2, 3))],
    )
    return pl.pallas_call(
        _expert_kernel,
        grid_spec=grid_spec,
        out_shape=jax.ShapeDtypeStruct((2, n_rows, PACK_W), jnp.uint32),
        compiler_params=_params(56, ("arbitrary",)),
        name="experts",
    )(*plan, xs, wg, wu, wd)


ROUTE_TILE = 128
ROUTE_STEP = 1024
NEG_INF = float("-inf")


def _first_max(x, idx, big):
    m = jnp.max(x, axis=0, keepdims=True)
    first = jnp.min(jnp.where(x == m, idx, big), axis=0, keepdims=True)
    return m, first


def _route_tile(lg, bias, tri, base):
    per_group = N_EXPERTS // N_GROUPS
    scores = jax.nn.sigmoid(lg)
    sel = scores + bias
    lrow = lax.broadcasted_iota(jnp.int32, (per_group, ROUTE_TILE), 0)
    gs = []
    for g in range(N_GROUPS):
        x = sel[g * per_group:(g + 1) * per_group]
        m1, first = _first_max(x, lrow, per_group)
        m2 = jnp.max(jnp.where(lrow == first, NEG_INF, x), axis=0, keepdims=True)
        gs.append(m1 + m2)
    cur = jnp.concatenate(gs, axis=0)
    grow = lax.broadcasted_iota(jnp.int32, (N_GROUPS, ROUTE_TILE), 0)
    chosen = jnp.zeros((N_GROUPS, ROUTE_TILE), jnp.int32)
    for _ in range(TOPK_GROUPS):
        _, first = _first_max(cur, grow, N_GROUPS)
        hit = grow == first
        chosen = jnp.where(hit, 1, chosen)
        cur = jnp.where(hit, NEG_INF, cur)
    cur = jnp.concatenate(
        [jnp.where(chosen[g:g + 1] > 0, sel[g * per_group:(g + 1) * per_group], NEG_INF)
         for g in range(N_GROUPS)], axis=0)
    row = lax.broadcasted_iota(jnp.int32, (N_EXPERTS, ROUTE_TILE), 0)
    member = jnp.zeros((N_EXPERTS, ROUTE_TILE), F32)
    es, ws = [], []
    for _ in range(TOP_K):
        _, first = _first_max(cur, row, N_EXPERTS)
        hit = row == first
        es.append(first)
        ws.append(jnp.sum(jnp.where(hit, scores, 0.0), axis=0, keepdims=True))
        cur = jnp.where(hit, NEG_INF, cur)
        member = jnp.where(hit, 1.0, member)
    e = jnp.concatenate(es, axis=0)
    w = jnp.concatenate(ws, axis=0)
    w = w / jnp.sum(w, axis=0, keepdims=True) * ROUTE_SCALE
    before = jnp.dot(member.astype(BF16), tri, preferred_element_type=F32) + base
    rank = jnp.concatenate(
        [jnp.sum(jnp.where(row == es[k], before, 0.0), axis=0, keepdims=True) for k in range(TOP_K)],
        axis=0)
    return e, w, rank.astype(jnp.int32), member


def _route_kernel(lg_ref, b_ref, e_ref, w_ref, r_ref, cnt_ref, base):
    @pl.when(pl.program_id(0) == 0)
    def _():
        base[...] = jnp.zeros_like(base)

    r_i = lax.broadcasted_iota(jnp.int32, (ROUTE_TILE, ROUTE_TILE), 0)
    c_i = lax.broadcasted_iota(jnp.int32, (ROUTE_TILE, ROUTE_TILE), 1)
    tri = (r_i < c_i).astype(BF16)
    bias = b_ref[...]
    for j in range(ROUTE_STEP // ROUTE_TILE):
        ls = slice(j * ROUTE_TILE, (j + 1) * ROUTE_TILE)
        e, w, rank, member = _route_tile(lg_ref[:, ls], bias, tri, base[...])
        e_ref[:, ls] = e
        w_ref[:, ls] = w
        r_ref[:, ls] = rank
        base[...] = base[...] + jnp.sum(member, axis=1, keepdims=True)
    cnt_ref[...] = base[...]


def _route(logits_t, b_router):
    n = logits_t.shape[1]
    tok = lambda i: (0, i)
    return pl.pallas_call(
        _route_kernel,
        grid=(n // ROUTE_STEP,),
        in_specs=[pl.BlockSpec((N_EXPERTS, ROUTE_STEP), tok),
                  pl.BlockSpec((N_EXPERTS, 1), lambda i: (0, 0))],
        out_specs=[pl.BlockSpec((TOP_K, ROUTE_STEP), tok),
                   pl.BlockSpec((TOP_K, ROUTE_STEP), tok),
                   pl.BlockSpec((TOP_K, ROUTE_STEP), tok),
                   pl.BlockSpec((N_EXPERTS, 1), lambda i: (0, 0))],
        out_shape=[jax.ShapeDtypeStruct((TOP_K, n), jnp.int32),
                   jax.ShapeDtypeStruct((TOP_K, n), F32),
                   jax.ShapeDtypeStruct((TOP_K, n), jnp.int32),
                   jax.ShapeDtypeStruct((N_EXPERTS, 1), F32)],
        scratch_shapes=[pltpu.VMEM((N_EXPERTS, 1), F32)],
        compiler_params=_params(32, ("arbitrary",)),
        name="route",
    )(logits_t, b_router.reshape(N_EXPERTS, 1).astype(F32))


def _pos_kernel(e_ref, r_ref, ps_ref, pos_ref):
    row = lax.broadcasted_iota(jnp.int32, (N_EXPERTS, ROUTE_TILE), 0)
    pstart = ps_ref[...]
    for j in range(ROUTE_STEP // ROUTE_TILE):
        ls = slice(j * ROUTE_TILE, (j + 1) * ROUTE_TILE)
        e = e_ref[:, ls]
        off = jnp.concatenate(
            [jnp.sum(jnp.where(row == e[k:k + 1], pstart, 0.0), axis=0, keepdims=True)
             for k in range(TOP_K)], axis=0)
        pos_ref[:, ls] = off.astype(jnp.int32) + r_ref[:, ls]


def _positions(eidx, rank, pstarts):
    n = eidx.shape[1]
    tok = lambda i: (0, i)
    return pl.pallas_call(
        _pos_kernel,
        grid=(n // ROUTE_STEP,),
        in_specs=[pl.BlockSpec((TOP_K, ROUTE_STEP), tok),
                  pl.BlockSpec((TOP_K, ROUTE_STEP), tok),
                  pl.BlockSpec((N_EXPERTS, 1), lambda i: (0, 0))],
        out_specs=pl.BlockSpec((TOP_K, ROUTE_STEP), tok),
        out_shape=jax.ShapeDtypeStruct((TOP_K, n), jnp.int32),
        compiler_params=_params(32, ("arbitrary",)),
        name="positions",
    )(eidx, rank, pstarts.reshape(N_EXPERTS, 1).astype(F32))


def _block_tables(counts, n_blocks):
    counts = counts.reshape(N_EXPERTS).astype(jnp.int32)
    padded = (counts + MOE_BLOCK - 1) // MOE_BLOCK * MOE_BLOCK
    pend = jnp.cumsum(padded)
    pstarts = pend - padded
    first_row = jnp.arange(n_blocks, dtype=jnp.int32) * MOE_BLOCK
    block_e = jnp.sum((pend[None, :] <= first_row[:, None]).astype(jnp.int32), axis=1)
    block_e = jnp.minimum(block_e, N_EXPERTS - 1).astype(jnp.int32)
    n_used = (pend[-1] // MOE_BLOCK).astype(jnp.int32).reshape(1)
    ids = jnp.arange(N_EXPERTS, dtype=jnp.int32)
    used = padded > 0
    ordinal = jnp.cumsum(used.astype(jnp.int32)) - 1
    from_here = jnp.flip(lax.cummin(jnp.flip(jnp.where(used, ids, N_EXPERTS))))
    after = jnp.concatenate([from_here[1:], jnp.full((1,), N_EXPERTS, jnp.int32)])
    after = jnp.where(after >= N_EXPERTS, -1, after)
    mine = block_e[:, None] == ids[None, :]
    next_e = jnp.sum(jnp.where(mine, after[None, :], 0), axis=1).astype(jnp.int32)
    slot = jnp.sum(jnp.where(mine, ordinal[None, :] % 2, 0), axis=1).astype(jnp.int32)
    return pstarts, (block_e, n_used, next_e, slot)


SC_WINDOW = 128


def _sc_mesh():
    return plsc.VectorSubcoreMesh(core_axis_name="core", subcore_axis_name="subcore")


def _both_halves(pos, n_rows):
    return jnp.concatenate([pos, pos + n_rows]).reshape(1, -1)


def _sc_dispatch(rows, pos, n_rows):
    _, n, w = rows.shape
    tiles = n // SC_WINDOW
    top_k = pos.shape[0] // n
    idx = jnp.transpose(pos.reshape(top_k, tiles, SC_WINDOW), (1, 0, 2))
    idx = jnp.stack([idx, idx + n_rows]).reshape(1, -1)

    @functools.partial(pl.kernel, out_type=jax.ShapeDtypeStruct((2 * n_rows, w), rows.dtype),
                       mesh=_sc_mesh(), scratch_types=[])
    def scatter_kernel(x_hbm, i_hbm, o_hbm):
        def body(x_vmem, i_vmem):
            pltpu.sync_copy(x_vmem, o_hbm.at[i_vmem.at[0]])

        pltpu.emit_pipeline(
            body,
            grid=(2 * tiles * top_k,),
            in_specs=[pl.BlockSpec((SC_WINDOW, w), lambda i: (i // top_k, 0)),
                      pl.BlockSpec((1, SC_WINDOW), lambda i: (0, i))],
            out_specs=[],
            core_axis_name=("core", "subcore"),
            dimension_semantics=(pltpu.PARALLEL,),
        )(x_hbm, i_hbm)

    out = scatter_kernel(rows.reshape(2 * n, w), idx)
    return out.reshape(2, n_rows, w)


def _sc_gather(table, pos):
    _, n_rows, w = table.shape
    m = pos.shape[0]

    @functools.partial(pl.kernel, out_type=jax.ShapeDtypeStruct((2 * m, w), table.dtype),
                       mesh=_sc_mesh(), scratch_types=[])
    def gather_kernel(t_hbm, i_hbm, o_hbm):
        def body(i_vmem, o_vmem):
            pltpu.sync_copy(t_hbm.at[i_vmem.at[0]], o_vmem)

        pltpu.emit_pipeline(
            body,
            grid=(2 * m // SC_WINDOW,),
            in_specs=[pl.BlockSpec((1, SC_WINDOW), lambda i: (0, i))],
            out_specs=[pl.BlockSpec((SC_WINDOW, w), lambda i: (i, 0))],
            core_axis_name=("core", "subcore"),
            dimension_semantics=(pltpu.PARALLEL,),
        )(i_hbm, o_hbm)

    out = gather_kernel(table.reshape(2 * n_rows, w), _both_halves(pos, n_rows))
    return out.reshape(2, m, w)


def _final_kernel(npt, x1_ref, u2_ref, yg_ref, w_ref, mod_ref, nf_ref, wsg_ref, wsu_ref, wsd_ref,
                  op_ref, os_ref):
    uq = _unpack_rows(u2_ref[0], u2_ref[1])
    a = _dot_quarters(uq, wsg_ref)
    b = _dot_quarters(uq, wsu_ref)
    f = jnp.dot((a * _sigmoid(a) * b).astype(BF16), wsd_ref[...], preferred_element_type=F32)
    w = w_ref[...]
    fq = [f[:, q * PACK_W:(q + 1) * PACK_W] for q in range(4)]
    for k in range(TOP_K):
        yq = _unpack_rows(yg_ref[0, k], yg_ref[1, k], F32)
        fq = [fq[q] + w[:, k:k + 1] * yq[q] for q in range(4)]
    x2 = x1_ref[...] + mod_ref[5:6, :] * jnp.concatenate(fq, axis=1)
    out = _rms(x2) * nf_ref[...]
    i = pl.program_id(0)

    @pl.when(i < npt)
    def _():
        op_ref[...] = out

    @pl.when(i >= npt)
    def _():
        os_ref[...] = out


def _final(seqs, x1, u2, yg, w, mod, norm_final, wsg, wsu, wsd):
    n = seqs.n
    tm = 512
    tok = lambda i: (i, 0)
    const = lambda i: (0, 0)
    npt, p_spec, s_spec = _two_stream_specs(seqs, tm)
    return pl.pallas_call(
        functools.partial(_final_kernel, npt),
        grid=(n // tm,),
        in_specs=[pl.BlockSpec((tm, D_MODEL), tok),
                  pl.BlockSpec((2, tm, PACK_W), lambda i: (0, i, 0)),
                  pl.BlockSpec((2, TOP_K, tm, PACK_W), lambda i: (0, 0, i, 0)),
                  pl.BlockSpec((tm, TOP_K), tok),
                  pl.BlockSpec((None, N_MOD, D_MODEL), lambda i: (seqs.info(i * tm)[0], 0, 0)),
                  pl.BlockSpec((1, D_MODEL), const),
                  pl.BlockSpec((D_MODEL, D_EXPERT), const),
                  pl.BlockSpec((D_MODEL, D_EXPERT), const),
                  pl.BlockSpec((D_EXPERT, D_MODEL), const)],
        out_specs=[p_spec, s_spec],
        out_shape=[jax.ShapeDtypeStruct((seqs.np_, D_MODEL), F32),
                   jax.ShapeDtypeStruct((n - seqs.np_, D_MODEL), F32)],
        compiler_params=_params(48, ("arbitrary",)),
        name="final",
    )(x1, u2, yg, w, mod, norm_final.reshape(1, D_MODEL), wsg, wsu, wsd)


def _layer(seqs, x_p, x_s, c, w_ada, b_ada, norm_mix, w_in, na_rpb, hg_lb, hg_norm, w_branch_a,
           w_branch_b, w_out, norm_ffn, w_router, b_router, w_exp_gate, w_exp_up, w_exp_down,
           w_sh_gate, w_sh_up, w_sh_down, norm_final):
    n = seqs.n
    c_rows = -(-seqs.nseq // 8) * 8
    c_pad = jnp.zeros((c_rows, D_MODEL), F32).at[:seqs.nseq].set(c)
    mod = _ada(c_pad, w_ada[0], b_ada[0])[:seqs.nseq].reshape(seqs.nseq, N_MOD, D_MODEL)
    lb = jnp.cumsum(jax.nn.softmax(hg_lb.astype(F32), axis=0), axis=0)[0]

    proj = _inproj(seqs, x_p, x_s, mod, norm_mix[0], w_in[0].astype(BF16))
    att = _na(seqs, proj, _na_bias_table(na_rpb[0]))
    o_f, o_b = _hgrn(seqs, proj, lb)
    x1, u2, logits = _merge(seqs, x_p, x_s, att, o_f, o_b, proj, mod, hg_norm[0], norm_ffn[0],
                            w_branch_a[0].astype(BF16), w_branch_b[0].astype(BF16),
                            w_out[0].astype(BF16), _split_hi_lo(w_router[0].T))

    eidx, w, rank, counts = _route(logits, b_router[0])
    step_rows = MOE_SUB * MOE_BLOCK
    n_rows = -(-(n * TOP_K + N_EXPERTS * MOE_BLOCK) // step_rows) * step_rows
    pstarts, plan = _block_tables(counts, n_rows // MOE_BLOCK)
    pos = _positions(eidx, rank, pstarts).reshape(-1)
    xs = _sc_dispatch(u2, pos, n_rows)
    ys = _experts(xs, plan, w_exp_gate[0], w_exp_up[0], w_exp_down[0])
    yg = _sc_gather(ys, pos).reshape(2, TOP_K, n, PACK_W)
    return _final(seqs, x1, u2, yg, w.T, mod, norm_final, w_sh_gate[0].astype(BF16),
                  w_sh_up[0].astype(BF16), w_sh_down[0].astype(BF16))


def kernel(x_prompt, x_sample, c_prompt, c_sample, w_ada, b_ada, norm_mix, w_in, na_rpb, hg_lb, hg_norm, w_branch_a, w_branch_b, w_out, norm_ffn, w_router, b_router, w_exp_gate, w_exp_up, w_exp_down, w_sh_gate, w_sh_up, w_sh_down, norm_final):
    bp, tp, _ = x_prompt.shape
    bs, ts, _ = x_sample.shape
    for t in (tp, ts):
        assert t % 1024 == 0, "sequence lengths must be multiples of the 1024-token tiles"
        assert t // GRID_W >= NA_WIN, "a sequence needs at least NA_WIN grid rows"
    seqs = _Seqs(bp, tp, bs, ts)
    c = jnp.concatenate([c_prompt, c_sample])
    y_p, y_s = _layer(seqs, x_prompt.reshape(bp * tp, D_MODEL), x_sample.reshape(bs * ts, D_MODEL),
                      c, w_ada, b_ada, norm_mix, w_in, na_rpb, hg_lb, hg_norm, w_branch_a,
                      w_branch_b, w_out, norm_ffn, w_router, b_router, w_exp_gate, w_exp_up,
                      w_exp_down, w_sh_gate, w_sh_up, w_sh_down, norm_final)
    return (y_p.reshape(bp, tp, D_MODEL), y_s.reshape(bs, ts, D_MODEL))
```

```python
import functools

import jax
import jax.numpy as jnp
import numpy as np
from jax import lax
from jax.experimental import pallas as pl
from jax.experimental.pallas import tpu as pltpu
from jax.experimental.pallas import tpu_sc as plsc

D_MODEL = 1024
GRID_W = 64
NA_HEADS = 8
NA_HEAD_DIM = 64
NA_WIDTH = NA_HEADS * NA_HEAD_DIM
NA_ROWS = 8
NA_COLS = 16
HG_HEADS = 4
HG_KEY_DIM = 128
HG_WIDTH = HG_HEADS * HG_KEY_DIM
HG_CHUNK = 128
HG_EXP_LIMIT = 80.0
N_EXPERTS = 256
TOP_K = 8
N_GROUPS = 8
TOPK_GROUPS = 4
D_EXPERT = 256
ROUTE_SCALE = 2.5
N_MOD = 6
RMS_EPS = 1e-6

MOE_BLOCK = 512
MOE_SUB = 2
NA_GROUP = 4
NA_WIN = 3 * NA_GROUP
NA_TOK = NA_GROUP * GRID_W
LOG2E = 1.4426950408889634
NA_Q_SCALE = NA_HEAD_DIM ** -0.5 * LOG2E
HG_STEP = 512
MASK_VALUE = -1e30

F32 = jnp.float32
BF16 = jnp.bfloat16
HIGHEST = lax.Precision.HIGHEST
NT_DIMS = (((1,), (1,)), ((), ()))
TN_DIMS = (((0,), (0,)), ((), ()))

PROJ_TN = 2048
SLAB_Q, SLAB_K, SLAB_V, SLAB_HQ = (0, 0), (0, 1), (0, 2), (0, 3)
SLAB_FF, SLAB_FB, SLAB_HI, SLAB_HG = (1, 0), (1, 1), (1, 2), (1, 3)
SLAB_GA, SLAB_GB = (2, 0), (2, 1)


def _params(vmem_mb, sem=None):
    kw = dict(vmem_limit_bytes=vmem_mb * 1024 * 1024)
    if sem is not None:
        kw["dimension_semantics"] = sem
    return pltpu.CompilerParams(**kw)


class _Seqs:
    def __init__(self, bp, tp, bs, ts):
        self.bp, self.tp, self.bs, self.ts = bp, tp, bs, ts
        self.np_ = bp * tp
        self.n = bp * tp + bs * ts
        self.nseq = bp + bs

    def info(self, t0):
        in_p = t0 < self.np_
        rel = jnp.maximum(t0 - self.np_, 0)
        sid = jnp.where(in_p, t0 // self.tp, self.bp + rel // self.ts)
        start = jnp.where(in_p, (t0 // self.tp) * self.tp, self.np_ + (rel // self.ts) * self.ts)
        length = jnp.where(in_p, self.tp, self.ts)
        return sid, start, length


def _ada_kernel(c_ref, w_ref, b_ref, o_ref):
    c = c_ref[...]
    a = c * jax.nn.sigmoid(c)
    o_ref[...] = jnp.dot(a, w_ref[...], precision=HIGHEST, preferred_element_type=F32) + b_ref[...]


def _ada(c_pad, w_ada, b_ada):
    rows = c_pad.shape[0]
    n_out = w_ada.shape[1]
    tn = 1024
    return pl.pallas_call(
        _ada_kernel,
        grid=(n_out // tn,),
        in_specs=[pl.BlockSpec((rows, D_MODEL), lambda j: (0, 0)),
                  pl.BlockSpec((D_MODEL, tn), lambda j: (0, j)),
                  pl.BlockSpec((1, tn), lambda j: (0, j))],
        out_specs=pl.BlockSpec((rows, tn), lambda j: (0, j)),
        out_shape=jax.ShapeDtypeStruct((rows, n_out), F32),
        compiler_params=_params(32),
        name="ada",
    )(c_pad, w_ada, b_ada.reshape(1, n_out))


def _rms(x):
    return x * lax.rsqrt(jnp.mean(x * x, axis=-1, keepdims=True) + RMS_EPS)


def _sigmoid(x):
    return 0.5 * jnp.tanh(0.5 * x) + 0.5


PACK_W = D_MODEL // 4


def _pack_rows(x):
    out = []
    for h in range(2):
        lo = x[:, (2 * h) * PACK_W:(2 * h + 1) * PACK_W].astype(BF16).astype(F32)
        hi = x[:, (2 * h + 1) * PACK_W:(2 * h + 2) * PACK_W].astype(BF16).astype(F32)
        out.append(lax.bitcast_convert_type(hi, jnp.uint32)
                   | (lax.bitcast_convert_type(lo, jnp.uint32) >> 16))
    return out


def _unpack_rows(p0, p1, dtype=BF16):
    quarters = []
    for p in (p0, p1):
        quarters.append(lax.bitcast_convert_type(p << 16, F32).astype(dtype))
        quarters.append(lax.bitcast_convert_type(p & jnp.uint32(0xFFFF0000), F32).astype(dtype))
    return quarters


def _dot_quarters(quarters, w_ref):
    acc = None
    for q, xq in enumerate(quarters):
        part = jnp.dot(xq, w_ref[q * PACK_W:(q + 1) * PACK_W, :], preferred_element_type=F32)
        acc = part if acc is None else acc + part
    return acc


def _two_stream_specs(seqs, tm, grid_rank=1):
    npt = seqs.np_ // tm
    nst = (seqs.n - seqs.np_) // tm
    if grid_rank == 1:
        p_map = lambda i: (jnp.minimum(i, npt - 1), 0)
        s_map = lambda i: (jnp.clip(i - npt, 0, nst - 1), 0)
    else:
        p_map = lambda i, j: (jnp.minimum(i, npt - 1), 0)
        s_map = lambda i, j: (jnp.clip(i - npt, 0, nst - 1), 0)
    return npt, pl.BlockSpec((tm, D_MODEL), p_map), pl.BlockSpec((tm, D_MODEL), s_map)


def _inproj_kernel(npt, xp_ref, xs_ref, mod_ref, g_ref, w_ref, cs_ref, o_ref, u_scr):
    @pl.when(pl.program_id(1) == 0)
    def _():
        x = jnp.where(pl.program_id(0) < npt, xp_ref[...], xs_ref[...])
        y = _rms(x) * g_ref[...]
        u = y * (1.0 + mod_ref[1:2, :]) + mod_ref[0:1, :]
        u_scr[...] = u.astype(BF16)

    acc = jnp.dot(u_scr[...], w_ref[...], preferred_element_type=F32)
    o_ref[...] = (acc * cs_ref[...]).astype(o_ref.dtype)


def _inproj(seqs, x_p, x_s, mod, norm_mix, w_in_bf):
    n = seqs.n
    tm = min(1024, seqs.tp, seqs.ts)
    tn = PROJ_TN
    n_slab = w_in_bf.shape[1] // tn
    npt, p_spec, s_spec = _two_stream_specs(seqs, tm, grid_rank=2)
    col_scale = jnp.ones((1, w_in_bf.shape[1]), F32).at[:, :NA_WIDTH].set(NA_Q_SCALE)
    return pl.pallas_call(
        functools.partial(_inproj_kernel, npt),
        grid=(n // tm, n_slab),
        in_specs=[p_spec, s_spec,
                  pl.BlockSpec((None, N_MOD, D_MODEL), lambda i, j: (seqs.info(i * tm)[0], 0, 0)),
                  pl.BlockSpec((1, D_MODEL), lambda i, j: (0, 0)),
                  pl.BlockSpec((D_MODEL, tn), lambda i, j: (0, j)),
                  pl.BlockSpec((1, tn), lambda i, j: (0, j))],
        out_specs=pl.BlockSpec((None, tm, tn), lambda i, j: (j, i, 0)),
        out_shape=jax.ShapeDtypeStruct((n_slab, n, tn), BF16),
        scratch_shapes=[pltpu.VMEM((tm, D_MODEL), BF16)],
        compiler_params=_params(56, ("arbitrary", "arbitrary")),
        name="inproj",
    )(x_p, x_s, mod, norm_mix.reshape(1, D_MODEL), w_in_bf, col_scale)


def _na_bias_table(rpb):
    col = np.arange(GRID_W)
    cs = np.clip(col - NA_COLS // 2, 0, GRID_W - NA_COLS)
    valid = (col[None, :] >= cs[:, None]) & (col[None, :] < cs[:, None] + NA_COLS)
    coff = col[None, :] - col[:, None] + NA_COLS - 1
    onehot = (coff[None] == np.arange(2 * NA_COLS - 1)[:, None, None]) & valid[None]
    toep = jnp.einsum("hrc,cqk->hrqk", rpb.astype(F32), jnp.asarray(onehot, F32),
                      precision=HIGHEST)
    toep = jnp.where(valid[None, None], toep * LOG2E, MASK_VALUE)
    masked =jnp.full((NA_HEADS, GRID_W, GRID_W), MASK_VALUE, F32)
    cases = (([0] * NA_GROUP, [NA_ROWS - 1 - i for i in range(NA_GROUP)]),
             (list(range(NA_GROUP)), [NA_ROWS // 2 - 1] * NA_GROUP),
             ([NA_GROUP] * NA_GROUP, [NA_ROWS // 2 - 1 - i for i in range(NA_GROUP)]))
    tabs = []
    for first_row, first_off in cases:
        q_rows = []
        for i in range(NA_GROUP):
            blocks = [toep[:, first_off[i] + w - first_row[i]]
                      if 0 <= w - first_row[i] < NA_ROWS else masked for w in range(NA_WIN)]
            q_rows.append(jnp.concatenate(blocks, axis=2))
        tabs.append(jnp.concatenate(q_rows, axis=1))
    return jnp.stack(tabs)


def _na_geometry(seqs, g):
    _, start, length = seqs.info(g * NA_TOK)
    row0 = start // GRID_W
    rows = length // GRID_W
    r0 = g * NA_GROUP - row0
    wb = jnp.clip(r0 - NA_ROWS // 2, 0, rows - NA_WIN)
    case = jnp.where(r0 == 0, 0, jnp.where(r0 == rows - NA_GROUP, 2, 1))
    return (row0 + wb) // NA_GROUP, case


def _na_kernel(q_ref, k0, k1, k2, v0, v1, v2, bias_ref, o_ref):
    k_refs = (k0, k1, k2)
    v_refs = (v0, v1, v2)
    outs = []
    for h in range(NA_HEADS):
        hs = slice(h * NA_HEAD_DIM, (h + 1) * NA_HEAD_DIM)
        q = q_ref[:, hs]
        s = [lax.dot_general(q, kr[:, hs], NT_DIMS, preferred_element_type=F32)
             + bias_ref[h, :, d * NA_TOK:(d + 1) * NA_TOK] for d, kr in enumerate(k_refs)]
        m = jnp.max(jnp.maximum(jnp.maximum(s[0], s[1]), s[2]), axis=-1, keepdims=True)
        p = [jnp.exp2(sd - m) for sd in s]
        l = jnp.sum((p[0] + p[1]) + p[2], axis=-1, keepdims=True)
        o = sum(jnp.dot(pd.astype(BF16), vr[:, hs], preferred_element_type=F32)
                for pd, vr in zip(p, v_refs))
        outs.append(o / l)
    o_ref[...] = jnp.concatenate(outs, axis=1).astype(o_ref.dtype)


def _na(seqs, proj, bias_tab):
    n = seqs.n

    def kv_spec(slab, d):
        return pl.BlockSpec((None, NA_TOK, NA_WIDTH),
                            lambda g: (slab[0], _na_geometry(seqs, g)[0] + d, slab[1]))

    return pl.pallas_call(
        _na_kernel,
        grid=(n // NA_TOK,),
        in_specs=[pl.BlockSpec((None, NA_TOK, NA_WIDTH), lambda g: (SLAB_Q[0], g, SLAB_Q[1]))]
        + [kv_spec(SLAB_K, d) for d in range(3)] + [kv_spec(SLAB_V, d) for d in range(3)]
        + [pl.BlockSpec((None,) + bias_tab.shape[1:],
                        lambda g: (_na_geometry(seqs, g)[1], 0, 0, 0))],
        out_specs=pl.BlockSpec((NA_TOK, NA_WIDTH), lambda g: (g, 0)),
        out_shape=jax.ShapeDtypeStruct((n, NA_WIDTH), BF16),
        compiler_params=_params(48, ("arbitrary",)),
        name="natten",
    )(proj, proj, proj, proj, proj, proj, proj, bias_tab)


def _hg_chunk(q, z, v, lb, tri, mask, mid, last, st_ref):
    sig = _sigmoid(z)
    f = lb + (1.0 - lb) * sig
    lf = jnp.log(f)
    kin = (1.0 - lb) * (1.0 - sig)
    hi = lf.astype(BF16)
    lo = (lf - hi.astype(F32)).astype(BF16)
    g2 = jnp.dot(tri, jnp.concatenate([lo, hi], axis=1), preferred_element_type=F32)
    gcum = g2[:, :HG_WIDTH] + g2[:, HG_WIDTH:]
    gm = gcum[mid:mid + 1, :]
    gl = gcum[last:last + 1, :]
    up = jnp.exp(gcum - gm)
    dn = jnp.exp(gm - gcum)
    qa = (q * up).astype(BF16)
    ka = (kin * dn).astype(BF16)
    qe = (q * (up * jnp.exp(gm))).astype(BF16)
    kd = (kin * (dn * jnp.exp(gl - gm))).astype(BF16)
    eg = jnp.exp(gl)
    vb = v.astype(BF16)
    outs = []
    for h in range(HG_HEADS):
        hs = slice(h * HG_KEY_DIM, (h + 1) * HG_KEY_DIM)
        a = lax.dot_general(qa[:, hs], ka[:, hs], NT_DIMS, preferred_element_type=F32)
        a = jnp.where(mask, a, 0.0)
        st = st_ref[h]
        o = jnp.dot(a.astype(BF16), vb[:, hs], preferred_element_type=F32)
        o = o + lax.dot_general(qe[:, hs], st.astype(BF16), NT_DIMS, preferred_element_type=F32)
        st_ref[h] = st * eg[:, hs] + lax.dot_general(vb[:, hs], kd[:, hs], TN_DIMS,
                                                    preferred_element_type=F32)
        outs.append(o)
    return jnp.concatenate(outs, axis=1)


def _hg_exact(q_ref, z_ref, v_ref, lb, reverse, st_ref, o_ref, qs, fs, ks, vs):
    sig = jax.nn.sigmoid(z_ref[...].astype(F32))
    qs[...] = q_ref[...].astype(F32)
    fs[...] = lb + (1.0 - lb) * sig
    ks[...] = (1.0 - lb) * (1.0 - sig)
    vs[...] = v_ref[...].astype(F32)
    eye = (lax.broadcasted_iota(jnp.int32, (HG_KEY_DIM, HG_KEY_DIM), 0)
           == lax.broadcasted_iota(jnp.int32, (HG_KEY_DIM, HG_KEY_DIM), 1)).astype(F32)

    def body(i, carry):
        t = HG_STEP - 1 - i if reverse else i
        q_t, f_t, k_t, v_t = (r[pl.ds(t, 1), :] for r in (qs, fs, ks, vs))
        outs = []
        for h in range(HG_HEADS):
            hs = slice(h * HG_KEY_DIM, (h + 1) * HG_KEY_DIM)
            v_col = jnp.sum(eye * v_t[:, hs], axis=1, keepdims=True)
            st = st_ref[h] * f_t[:, hs] + v_col * k_t[:, hs]
            st_ref[h] = st
            o_col = jnp.sum(st * q_t[:, hs], axis=1, keepdims=True)
            outs.append(jnp.sum(eye * o_col, axis=0, keepdims=True))
        o_ref[pl.ds(t, 1), :] = jnp.concatenate(outs, axis=1)
        return carry

    lax.fori_loop(0, HG_STEP, body, 0)


def _hg_kernel(seqs, safe_ref, qf_ref, zf_ref, vf_ref, qb_ref, zb_ref, vb_ref, lb_ref, of_ref,
               ob_ref, stf, stb, qs, fs, ks, vs):
    i = pl.program_id(0)
    nsteps = pl.num_programs(0)
    tf = i * HG_STEP
    tb = (nsteps - 1 - i) * HG_STEP
    _, start_f, _ = seqs.info(tf)
    _, start_b, len_b = seqs.info(tb)

    @pl.when(tf == start_f)
    def _():
        stf[...] = jnp.zeros_like(stf)

    @pl.when(tb + HG_STEP == start_b + len_b)
    def _():
        stb[...] = jnp.zeros_like(stb)

    lb = lb_ref[...]

    @pl.when(safe_ref[0] > 0)
    def _():
        row = lax.broadcasted_iota(jnp.int32, (HG_CHUNK, HG_CHUNK), 0)
        col = lax.broadcasted_iota(jnp.int32, (HG_CHUNK, HG_CHUNK), 1)
        lower = row >= col
        upper = col >= row
        tri_f = lower.astype(BF16)
        tri_b = upper.astype(BF16)
        nchunk = HG_STEP // HG_CHUNK
        for c in range(nchunk):
            cs = slice(c * HG_CHUNK, (c + 1) * HG_CHUNK)
            of_ref[cs, :] = _hg_chunk(qf_ref[cs, :].astype(F32), zf_ref[cs, :].astype(F32),
                                      vf_ref[cs, :].astype(F32), lb, tri_f, lower,
                                      HG_CHUNK // 2 - 1, HG_CHUNK - 1, stf)
            cb = nchunk - 1 - c
            bs = slice(cb * HG_CHUNK, (cb + 1) * HG_CHUNK)
            ob_ref[bs, :] = _hg_chunk(qb_ref[bs, :].astype(F32), zb_ref[bs, :].astype(F32),
                                      vb_ref[bs, :].astype(F32), lb, tri_b, upper,
                                      HG_CHUNK // 2, 0, stb)

    @pl.when(safe_ref[0] == 0)
    def _():
        _hg_exact(qf_ref, zf_ref, vf_ref, lb, False, stf, of_ref, qs, fs, ks, vs)
        _hg_exact(qb_ref, zb_ref, vb_ref, lb, True, stb, ob_ref, qs, fs, ks, vs)


def _hgrn(seqs, proj, lb):
    n = seqs.n
    nsteps = n // HG_STEP

    def spec(slab, rev):
        if rev:
            return pl.BlockSpec((None, HG_STEP, HG_WIDTH), lambda i: (slab[0], nsteps - 1 - i, slab[1]))
        return pl.BlockSpec((None, HG_STEP, HG_WIDTH), lambda i: (slab[0], i, slab[1]))

    safe = (jnp.max(-jnp.log(lb)) * (HG_CHUNK // 2) < HG_EXP_LIMIT).astype(jnp.int32).reshape(1)
    step_scratch = pltpu.VMEM((HG_STEP, HG_WIDTH), F32)
    return pl.pallas_call(
        functools.partial(_hg_kernel, seqs),
        grid=(nsteps,),
        in_specs=[pl.BlockSpec(memory_space=pltpu.SMEM),
                  spec(SLAB_HQ, False), spec(SLAB_FF, False), spec(SLAB_HI, False),
                  spec(SLAB_HQ, True), spec(SLAB_FB, True), spec(SLAB_HI, True),
                  pl.BlockSpec((1, HG_WIDTH), lambda i: (0, 0))],
        out_specs=[pl.BlockSpec((HG_STEP, HG_WIDTH), lambda i: (i, 0)),
                   pl.BlockSpec((HG_STEP, HG_WIDTH), lambda i: (nsteps - 1 - i, 0))],
        out_shape=[jax.ShapeDtypeStruct((n, HG_WIDTH), F32)] * 2,
        scratch_shapes=[pltpu.VMEM((HG_HEADS, HG_KEY_DIM, HG_KEY_DIM), F32)] * 2
        + [step_scratch] * 4,
        compiler_params=_params(32, ("arbitrary",)),
        name="hgrn2",
    )(safe, proj, proj, proj, proj, proj, proj, lb.reshape(1, HG_WIDTH))


def _merge_kernel(npt, xp_ref, xs_ref, att_ref, of_ref, ob_ref, hg_ref, ga_ref, gb_ref, mod_ref,
                  hgn_ref, nffn_ref, wa_ref, wb_ref, wo_ref, wr_ref, x1_ref, u2_ref, lg_ref):
    x = jnp.where(pl.program_id(0) < npt, xp_ref[...], xs_ref[...])
    o = of_ref[...] + ob_ref[...]
    parts = []
    for h in range(HG_HEADS):
        hs = slice(h * HG_KEY_DIM, (h + 1) * HG_KEY_DIM)
        parts.append(_rms(o[:, hs]))
    on = jnp.concatenate(parts, axis=1) * hgn_ref[...]
    gate = hg_ref[...].astype(F32)
    hb = (on * (gate * _sigmoid(gate))).astype(BF16)
    ya = jnp.dot(att_ref[...], wa_ref[...], preferred_element_type=F32)
    yb = jnp.dot(hb, wb_ref[...], preferred_element_type=F32)
    merged = (_sigmoid(ga_ref[...].astype(F32)) * ya
              + _sigmoid(gb_ref[...].astype(F32)) * yb)
    x1 = x + mod_ref[2:3, :] * jnp.dot(merged.astype(BF16), wo_ref[...],
                                       preferred_element_type=F32)
    x1_ref[...] = x1
    u2 = _rms(x1) * nffn_ref[...] * (1.0 + mod_ref[4:5, :]) + mod_ref[3:4, :]
    u2_ref[0], u2_ref[1] = _pack_rows(u2)
    w_hi = wr_ref[0]
    u_hi = u2.astype(BF16)
    u_lo = (u2 - u_hi.astype(F32)).astype(BF16)
    lg_ref[...] = (lax.dot_general(w_hi, u_hi, NT_DIMS, preferred_element_type=F32)
                   + (lax.dot_general(w_hi, u_lo, NT_DIMS, preferred_element_type=F32)
                      + lax.dot_general(wr_ref[1], u_hi, NT_DIMS, preferred_element_type=F32)))


def _split_hi_lo(w):
    hi = w.astype(BF16)
    return jnp.stack([hi, (w - hi.astype(F32)).astype(BF16)])


def _merge(seqs, x_p, x_s, att, o_f, o_b, proj, mod, hg_norm, norm_ffn, wa, wb, wo, wr):
    n = seqs.n
    tm = 512
    tok = lambda i: (i, 0)
    const = lambda i: (0, 0)
    npt, p_spec, s_spec = _two_stream_specs(seqs, tm)
    return pl.pallas_call(
        functools.partial(_merge_kernel, npt),
        grid=(n // tm,),
        in_specs=[p_spec, s_spec,
                  pl.BlockSpec((tm, NA_WIDTH), tok),
                  pl.BlockSpec((tm, HG_WIDTH), tok),
                  pl.BlockSpec((tm, HG_WIDTH), tok),
                  pl.BlockSpec((None, tm, HG_WIDTH), lambda i: (SLAB_HG[0], i, SLAB_HG[1])),
                  pl.BlockSpec((None, tm, D_MODEL), lambda i: (SLAB_GA[0], i, SLAB_GA[1])),
                  pl.BlockSpec((None, tm, D_MODEL), lambda i: (SLAB_GB[0], i, SLAB_GB[1])),
                  pl.BlockSpec((None, N_MOD, D_MODEL), lambda i: (seqs.info(i * tm)[0], 0, 0)),
                  pl.BlockSpec((1, HG_WIDTH), const),
                  pl.BlockSpec((1, D_MODEL), const),
                  pl.BlockSpec((NA_WIDTH, D_MODEL), const),
                  pl.BlockSpec((HG_WIDTH, D_MODEL), const),
                  pl.BlockSpec((D_MODEL, D_MODEL), const),
                  pl.BlockSpec((2, N_EXPERTS, D_MODEL), lambda i: (0, 0, 0))],
        out_specs=[pl.BlockSpec((tm, D_MODEL), tok),
                   pl.BlockSpec((2, tm, PACK_W), lambda i: (0, i, 0)),
                   pl.BlockSpec((N_EXPERTS, tm), lambda i: (0, i))],
        out_shape=[jax.ShapeDtypeStruct((n, D_MODEL), F32),
                   jax.ShapeDtypeStruct((2, n, PACK_W), jnp.uint32),
                   jax.ShapeDtypeStruct((N_EXPERTS, n), F32)],
        compiler_params=_params(56, ("arbitrary",)),
        name="merge",
    )(x_p, x_s, att, o_f, o_b, proj, proj, proj, mod, hg_norm.reshape(1, HG_WIDTH),
      norm_ffn.reshape(1, D_MODEL), wa, wb, wo, wr)


def _expert_kernel(be_ref, nused_ref, next_ref, slot_ref, xs_ref, wg_hbm, wu_hbm, wd_hbm, ys_ref,
                   wg_s, wu_s, wd_s, wg_st, wu_st, wd_st, sems):
    def weight_copies(e, slot):
        pairs = ((wg_hbm, wg_st), (wu_hbm, wu_st), (wd_hbm, wd_st))
        return [pltpu.make_async_copy(hbm.at[e], stage.at[slot], sems.at[slot, j])
                for j, (hbm, stage) in enumerate(pairs)]

    def block(i, rows):
        e = be_ref[i]
        slot = slot_ref[i]
        nxt = next_ref[i]
        last = nused_ref[0] - 1
        first = (i == 0) | (e != be_ref[jnp.maximum(i - 1, 0)])
        only = (i == last) | (be_ref[jnp.minimum(i + 1, last)] != e)
        second = (i >= 1) & jnp.logical_not(first) & (
            (i == 1) | (be_ref[jnp.maximum(i - 2, 0)] != e))

        @pl.when(first)
        def _():
            @pl.when(i == 0)
            def _():
                for c in weight_copies(e, slot):
                    c.start()

            for c in weight_copies(e, slot):
                c.wait()

            @pl.when(nxt >= 0)
            def _():
                ahead = weight_copies(nxt, 1 - slot)
                ahead[0].start()
                ahead[1].start()

                @pl.when(only)
                def _():
                    ahead[2].start()

            wg_s[...] = wg_st[slot].astype(BF16)
            wu_s[...] = wu_st[slot].astype(BF16)
            wd_s[...] = wd_st[slot].astype(BF16)

        @pl.when(second & (nxt >= 0))
        def _():
            weight_copies(nxt, 1 - slot)[2].start()

        xq = _unpack_rows(xs_ref[0, rows, :], xs_ref[1, rows, :])
        a = _dot_quarters(xq, wg_s)
        b = _dot_quarters(xq, wu_s)
        h = (a * _sigmoid(a) * b).astype(BF16)
        ys_ref[0, rows, :], ys_ref[1, rows, :] = _pack_rows(
            jnp.dot(h, wd_s[...], preferred_element_type=F32))

    for j in range(MOE_SUB):
        i = pl.program_id(0) * MOE_SUB + j
        pl.when(i < nused_ref[0])(functools.partial(block, i, slice(j * MOE_BLOCK, (j + 1) * MOE_BLOCK)))


def _experts(xs, plan, wg, wu, wd):
    n_rows = xs.shape[1]
    step_rows = MOE_SUB * MOE_BLOCK
    rows_map = lambda i, be, nu, nx, sl: (0, jnp.minimum(i, (nu[0] - 1) // MOE_SUB), 0)
    whole = pl.BlockSpec(memory_space=pl.ANY)
    grid_spec = pltpu.PrefetchScalarGridSpec(
        num_scalar_prefetch=4,
        grid=(n_rows // step_rows,),
        in_specs=[pl.BlockSpec((2, step_rows, PACK_W), rows_map), whole, whole, whole],
        out_specs=pl.BlockSpec((2, step_rows, PACK_W), rows_map),
        scratch_shapes=[pltpu.VMEM((D_MODEL, D_EXPERT), BF16),
                        pltpu.VMEM((D_MODEL, D_EXPERT), BF16),
                        pltpu.VMEM((D_EXPERT, D_MODEL), BF16),
                        pltpu.VMEM((2, D_MODEL, D_EXPERT), F32),
                        pltpu.VMEM((2, D_MODEL, D_EXPERT), F32),
                        pltpu.VMEM((2, D_EXPERT, D_MODEL), F32),
                        pltpu.SemaphoreType.DMA((2, 3))],
    )
    return pl.pallas_call(
        _expert_kernel,
        grid_spec=grid_spec,
        out_shape=jax.ShapeDtypeStruct((2, n_rows, PACK_W), jnp.uint32),
        compiler_params=_params(56, ("arbitrary",)),
        name="experts",
    )(*plan, xs, wg, wu, wd)


ROUTE_TILE = 128
ROUTE_STEP = 1024
NEG_INF = float("-inf")


def _first_max(x, idx, big):
    m = jnp.max(x, axis=0, keepdims=True)
    first = jnp.min(jnp.where(x == m, idx, big), axis=0, keepdims=True)
    return m, first


def _route_tile(lg, bias, tri, base):
    per_group = N_EXPERTS // N_GROUPS
    scores = jax.nn.sigmoid(lg)
    sel = scores + bias
    lrow = lax.broadcasted_iota(jnp.int32, (per_group, ROUTE_TILE), 0)
    gs = []
    for g in range(N_GROUPS):
        x = sel[g * per_group:(g + 1) * per_group]
        m1, first = _first_max(x, lrow, per_group)
        m2 = jnp.max(jnp.where(lrow == first, NEG_INF, x), axis=0, keepdims=True)
        gs.append(m1 + m2)
    cur = jnp.concatenate(gs, axis=0)
    grow = lax.broadcasted_iota(jnp.int32, (N_GROUPS, ROUTE_TILE), 0)
    chosen = jnp.zeros((N_GROUPS, ROUTE_TILE), jnp.int32)
    for _ in range(TOPK_GROUPS):
        _, first = _first_max(cur, grow, N_GROUPS)
        hit = grow == first
        chosen = jnp.where(hit, 1, chosen)
        cur = jnp.where(hit, NEG_INF, cur)
    cur = jnp.concatenate(
        [jnp.where(chosen[g:g + 1] > 0, sel[g * per_group:(g + 1) * per_group], NEG_INF)
         for g in range(N_GROUPS)], axis=0)
    row = lax.broadcasted_iota(jnp.int32, (N_EXPERTS, ROUTE_TILE), 0)
    member = jnp.zeros((N_EXPERTS, ROUTE_TILE), F32)
    es, ws = [], []
    for _ in range(TOP_K):
        _, first = _first_max(cur, row, N_EXPERTS)
        hit = row == first
        es.append(first)
        ws.append(jnp.sum(jnp.where(hit, scores, 0.0), axis=0, keepdims=True))
        cur = jnp.where(hit, NEG_INF, cur)
        member = jnp.where(hit, 1.0, member)
    e = jnp.concatenate(es, axis=0)
    w = jnp.concatenate(ws, axis=0)
    w = w / jnp.sum(w, axis=0, keepdims=True) * ROUTE_SCALE
    before = jnp.dot(member.astype(BF16), tri, preferred_element_type=F32) + base
    rank = jnp.concatenate(
        [jnp.sum(jnp.where(row == es[k], before, 0.0), axis=0, keepdims=True) for k in range(TOP_K)],
        axis=0)
    return e, w, rank.astype(jnp.int32), member


def _route_kernel(lg_ref, b_ref, e_ref, w_ref, r_ref, cnt_ref, base):
    @pl.when(pl.program_id(0) == 0)
    def _():
        base[...] = jnp.zeros_like(base)

    r_i = lax.broadcasted_iota(jnp.int32, (ROUTE_TILE, ROUTE_TILE), 0)
    c_i = lax.broadcasted_iota(jnp.int32, (ROUTE_TILE, ROUTE_TILE), 1)
    tri = (r_i < c_i).astype(BF16)
    bias = b_ref[...]
    for j in range(ROUTE_STEP // ROUTE_TILE):
        ls = slice(j * ROUTE_TILE, (j + 1) * ROUTE_TILE)
        e, w, rank, member = _route_tile(lg_ref[:, ls], bias, tri, base[...])
        e_ref[:, ls] = e
        w_ref[:, ls] = w
        r_ref[:, ls] = rank
        base[...] = base[...] + jnp.sum(member, axis=1, keepdims=True)
    cnt_ref[...] = base[...]


def _route(logits_t, b_router):
    n = logits_t.shape[1]
    tok = lambda i: (0, i)
    return pl.pallas_call(
        _route_kernel,
        grid=(n // ROUTE_STEP,),
        in_specs=[pl.BlockSpec((N_EXPERTS, ROUTE_STEP), tok),
                  pl.BlockSpec((N_EXPERTS, 1), lambda i: (0, 0))],
        out_specs=[pl.BlockSpec((TOP_K, ROUTE_STEP), tok),
                   pl.BlockSpec((TOP_K, ROUTE_STEP), tok),
                   pl.BlockSpec((TOP_K, ROUTE_STEP), tok),
                   pl.BlockSpec((N_EXPERTS, 1), lambda i: (0, 0))],
        out_shape=[jax.ShapeDtypeStruct((TOP_K, n), jnp.int32),
                   jax.ShapeDtypeStruct((TOP_K, n), F32),
                   jax.ShapeDtypeStruct((TOP_K, n), jnp.int32),
                   jax.ShapeDtypeStruct((N_EXPERTS, 1), F32)],
        scratch_shapes=[pltpu.VMEM((N_EXPERTS, 1), F32)],
        compiler_params=_params(32, ("arbitrary",)),
        name="route",
    )(logits_t, b_router.reshape(N_EXPERTS, 1).astype(F32))


def _pos_kernel(e_ref, r_ref, ps_ref, pos_ref):
    row = lax.broadcasted_iota(jnp.int32, (N_EXPERTS, ROUTE_TILE), 0)
    pstart = ps_ref[...]
    for j in range(ROUTE_STEP // ROUTE_TILE):
        ls = slice(j * ROUTE_TILE, (j + 1) * ROUTE_TILE)
        e = e_ref[:, ls]
        off = jnp.concatenate(
            [jnp.sum(jnp.where(row == e[k:k + 1], pstart, 0.0), axis=0, keepdims=True)
             for k in range(TOP_K)], axis=0)
        pos_ref[:, ls] = off.astype(jnp.int32) + r_ref[:, ls]


def _positions(eidx, rank, pstarts):
    n = eidx.shape[1]
    tok = lambda i: (0, i)
    return pl.pallas_call(
        _pos_kernel,
        grid=(n // ROUTE_STEP,),
        in_specs=[pl.BlockSpec((TOP_K, ROUTE_STEP), tok),
                  pl.BlockSpec((TOP_K, ROUTE_STEP), tok),
                  pl.BlockSpec((N_EXPERTS, 1), lambda i: (0, 0))],
        out_specs=pl.BlockSpec((TOP_K, ROUTE_STEP), tok),
        out_shape=jax.ShapeDtypeStruct((TOP_K, n), jnp.int32),
        compiler_params=_params(32, ("arbitrary",)),
        name="positions",
    )(eidx, rank, pstarts.reshape(N_EXPERTS, 1).astype(F32))


def _block_tables(counts, n_blocks):
    counts = counts.reshape(N_EXPERTS).astype(jnp.int32)
    padded = (counts + MOE_BLOCK - 1) // MOE_BLOCK * MOE_BLOCK
    pend = jnp.cumsum(padded)
    pstarts = pend - padded
    first_row = jnp.arange(n_blocks, dtype=jnp.int32) * MOE_BLOCK
    block_e = jnp.sum((pend[None, :] <= first_row[:, None]).astype(jnp.int32), axis=1)
    block_e = jnp.minimum(block_e, N_EXPERTS - 1).astype(jnp.int32)
    n_used = (pend[-1] // MOE_BLOCK).astype(jnp.int32).reshape(1)
    ids = jnp.arange(N_EXPERTS, dtype=jnp.int32)
    used = padded > 0
    ordinal = jnp.cumsum(used.astype(jnp.int32)) - 1
    from_here = jnp.flip(lax.cummin(jnp.flip(jnp.where(used, ids, N_EXPERTS))))
    after = jnp.concatenate([from_here[1:], jnp.full((1,), N_EXPERTS, jnp.int32)])
    after = jnp.where(after >= N_EXPERTS, -1, after)
    mine = block_e[:, None] == ids[None, :]
    next_e = jnp.sum(jnp.where(mine, after[None, :], 0), axis=1).astype(jnp.int32)
    slot = jnp.sum(jnp.where(mine, ordinal[None, :] % 2, 0), axis=1).astype(jnp.int32)
    return pstarts, (block_e, n_used, next_e, slot)


SC_WINDOW = 128


def _sc_mesh():
    return plsc.VectorSubcoreMesh(core_axis_name="core", subcore_axis_name="subcore")


def _both_halves(pos, n_rows):
    return jnp.concatenate([pos, pos + n_rows]).reshape(1, -1)


def _sc_dispatch(rows, pos, n_rows):
    _, n, w = rows.shape
    tiles = n // SC_WINDOW
    top_k = pos.shape[0] // n
    idx = jnp.transpose(pos.reshape(top_k, tiles, SC_WINDOW), (1, 0, 2))
    idx = jnp.stack([idx, idx + n_rows]).reshape(1, -1)

    @functools.partial(pl.kernel, out_type=jax.ShapeDtypeStruct((2 * n_rows, w), rows.dtype),
                       mesh=_sc_mesh(), scratch_types=[])
    def scatter_kernel(x_hbm, i_hbm, o_hbm):
        def body(x_vmem, i_vmem):
            pltpu.sync_copy(x_vmem, o_hbm.at[i_vmem.at[0]])

        pltpu.emit_pipeline(
            body,
            grid=(2 * tiles * top_k,),
            in_specs=[pl.BlockSpec((SC_WINDOW, w), lambda i: (i // top_k, 0)),
                      pl.BlockSpec((1, SC_WINDOW), lambda i: (0, i))],
            out_specs=[],
            core_axis_name=("core", "subcore"),
            dimension_semantics=(pltpu.PARALLEL,),
        )(x_hbm, i_hbm)

    out = scatter_kernel(rows.reshape(2 * n, w), idx)
    return out.reshape(2, n_rows, w)


def _sc_gather(table, pos):
    _, n_rows, w = table.shape
    m = pos.shape[0]

    @functools.partial(pl.kernel, out_type=jax.ShapeDtypeStruct((2 * m, w), table.dtype),
                       mesh=_sc_mesh(), scratch_types=[])
    def gather_kernel(t_hbm, i_hbm, o_hbm):
        def body(i_vmem, o_vmem):
            pltpu.sync_copy(t_hbm.at[i_vmem.at[0]], o_vmem)

        pltpu.emit_pipeline(
            body,
            grid=(2 * m // SC_WINDOW,),
            in_specs=[pl.BlockSpec((1, SC_WINDOW), lambda i: (0, i))],
            out_specs=[pl.BlockSpec((SC_WINDOW, w), lambda i: (i, 0))],
            core_axis_name=("core", "subcore"),
            dimension_semantics=(pltpu.PARALLEL,),
        )(i_hbm, o_hbm)

    out = gather_kernel(table.reshape(2 * n_rows, w), _both_halves(pos, n_rows))
    return out.reshape(2, m, w)


def _final_kernel(npt, x1_ref, u2_ref, yg_ref, w_ref, mod_ref, nf_ref, wsg_ref, wsu_ref, wsd_ref,
                  op_ref, os_ref):
    uq = _unpack_rows(u2_ref[0], u2_ref[1])
    a = _dot_quarters(uq, wsg_ref)
    b = _dot_quarters(uq, wsu_ref)
    f = jnp.dot((a * _sigmoid(a) * b).astype(BF16), wsd_ref[...], preferred_element_type=F32)
    w = w_ref[...]
    fq = [f[:, q * PACK_W:(q + 1) * PACK_W] for q in range(4)]
    for k in range(TOP_K):
        yq = _unpack_rows(yg_ref[0, k], yg_ref[1, k], F32)
        fq = [fq[q] + w[:, k:k + 1] * yq[q] for q in range(4)]
    x2 = x1_ref[...] + mod_ref[5:6, :] * jnp.concatenate(fq, axis=1)
    out = _rms(x2) * nf_ref[...]
    i = pl.program_id(0)

    @pl.when(i < npt)
    def _():
        op_ref[...] = out

    @pl.when(i >= npt)
    def _():
        os_ref[...] = out


def _final(seqs, x1, u2, yg, w, mod, norm_final, wsg, wsu, wsd):
    n = seqs.n
    tm = 512
    tok = lambda i: (i, 0)
    const = lambda i: (0, 0)
    npt, p_spec, s_spec = _two_stream_specs(seqs, tm)
    return pl.pallas_call(
        functools.partial(_final_kernel, npt),
        grid=(n // tm,),
        in_specs=[pl.BlockSpec((tm, D_MODEL), tok),
                  pl.BlockSpec((2, tm, PACK_W), lambda i: (0, i, 0)),
                  pl.BlockSpec((2, TOP_K, tm, PACK_W), lambda i: (0, 0, i, 0)),
                  pl.BlockSpec((tm, TOP_K), tok),
                  pl.BlockSpec((None, N_MOD, D_MODEL), lambda i: (seqs.info(i * tm)[0], 0, 0)),
                  pl.BlockSpec((1, D_MODEL), const),
                  pl.BlockSpec((D_MODEL, D_EXPERT), const),
                  pl.BlockSpec((D_MODEL, D_EXPERT), const),
                  pl.BlockSpec((D_EXPERT, D_MODEL), const)],
        out_specs=[p_spec, s_spec],
        out_shape=[jax.ShapeDtypeStruct((seqs.np_, D_MODEL), F32),
                   jax.ShapeDtypeStruct((n - seqs.np_, D_MODEL), F32)],
        compiler_params=_params(48, ("arbitrary",)),
        name="final",
    )(x1, u2, yg, w, mod, norm_final.reshape(1, D_MODEL), wsg, wsu, wsd)


def _layer(seqs, x_p, x_s, c, w_ada, b_ada, norm_mix, w_in, na_rpb, hg_lb, hg_norm, w_branch_a,
           w_branch_b, w_out, norm_ffn, w_router, b_router, w_exp_gate, w_exp_up, w_exp_down,
           w_sh_gate, w_sh_up, w_sh_down, norm_final):
    n = seqs.n
    c_rows = -(-seqs.nseq // 8) * 8
    c_pad = jnp.zeros((c_rows, D_MODEL), F32).at[:seqs.nseq].set(c)
    mod = _ada(c_pad, w_ada[0], b_ada[0])[:seqs.nseq].reshape(seqs.nseq, N_MOD, D_MODEL)
    lb = jnp.cumsum(jax.nn.softmax(hg_lb.astype(F32), axis=0), axis=0)[0]

    proj = _inproj(seqs, x_p, x_s, mod, norm_mix[0], w_in[0].astype(BF16))
    att = _na(seqs, proj, _na_bias_table(na_rpb[0]))
    o_f, o_b = _hgrn(seqs, proj, lb)
    x1, u2, logits = _merge(seqs, x_p, x_s, att, o_f, o_b, proj, mod, hg_norm[0], norm_ffn[0],
                            w_branch_a[0].astype(BF16), w_branch_b[0].astype(BF16),
                            w_out[0].astype(BF16), _split_hi_lo(w_router[0].T))

    eidx, w, rank, counts = _route(logits, b_router[0])
    step_rows = MOE_SUB * MOE_BLOCK
    n_rows = -(-(n * TOP_K + N_EXPERTS * MOE_BLOCK) // step_rows) * step_rows
    pstarts, plan = _block_tables(counts, n_rows // MOE_BLOCK)
    pos = _positions(eidx, rank, pstarts).reshape(-1)
    xs = _sc_dispatch(u2, pos, n_rows)
    ys = _experts(xs, plan, w_exp_gate[0], w_exp_up[0], w_exp_down[0])
    yg = _sc_gather(ys, pos).reshape(2, TOP_K, n, PACK_W)
    return _final(seqs, x1, u2, yg, w.T, mod, norm_final, w_sh_gate[0].astype(BF16),
                  w_sh_up[0].astype(BF16), w_sh_down[0].astype(BF16))


def kernel(x_prompt, x_sample, c_prompt, c_sample, w_ada, b_ada, norm_mix, w_in, na_rpb, hg_lb, hg_norm, w_branch_a, w_branch_b, w_out, norm_ffn, w_router, b_router, w_exp_gate, w_exp_up, w_exp_down, w_sh_gate, w_sh_up, w_sh_down, norm_final):
    bp, tp, _ = x_prompt.shape
    bs, ts, _ = x_sample.shape
    for t in (tp, ts):
        assert t % 1024 == 0, "sequence lengths must be multiples of the 1024-token tiles"
        assert t // GRID_W >= NA_WIN, "a sequence needs at least NA_WIN grid rows"
    seqs = _Seqs(bp, tp, bs, ts)
    c = jnp.concatenate([c_prompt, c_sample])
    y_p, y_s = _layer(seqs, x_prompt.reshape(bp * tp, D_MODEL), x_sample.reshape(bs * ts, D_MODEL),
                      c, w_ada, b_ada, norm_mix, w_in, na_rpb, hg_lb, hg_norm, w_branch_a,
                      w_branch_b, w_out, norm_ffn, w_router, b_router, w_exp_gate, w_exp_up,
                      w_exp_down, w_sh_gate, w_sh_up, w_sh_down, norm_final)
    return (y_p.reshape(bp, tp, D_MODEL), y_s.reshape(bs, ts, D_MODEL))
```

```python
import functools

import jax
import jax.numpy as jnp
import numpy as np
from jax import lax
from jax.experimental import pallas as pl
from jax.experimental.pallas import tpu as pltpu
from jax.experimental.pallas import tpu_sc as plsc

D_MODEL = 1024
GRID_W = 64
NA_HEADS = 8
NA_HEAD_DIM = 64
NA_WIDTH = NA_HEADS * NA_HEAD_DIM
NA_ROWS = 8
NA_COLS = 16
HG_HEADS = 4
HG_KEY_DIM = 128
HG_WIDTH = HG_HEADS * HG_KEY_DIM
HG_CHUNK = 128
HG_EXP_LIMIT = 80.0
N_EXPERTS = 256
TOP_K = 8
N_GROUPS = 8
TOPK_GROUPS = 4
D_EXPERT = 256
ROUTE_SCALE = 2.5
N_MOD = 6
RMS_EPS = 1e-6

MOE_BLOCK = 512
MOE_SUB = 2
NA_GROUP = 4
NA_WIN = 3 * NA_GROUP
NA_TOK = NA_GROUP * GRID_W
NA_PER_STEP = 2
LOG2E = 1.4426950408889634
NA_Q_SCALE = NA_HEAD_DIM ** -0.5 * LOG2E
HG_STEP = 512
MASK_VALUE = -1e30

F32 = jnp.float32
BF16 = jnp.bfloat16
HIGHEST = lax.Precision.HIGHEST
NT_DIMS = (((1,), (1,)), ((), ()))
TN_DIMS = (((0,), (0,)), ((), ()))

PROJ_TN = 2048
SLAB_Q, SLAB_K, SLAB_V, SLAB_HQ = (0, 0), (0, 1), (0, 2), (0, 3)
SLAB_FF, SLAB_FB, SLAB_HI, SLAB_HG = (1, 0), (1, 1), (1, 2), (1, 3)
SLAB_GA, SLAB_GB = (2, 0), (2, 1)


def _params(vmem_mb, sem=None):
    kw = dict(vmem_limit_bytes=vmem_mb * 1024 * 1024)
    if sem is not None:
        kw["dimension_semantics"] = sem
    return pltpu.CompilerParams(**kw)


class _Seqs:
    def __init__(self, bp, tp, bs, ts):
        self.bp, self.tp, self.bs, self.ts = bp, tp, bs, ts
        self.np_ = bp * tp
        self.n = bp * tp + bs * ts
        self.nseq = bp + bs

    def info(self, t0):
        in_p = t0 < self.np_
        rel = jnp.maximum(t0 - self.np_, 0)
        sid = jnp.where(in_p, t0 // self.tp, self.bp + rel // self.ts)
        start = jnp.where(in_p, (t0 // self.tp) * self.tp, self.np_ + (rel // self.ts) * self.ts)
        length = jnp.where(in_p, self.tp, self.ts)
        return sid, start, length


def _ada_kernel(c_ref, w_ref, b_ref, o_ref):
    c = c_ref[...]
    a = c * jax.nn.sigmoid(c)
    o_ref[...] = jnp.dot(a, w_ref[...], precision=HIGHEST, preferred_element_type=F32) + b_ref[...]


def _ada(c_pad, w_ada, b_ada):
    rows = c_pad.shape[0]
    n_out = w_ada.shape[1]
    tn = 1024
    return pl.pallas_call(
        _ada_kernel,
        grid=(n_out // tn,),
        in_specs=[pl.BlockSpec((rows, D_MODEL), lambda j: (0, 0)),
                  pl.BlockSpec((D_MODEL, tn), lambda j: (0, j)),
                  pl.BlockSpec((1, tn), lambda j: (0, j))],
        out_specs=pl.BlockSpec((rows, tn), lambda j: (0, j)),
        out_shape=jax.ShapeDtypeStruct((rows, n_out), F32),
        compiler_params=_params(32),
        name="ada",
    )(c_pad, w_ada, b_ada.reshape(1, n_out))


def _rms(x):
    return x * lax.rsqrt(jnp.mean(x * x, axis=-1, keepdims=True) + RMS_EPS)


def _sigmoid(x):
    return 0.5 * jnp.tanh(0.5 * x) + 0.5


PACK_W = D_MODEL // 4


def _pack_rows(x):
    out = []
    for h in range(2):
        lo = x[:, (2 * h) * PACK_W:(2 * h + 1) * PACK_W].astype(BF16).astype(F32)
        hi = x[:, (2 * h + 1) * PACK_W:(2 * h + 2) * PACK_W].astype(BF16).astype(F32)
        out.append(lax.bitcast_convert_type(hi, jnp.uint32)
                   | (lax.bitcast_convert_type(lo, jnp.uint32) >> 16))
    return out


def _unpack_rows(p0, p1, dtype=BF16):
    quarters = []
    for p in (p0, p1):
        quarters.append(lax.bitcast_convert_type(p << 16, F32).astype(dtype))
        quarters.append(lax.bitcast_convert_type(p & jnp.uint32(0xFFFF0000), F32).astype(dtype))
    return quarters


def _dot_quarters(quarters, w_ref):
    acc = None
    for q, xq in enumerate(quarters):
        part = jnp.dot(xq, w_ref[q * PACK_W:(q + 1) * PACK_W, :], preferred_element_type=F32)
        acc = part if acc is None else acc + part
    return acc


def _two_stream_specs(seqs, tm, grid_rank=1):
    npt = seqs.np_ // tm
    nst = (seqs.n - seqs.np_) // tm
    if grid_rank == 1:
        p_map = lambda i: (jnp.minimum(i, npt - 1), 0)
        s_map = lambda i: (jnp.clip(i - npt, 0, nst - 1), 0)
    else:
        p_map = lambda i, j: (jnp.minimum(i, npt - 1), 0)
        s_map = lambda i, j: (jnp.clip(i - npt, 0, nst - 1), 0)
    return npt, pl.BlockSpec((tm, D_MODEL), p_map), pl.BlockSpec((tm, D_MODEL), s_map)


def _inproj_kernel(npt, xp_ref, xs_ref, mod_ref, g_ref, w_ref, cs_ref, o_ref, u_scr):
    @pl.when(pl.program_id(1) == 0)
    def _():
        x = jnp.where(pl.program_id(0) < npt, xp_ref[...], xs_ref[...])
        y = _rms(x) * g_ref[...]
        u = y * (1.0 + mod_ref[1:2, :]) + mod_ref[0:1, :]
        u_scr[...] = u.astype(BF16)

    acc = jnp.dot(u_scr[...], w_ref[...], preferred_element_type=F32)
    o_ref[...] = (acc * cs_ref[...]).astype(o_ref.dtype)


def _inproj(seqs, x_p, x_s, mod, norm_mix, w_in_bf):
    n = seqs.n
    tm = min(1024, seqs.tp, seqs.ts)
    tn = PROJ_TN
    n_slab = w_in_bf.shape[1] // tn
    npt, p_spec, s_spec = _two_stream_specs(seqs, tm, grid_rank=2)
    col_scale = jnp.ones((1, w_in_bf.shape[1]), F32).at[:, :NA_WIDTH].set(NA_Q_SCALE)
    return pl.pallas_call(
        functools.partial(_inproj_kernel, npt),
        grid=(n // tm, n_slab),
        in_specs=[p_spec, s_spec,
                  pl.BlockSpec((None, N_MOD, D_MODEL), lambda i, j: (seqs.info(i * tm)[0], 0, 0)),
                  pl.BlockSpec((1, D_MODEL), lambda i, j: (0, 0)),
                  pl.BlockSpec((D_MODEL, tn), lambda i, j: (0, j)),
                  pl.BlockSpec((1, tn), lambda i, j: (0, j))],
        out_specs=pl.BlockSpec((None, tm, tn), lambda i, j: (j, i, 0)),
        out_shape=jax.ShapeDtypeStruct((n_slab, n, tn), BF16),
        scratch_shapes=[pltpu.VMEM((tm, D_MODEL), BF16)],
        compiler_params=_params(56, ("arbitrary", "arbitrary")),
        name="inproj",
    )(x_p, x_s, mod, norm_mix.reshape(1, D_MODEL), w_in_bf, col_scale)


def _na_bias_table(rpb):
    col = np.arange(GRID_W)
    cs = np.clip(col - NA_COLS // 2, 0, GRID_W - NA_COLS)
    valid = (col[None, :] >= cs[:, None]) & (col[None, :] < cs[:, None] + NA_COLS)
    coff = col[None, :] - col[:, None] + NA_COLS - 1
    onehot = (coff[None] == np.arange(2 * NA_COLS - 1)[:, None, None]) & valid[None]
    toep = jnp.einsum("hrc,cqk->hrqk", rpb.astype(F32), jnp.asarray(onehot, F32),
                      precision=HIGHEST)
    toep = jnp.where(valid[None, None], toep * LOG2E, MASK_VALUE)
    masked =jnp.full((NA_HEADS, GRID_W, GRID_W), MASK_VALUE, F32)
    cases = (([0] * NA_GROUP, [NA_ROWS - 1 - i for i in range(NA_GROUP)]),
             (list(range(NA_GROUP)), [NA_ROWS // 2 - 1] * NA_GROUP),
             ([NA_GROUP] * NA_GROUP, [NA_ROWS // 2 - 1 - i for i in range(NA_GROUP)]))
    tabs = []
    for first_row, first_off in cases:
        q_rows = []
        for i in range(NA_GROUP):
            blocks = [toep[:, first_off[i] + w - first_row[i]]
                      if 0 <= w - first_row[i] < NA_ROWS else masked for w in range(NA_WIN)]
            q_rows.append(jnp.concatenate(blocks, axis=2))
        tabs.append(jnp.concatenate(q_rows, axis=1))
    return jnp.stack(tabs)


def _na_geometry(seqs, g):
    _, start, length = seqs.info(g * NA_TOK)
    row0 = start // GRID_W
    rows = length // GRID_W
    r0 = g * NA_GROUP - row0
    wb = jnp.clip(r0 - NA_ROWS // 2, 0, rows - NA_WIN)
    case = jnp.where(r0 == 0, 0, jnp.where(r0 == rows - NA_GROUP, 2, 1))
    return (row0 + wb) // NA_GROUP, case


def _na_kernel(q_ref, *refs):
    o_ref = refs[-1]
    for u in range(NA_PER_STEP):
        k_refs = refs[7 * u:7 * u + 3]
        v_refs = refs[7 * u + 3:7 * u + 6]
        bias_ref = refs[7 * u + 6]
        rows = slice(u * NA_TOK, (u + 1) * NA_TOK)
        outs = []
        for h in range(NA_HEADS):
            hs = slice(h * NA_HEAD_DIM, (h + 1) * NA_HEAD_DIM)
            q = q_ref[rows, hs]
            s = [lax.dot_general(q, kr[:, hs], NT_DIMS, preferred_element_type=F32)
                 + bias_ref[h, :, d * NA_TOK:(d + 1) * NA_TOK] for d, kr in enumerate(k_refs)]
            m = jnp.max(jnp.maximum(jnp.maximum(s[0], s[1]), s[2]), axis=-1, keepdims=True)
            p = [jnp.exp2(sd - m) for sd in s]
            l = jnp.sum((p[0] + p[1]) + p[2], axis=-1, keepdims=True)
            o = sum(jnp.dot(pd.astype(BF16), vr[:, hs], preferred_element_type=F32)
                    for pd, vr in zip(p, v_refs))
            outs.append(o / l)
        o_ref[rows, :] = jnp.concatenate(outs, axis=1).astype(o_ref.dtype)


def _na(seqs, proj, bias_tab):
    n = seqs.n
    step_tok = NA_PER_STEP * NA_TOK

    def group_specs(u):
        geometry = lambda s: _na_geometry(seqs, NA_PER_STEP * s + u)
        kv = [pl.BlockSpec((None, NA_TOK, NA_WIDTH),
                           lambda s, slab=slab, d=d: (slab[0], geometry(s)[0] + d, slab[1]))
              for slab in (SLAB_K, SLAB_V) for d in range(3)]
        bias = pl.BlockSpec((None,) + bias_tab.shape[1:], lambda s: (geometry(s)[1], 0, 0, 0))
        return kv + [bias]

    specs = [pl.BlockSpec((None, step_tok, NA_WIDTH), lambda s: (SLAB_Q[0], s, SLAB_Q[1]))]
    operands = [proj]
    for u in range(NA_PER_STEP):
        specs += group_specs(u)
        operands += [proj] * 6 + [bias_tab]
    return pl.pallas_call(
        _na_kernel,
        grid=(n // step_tok,),
        in_specs=specs,
        out_specs=pl.BlockSpec((step_tok, NA_WIDTH), lambda s: (s, 0)),
        out_shape=jax.ShapeDtypeStruct((n, NA_WIDTH), BF16),
        compiler_params=_params(56, ("arbitrary",)),
        name="natten",
    )(*operands)


def _hg_chunk(q, z, v, lb, tri, mask, mid, last, st_ref):
    sig = _sigmoid(z)
    f = lb + (1.0 - lb) * sig
    lf = jnp.log(f)
    kin = (1.0 - lb) * (1.0 - sig)
    hi = lf.astype(BF16)
    lo = (lf - hi.astype(F32)).astype(BF16)
    g2 = jnp.dot(tri, jnp.concatenate([lo, hi], axis=1), preferred_element_type=F32)
    gcum = g2[:, :HG_WIDTH] + g2[:, HG_WIDTH:]
    gm = gcum[mid:mid + 1, :]
    gl = gcum[last:last + 1, :]
    up = jnp.exp(gcum - gm)
    dn = jnp.exp(gm - gcum)
    qa = (q * up).astype(BF16)
    ka = (kin * dn).astype(BF16)
    qe = (q * (up * jnp.exp(gm))).astype(BF16)
    kd = (kin * (dn * jnp.exp(gl - gm))).astype(BF16)
    eg = jnp.exp(gl)
    vb = v.astype(BF16)
    outs = []
    for h in range(HG_HEADS):
        hs = slice(h * HG_KEY_DIM, (h + 1) * HG_KEY_DIM)
        a = lax.dot_general(qa[:, hs], ka[:, hs], NT_DIMS, preferred_element_type=F32)
        a = jnp.where(mask, a, 0.0)
        st = st_ref[h]
        o = jnp.dot(a.astype(BF16), vb[:, hs], preferred_element_type=F32)
        o = o + lax.dot_general(qe[:, hs], st.astype(BF16), NT_DIMS, preferred_element_type=F32)
        st_ref[h] = st * eg[:, hs] + lax.dot_general(vb[:, hs], kd[:, hs], TN_DIMS,
                                                    preferred_element_type=F32)
        outs.append(o)
    return jnp.concatenate(outs, axis=1)


def _hg_exact(q_ref, z_ref, v_ref, lb, reverse, st_ref, o_ref, qs, fs, ks, vs):
    sig = jax.nn.sigmoid(z_ref[...].astype(F32))
    qs[...] = q_ref[...].astype(F32)
    fs[...] = lb + (1.0 - lb) * sig
    ks[...] = (1.0 - lb) * (1.0 - sig)
    vs[...] = v_ref[...].astype(F32)
    eye = (lax.broadcasted_iota(jnp.int32, (HG_KEY_DIM, HG_KEY_DIM), 0)
           == lax.broadcasted_iota(jnp.int32, (HG_KEY_DIM, HG_KEY_DIM), 1)).astype(F32)

    def body(i, carry):
        t = HG_STEP - 1 - i if reverse else i
        q_t, f_t, k_t, v_t = (r[pl.ds(t, 1), :] for r in (qs, fs, ks, vs))
        outs = []
        for h in range(HG_HEADS):
            hs = slice(h * HG_KEY_DIM, (h + 1) * HG_KEY_DIM)
            v_col = jnp.sum(eye * v_t[:, hs], axis=1, keepdims=True)
            st = st_ref[h] * f_t[:, hs] + v_col * k_t[:, hs]
            st_ref[h] = st
            o_col = jnp.sum(st * q_t[:, hs], axis=1, keepdims=True)
            outs.append(jnp.sum(eye * o_col, axis=0, keepdims=True))
        o_ref[pl.ds(t, 1), :] = jnp.concatenate(outs, axis=1)
        return carry

    lax.fori_loop(0, HG_STEP, body, 0)


def _hg_kernel(seqs, safe_ref, qf_ref, zf_ref, vf_ref, qb_ref, zb_ref, vb_ref, lb_ref, of_ref,
               ob_ref, stf, stb, qs, fs, ks, vs):
    i = pl.program_id(0)
    nsteps = pl.num_programs(0)
    tf = i * HG_STEP
    tb = (nsteps - 1 - i) * HG_STEP
    _, start_f, _ = seqs.info(tf)
    _, start_b, len_b = seqs.info(tb)

    @pl.when(tf == start_f)
    def _():
        stf[...] = jnp.zeros_like(stf)

    @pl.when(tb + HG_STEP == start_b + len_b)
    def _():
        stb[...] = jnp.zeros_like(stb)

    lb = lb_ref[...]

    @pl.when(safe_ref[0] > 0)
    def _():
        row = lax.broadcasted_iota(jnp.int32, (HG_CHUNK, HG_CHUNK), 0)
        col = lax.broadcasted_iota(jnp.int32, (HG_CHUNK, HG_CHUNK), 1)
        lower = row >= col
        upper = col >= row
        tri_f = lower.astype(BF16)
        tri_b = upper.astype(BF16)
        nchunk = HG_STEP // HG_CHUNK
        for c in range(nchunk):
            cs = slice(c * HG_CHUNK, (c + 1) * HG_CHUNK)
            of_ref[cs, :] = _hg_chunk(qf_ref[cs, :].astype(F32), zf_ref[cs, :].astype(F32),
                                      vf_ref[cs, :].astype(F32), lb, tri_f, lower,
                                      HG_CHUNK // 2 - 1, HG_CHUNK - 1, stf)
            cb = nchunk - 1 - c
            bs = slice(cb * HG_CHUNK, (cb + 1) * HG_CHUNK)
            ob_ref[bs, :] = _hg_chunk(qb_ref[bs, :].astype(F32), zb_ref[bs, :].astype(F32),
                                      vb_ref[bs, :].astype(F32), lb, tri_b, upper,
                                      HG_CHUNK // 2, 0, stb)

    @pl.when(safe_ref[0] == 0)
    def _():
        _hg_exact(qf_ref, zf_ref, vf_ref, lb, False, stf, of_ref, qs, fs, ks, vs)
        _hg_exact(qb_ref, zb_ref, vb_ref, lb, True, stb, ob_ref, qs, fs, ks, vs)


def _hgrn(seqs, proj, lb):
    n = seqs.n
    nsteps = n // HG_STEP

    def spec(slab, rev):
        if rev:
            return pl.BlockSpec((None, HG_STEP, HG_WIDTH), lambda i: (slab[0], nsteps - 1 - i, slab[1]))
        return pl.BlockSpec((None, HG_STEP, HG_WIDTH), lambda i: (slab[0], i, slab[1]))

    safe = (jnp.max(-jnp.log(lb)) * (HG_CHUNK // 2) < HG_EXP_LIMIT).astype(jnp.int32).reshape(1)
    step_scratch = pltpu.VMEM((HG_STEP, HG_WIDTH), F32)
    return pl.pallas_call(
        functools.partial(_hg_kernel, seqs),
        grid=(nsteps,),
        in_specs=[pl.BlockSpec(memory_space=pltpu.SMEM),
                  spec(SLAB_HQ, False), spec(SLAB_FF, False), spec(SLAB_HI, False),
                  spec(SLAB_HQ, True), spec(SLAB_FB, True), spec(SLAB_HI, True),
                  pl.BlockSpec((1, HG_WIDTH), lambda i: (0, 0))],
        out_specs=[pl.BlockSpec((HG_STEP, HG_WIDTH), lambda i: (i, 0)),
                   pl.BlockSpec((HG_STEP, HG_WIDTH), lambda i: (nsteps - 1 - i, 0))],
        out_shape=[jax.ShapeDtypeStruct((n, HG_WIDTH), F32)] * 2,
        scratch_shapes=[pltpu.VMEM((HG_HEADS, HG_KEY_DIM, HG_KEY_DIM), F32)] * 2
        + [step_scratch] * 4,
        compiler_params=_params(32, ("arbitrary",)),
        name="hgrn2",
    )(safe, proj, proj, proj, proj, proj, proj, lb.reshape(1, HG_WIDTH))


def _merge_kernel(npt, xp_ref, xs_ref, att_ref, of_ref, ob_ref, hg_ref, ga_ref, gb_ref, mod_ref,
                  hgn_ref, nffn_ref, wa_ref, wb_ref, wo_ref, wr_ref, x1_ref, u2_ref, lg_ref):
    x = jnp.where(pl.program_id(0) < npt, xp_ref[...], xs_ref[...])
    o = of_ref[...] + ob_ref[...]
    parts = []
    for h in range(HG_HEADS):
        hs = slice(h * HG_KEY_DIM, (h + 1) * HG_KEY_DIM)
        parts.append(_rms(o[:, hs]))
    on = jnp.concatenate(parts, axis=1) * hgn_ref[...]
    gate = hg_ref[...].astype(F32)
    hb = (on * (gate * _sigmoid(gate))).astype(BF16)
    ya = jnp.dot(att_ref[...], wa_ref[...], preferred_element_type=F32)
    yb = jnp.dot(hb, wb_ref[...], preferred_element_type=F32)
    merged = (_sigmoid(ga_ref[...].astype(F32)) * ya
              + _sigmoid(gb_ref[...].astype(F32)) * yb)
    x1 = x + mod_ref[2:3, :] * jnp.dot(merged.astype(BF16), wo_ref[...],
                                       preferred_element_type=F32)
    x1_ref[...] = x1
    u2 = _rms(x1) * nffn_ref[...] * (1.0 + mod_ref[4:5, :]) + mod_ref[3:4, :]
    u2_ref[0], u2_ref[1] = _pack_rows(u2)
    w_hi = wr_ref[0]
    u_hi = u2.astype(BF16)
    u_lo = (u2 - u_hi.astype(F32)).astype(BF16)
    lg_ref[...] = (lax.dot_general(w_hi, u_hi, NT_DIMS, preferred_element_type=F32)
                   + (lax.dot_general(w_hi, u_lo, NT_DIMS, preferred_element_type=F32)
                      + lax.dot_general(wr_ref[1], u_hi, NT_DIMS, preferred_element_type=F32)))


def _split_hi_lo(w):
    hi = w.astype(BF16)
    return jnp.stack([hi, (w - hi.astype(F32)).astype(BF16)])


def _merge(seqs, x_p, x_s, att, o_f, o_b, proj, mod, hg_norm, norm_ffn, wa, wb, wo, wr):
    n = seqs.n
    tm = 512
    tok = lambda i: (i, 0)
    const = lambda i: (0, 0)
    npt, p_spec, s_spec = _two_stream_specs(seqs, tm)
    return pl.pallas_call(
        functools.partial(_merge_kernel, npt),
        grid=(n // tm,),
        in_specs=[p_spec, s_spec,
                  pl.BlockSpec((tm, NA_WIDTH), tok),
                  pl.BlockSpec((tm, HG_WIDTH), tok),
                  pl.BlockSpec((tm, HG_WIDTH), tok),
                  pl.BlockSpec((None, tm, HG_WIDTH), lambda i: (SLAB_HG[0], i, SLAB_HG[1])),
                  pl.BlockSpec((None, tm, D_MODEL), lambda i: (SLAB_GA[0], i, SLAB_GA[1])),
                  pl.BlockSpec((None, tm, D_MODEL), lambda i: (SLAB_GB[0], i, SLAB_GB[1])),
                  pl.BlockSpec((None, N_MOD, D_MODEL), lambda i: (seqs.info(i * tm)[0], 0, 0)),
                  pl.BlockSpec((1, HG_WIDTH), const),
                  pl.BlockSpec((1, D_MODEL), const),
                  pl.BlockSpec((NA_WIDTH, D_MODEL), const),
                  pl.BlockSpec((HG_WIDTH, D_MODEL), const),
                  pl.BlockSpec((D_MODEL, D_MODEL), const),
                  pl.BlockSpec((2, N_EXPERTS, D_MODEL), lambda i: (0, 0, 0))],
        out_specs=[pl.BlockSpec((tm, D_MODEL), tok),
                   pl.BlockSpec((2, tm, PACK_W), lambda i: (0, i, 0)),
                   pl.BlockSpec((N_EXPERTS, tm), lambda i: (0, i))],
        out_shape=[jax.ShapeDtypeStruct((n, D_MODEL), F32),
                   jax.ShapeDtypeStruct((2, n, PACK_W), jnp.uint32),
                   jax.ShapeDtypeStruct((N_EXPERTS, n), F32)],
        compiler_params=_params(56, ("arbitrary",)),
        name="merge",
    )(x_p, x_s, att, o_f, o_b, proj, proj, proj, mod, hg_norm.reshape(1, HG_WIDTH),
      norm_ffn.reshape(1, D_MODEL), wa, wb, wo, wr)


def _expert_kernel(be_ref, nused_ref, next_ref, slot_ref, xs_ref, wg_hbm, wu_hbm, wd_hbm, ys_ref,
                   wg_s, wu_s, wd_s, wg_st, wu_st, wd_st, sems):
    def weight_copies(e, slot):
        pairs = ((wg_hbm, wg_st), (wu_hbm, wu_st), (wd_hbm, wd_st))
        return [pltpu.make_async_copy(hbm.at[e], stage.at[slot], sems.at[slot, j])
                for j, (hbm, stage) in enumerate(pairs)]

    def block(i, rows):
        e = be_ref[i]
        slot = slot_ref[i]
        nxt = next_ref[i]
        last = nused_ref[0] - 1
        first = (i == 0) | (e != be_ref[jnp.maximum(i - 1, 0)])
        only = (i == last) | (be_ref[jnp.minimum(i + 1, last)] != e)
        second = (i >= 1) & jnp.logical_not(first) & (
            (i == 1) | (be_ref[jnp.maximum(i - 2, 0)] != e))

        @pl.when(first)
        def _():
            @pl.when(i == 0)
            def _():
                for c in weight_copies(e, slot):
                    c.start()

            for c in weight_copies(e, slot):
                c.wait()

            @pl.when(nxt >= 0)
            def _():
                ahead = weight_copies(nxt, 1 - slot)
                ahead[0].start()
                ahead[1].start()

                @pl.when(only)
                def _():
                    ahead[2].start()

            wg_s[...] = wg_st[slot].astype(BF16)
            wu_s[...] = wu_st[slot].astype(BF16)
            wd_s[...] = wd_st[slot].astype(BF16)

        @pl.when(second & (nxt >= 0))
        def _():
            weight_copies(nxt, 1 - slot)[2].start()

        xq = _unpack_rows(xs_ref[0, rows, :], xs_ref[1, rows, :])
        a = _dot_quarters(xq, wg_s)
        b = _dot_quarters(xq, wu_s)
        h = (a * _sigmoid(a) * b).astype(BF16)
        ys_ref[0, rows, :], ys_ref[1, rows, :] = _pack_rows(
            jnp.dot(h, wd_s[...], preferred_element_type=F32))

    for j in range(MOE_SUB):
        i = pl.program_id(0) * MOE_SUB + j
        pl.when(i < nused_ref[0])(functools.partial(block, i, slice(j * MOE_BLOCK, (j + 1) * MOE_BLOCK)))


def _experts(xs, plan, wg, wu, wd):
    n_rows = xs.shape[1]
    step_rows = MOE_SUB * MOE_BLOCK
    rows_map = lambda i, be, nu, nx, sl: (0, jnp.minimum(i, (nu[0] - 1) // MOE_SUB), 0)
    whole = pl.BlockSpec(memory_space=pl.ANY)
    grid_spec = pltpu.PrefetchScalarGridSpec(
        num_scalar_prefetch=4,
        grid=(n_rows // step_rows,),
        in_specs=[pl.BlockSpec((2, step_rows, PACK_W), rows_map), whole, whole, whole],
        out_specs=pl.BlockSpec((2, step_rows, PACK_W), rows_map),
        scratch_shapes=[pltpu.VMEM((D_MODEL, D_EXPERT), BF16),
                        pltpu.VMEM((D_MODEL, D_EXPERT), BF16),
                        pltpu.VMEM((D_EXPERT, D_MODEL), BF16),
                        pltpu.VMEM((2, D_MODEL, D_EXPERT), F32),
                        pltpu.VMEM((2, D_MODEL, D_EXPERT), F32),
                        pltpu.VMEM((2, D_EXPERT, D_MODEL), F32),
                        pltpu.SemaphoreType.DMA((2, 3))],
    )
    return pl.pallas_call(
        _expert_kernel,
        grid_spec=grid_spec,
        out_shape=jax.ShapeDtypeStruct((2, n_rows, PACK_W), jnp.uint32),
        compiler_params=_params(56, ("arbitrary",)),
        name="experts",
    )(*plan, xs, wg, wu, wd)


ROUTE_TILE = 128
ROUTE_STEP = 1024
NEG_INF = float("-inf")


def _first_max(x, idx, big):
    m = jnp.max(x, axis=0, keepdims=True)
    first = jnp.min(jnp.where(x == m, idx, big), axis=0, keepdims=True)
    return m, first


def _route_tile(lg, bias, tri, base):
    per_group = N_EXPERTS // N_GROUPS
    scores = jax.nn.sigmoid(lg)
    sel = scores + bias
    lrow = lax.broadcasted_iota(jnp.int32, (per_group, ROUTE_TILE), 0)
    gs = []
    for g in range(N_GROUPS):
        x = sel[g * per_group:(g + 1) * per_group]
        m1, first = _first_max(x, lrow, per_group)
        m2 = jnp.max(jnp.where(lrow == first, NEG_INF, x), axis=0, keepdims=True)
        gs.append(m1 + m2)
    cur = jnp.concatenate(gs, axis=0)
    grow = lax.broadcasted_iota(jnp.int32, (N_GROUPS, ROUTE_TILE), 0)
    chosen = jnp.zeros((N_GROUPS, ROUTE_TILE), jnp.int32)
    for _ in range(TOPK_GROUPS):
        _, first = _first_max(cur, grow, N_GROUPS)
        hit = grow == first
        chosen = jnp.where(hit, 1, chosen)
        cur = jnp.where(hit, NEG_INF, cur)
    cur = jnp.concatenate(
        [jnp.where(chosen[g:g + 1] > 0, sel[g * per_group:(g + 1) * per_group], NEG_INF)
         for g in range(N_GROUPS)], axis=0)
    row = lax.broadcasted_iota(jnp.int32, (N_EXPERTS, ROUTE_TILE), 0)
    member = jnp.zeros((N_EXPERTS, ROUTE_TILE), F32)
    es, ws = [], []
    for _ in range(TOP_K):
        _, first = _first_max(cur, row, N_EXPERTS)
        hit = row == first
        es.append(first)
        ws.append(jnp.sum(jnp.where(hit, scores, 0.0), axis=0, keepdims=True))
        cur = jnp.where(hit, NEG_INF, cur)
        member = jnp.where(hit, 1.0, member)
    e = jnp.concatenate(es, axis=0)
    w = jnp.concatenate(ws, axis=0)
    w = w / jnp.sum(w, axis=0, keepdims=True) * ROUTE_SCALE
    before = jnp.dot(member.astype(BF16), tri, preferred_element_type=F32) + base
    rank = jnp.concatenate(
        [jnp.sum(jnp.where(row == es[k], before, 0.0), axis=0, keepdims=True) for k in range(TOP_K)],
        axis=0)
    return e, w, rank.astype(jnp.int32), member


def _route_kernel(lg_ref, b_ref, e_ref, w_ref, r_ref, cnt_ref, base):
    @pl.when(pl.program_id(0) == 0)
    def _():
        base[...] = jnp.zeros_like(base)

    r_i = lax.broadcasted_iota(jnp.int32, (ROUTE_TILE, ROUTE_TILE), 0)
    c_i = lax.broadcasted_iota(jnp.int32, (ROUTE_TILE, ROUTE_TILE), 1)
    tri = (r_i < c_i).astype(BF16)
    bias = b_ref[...]
    for j in range(ROUTE_STEP // ROUTE_TILE):
        ls = slice(j * ROUTE_TILE, (j + 1) * ROUTE_TILE)
        e, w, rank, member = _route_tile(lg_ref[:, ls], bias, tri, base[...])
        e_ref[:, ls] = e
        w_ref[:, ls] = w
        r_ref[:, ls] = rank
        base[...] = base[...] + jnp.sum(member, axis=1, keepdims=True)
    cnt_ref[...] = base[...]


def _route(logits_t, b_router):
    n = logits_t.shape[1]
    tok = lambda i: (0, i)
    return pl.pallas_call(
        _route_kernel,
        grid=(n // ROUTE_STEP,),
        in_specs=[pl.BlockSpec((N_EXPERTS, ROUTE_STEP), tok),
                  pl.BlockSpec((N_EXPERTS, 1), lambda i: (0, 0))],
        out_specs=[pl.BlockSpec((TOP_K, ROUTE_STEP), tok),
                   pl.BlockSpec((TOP_K, ROUTE_STEP), tok),
                   pl.BlockSpec((TOP_K, ROUTE_STEP), tok),
                   pl.BlockSpec((N_EXPERTS, 1), lambda i: (0, 0))],
        out_shape=[jax.ShapeDtypeStruct((TOP_K, n), jnp.int32),
                   jax.ShapeDtypeStruct((TOP_K, n), F32),
                   jax.ShapeDtypeStruct((TOP_K, n), jnp.int32),
                   jax.ShapeDtypeStruct((N_EXPERTS, 1), F32)],
        scratch_shapes=[pltpu.VMEM((N_EXPERTS, 1), F32)],
        compiler_params=_params(32, ("arbitrary",)),
        name="route",
    )(logits_t, b_router.reshape(N_EXPERTS, 1).astype(F32))


def _pos_kernel(e_ref, r_ref, ps_ref, pos_ref):
    row = lax.broadcasted_iota(jnp.int32, (N_EXPERTS, ROUTE_TILE), 0)
    pstart = ps_ref[...]
    for j in range(ROUTE_STEP // ROUTE_TILE):
        ls = slice(j * ROUTE_TILE, (j + 1) * ROUTE_TILE)
        e = e_ref[:, ls]
        off = jnp.concatenate(
            [jnp.sum(jnp.where(row == e[k:k + 1], pstart, 0.0), axis=0, keepdims=True)
             for k in range(TOP_K)], axis=0)
        pos_ref[:, ls] = off.astype(jnp.int32) + r_ref[:, ls]


def _positions(eidx, rank, pstarts):
    n = eidx.shape[1]
    tok = lambda i: (0, i)
    return pl.pallas_call(
        _pos_kernel,
        grid=(n // ROUTE_STEP,),
        in_specs=[pl.BlockSpec((TOP_K, ROUTE_STEP), tok),
                  pl.BlockSpec((TOP_K, ROUTE_STEP), tok),
                  pl.BlockSpec((N_EXPERTS, 1), lambda i: (0, 0))],
        out_specs=pl.BlockSpec((TOP_K, ROUTE_STEP), tok),
        out_shape=jax.ShapeDtypeStruct((TOP_K, n), jnp.int32),
        compiler_params=_params(32, ("arbitrary",)),
        name="positions",
    )(eidx, rank, pstarts.reshape(N_EXPERTS, 1).astype(F32))


def _block_tables(counts, n_blocks):
    counts = counts.reshape(N_EXPERTS).astype(jnp.int32)
    padded = (counts + MOE_BLOCK - 1) // MOE_BLOCK * MOE_BLOCK
    pend = jnp.cumsum(padded)
    pstarts = pend - padded
    first_row = jnp.arange(n_blocks, dtype=jnp.int32) * MOE_BLOCK
    block_e = jnp.sum((pend[None, :] <= first_row[:, None]).astype(jnp.int32), axis=1)
    block_e = jnp.minimum(block_e, N_EXPERTS - 1).astype(jnp.int32)
    n_used = (pend[-1] // MOE_BLOCK).astype(jnp.int32).reshape(1)
    ids = jnp.arange(N_EXPERTS, dtype=jnp.int32)
    used = padded > 0
    ordinal = jnp.cumsum(used.astype(jnp.int32)) - 1
    from_here = jnp.flip(lax.cummin(jnp.flip(jnp.where(used, ids, N_EXPERTS))))
    after = jnp.concatenate([from_here[1:], jnp.full((1,), N_EXPERTS, jnp.int32)])
    after = jnp.where(after >= N_EXPERTS, -1, after)
    mine = block_e[:, None] == ids[None, :]
    next_e = jnp.sum(jnp.where(mine, after[None, :], 0), axis=1).astype(jnp.int32)
    slot = jnp.sum(jnp.where(mine, ordinal[None, :] % 2, 0), axis=1).astype(jnp.int32)
    return pstarts, (block_e, n_used, next_e, slot)


SC_WINDOW = 128


def _sc_mesh():
    return plsc.VectorSubcoreMesh(core_axis_name="core", subcore_axis_name="subcore")


def _both_halves(pos, n_rows):
    return jnp.concatenate([pos, pos + n_rows]).reshape(1, -1)


def _sc_dispatch(rows, pos, n_rows):
    _, n, w = rows.shape
    tiles = n // SC_WINDOW
    top_k = pos.shape[0] // n
    idx = jnp.transpose(pos.reshape(top_k, tiles, SC_WINDOW), (1, 0, 2))
    idx = jnp.stack([idx, idx + n_rows]).reshape(1, -1)

    @functools.partial(pl.kernel, out_type=jax.ShapeDtypeStruct((2 * n_rows, w), rows.dtype),
                       mesh=_sc_mesh(), scratch_types=[])
    def scatter_kernel(x_hbm, i_hbm, o_hbm):
        def body(x_vmem, i_vmem):
            pltpu.sync_copy(x_vmem, o_hbm.at[i_vmem.at[0]])

        pltpu.emit_pipeline(
            body,
            grid=(2 * tiles * top_k,),
            in_specs=[pl.BlockSpec((SC_WINDOW, w), lambda i: (i // top_k, 0)),
                      pl.BlockSpec((1, SC_WINDOW), lambda i: (0, i))],
            out_specs=[],
            core_axis_name=("core", "subcore"),
            dimension_semantics=(pltpu.PARALLEL,),
        )(x_hbm, i_hbm)

    out = scatter_kernel(rows.reshape(2 * n, w), idx)
    return out.reshape(2, n_rows, w)


def _sc_gather(table, pos):
    _, n_rows, w = table.shape
    m = pos.shape[0]

    @functools.partial(pl.kernel, out_type=jax.ShapeDtypeStruct((2 * m, w), table.dtype),
                       mesh=_sc_mesh(), scratch_types=[])
    def gather_kernel(t_hbm, i_hbm, o_hbm):
        def body(i_vmem, o_vmem):
            pltpu.sync_copy(t_hbm.at[i_vmem.at[0]], o_vmem)

        pltpu.emit_pipeline(
            body,
            grid=(2 * m // SC_WINDOW,),
            in_specs=[pl.BlockSpec((1, SC_WINDOW), lambda i: (0, i))],
            out_specs=[pl.BlockSpec((SC_WINDOW, w), lambda i: (i, 0))],
            core_axis_name=("core", "subcore"),
            dimension_semantics=(pltpu.PARALLEL,),
        )(i_hbm, o_hbm)

    out = gather_kernel(table.reshape(2 * n_rows, w), _both_halves(pos, n_rows))
    return out.reshape(2, m, w)


def _final_kernel(npt, x1_ref, u2_ref, yg_ref, w_ref, mod_ref, nf_ref, wsg_ref, wsu_ref, wsd_ref,
                  op_ref, os_ref):
    uq = _unpack_rows(u2_ref[0], u2_ref[1])
    a = _dot_quarters(uq, wsg_ref)
    b = _dot_quarters(uq, wsu_ref)
    f = jnp.dot((a * _sigmoid(a) * b).astype(BF16), wsd_ref[...], preferred_element_type=F32)
    w = w_ref[...]
    fq = [f[:, q * PACK_W:(q + 1) * PACK_W] for q in range(4)]
    for k in range(TOP_K):
        yq = _unpack_rows(yg_ref[0, k], yg_ref[1, k], F32)
        fq = [fq[q] + w[:, k:k + 1] * yq[q] for q in range(4)]
    x2 = x1_ref[...] + mod_ref[5:6, :] * jnp.concatenate(fq, axis=1)
    out = _rms(x2) * nf_ref[...]
    i = pl.program_id(0)

    @pl.when(i < npt)
    def _():
        op_ref[...] = out

    @pl.when(i >= npt)
    def _():
        os_ref[...] = out


def _final(seqs, x1, u2, yg, w, mod, norm_final, wsg, wsu, wsd):
    n = seqs.n
    tm = 512
    tok = lambda i: (i, 0)
    const = lambda i: (0, 0)
    npt, p_spec, s_spec = _two_stream_specs(seqs, tm)
    return pl.pallas_call(
        functools.partial(_final_kernel, npt),
        grid=(n // tm,),
        in_specs=[pl.BlockSpec((tm, D_MODEL), tok),
                  pl.BlockSpec((2, tm, PACK_W), lambda i: (0, i, 0)),
                  pl.BlockSpec((2, TOP_K, tm, PACK_W), lambda i: (0, 0, i, 0)),
                  pl.BlockSpec((tm, TOP_K), tok),
                  pl.BlockSpec((None, N_MOD, D_MODEL), lambda i: (seqs.info(i * tm)[0], 0, 0)),
                  pl.BlockSpec((1, D_MODEL), const),
                  pl.BlockSpec((D_MODEL, D_EXPERT), const),
                  pl.BlockSpec((D_MODEL, D_EXPERT), const),
                  pl.BlockSpec((D_EXPERT, D_MODEL), const)],
        out_specs=[p_spec, s_spec],
        out_shape=[jax.ShapeDtypeStruct((seqs.np_, D_MODEL), F32),
                   jax.ShapeDtypeStruct((n - seqs.np_, D_MODEL), F32)],
        compiler_params=_params(48, ("arbitrary",)),
        name="final",
    )(x1, u2, yg, w, mod, norm_final.reshape(1, D_MODEL), wsg, wsu, wsd)


def _layer(seqs, x_p, x_s, c, w_ada, b_ada, norm_mix, w_in, na_rpb, hg_lb, hg_norm, w_branch_a,
           w_branch_b, w_out, norm_ffn, w_router, b_router, w_exp_gate, w_exp_up, w_exp_down,
           w_sh_gate, w_sh_up, w_sh_down, norm_final):
    n = seqs.n
    c_rows = -(-seqs.nseq // 8) * 8
    c_pad = jnp.zeros((c_rows, D_MODEL), F32).at[:seqs.nseq].set(c)
    mod = _ada(c_pad, w_ada[0], b_ada[0])[:seqs.nseq].reshape(seqs.nseq, N_MOD, D_MODEL)
    lb = jnp.cumsum(jax.nn.softmax(hg_lb.astype(F32), axis=0), axis=0)[0]

    proj = _inproj(seqs, x_p, x_s, mod, norm_mix[0], w_in[0].astype(BF16))
    att = _na(seqs, proj, _na_bias_table(na_rpb[0]))
    o_f, o_b = _hgrn(seqs, proj, lb)
    x1, u2, logits = _merge(seqs, x_p, x_s, att, o_f, o_b, proj, mod, hg_norm[0], norm_ffn[0],
                            w_branch_a[0].astype(BF16), w_branch_b[0].astype(BF16),
                            w_out[0].astype(BF16), _split_hi_lo(w_router[0].T))

    eidx, w, rank, counts = _route(logits, b_router[0])
    step_rows = MOE_SUB * MOE_BLOCK
    n_rows = -(-(n * TOP_K + N_EXPERTS * MOE_BLOCK) // step_rows) * step_rows
    pstarts, plan = _block_tables(counts, n_rows // MOE_BLOCK)
    pos = _positions(eidx, rank, pstarts).reshape(-1)
    xs = _sc_dispatch(u2, pos, n_rows)
    ys = _experts(xs, plan, w_exp_gate[0], w_exp_up[0], w_exp_down[0])
    yg = _sc_gather(ys, pos).reshape(2, TOP_K, n, PACK_W)
    return _final(seqs, x1, u2, yg, w.T, mod, norm_final, w_sh_gate[0].astype(BF16),
                  w_sh_up[0].astype(BF16), w_sh_down[0].astype(BF16))


def kernel(x_prompt, x_sample, c_prompt, c_sample, w_ada, b_ada, norm_mix, w_in, na_rpb, hg_lb, hg_norm, w_branch_a, w_branch_b, w_out, norm_ffn, w_router, b_router, w_exp_gate, w_exp_up, w_exp_down, w_sh_gate, w_sh_up, w_sh_down, norm_final):
    bp, tp, _ = x_prompt.shape
    bs, ts, _ = x_sample.shape
    for t in (tp, ts):
        assert t % 1024 == 0, "sequence lengths must be multiples of the 1024-token tiles"
        assert t // GRID_W >= NA_WIN, "a sequence needs at least NA_WIN grid rows"
    seqs = _Seqs(bp, tp, bs, ts)
    c = jnp.concatenate([c_prompt, c_sample])
    y_p, y_s = _layer(seqs, x_prompt.reshape(bp * tp, D_MODEL), x_sample.reshape(bs * ts, D_MODEL),
                      c, w_ada, b_ada, norm_mix, w_in, na_rpb, hg_lb, hg_norm, w_branch_a,
                      w_branch_b, w_out, norm_ffn, w_router, b_router, w_exp_gate, w_exp_up,
                      w_exp_down, w_sh_gate, w_sh_up, w_sh_down, norm_final)
    return (y_p.reshape(bp, tp, D_MODEL), y_s.reshape(bs, ts, D_MODEL))
```

```python
import functools

import jax
import jax.numpy as jnp
import numpy as np
from jax import lax
from jax.experimental import pallas as pl
from jax.experimental.pallas import tpu as pltpu
from jax.experimental.pallas import tpu_sc as plsc

D_MODEL = 1024
GRID_W = 64
NA_HEADS = 8
NA_HEAD_DIM = 64
NA_WIDTH = NA_HEADS * NA_HEAD_DIM
NA_ROWS = 8
NA_COLS = 16
HG_HEADS = 4
HG_KEY_DIM = 128
HG_WIDTH = HG_HEADS * HG_KEY_DIM
HG_CHUNK = 128
HG_EXP_LIMIT = 80.0
N_EXPERTS = 256
TOP_K = 8
N_GROUPS = 8
TOPK_GROUPS = 4
D_EXPERT = 256
ROUTE_SCALE = 2.5
N_MOD = 6
RMS_EPS = 1e-6

MOE_BLOCK = 512
MOE_SUB = 2
MOE_RING = 3
NA_GROUP = 4
NA_WIN = 3 * NA_GROUP
NA_TOK = NA_GROUP * GRID_W
NA_PER_STEP = 2
LOG2E = 1.4426950408889634
NA_Q_SCALE = NA_HEAD_DIM ** -0.5 * LOG2E
HG_STEP = 512
MASK_VALUE = -1e30

F32 = jnp.float32
BF16 = jnp.bfloat16
HIGHEST = lax.Precision.HIGHEST
NT_DIMS = (((1,), (1,)), ((), ()))
TN_DIMS = (((0,), (0,)), ((), ()))

PROJ_TN = 2048
SLAB_Q, SLAB_K, SLAB_V, SLAB_HQ = (0, 0), (0, 1), (0, 2), (0, 3)
SLAB_FF, SLAB_FB, SLAB_HI, SLAB_HG = (1, 0), (1, 1), (1, 2), (1, 3)
SLAB_GA, SLAB_GB = (2, 0), (2, 1)


def _params(vmem_mb, sem=None):
    kw = dict(vmem_limit_bytes=vmem_mb * 1024 * 1024)
    if sem is not None:
        kw["dimension_semantics"] = sem
    return pltpu.CompilerParams(**kw)


class _Seqs:
    def __init__(self, bp, tp, bs, ts):
        self.bp, self.tp, self.bs, self.ts = bp, tp, bs, ts
        self.np_ = bp * tp
        self.n = bp * tp + bs * ts
        self.nseq = bp + bs

    def info(self, t0):
        in_p = t0 < self.np_
        rel = jnp.maximum(t0 - self.np_, 0)
        sid = jnp.where(in_p, t0 // self.tp, self.bp + rel // self.ts)
        start = jnp.where(in_p, (t0 // self.tp) * self.tp, self.np_ + (rel // self.ts) * self.ts)
        length = jnp.where(in_p, self.tp, self.ts)
        return sid, start, length


def _ada_kernel(c_ref, w_ref, b_ref, o_ref):
    c = c_ref[...]
    a = c * jax.nn.sigmoid(c)
    o_ref[...] = jnp.dot(a, w_ref[...], precision=HIGHEST, preferred_element_type=F32) + b_ref[...]


def _ada(c_pad, w_ada, b_ada):
    rows = c_pad.shape[0]
    n_out = w_ada.shape[1]
    tn = 1024
    return pl.pallas_call(
        _ada_kernel,
        grid=(n_out // tn,),
        in_specs=[pl.BlockSpec((rows, D_MODEL), lambda j: (0, 0)),
                  pl.BlockSpec((D_MODEL, tn), lambda j: (0, j)),
                  pl.BlockSpec((1, tn), lambda j: (0, j))],
        out_specs=pl.BlockSpec((rows, tn), lambda j: (0, j)),
        out_shape=jax.ShapeDtypeStruct((rows, n_out), F32),
        compiler_params=_params(32),
        name="ada",
    )(c_pad, w_ada, b_ada.reshape(1, n_out))


def _rms(x):
    return x * lax.rsqrt(jnp.mean(x * x, axis=-1, keepdims=True) + RMS_EPS)


def _sigmoid(x):
    return 0.5 * jnp.tanh(0.5 * x) + 0.5


PACK_W = D_MODEL // 4


def _pack_rows(x):
    out = []
    for h in range(2):
        lo = x[:, (2 * h) * PACK_W:(2 * h + 1) * PACK_W].astype(BF16).astype(F32)
        hi = x[:, (2 * h + 1) * PACK_W:(2 * h + 2) * PACK_W].astype(BF16).astype(F32)
        out.append(lax.bitcast_convert_type(hi, jnp.uint32)
                   | (lax.bitcast_convert_type(lo, jnp.uint32) >> 16))
    return out


def _unpack_rows(p0, p1, dtype=BF16):
    quarters = []
    for p in (p0, p1):
        quarters.append(lax.bitcast_convert_type(p << 16, F32).astype(dtype))
        quarters.append(lax.bitcast_convert_type(p & jnp.uint32(0xFFFF0000), F32).astype(dtype))
    return quarters


def _dot_quarters(quarters, w_ref):
    acc = None
    for q, xq in enumerate(quarters):
        part = jnp.dot(xq, w_ref[q * PACK_W:(q + 1) * PACK_W, :], preferred_element_type=F32)
        acc = part if acc is None else acc + part
    return acc


def _two_stream_specs(seqs, tm, grid_rank=1):
    npt = seqs.np_ // tm
    nst = (seqs.n - seqs.np_) // tm
    if grid_rank == 1:
        p_map = lambda i: (jnp.minimum(i, npt - 1), 0)
        s_map = lambda i: (jnp.clip(i - npt, 0, nst - 1), 0)
    else:
        p_map = lambda i, j: (jnp.minimum(i, npt - 1), 0)
        s_map = lambda i, j: (jnp.clip(i - npt, 0, nst - 1), 0)
    return npt, pl.BlockSpec((tm, D_MODEL), p_map), pl.BlockSpec((tm, D_MODEL), s_map)


def _inproj_kernel(npt, xp_ref, xs_ref, mod_ref, g_ref, w_ref, cs_ref, o_ref, u_scr):
    @pl.when(pl.program_id(1) == 0)
    def _():
        x = jnp.where(pl.program_id(0) < npt, xp_ref[...], xs_ref[...])
        y = _rms(x) * g_ref[...]
        u = y * (1.0 + mod_ref[1:2, :]) + mod_ref[0:1, :]
        u_scr[...] = u.astype(BF16)

    acc = jnp.dot(u_scr[...], w_ref[...], preferred_element_type=F32)
    o_ref[...] = (acc * cs_ref[...]).astype(o_ref.dtype)


def _inproj(seqs, x_p, x_s, mod, norm_mix, w_in_bf):
    n = seqs.n
    tm = min(1024, seqs.tp, seqs.ts)
    tn = PROJ_TN
    n_slab = w_in_bf.shape[1] // tn
    npt, p_spec, s_spec = _two_stream_specs(seqs, tm, grid_rank=2)
    col_scale = jnp.ones((1, w_in_bf.shape[1]), F32).at[:, :NA_WIDTH].set(NA_Q_SCALE)
    return pl.pallas_call(
        functools.partial(_inproj_kernel, npt),
        grid=(n // tm, n_slab),
        in_specs=[p_spec, s_spec,
                  pl.BlockSpec((None, N_MOD, D_MODEL), lambda i, j: (seqs.info(i * tm)[0], 0, 0)),
                  pl.BlockSpec((1, D_MODEL), lambda i, j: (0, 0)),
                  pl.BlockSpec((D_MODEL, tn), lambda i, j: (0, j)),
                  pl.BlockSpec((1, tn), lambda i, j: (0, j))],
        out_specs=pl.BlockSpec((None, tm, tn), lambda i, j: (j, i, 0)),
        out_shape=jax.ShapeDtypeStruct((n_slab, n, tn), BF16),
        scratch_shapes=[pltpu.VMEM((tm, D_MODEL), BF16)],
        compiler_params=_params(56, ("arbitrary", "arbitrary")),
        name="inproj",
    )(x_p, x_s, mod, norm_mix.reshape(1, D_MODEL), w_in_bf, col_scale)


def _na_bias_table(rpb):
    col = np.arange(GRID_W)
    cs = np.clip(col - NA_COLS // 2, 0, GRID_W - NA_COLS)
    valid = (col[None, :] >= cs[:, None]) & (col[None, :] < cs[:, None] + NA_COLS)
    coff = col[None, :] - col[:, None] + NA_COLS - 1
    onehot = (coff[None] == np.arange(2 * NA_COLS - 1)[:, None, None]) & valid[None]
    toep = jnp.einsum("hrc,cqk->hrqk", rpb.astype(F32), jnp.asarray(onehot, F32),
                      precision=HIGHEST)
    toep = jnp.where(valid[None, None], toep * LOG2E, MASK_VALUE)
    masked =jnp.full((NA_HEADS, GRID_W, GRID_W), MASK_VALUE, F32)
    cases = (([0] * NA_GROUP, [NA_ROWS - 1 - i for i in range(NA_GROUP)]),
             (list(range(NA_GROUP)), [NA_ROWS // 2 - 1] * NA_GROUP),
             ([NA_GROUP] * NA_GROUP, [NA_ROWS // 2 - 1 - i for i in range(NA_GROUP)]))
    tabs = []
    for first_row, first_off in cases:
        q_rows = []
        for i in range(NA_GROUP):
            blocks = [toep[:, first_off[i] + w - first_row[i]]
                      if 0 <= w - first_row[i] < NA_ROWS else masked for w in range(NA_WIN)]
            q_rows.append(jnp.concatenate(blocks, axis=2))
        tabs.append(jnp.concatenate(q_rows, axis=1))
    return jnp.stack(tabs)


def _na_geometry(seqs, g):
    _, start, length = seqs.info(g * NA_TOK)
    row0 = start // GRID_W
    rows = length // GRID_W
    r0 = g * NA_GROUP - row0
    wb = jnp.clip(r0 - NA_ROWS // 2, 0, rows - NA_WIN)
    case = jnp.where(r0 == 0, 0, jnp.where(r0 == rows - NA_GROUP, 2, 1))
    return (row0 + wb) // NA_GROUP, case


def _na_kernel(q_ref, *refs):
    o_ref = refs[-1]
    for u in range(NA_PER_STEP):
        k_refs = refs[7 * u:7 * u + 3]
        v_refs = refs[7 * u + 3:7 * u + 6]
        bias_ref = refs[7 * u + 6]
        rows = slice(u * NA_TOK, (u + 1) * NA_TOK)
        outs = []
        for h in range(NA_HEADS):
            hs = slice(h * NA_HEAD_DIM, (h + 1) * NA_HEAD_DIM)
            q = q_ref[rows, hs]
            s = [lax.dot_general(q, kr[:, hs], NT_DIMS, preferred_element_type=F32)
                 + bias_ref[h, :, d * NA_TOK:(d + 1) * NA_TOK] for d, kr in enumerate(k_refs)]
            m = jnp.max(jnp.maximum(jnp.maximum(s[0], s[1]), s[2]), axis=-1, keepdims=True)
            p = [jnp.exp2(sd - m) for sd in s]
            l = jnp.sum((p[0] + p[1]) + p[2], axis=-1, keepdims=True)
            o = sum(jnp.dot(pd.astype(BF16), vr[:, hs], preferred_element_type=F32)
                    for pd, vr in zip(p, v_refs))
            outs.append(o / l)
        o_ref[rows, :] = jnp.concatenate(outs, axis=1).astype(o_ref.dtype)


def _na(seqs, proj, bias_tab):
    n = seqs.n
    step_tok = NA_PER_STEP * NA_TOK

    def group_specs(u):
        geometry = lambda s: _na_geometry(seqs, NA_PER_STEP * s + u)
        kv = [pl.BlockSpec((None, NA_TOK, NA_WIDTH),
                           lambda s, slab=slab, d=d: (slab[0], geometry(s)[0] + d, slab[1]))
              for slab in (SLAB_K, SLAB_V) for d in range(3)]
        bias = pl.BlockSpec((None,) + bias_tab.shape[1:], lambda s: (geometry(s)[1], 0, 0, 0))
        return kv + [bias]

    specs = [pl.BlockSpec((None, step_tok, NA_WIDTH), lambda s: (SLAB_Q[0], s, SLAB_Q[1]))]
    operands = [proj]
    for u in range(NA_PER_STEP):
        specs += group_specs(u)
        operands += [proj] * 6 + [bias_tab]
    return pl.pallas_call(
        _na_kernel,
        grid=(n // step_tok,),
        in_specs=specs,
        out_specs=pl.BlockSpec((step_tok, NA_WIDTH), lambda s: (s, 0)),
        out_shape=jax.ShapeDtypeStruct((n, NA_WIDTH), BF16),
        compiler_params=_params(56, ("arbitrary",)),
        name="natten",
    )(*operands)


def _hg_chunk(q, z, v, lb, tri, mask, mid, last, st_ref):
    sig = _sigmoid(z)
    f = lb + (1.0 - lb) * sig
    lf = jnp.log(f)
    kin = (1.0 - lb) * (1.0 - sig)
    hi = lf.astype(BF16)
    lo = (lf - hi.astype(F32)).astype(BF16)
    g2 = jnp.dot(tri, jnp.concatenate([lo, hi], axis=1), preferred_element_type=F32)
    gcum = g2[:, :HG_WIDTH] + g2[:, HG_WIDTH:]
    gm = gcum[mid:mid + 1, :]
    gl = gcum[last:last + 1, :]
    up = jnp.exp(gcum - gm)
    dn = jnp.exp(gm - gcum)
    qa = (q * up).astype(BF16)
    ka = (kin * dn).astype(BF16)
    qe = (q * (up * jnp.exp(gm))).astype(BF16)
    kd = (kin * (dn * jnp.exp(gl - gm))).astype(BF16)
    eg = jnp.exp(gl)
    vb = v.astype(BF16)
    outs = []
    for h in range(HG_HEADS):
        hs = slice(h * HG_KEY_DIM, (h + 1) * HG_KEY_DIM)
        a = lax.dot_general(qa[:, hs], ka[:, hs], NT_DIMS, preferred_element_type=F32)
        a = jnp.where(mask, a, 0.0)
        st = st_ref[h]
        o = jnp.dot(a.astype(BF16), vb[:, hs], preferred_element_type=F32)
        o = o + lax.dot_general(qe[:, hs], st.astype(BF16), NT_DIMS, preferred_element_type=F32)
        st_ref[h] = st * eg[:, hs] + lax.dot_general(vb[:, hs], kd[:, hs], TN_DIMS,
                                                    preferred_element_type=F32)
        outs.append(o)
    return jnp.concatenate(outs, axis=1)


def _hg_exact(q_ref, z_ref, v_ref, lb, reverse, st_ref, o_ref, qs, fs, ks, vs):
    sig = jax.nn.sigmoid(z_ref[...].astype(F32))
    qs[...] = q_ref[...].astype(F32)
    fs[...] = lb + (1.0 - lb) * sig
    ks[...] = (1.0 - lb) * (1.0 - sig)
    vs[...] = v_ref[...].astype(F32)
    eye = (lax.broadcasted_iota(jnp.int32, (HG_KEY_DIM, HG_KEY_DIM), 0)
           == lax.broadcasted_iota(jnp.int32, (HG_KEY_DIM, HG_KEY_DIM), 1)).astype(F32)

    def body(i, carry):
        t = HG_STEP - 1 - i if reverse else i
        q_t, f_t, k_t, v_t = (r[pl.ds(t, 1), :] for r in (qs, fs, ks, vs))
        outs = []
        for h in range(HG_HEADS):
            hs = slice(h * HG_KEY_DIM, (h + 1) * HG_KEY_DIM)
            v_col = jnp.sum(eye * v_t[:, hs], axis=1, keepdims=True)
            st = st_ref[h] * f_t[:, hs] + v_col * k_t[:, hs]
            st_ref[h] = st
            o_col = jnp.sum(st * q_t[:, hs], axis=1, keepdims=True)
            outs.append(jnp.sum(eye * o_col, axis=0, keepdims=True))
        o_ref[pl.ds(t, 1), :] = jnp.concatenate(outs, axis=1)
        return carry

    lax.fori_loop(0, HG_STEP, body, 0)


def _hg_kernel(seqs, safe_ref, qf_ref, zf_ref, vf_ref, qb_ref, zb_ref, vb_ref, lb_ref, of_ref,
               ob_ref, stf, stb, qs, fs, ks, vs):
    i = pl.program_id(0)
    nsteps = pl.num_programs(0)
    tf = i * HG_STEP
    tb = (nsteps - 1 - i) * HG_STEP
    _, start_f, _ = seqs.info(tf)
    _, start_b, len_b = seqs.info(tb)

    @pl.when(tf == start_f)
    def _():
        stf[...] = jnp.zeros_like(stf)

    @pl.when(tb + HG_STEP == start_b + len_b)
    def _():
        stb[...] = jnp.zeros_like(stb)

    lb = lb_ref[...]

    @pl.when(safe_ref[0] > 0)
    def _():
        row = lax.broadcasted_iota(jnp.int32, (HG_CHUNK, HG_CHUNK), 0)
        col = lax.broadcasted_iota(jnp.int32, (HG_CHUNK, HG_CHUNK), 1)
        lower = row >= col
        upper = col >= row
        tri_f = lower.astype(BF16)
        tri_b = upper.astype(BF16)
        nchunk = HG_STEP // HG_CHUNK
        for c in range(nchunk):
            cs = slice(c * HG_CHUNK, (c + 1) * HG_CHUNK)
            of_ref[cs, :] = _hg_chunk(qf_ref[cs, :].astype(F32), zf_ref[cs, :].astype(F32),
                                      vf_ref[cs, :].astype(F32), lb, tri_f, lower,
                                      HG_CHUNK // 2 - 1, HG_CHUNK - 1, stf)
            cb = nchunk - 1 - c
            bs = slice(cb * HG_CHUNK, (cb + 1) * HG_CHUNK)
            ob_ref[bs, :] = _hg_chunk(qb_ref[bs, :].astype(F32), zb_ref[bs, :].astype(F32),
                                      vb_ref[bs, :].astype(F32), lb, tri_b, upper,
                                      HG_CHUNK // 2, 0, stb)

    @pl.when(safe_ref[0] == 0)
    def _():
        _hg_exact(qf_ref, zf_ref, vf_ref, lb, False, stf, of_ref, qs, fs, ks, vs)
        _hg_exact(qb_ref, zb_ref, vb_ref, lb, True, stb, ob_ref, qs, fs, ks, vs)


def _hgrn(seqs, proj, lb):
    n = seqs.n
    nsteps = n // HG_STEP

    def spec(slab, rev):
        if rev:
            return pl.BlockSpec((None, HG_STEP, HG_WIDTH), lambda i: (slab[0], nsteps - 1 - i, slab[1]))
        return pl.BlockSpec((None, HG_STEP, HG_WIDTH), lambda i: (slab[0], i, slab[1]))

    safe = (jnp.max(-jnp.log(lb)) * (HG_CHUNK // 2) < HG_EXP_LIMIT).astype(jnp.int32).reshape(1)
    step_scratch = pltpu.VMEM((HG_STEP, HG_WIDTH), F32)
    return pl.pallas_call(
        functools.partial(_hg_kernel, seqs),
        grid=(nsteps,),
        in_specs=[pl.BlockSpec(memory_space=pltpu.SMEM),
                  spec(SLAB_HQ, False), spec(SLAB_FF, False), spec(SLAB_HI, False),
                  spec(SLAB_HQ, True), spec(SLAB_FB, True), spec(SLAB_HI, True),
                  pl.BlockSpec((1, HG_WIDTH), lambda i: (0, 0))],
        out_specs=[pl.BlockSpec((HG_STEP, HG_WIDTH), lambda i: (i, 0)),
                   pl.BlockSpec((HG_STEP, HG_WIDTH), lambda i: (nsteps - 1 - i, 0))],
        out_shape=[jax.ShapeDtypeStruct((n, HG_WIDTH), F32)] * 2,
        scratch_shapes=[pltpu.VMEM((HG_HEADS, HG_KEY_DIM, HG_KEY_DIM), F32)] * 2
        + [step_scratch] * 4,
        compiler_params=_params(32, ("arbitrary",)),
        name="hgrn2",
    )(safe, proj, proj, proj, proj, proj, proj, lb.reshape(1, HG_WIDTH))


def _merge_kernel(npt, xp_ref, xs_ref, att_ref, of_ref, ob_ref, hg_ref, ga_ref, gb_ref, mod_ref,
                  hgn_ref, nffn_ref, wa_ref, wb_ref, wo_ref, wr_ref, x1_ref, u2_ref, lg_ref):
    x = jnp.where(pl.program_id(0) < npt, xp_ref[...], xs_ref[...])
    o = of_ref[...] + ob_ref[...]
    parts = []
    for h in range(HG_HEADS):
        hs = slice(h * HG_KEY_DIM, (h + 1) * HG_KEY_DIM)
        parts.append(_rms(o[:, hs]))
    on = jnp.concatenate(parts, axis=1) * hgn_ref[...]
    gate = hg_ref[...].astype(F32)
    hb = (on * (gate * _sigmoid(gate))).astype(BF16)
    ya = jnp.dot(att_ref[...], wa_ref[...], preferred_element_type=F32)
    yb = jnp.dot(hb, wb_ref[...], preferred_element_type=F32)
    merged = (_sigmoid(ga_ref[...].astype(F32)) * ya
              + _sigmoid(gb_ref[...].astype(F32)) * yb)
    x1 = x + mod_ref[2:3, :] * jnp.dot(merged.astype(BF16), wo_ref[...],
                                       preferred_element_type=F32)
    x1_ref[...] = x1
    u2 = _rms(x1) * nffn_ref[...] * (1.0 + mod_ref[4:5, :]) + mod_ref[3:4, :]
    u2_ref[0], u2_ref[1] = _pack_rows(u2)
    w_hi = wr_ref[0]
    u_hi = u2.astype(BF16)
    u_lo = (u2 - u_hi.astype(F32)).astype(BF16)
    lg_ref[...] = (lax.dot_general(w_hi, u_hi, NT_DIMS, preferred_element_type=F32)
                   + (lax.dot_general(w_hi, u_lo, NT_DIMS, preferred_element_type=F32)
                      + lax.dot_general(wr_ref[1], u_hi, NT_DIMS, preferred_element_type=F32)))


def _split_hi_lo(w):
    hi = w.astype(BF16)
    return jnp.stack([hi, (w - hi.astype(F32)).astype(BF16)])


def _merge(seqs, x_p, x_s, att, o_f, o_b, proj, mod, hg_norm, norm_ffn, wa, wb, wo, wr):
    n = seqs.n
    tm = 512
    tok = lambda i: (i, 0)
    const = lambda i: (0, 0)
    npt, p_spec, s_spec = _two_stream_specs(seqs, tm)
    return pl.pallas_call(
        functools.partial(_merge_kernel, npt),
        grid=(n // tm,),
        in_specs=[p_spec, s_spec,
                  pl.BlockSpec((tm, NA_WIDTH), tok),
                  pl.BlockSpec((tm, HG_WIDTH), tok),
                  pl.BlockSpec((tm, HG_WIDTH), tok),
                  pl.BlockSpec((None, tm, HG_WIDTH), lambda i: (SLAB_HG[0], i, SLAB_HG[1])),
                  pl.BlockSpec((None, tm, D_MODEL), lambda i: (SLAB_GA[0], i, SLAB_GA[1])),
                  pl.BlockSpec((None, tm, D_MODEL), lambda i: (SLAB_GB[0], i, SLAB_GB[1])),
                  pl.BlockSpec((None, N_MOD, D_MODEL), lambda i: (seqs.info(i * tm)[0], 0, 0)),
                  pl.BlockSpec((1, HG_WIDTH), const),
                  pl.BlockSpec((1, D_MODEL), const),
                  pl.BlockSpec((NA_WIDTH, D_MODEL), const),
                  pl.BlockSpec((HG_WIDTH, D_MODEL), const),
                  pl.BlockSpec((D_MODEL, D_MODEL), const),
                  pl.BlockSpec((2, N_EXPERTS, D_MODEL), lambda i: (0, 0, 0))],
        out_specs=[pl.BlockSpec((tm, D_MODEL), tok),
                   pl.BlockSpec((2, tm, PACK_W), lambda i: (0, i, 0)),
                   pl.BlockSpec((N_EXPERTS, tm), lambda i: (0, i))],
        out_shape=[jax.ShapeDtypeStruct((n, D_MODEL), F32),
                   jax.ShapeDtypeStruct((2, n, PACK_W), jnp.uint32),
                   jax.ShapeDtypeStruct((N_EXPERTS, n), F32)],
        compiler_params=_params(56, ("arbitrary",)),
        name="merge",
    )(x_p, x_s, att, o_f, o_b, proj, proj, proj, mod, hg_norm.reshape(1, HG_WIDTH),
      norm_ffn.reshape(1, D_MODEL), wa, wb, wo, wr)


def _expert_kernel(be_ref, nused_ref, next_ref, slot_ref, xs_hbm, wg_hbm, wu_hbm, wd_hbm, ys_ref,
                   wg_s, wu_s, wd_s, wg_st, wu_st, wd_st, sems, xring, xsems):
    step = pl.program_id(0)
    step_rows = MOE_SUB * MOE_BLOCK
    n_steps = (nused_ref[0] + MOE_SUB - 1) // MOE_SUB
    ring = step % MOE_RING

    def rows_copy(s):
        src = xs_hbm.at[:, pl.ds(pl.multiple_of(s * step_rows, step_rows), step_rows), :]
        return pltpu.make_async_copy(src, xring.at[s % MOE_RING], xsems.at[s % MOE_RING])

    @pl.when(step < n_steps)
    def _():
        @pl.when(step == 0)
        def _():
            rows_copy(0).start()

            @pl.when(n_steps > 1)
            def _():
                rows_copy(1).start()

        rows_copy(step).wait()

        @pl.when(step + 2 < n_steps)
        def _():
            rows_copy(step + 2).start()

    def weight_copies(e, slot):
        pairs = ((wg_hbm, wg_st), (wu_hbm, wu_st), (wd_hbm, wd_st))
        return [pltpu.make_async_copy(hbm.at[e], stage.at[slot], sems.at[slot, j])
                for j, (hbm, stage) in enumerate(pairs)]

    def block(i, rows):
        e = be_ref[i]
        slot = slot_ref[i]
        nxt = next_ref[i]
        last = nused_ref[0] - 1
        first = (i == 0) | (e != be_ref[jnp.maximum(i - 1, 0)])
        only = (i == last) | (be_ref[jnp.minimum(i + 1, last)] != e)
        second = (i >= 1) & jnp.logical_not(first) & (
            (i == 1) | (be_ref[jnp.maximum(i - 2, 0)] != e))

        @pl.when(first)
        def _():
            @pl.when(i == 0)
            def _():
                for c in weight_copies(e, slot):
                    c.start()

            for c in weight_copies(e, slot):
                c.wait()

            @pl.when(nxt >= 0)
            def _():
                ahead = weight_copies(nxt, 1 - slot)
                ahead[0].start()
                ahead[1].start()

                @pl.when(only)
                def _():
                    ahead[2].start()

            wg_s[...] = wg_st[slot].astype(BF16)
            wu_s[...] = wu_st[slot].astype(BF16)
            wd_s[...] = wd_st[slot].astype(BF16)

        @pl.when(second & (nxt >= 0))
        def _():
            weight_copies(nxt, 1 - slot)[2].start()

        xq = _unpack_rows(xring[ring, 0, rows, :], xring[ring, 1, rows, :])
        a = _dot_quarters(xq, wg_s)
        b = _dot_quarters(xq, wu_s)
        h = (a * _sigmoid(a) * b).astype(BF16)
        ys_ref[0, rows, :], ys_ref[1, rows, :] = _pack_rows(
            jnp.dot(h, wd_s[...], preferred_element_type=F32))

    for j in range(MOE_SUB):
        i = pl.program_id(0) * MOE_SUB + j
        pl.when(i < nused_ref[0])(functools.partial(block, i, slice(j * MOE_BLOCK, (j + 1) * MOE_BLOCK)))


def _experts(xs, plan, wg, wu, wd):
    n_rows = xs.shape[1]
    step_rows = MOE_SUB * MOE_BLOCK
    rows_map = lambda i, be, nu, nx, sl: (0, jnp.minimum(i, (nu[0] - 1) // MOE_SUB), 0)
    whole = pl.BlockSpec(memory_space=pl.ANY)
    grid_spec = pltpu.PrefetchScalarGridSpec(
        num_scalar_prefetch=4,
        grid=(n_rows // step_rows,),
        in_specs=[whole, whole, whole, whole],
        out_specs=pl.BlockSpec((2, step_rows, PACK_W), rows_map),
        scratch_shapes=[pltpu.VMEM((D_MODEL, D_EXPERT), BF16),
                        pltpu.VMEM((D_MODEL, D_EXPERT), BF16),
                        pltpu.VMEM((D_EXPERT, D_MODEL), BF16),
                        pltpu.VMEM((2, D_MODEL, D_EXPERT), F32),
                        pltpu.VMEM((2, D_MODEL, D_EXPERT), F32),
                        pltpu.VMEM((2, D_EXPERT, D_MODEL), F32),
                        pltpu.SemaphoreType.DMA((2, 3)),
                        pltpu.VMEM((MOE_RING, 2, step_rows, PACK_W), jnp.uint32),
                        pltpu.SemaphoreType.DMA((MOE_RING,))],
    )
    return pl.pallas_call(
        _expert_kernel,
        grid_spec=grid_spec,
        out_shape=jax.ShapeDtypeStruct((2, n_rows, PACK_W), jnp.uint32),
        compiler_params=_params(56, ("arbitrary",)),
        name="experts",
    )(*plan, xs, wg, wu, wd)


ROUTE_TILE = 128
ROUTE_STEP = 1024
NEG_INF = float("-inf")


def _first_max(x, idx, big):
    m = jnp.max(x, axis=0, keepdims=True)
    first = jnp.min(jnp.where(x == m, idx, big), axis=0, keepdims=True)
    return m, first


def _route_tile(lg, bias, tri, base):
    per_group = N_EXPERTS // N_GROUPS
    scores = jax.nn.sigmoid(lg)
    sel = scores + bias
    lrow = lax.broadcasted_iota(jnp.int32, (per_group, ROUTE_TILE), 0)
    gs = []
    for g in range(N_GROUPS):
        x = sel[g * per_group:(g + 1) * per_group]
        m1, first = _first_max(x, lrow, per_group)
        m2 = jnp.max(jnp.where(lrow == first, NEG_INF, x), axis=0, keepdims=True)
        gs.append(m1 + m2)
    cur = jnp.concatenate(gs, axis=0)
    grow = lax.broadcasted_iota(jnp.int32, (N_GROUPS, ROUTE_TILE), 0)
    chosen = jnp.zeros((N_GROUPS, ROUTE_TILE), jnp.int32)
    for _ in range(TOPK_GROUPS):
        _, first = _first_max(cur, grow, N_GROUPS)
        hit = grow == first
        chosen = jnp.where(hit, 1, chosen)
        cur = jnp.where(hit, NEG_INF, cur)
    cur = jnp.concatenate(
        [jnp.where(chosen[g:g + 1] > 0, sel[g * per_group:(g + 1) * per_group], NEG_INF)
         for g in range(N_GROUPS)], axis=0)
    row = lax.broadcasted_iota(jnp.int32, (N_EXPERTS, ROUTE_TILE), 0)
    member = jnp.zeros((N_EXPERTS, ROUTE_TILE), F32)
    es, ws = [], []
    for _ in range(TOP_K):
        _, first = _first_max(cur, row, N_EXPERTS)
        hit = row == first
        es.append(first)
        ws.append(jnp.sum(jnp.where(hit, scores, 0.0), axis=0, keepdims=True))
        cur = jnp.where(hit, NEG_INF, cur)
        member = jnp.where(hit, 1.0, member)
    e = jnp.concatenate(es, axis=0)
    w = jnp.concatenate(ws, axis=0)
    w = w / jnp.sum(w, axis=0, keepdims=True) * ROUTE_SCALE
    before = jnp.dot(member.astype(BF16), tri, preferred_element_type=F32) + base
    rank = jnp.concatenate(
        [jnp.sum(jnp.where(row == es[k], before, 0.0), axis=0, keepdims=True) for k in range(TOP_K)],
        axis=0)
    return e, w, rank.astype(jnp.int32), member


def _route_kernel(lg_ref, b_ref, e_ref, w_ref, r_ref, cnt_ref, base):
    @pl.when(pl.program_id(0) == 0)
    def _():
        base[...] = jnp.zeros_like(base)

    r_i = lax.broadcasted_iota(jnp.int32, (ROUTE_TILE, ROUTE_TILE), 0)
    c_i = lax.broadcasted_iota(jnp.int32, (ROUTE_TILE, ROUTE_TILE), 1)
    tri = (r_i < c_i).astype(BF16)
    bias = b_ref[...]
    for j in range(ROUTE_STEP // ROUTE_TILE):
        ls = slice(j * ROUTE_TILE, (j + 1) * ROUTE_TILE)
        e, w, rank, member = _route_tile(lg_ref[:, ls], bias, tri, base[...])
        e_ref[:, ls] = e
        w_ref[:, ls] = w
        r_ref[:, ls] = rank
        base[...] = base[...] + jnp.sum(member, axis=1, keepdims=True)
    cnt_ref[...] = base[...]


def _route(logits_t, b_router):
    n = logits_t.shape[1]
    tok = lambda i: (0, i)
    return pl.pallas_call(
        _route_kernel,
        grid=(n // ROUTE_STEP,),
        in_specs=[pl.BlockSpec((N_EXPERTS, ROUTE_STEP), tok),
                  pl.BlockSpec((N_EXPERTS, 1), lambda i: (0, 0))],
        out_specs=[pl.BlockSpec((TOP_K, ROUTE_STEP), tok),
                   pl.BlockSpec((TOP_K, ROUTE_STEP), tok),
                   pl.BlockSpec((TOP_K, ROUTE_STEP), tok),
                   pl.BlockSpec((N_EXPERTS, 1), lambda i: (0, 0))],
        out_shape=[jax.ShapeDtypeStruct((TOP_K, n), jnp.int32),
                   jax.ShapeDtypeStruct((TOP_K, n), F32),
                   jax.ShapeDtypeStruct((TOP_K, n), jnp.int32),
                   jax.ShapeDtypeStruct((N_EXPERTS, 1), F32)],
        scratch_shapes=[pltpu.VMEM((N_EXPERTS, 1), F32)],
        compiler_params=_params(32, ("arbitrary",)),
        name="route",
    )(logits_t, b_router.reshape(N_EXPERTS, 1).astype(F32))


def _pos_kernel(e_ref, r_ref, ps_ref, pos_ref):
    row = lax.broadcasted_iota(jnp.int32, (N_EXPERTS, ROUTE_TILE), 0)
    pstart = ps_ref[...]
    for j in range(ROUTE_STEP // ROUTE_TILE):
        ls = slice(j * ROUTE_TILE, (j + 1) * ROUTE_TILE)
        e = e_ref[:, ls]
        off = jnp.concatenate(
            [jnp.sum(jnp.where(row == e[k:k + 1], pstart, 0.0), axis=0, keepdims=True)
             for k in range(TOP_K)], axis=0)
        pos_ref[:, ls] = off.astype(jnp.int32) + r_ref[:, ls]


def _positions(eidx, rank, pstarts):
    n = eidx.shape[1]
    tok = lambda i: (0, i)
    return pl.pallas_call(
        _pos_kernel,
        grid=(n // ROUTE_STEP,),
        in_specs=[pl.BlockSpec((TOP_K, ROUTE_STEP), tok),
                  pl.BlockSpec((TOP_K, ROUTE_STEP), tok),
                  pl.BlockSpec((N_EXPERTS, 1), lambda i: (0, 0))],
        out_specs=pl.BlockSpec((TOP_K, ROUTE_STEP), tok),
        out_shape=jax.ShapeDtypeStruct((TOP_K, n), jnp.int32),
        compiler_params=_params(32, ("arbitrary",)),
        name="positions",
    )(eidx, rank, pstarts.reshape(N_EXPERTS, 1).astype(F32))


def _block_tables(counts, n_blocks):
    counts = counts.reshape(N_EXPERTS).astype(jnp.int32)
    padded = (counts + MOE_BLOCK - 1) // MOE_BLOCK * MOE_BLOCK
    pend = jnp.cumsum(padded)
    pstarts = pend - padded
    first_row = jnp.arange(n_blocks, dtype=jnp.int32) * MOE_BLOCK
    block_e = jnp.sum((pend[None, :] <= first_row[:, None]).astype(jnp.int32), axis=1)
    block_e = jnp.minimum(block_e, N_EXPERTS - 1).astype(jnp.int32)
    n_used = (pend[-1] // MOE_BLOCK).astype(jnp.int32).reshape(1)
    ids = jnp.arange(N_EXPERTS, dtype=jnp.int32)
    used = padded > 0
    ordinal = jnp.cumsum(used.astype(jnp.int32)) - 1
    from_here = jnp.flip(lax.cummin(jnp.flip(jnp.where(used, ids, N_EXPERTS))))
    after = jnp.concatenate([from_here[1:], jnp.full((1,), N_EXPERTS, jnp.int32)])
    after = jnp.where(after >= N_EXPERTS, -1, after)
    mine = block_e[:, None] == ids[None, :]
    next_e = jnp.sum(jnp.where(mine, after[None, :], 0), axis=1).astype(jnp.int32)
    slot = jnp.sum(jnp.where(mine, ordinal[None, :] % 2, 0), axis=1).astype(jnp.int32)
    return pstarts, (block_e, n_used, next_e, slot)


SC_WINDOW = 128


def _sc_mesh():
    return plsc.VectorSubcoreMesh(core_axis_name="core", subcore_axis_name="subcore")


def _both_halves(pos, n_rows):
    return jnp.concatenate([pos, pos + n_rows]).reshape(1, -1)


def _sc_dispatch(rows, pos, n_rows):
    _, n, w = rows.shape
    tiles = n // SC_WINDOW
    top_k = pos.shape[0] // n
    idx = jnp.transpose(pos.reshape(top_k, tiles, SC_WINDOW), (1, 0, 2))
    idx = jnp.stack([idx, idx + n_rows]).reshape(1, -1)

    @functools.partial(pl.kernel, out_type=jax.ShapeDtypeStruct((2 * n_rows, w), rows.dtype),
                       mesh=_sc_mesh(), scratch_types=[])
    def scatter_kernel(x_hbm, i_hbm, o_hbm):
        def body(x_vmem, i_vmem):
            pltpu.sync_copy(x_vmem, o_hbm.at[i_vmem.at[0]])

        pltpu.emit_pipeline(
            body,
            grid=(2 * tiles * top_k,),
            in_specs=[pl.BlockSpec((SC_WINDOW, w), lambda i: (i // top_k, 0)),
                      pl.BlockSpec((1, SC_WINDOW), lambda i: (0, i))],
            out_specs=[],
            core_axis_name=("core", "subcore"),
            dimension_semantics=(pltpu.PARALLEL,),
        )(x_hbm, i_hbm)

    out = scatter_kernel(rows.reshape(2 * n, w), idx)
    return out.reshape(2, n_rows, w)


def _sc_gather(table, pos):
    _, n_rows, w = table.shape
    m = pos.shape[0]

    @functools.partial(pl.kernel, out_type=jax.ShapeDtypeStruct((2 * m, w), table.dtype),
                       mesh=_sc_mesh(), scratch_types=[])
    def gather_kernel(t_hbm, i_hbm, o_hbm):
        def body(i_vmem, o_vmem):
            pltpu.sync_copy(t_hbm.at[i_vmem.at[0]], o_vmem)

        pltpu.emit_pipeline(
            body,
            grid=(2 * m // SC_WINDOW,),
            in_specs=[pl.BlockSpec((1, SC_WINDOW), lambda i: (0, i))],
            out_specs=[pl.BlockSpec((SC_WINDOW, w), lambda i: (i, 0))],
            core_axis_name=("core", "subcore"),
            dimension_semantics=(pltpu.PARALLEL,),
        )(i_hbm, o_hbm)

    out = gather_kernel(table.reshape(2 * n_rows, w), _both_halves(pos, n_rows))
    return out.reshape(2, m, w)


def _final_kernel(npt, x1_ref, u2_ref, yg_ref, w_ref, mod_ref, nf_ref, wsg_ref, wsu_ref, wsd_ref,
                  op_ref, os_ref):
    uq = _unpack_rows(u2_ref[0], u2_ref[1])
    a = _dot_quarters(uq, wsg_ref)
    b = _dot_quarters(uq, wsu_ref)
    f = jnp.dot((a * _sigmoid(a) * b).astype(BF16), wsd_ref[...], preferred_element_type=F32)
    w = w_ref[...]
    fq = [f[:, q * PACK_W:(q + 1) * PACK_W] for q in range(4)]
    for k in range(TOP_K):
        yq = _unpack_rows(yg_ref[0, k], yg_ref[1, k], F32)
        fq = [fq[q] + w[:, k:k + 1] * yq[q] for q in range(4)]
    x2 = x1_ref[...] + mod_ref[5:6, :] * jnp.concatenate(fq, axis=1)
    out = _rms(x2) * nf_ref[...]
    i = pl.program_id(0)

    @pl.when(i < npt)
    def _():
        op_ref[...] = out

    @pl.when(i >= npt)
    def _():
        os_ref[...] = out


def _final(seqs, x1, u2, yg, w, mod, norm_final, wsg, wsu, wsd):
    n = seqs.n
    tm = 512
    tok = lambda i: (i, 0)
    const = lambda i: (0, 0)
    npt, p_spec, s_spec = _two_stream_specs(seqs, tm)
    return pl.pallas_call(
        functools.partial(_final_kernel, npt),
        grid=(n // tm,),
        in_specs=[pl.BlockSpec((tm, D_MODEL), tok),
                  pl.BlockSpec((2, tm, PACK_W), lambda i: (0, i, 0)),
                  pl.BlockSpec((2, TOP_K, tm, PACK_W), lambda i: (0, 0, i, 0)),
                  pl.BlockSpec((tm, TOP_K), tok),
                  pl.BlockSpec((None, N_MOD, D_MODEL), lambda i: (seqs.info(i * tm)[0], 0, 0)),
                  pl.BlockSpec((1, D_MODEL), const),
                  pl.BlockSpec((D_MODEL, D_EXPERT), const),
                  pl.BlockSpec((D_MODEL, D_EXPERT), const),
                  pl.BlockSpec((D_EXPERT, D_MODEL), const)],
        out_specs=[p_spec, s_spec],
        out_shape=[jax.ShapeDtypeStruct((seqs.np_, D_MODEL), F32),
                   jax.ShapeDtypeStruct((n - seqs.np_, D_MODEL), F32)],
        compiler_params=_params(48, ("arbitrary",)),
        name="final",
    )(x1, u2, yg, w, mod, norm_final.reshape(1, D_MODEL), wsg, wsu, wsd)


def _layer(seqs, x_p, x_s, c, w_ada, b_ada, norm_mix, w_in, na_rpb, hg_lb, hg_norm, w_branch_a,
           w_branch_b, w_out, norm_ffn, w_router, b_router, w_exp_gate, w_exp_up, w_exp_down,
           w_sh_gate, w_sh_up, w_sh_down, norm_final):
    n = seqs.n
    c_rows = -(-seqs.nseq // 8) * 8
    c_pad = jnp.zeros((c_rows, D_MODEL), F32).at[:seqs.nseq].set(c)
    mod = _ada(c_pad, w_ada[0], b_ada[0])[:seqs.nseq].reshape(seqs.nseq, N_MOD, D_MODEL)
    lb = jnp.cumsum(jax.nn.softmax(hg_lb.astype(F32), axis=0), axis=0)[0]

    proj = _inproj(seqs, x_p, x_s, mod, norm_mix[0], w_in[0].astype(BF16))
    att = _na(seqs, proj, _na_bias_table(na_rpb[0]))
    o_f, o_b = _hgrn(seqs, proj, lb)
    x1, u2, logits = _merge(seqs, x_p, x_s, att, o_f, o_b, proj, mod, hg_norm[0], norm_ffn[0],
                            w_branch_a[0].astype(BF16), w_branch_b[0].astype(BF16),
                            w_out[0].astype(BF16), _split_hi_lo(w_router[0].T))

    eidx, w, rank, counts = _route(logits, b_router[0])
    step_rows = MOE_SUB * MOE_BLOCK
    n_rows = -(-(n * TOP_K + N_EXPERTS * MOE_BLOCK) // step_rows) * step_rows
    pstarts, plan = _block_tables(counts, n_rows // MOE_BLOCK)
    pos = _positions(eidx, rank, pstarts).reshape(-1)
    xs = _sc_dispatch(u2, pos, n_rows)
    ys = _experts(xs, plan, w_exp_gate[0], w_exp_up[0], w_exp_down[0])
    yg = _sc_gather(ys, pos).reshape(2, TOP_K, n, PACK_W)
    return _final(seqs, x1, u2, yg, w.T, mod, norm_final, w_sh_gate[0].astype(BF16),
                  w_sh_up[0].astype(BF16), w_sh_down[0].astype(BF16))


def kernel(x_prompt, x_sample, c_prompt, c_sample, w_ada, b_ada, norm_mix, w_in, na_rpb, hg_lb, hg_norm, w_branch_a, w_branch_b, w_out, norm_ffn, w_router, b_router, w_exp_gate, w_exp_up, w_exp_down, w_sh_gate, w_sh_up, w_sh_down, norm_final):
    bp, tp, _ = x_prompt.shape
    bs, ts, _ = x_sample.shape
    for t in (tp, ts):
        assert t % 1024 == 0, "sequence lengths must be multiples of the 1024-token tiles"
        assert t // GRID_W >= NA_WIN, "a sequence needs at least NA_WIN grid rows"
    seqs = _Seqs(bp, tp, bs, ts)
    c = jnp.concatenate([c_prompt, c_sample])
    y_p, y_s = _layer(seqs, x_prompt.reshape(bp * tp, D_MODEL), x_sample.reshape(bs * ts, D_MODEL),
                      c, w_ada, b_ada, norm_mix, w_in, na_rpb, hg_lb, hg_norm, w_branch_a,
                      w_branch_b, w_out, norm_ffn, w_router, b_router, w_exp_gate, w_exp_up,
                      w_exp_down, w_sh_gate, w_sh_up, w_sh_down, norm_final)
    return (y_p.reshape(bp, tp, D_MODEL), y_s.reshape(bs, ts, D_MODEL))
```

```python
import functools

import jax
import jax.numpy as jnp
import numpy as np
from jax import lax
from jax.experimental import pallas as pl
from jax.experimental.pallas import tpu as pltpu
from jax.experimental.pallas import tpu_sc as plsc

D_MODEL = 1024
GRID_W = 64
NA_HEADS = 8
NA_HEAD_DIM = 64
NA_WIDTH = NA_HEADS * NA_HEAD_DIM
NA_ROWS = 8
NA_COLS = 16
HG_HEADS = 4
HG_KEY_DIM = 128
HG_WIDTH = HG_HEADS * HG_KEY_DIM
HG_CHUNK = 128
HG_EXP_LIMIT = 80.0
N_EXPERTS = 256
TOP_K = 8
N_GROUPS = 8
TOPK_GROUPS = 4
D_EXPERT = 256
ROUTE_SCALE = 2.5
N_MOD = 6
RMS_EPS = 1e-6

MOE_BLOCK = 512
MOE_SUB = 4
MOE_RING = 3
NA_GROUP = 4
NA_WIN = 3 * NA_GROUP
NA_TOK = NA_GROUP * GRID_W
NA_PER_STEP = 2
LOG2E = 1.4426950408889634
NA_Q_SCALE = NA_HEAD_DIM ** -0.5 * LOG2E
HG_STEP = 512
MASK_VALUE = -1e30

F32 = jnp.float32
BF16 = jnp.bfloat16
HIGHEST = lax.Precision.HIGHEST
NT_DIMS = (((1,), (1,)), ((), ()))
TN_DIMS = (((0,), (0,)), ((), ()))

PROJ_TN = 2048
SLAB_Q, SLAB_K, SLAB_V, SLAB_HQ = (0, 0), (0, 1), (0, 2), (0, 3)
SLAB_FF, SLAB_FB, SLAB_HI, SLAB_HG = (1, 0), (1, 1), (1, 2), (1, 3)
SLAB_GA, SLAB_GB = (2, 0), (2, 1)


def _params(vmem_mb, sem=None):
    kw = dict(vmem_limit_bytes=vmem_mb * 1024 * 1024)
    if sem is not None:
        kw["dimension_semantics"] = sem
    return pltpu.CompilerParams(**kw)


class _Seqs:
    def __init__(self, bp, tp, bs, ts):
        self.bp, self.tp, self.bs, self.ts = bp, tp, bs, ts
        self.np_ = bp * tp
        self.n = bp * tp + bs * ts
        self.nseq = bp + bs

    def info(self, t0):
        in_p = t0 < self.np_
        rel = jnp.maximum(t0 - self.np_, 0)
        sid = jnp.where(in_p, t0 // self.tp, self.bp + rel // self.ts)
        start = jnp.where(in_p, (t0 // self.tp) * self.tp, self.np_ + (rel // self.ts) * self.ts)
        length = jnp.where(in_p, self.tp, self.ts)
        return sid, start, length


def _ada_kernel(c_ref, w_ref, b_ref, o_ref):
    c = c_ref[...]
    a = c * jax.nn.sigmoid(c)
    o_ref[...] = jnp.dot(a, w_ref[...], precision=HIGHEST, preferred_element_type=F32) + b_ref[...]


def _ada(c_pad, w_ada, b_ada):
    rows = c_pad.shape[0]
    n_out = w_ada.shape[1]
    tn = 1024
    return pl.pallas_call(
        _ada_kernel,
        grid=(n_out // tn,),
        in_specs=[pl.BlockSpec((rows, D_MODEL), lambda j: (0, 0)),
                  pl.BlockSpec((D_MODEL, tn), lambda j: (0, j)),
                  pl.BlockSpec((1, tn), lambda j: (0, j))],
        out_specs=pl.BlockSpec((rows, tn), lambda j: (0, j)),
        out_shape=jax.ShapeDtypeStruct((rows, n_out), F32),
        compiler_params=_params(32),
        name="ada",
    )(c_pad, w_ada, b_ada.reshape(1, n_out))


def _rms(x):
    return x * lax.rsqrt(jnp.mean(x * x, axis=-1, keepdims=True) + RMS_EPS)


def _sigmoid(x):
    return 0.5 * jnp.tanh(0.5 * x) + 0.5


PACK_W = D_MODEL // 4


def _pack_rows(x):
    out = []
    for h in range(2):
        lo = x[:, (2 * h) * PACK_W:(2 * h + 1) * PACK_W].astype(BF16).astype(F32)
        hi = x[:, (2 * h + 1) * PACK_W:(2 * h + 2) * PACK_W].astype(BF16).astype(F32)
        out.append(lax.bitcast_convert_type(hi, jnp.uint32)
                   | (lax.bitcast_convert_type(lo, jnp.uint32) >> 16))
    return out


def _unpack_rows(p0, p1, dtype=BF16):
    quarters = []
    for p in (p0, p1):
        quarters.append(lax.bitcast_convert_type(p << 16, F32).astype(dtype))
        quarters.append(lax.bitcast_convert_type(p & jnp.uint32(0xFFFF0000), F32).astype(dtype))
    return quarters


def _dot_quarters(quarters, w_ref):
    acc = None
    for q, xq in enumerate(quarters):
        part = jnp.dot(xq, w_ref[q * PACK_W:(q + 1) * PACK_W, :], preferred_element_type=F32)
        acc = part if acc is None else acc + part
    return acc


def _two_stream_specs(seqs, tm, grid_rank=1):
    npt = seqs.np_ // tm
    nst = (seqs.n - seqs.np_) // tm
    if grid_rank == 1:
        p_map = lambda i: (jnp.minimum(i, npt - 1), 0)
        s_map = lambda i: (jnp.clip(i - npt, 0, nst - 1), 0)
    else:
        p_map = lambda i, j: (jnp.minimum(i, npt - 1), 0)
        s_map = lambda i, j: (jnp.clip(i - npt, 0, nst - 1), 0)
    return npt, pl.BlockSpec((tm, D_MODEL), p_map), pl.BlockSpec((tm, D_MODEL), s_map)


def _inproj_kernel(npt, xp_ref, xs_ref, mod_ref, g_ref, w_ref, cs_ref, o_ref, u_scr):
    @pl.when(pl.program_id(1) == 0)
    def _():
        x = jnp.where(pl.program_id(0) < npt, xp_ref[...], xs_ref[...])
        y = _rms(x) * g_ref[...]
        u = y * (1.0 + mod_ref[1:2, :]) + mod_ref[0:1, :]
        u_scr[...] = u.astype(BF16)

    acc = jnp.dot(u_scr[...], w_ref[...], preferred_element_type=F32)
    o_ref[...] = (acc * cs_ref[...]).astype(o_ref.dtype)


def _inproj(seqs, x_p, x_s, mod, norm_mix, w_in_bf):
    n = seqs.n
    tm = min(1024, seqs.tp, seqs.ts)
    tn = PROJ_TN
    n_slab = w_in_bf.shape[1] // tn
    npt, p_spec, s_spec = _two_stream_specs(seqs, tm, grid_rank=2)
    col_scale = jnp.ones((1, w_in_bf.shape[1]), F32).at[:, :NA_WIDTH].set(NA_Q_SCALE)
    return pl.pallas_call(
        functools.partial(_inproj_kernel, npt),
        grid=(n // tm, n_slab),
        in_specs=[p_spec, s_spec,
                  pl.BlockSpec((None, N_MOD, D_MODEL), lambda i, j: (seqs.info(i * tm)[0], 0, 0)),
                  pl.BlockSpec((1, D_MODEL), lambda i, j: (0, 0)),
                  pl.BlockSpec((D_MODEL, tn), lambda i, j: (0, j)),
                  pl.BlockSpec((1, tn), lambda i, j: (0, j))],
        out_specs=pl.BlockSpec((None, tm, tn), lambda i, j: (j, i, 0)),
        out_shape=jax.ShapeDtypeStruct((n_slab, n, tn), BF16),
        scratch_shapes=[pltpu.VMEM((tm, D_MODEL), BF16)],
        compiler_params=_params(56, ("arbitrary", "arbitrary")),
        name="inproj",
    )(x_p, x_s, mod, norm_mix.reshape(1, D_MODEL), w_in_bf, col_scale)


def _na_bias_table(rpb):
    col = np.arange(GRID_W)
    cs = np.clip(col - NA_COLS // 2, 0, GRID_W - NA_COLS)
    valid = (col[None, :] >= cs[:, None]) & (col[None, :] < cs[:, None] + NA_COLS)
    coff = col[None, :] - col[:, None] + NA_COLS - 1
    onehot = (coff[None] == np.arange(2 * NA_COLS - 1)[:, None, None]) & valid[None]
    toep = jnp.einsum("hrc,cqk->hrqk", rpb.astype(F32), jnp.asarray(onehot, F32),
                      precision=HIGHEST)
    toep = jnp.where(valid[None, None], toep * LOG2E, MASK_VALUE)
    masked =jnp.full((NA_HEADS, GRID_W, GRID_W), MASK_VALUE, F32)
    cases = (([0] * NA_GROUP, [NA_ROWS - 1 - i for i in range(NA_GROUP)]),
             (list(range(NA_GROUP)), [NA_ROWS // 2 - 1] * NA_GROUP),
             ([NA_GROUP] * NA_GROUP, [NA_ROWS // 2 - 1 - i for i in range(NA_GROUP)]))
    tabs = []
    for first_row, first_off in cases:
        q_rows = []
        for i in range(NA_GROUP):
            blocks = [toep[:, first_off[i] + w - first_row[i]]
                      if 0 <= w - first_row[i] < NA_ROWS else masked for w in range(NA_WIN)]
            q_rows.append(jnp.concatenate(blocks, axis=2))
        tabs.append(jnp.concatenate(q_rows, axis=1))
    return jnp.stack(tabs)


def _na_geometry(seqs, g):
    _, start, length = seqs.info(g * NA_TOK)
    row0 = start // GRID_W
    rows = length // GRID_W
    r0 = g * NA_GROUP - row0
    wb = jnp.clip(r0 - NA_ROWS // 2, 0, rows - NA_WIN)
    case = jnp.where(r0 == 0, 0, jnp.where(r0 == rows - NA_GROUP, 2, 1))
    return (row0 + wb) // NA_GROUP, case


def _na_kernel(q_ref, *refs):
    o_ref = refs[-1]
    for u in range(NA_PER_STEP):
        k_refs = refs[7 * u:7 * u + 3]
        v_refs = refs[7 * u + 3:7 * u + 6]
        bias_ref = refs[7 * u + 6]
        rows = slice(u * NA_TOK, (u + 1) * NA_TOK)
        outs = []
        for h in range(NA_HEADS):
            hs = slice(h * NA_HEAD_DIM, (h + 1) * NA_HEAD_DIM)
            q = q_ref[rows, hs]
            s = [lax.dot_general(q, kr[:, hs], NT_DIMS, preferred_element_type=F32)
                 + bias_ref[h, :, d * NA_TOK:(d + 1) * NA_TOK] for d, kr in enumerate(k_refs)]
            m = jnp.max(jnp.maximum(jnp.maximum(s[0], s[1]), s[2]), axis=-1, keepdims=True)
            p = [jnp.exp2(sd - m) for sd in s]
            l = jnp.sum((p[0] + p[1]) + p[2], axis=-1, keepdims=True)
            o = sum(jnp.dot(pd.astype(BF16), vr[:, hs], preferred_element_type=F32)
                    for pd, vr in zip(p, v_refs))
            outs.append(o / l)
        o_ref[rows, :] = jnp.concatenate(outs, axis=1).astype(o_ref.dtype)


def _na(seqs, proj, bias_tab):
    n = seqs.n
    step_tok = NA_PER_STEP * NA_TOK

    def group_specs(u):
        geometry = lambda s: _na_geometry(seqs, NA_PER_STEP * s + u)
        kv = [pl.BlockSpec((None, NA_TOK, NA_WIDTH),
                           lambda s, slab=slab, d=d: (slab[0], geometry(s)[0] + d, slab[1]))
              for slab in (SLAB_K, SLAB_V) for d in range(3)]
        bias = pl.BlockSpec((None,) + bias_tab.shape[1:], lambda s: (geometry(s)[1], 0, 0, 0))
        return kv + [bias]

    specs = [pl.BlockSpec((None, step_tok, NA_WIDTH), lambda s: (SLAB_Q[0], s, SLAB_Q[1]))]
    operands = [proj]
    for u in range(NA_PER_STEP):
        specs += group_specs(u)
        operands += [proj] * 6 + [bias_tab]
    return pl.pallas_call(
        _na_kernel,
        grid=(n // step_tok,),
        in_specs=specs,
        out_specs=pl.BlockSpec((step_tok, NA_WIDTH), lambda s: (s, 0)),
        out_shape=jax.ShapeDtypeStruct((n, NA_WIDTH), BF16),
        compiler_params=_params(56, ("arbitrary",)),
        name="natten",
    )(*operands)


def _hg_chunk(q, z, v, lb, tri, mask, mid, last, st_ref):
    sig = _sigmoid(z)
    f = lb + (1.0 - lb) * sig
    lf = jnp.log(f)
    kin = (1.0 - lb) * (1.0 - sig)
    hi = lf.astype(BF16)
    lo = (lf - hi.astype(F32)).astype(BF16)
    g2 = jnp.dot(tri, jnp.concatenate([lo, hi], axis=1), preferred_element_type=F32)
    gcum = g2[:, :HG_WIDTH] + g2[:, HG_WIDTH:]
    gm = gcum[mid:mid + 1, :]
    gl = gcum[last:last + 1, :]
    up = jnp.exp(gcum - gm)
    dn = jnp.exp(gm - gcum)
    qa = (q * up).astype(BF16)
    ka = (kin * dn).astype(BF16)
    qe = (q * (up * jnp.exp(gm))).astype(BF16)
    kd = (kin * (dn * jnp.exp(gl - gm))).astype(BF16)
    eg = jnp.exp(gl)
    vb = v.astype(BF16)
    outs = []
    for h in range(HG_HEADS):
        hs = slice(h * HG_KEY_DIM, (h + 1) * HG_KEY_DIM)
        a = lax.dot_general(qa[:, hs], ka[:, hs], NT_DIMS, preferred_element_type=F32)
        a = jnp.where(mask, a, 0.0)
        st = st_ref[h]
        o = jnp.dot(a.astype(BF16), vb[:, hs], preferred_element_type=F32)
        o = o + lax.dot_general(qe[:, hs], st.astype(BF16), NT_DIMS, preferred_element_type=F32)
        st_ref[h] = st * eg[:, hs] + lax.dot_general(vb[:, hs], kd[:, hs], TN_DIMS,
                                                    preferred_element_type=F32)
        outs.append(o)
    return jnp.concatenate(outs, axis=1)


def _hg_exact(q_ref, z_ref, v_ref, lb, reverse, st_ref, o_ref, qs, fs, ks, vs):
    sig = jax.nn.sigmoid(z_ref[...].astype(F32))
    qs[...] = q_ref[...].astype(F32)
    fs[...] = lb + (1.0 - lb) * sig
    ks[...] = (1.0 - lb) * (1.0 - sig)
    vs[...] = v_ref[...].astype(F32)
    eye = (lax.broadcasted_iota(jnp.int32, (HG_KEY_DIM, HG_KEY_DIM), 0)
           == lax.broadcasted_iota(jnp.int32, (HG_KEY_DIM, HG_KEY_DIM), 1)).astype(F32)

    def body(i, carry):
        t = HG_STEP - 1 - i if reverse else i
        q_t, f_t, k_t, v_t = (r[pl.ds(t, 1), :] for r in (qs, fs, ks, vs))
        outs = []
        for h in range(HG_HEADS):
            hs = slice(h * HG_KEY_DIM, (h + 1) * HG_KEY_DIM)
            v_col = jnp.sum(eye * v_t[:, hs], axis=1, keepdims=True)
            st = st_ref[h] * f_t[:, hs] + v_col * k_t[:, hs]
            st_ref[h] = st
            o_col = jnp.sum(st * q_t[:, hs], axis=1, keepdims=True)
            outs.append(jnp.sum(eye * o_col, axis=0, keepdims=True))
        o_ref[pl.ds(t, 1), :] = jnp.concatenate(outs, axis=1)
        return carry

    lax.fori_loop(0, HG_STEP, body, 0)


def _hg_kernel(seqs, safe_ref, qf_ref, zf_ref, vf_ref, qb_ref, zb_ref, vb_ref, lb_ref, of_ref,
               ob_ref, stf, stb, qs, fs, ks, vs):
    i = pl.program_id(0)
    nsteps = pl.num_programs(0)
    tf = i * HG_STEP
    tb = (nsteps - 1 - i) * HG_STEP
    _, start_f, _ = seqs.info(tf)
    _, start_b, len_b = seqs.info(tb)

    @pl.when(tf == start_f)
    def _():
        stf[...] = jnp.zeros_like(stf)

    @pl.when(tb + HG_STEP == start_b + len_b)
    def _():
        stb[...] = jnp.zeros_like(stb)

    lb = lb_ref[...]

    @pl.when(safe_ref[0] > 0)
    def _():
        row = lax.broadcasted_iota(jnp.int32, (HG_CHUNK, HG_CHUNK), 0)
        col = lax.broadcasted_iota(jnp.int32, (HG_CHUNK, HG_CHUNK), 1)
        lower = row >= col
        upper = col >= row
        tri_f = lower.astype(BF16)
        tri_b = upper.astype(BF16)
        nchunk = HG_STEP // HG_CHUNK
        for c in range(nchunk):
            cs = slice(c * HG_CHUNK, (c + 1) * HG_CHUNK)
            of_ref[cs, :] = _hg_chunk(qf_ref[cs, :].astype(F32), zf_ref[cs, :].astype(F32),
                                      vf_ref[cs, :].astype(F32), lb, tri_f, lower,
                                      HG_CHUNK // 2 - 1, HG_CHUNK - 1, stf)
            cb = nchunk - 1 - c
            bs = slice(cb * HG_CHUNK, (cb + 1) * HG_CHUNK)
            ob_ref[bs, :] = _hg_chunk(qb_ref[bs, :].astype(F32), zb_ref[bs, :].astype(F32),
                                      vb_ref[bs, :].astype(F32), lb, tri_b, upper,
                                      HG_CHUNK // 2, 0, stb)

    @pl.when(safe_ref[0] == 0)
    def _():
        _hg_exact(qf_ref, zf_ref, vf_ref, lb, False, stf, of_ref, qs, fs, ks, vs)
        _hg_exact(qb_ref, zb_ref, vb_ref, lb, True, stb, ob_ref, qs, fs, ks, vs)


def _hgrn(seqs, proj, lb):
    n = seqs.n
    nsteps = n // HG_STEP

    def spec(slab, rev):
        if rev:
            return pl.BlockSpec((None, HG_STEP, HG_WIDTH), lambda i: (slab[0], nsteps - 1 - i, slab[1]))
        return pl.BlockSpec((None, HG_STEP, HG_WIDTH), lambda i: (slab[0], i, slab[1]))

    safe = (jnp.max(-jnp.log(lb)) * (HG_CHUNK // 2) < HG_EXP_LIMIT).astype(jnp.int32).reshape(1)
    step_scratch = pltpu.VMEM((HG_STEP, HG_WIDTH), F32)
    return pl.pallas_call(
        functools.partial(_hg_kernel, seqs),
        grid=(nsteps,),
        in_specs=[pl.BlockSpec(memory_space=pltpu.SMEM),
                  spec(SLAB_HQ, False), spec(SLAB_FF, False), spec(SLAB_HI, False),
                  spec(SLAB_HQ, True), spec(SLAB_FB, True), spec(SLAB_HI, True),
                  pl.BlockSpec((1, HG_WIDTH), lambda i: (0, 0))],
        out_specs=[pl.BlockSpec((HG_STEP, HG_WIDTH), lambda i: (i, 0)),
                   pl.BlockSpec((HG_STEP, HG_WIDTH), lambda i: (nsteps - 1 - i, 0))],
        out_shape=[jax.ShapeDtypeStruct((n, HG_WIDTH), F32)] * 2,
        scratch_shapes=[pltpu.VMEM((HG_HEADS, HG_KEY_DIM, HG_KEY_DIM), F32)] * 2
        + [step_scratch] * 4,
        compiler_params=_params(32, ("arbitrary",)),
        name="hgrn2",
    )(safe, proj, proj, proj, proj, proj, proj, lb.reshape(1, HG_WIDTH))


def _merge_kernel(npt, xp_ref, xs_ref, att_ref, of_ref, ob_ref, hg_ref, ga_ref, gb_ref, mod_ref,
                  hgn_ref, nffn_ref, wa_ref, wb_ref, wo_ref, wr_ref, x1_ref, u2_ref, lg_ref):
    x = jnp.where(pl.program_id(0) < npt, xp_ref[...], xs_ref[...])
    o = of_ref[...] + ob_ref[...]
    parts = []
    for h in range(HG_HEADS):
        hs = slice(h * HG_KEY_DIM, (h + 1) * HG_KEY_DIM)
        parts.append(_rms(o[:, hs]))
    on = jnp.concatenate(parts, axis=1) * hgn_ref[...]
    gate = hg_ref[...].astype(F32)
    hb = (on * (gate * _sigmoid(gate))).astype(BF16)
    ya = jnp.dot(att_ref[...], wa_ref[...], preferred_element_type=F32)
    yb = jnp.dot(hb, wb_ref[...], preferred_element_type=F32)
    merged = (_sigmoid(ga_ref[...].astype(F32)) * ya
              + _sigmoid(gb_ref[...].astype(F32)) * yb)
    x1 = x + mod_ref[2:3, :] * jnp.dot(merged.astype(BF16), wo_ref[...],
                                       preferred_element_type=F32)
    x1_ref[...] = x1
    u2 = _rms(x1) * nffn_ref[...] * (1.0 + mod_ref[4:5, :]) + mod_ref[3:4, :]
    u2_ref[0], u2_ref[1] = _pack_rows(u2)
    w_hi = wr_ref[0]
    u_hi = u2.astype(BF16)
    u_lo = (u2 - u_hi.astype(F32)).astype(BF16)
    lg_ref[...] = (lax.dot_general(w_hi, u_hi, NT_DIMS, preferred_element_type=F32)
                   + (lax.dot_general(w_hi, u_lo, NT_DIMS, preferred_element_type=F32)
                      + lax.dot_general(wr_ref[1], u_hi, NT_DIMS, preferred_element_type=F32)))


def _split_hi_lo(w):
    hi = w.astype(BF16)
    return jnp.stack([hi, (w - hi.astype(F32)).astype(BF16)])


def _merge(seqs, x_p, x_s, att, o_f, o_b, proj, mod, hg_norm, norm_ffn, wa, wb, wo, wr):
    n = seqs.n
    tm = 512
    tok = lambda i: (i, 0)
    const = lambda i: (0, 0)
    npt, p_spec, s_spec = _two_stream_specs(seqs, tm)
    return pl.pallas_call(
        functools.partial(_merge_kernel, npt),
        grid=(n // tm,),
        in_specs=[p_spec, s_spec,
                  pl.BlockSpec((tm, NA_WIDTH), tok),
                  pl.BlockSpec((tm, HG_WIDTH), tok),
                  pl.BlockSpec((tm, HG_WIDTH), tok),
                  pl.BlockSpec((None, tm, HG_WIDTH), lambda i: (SLAB_HG[0], i, SLAB_HG[1])),
                  pl.BlockSpec((None, tm, D_MODEL), lambda i: (SLAB_GA[0], i, SLAB_GA[1])),
                  pl.BlockSpec((None, tm, D_MODEL), lambda i: (SLAB_GB[0], i, SLAB_GB[1])),
                  pl.BlockSpec((None, N_MOD, D_MODEL), lambda i: (seqs.info(i * tm)[0], 0, 0)),
                  pl.BlockSpec((1, HG_WIDTH), const),
                  pl.BlockSpec((1, D_MODEL), const),
                  pl.BlockSpec((NA_WIDTH, D_MODEL), const),
                  pl.BlockSpec((HG_WIDTH, D_MODEL), const),
                  pl.BlockSpec((D_MODEL, D_MODEL), const),
                  pl.BlockSpec((2, N_EXPERTS, D_MODEL), lambda i: (0, 0, 0))],
        out_specs=[pl.BlockSpec((tm, D_MODEL), tok),
                   pl.BlockSpec((2, tm, PACK_W), lambda i: (0, i, 0)),
                   pl.BlockSpec((N_EXPERTS, tm), lambda i: (0, i))],
        out_shape=[jax.ShapeDtypeStruct((n, D_MODEL), F32),
                   jax.ShapeDtypeStruct((2, n, PACK_W), jnp.uint32),
                   jax.ShapeDtypeStruct((N_EXPERTS, n), F32)],
        compiler_params=_params(56, ("arbitrary",)),
        name="merge",
    )(x_p, x_s, att, o_f, o_b, proj, proj, proj, mod, hg_norm.reshape(1, HG_WIDTH),
      norm_ffn.reshape(1, D_MODEL), wa, wb, wo, wr)


def _expert_kernel(be_ref, nused_ref, next_ref, slot_ref, xs_hbm, wg_hbm, wu_hbm, wd_hbm, ys_ref,
                   wg_s, wu_s, wd_s, wg_st, wu_st, wd_st, sems, xring, xsems):
    step = pl.program_id(0)
    step_rows = MOE_SUB * MOE_BLOCK
    n_steps = (nused_ref[0] + MOE_SUB - 1) // MOE_SUB
    ring = step % MOE_RING

    def rows_copy(s):
        src = xs_hbm.at[:, pl.ds(pl.multiple_of(s * step_rows, step_rows), step_rows), :]
        return pltpu.make_async_copy(src, xring.at[s % MOE_RING], xsems.at[s % MOE_RING])

    @pl.when(step < n_steps)
    def _():
        @pl.when(step == 0)
        def _():
            rows_copy(0).start()

            @pl.when(n_steps > 1)
            def _():
                rows_copy(1).start()

        rows_copy(step).wait()

        @pl.when(step + 2 < n_steps)
        def _():
            rows_copy(step + 2).start()

    def weight_copies(e, slot):
        pairs = ((wg_hbm, wg_st), (wu_hbm, wu_st), (wd_hbm, wd_st))
        return [pltpu.make_async_copy(hbm.at[e], stage.at[slot], sems.at[slot, j])
                for j, (hbm, stage) in enumerate(pairs)]

    def block(i, rows):
        e = be_ref[i]
        slot = slot_ref[i]
        nxt = next_ref[i]
        last = nused_ref[0] - 1
        first = (i == 0) | (e != be_ref[jnp.maximum(i - 1, 0)])
        only = (i == last) | (be_ref[jnp.minimum(i + 1, last)] != e)
        second = (i >= 1) & jnp.logical_not(first) & (
            (i == 1) | (be_ref[jnp.maximum(i - 2, 0)] != e))

        @pl.when(first)
        def _():
            @pl.when(i == 0)
            def _():
                for c in weight_copies(e, slot):
                    c.start()

            for c in weight_copies(e, slot):
                c.wait()

            @pl.when(nxt >= 0)
            def _():
                ahead = weight_copies(nxt, 1 - slot)
                ahead[0].start()
                ahead[1].start()

                @pl.when(only)
                def _():
                    ahead[2].start()

            wg_s[...] = wg_st[slot].astype(BF16)
            wu_s[...] = wu_st[slot].astype(BF16)
            wd_s[...] = wd_st[slot].astype(BF16)

        @pl.when(second & (nxt >= 0))
        def _():
            weight_copies(nxt, 1 - slot)[2].start()

        xq = _unpack_rows(xring[ring, 0, rows, :], xring[ring, 1, rows, :])
        a = _dot_quarters(xq, wg_s)
        b = _dot_quarters(xq, wu_s)
        h = (a * _sigmoid(a) * b).astype(BF16)
        ys_ref[0, rows, :], ys_ref[1, rows, :] = _pack_rows(
            jnp.dot(h, wd_s[...], preferred_element_type=F32))

    for j in range(MOE_SUB):
        i = pl.program_id(0) * MOE_SUB + j
        pl.when(i < nused_ref[0])(functools.partial(block, i, slice(j * MOE_BLOCK, (j + 1) * MOE_BLOCK)))


def _experts(xs, plan, wg, wu, wd):
    n_rows = xs.shape[1]
    step_rows = MOE_SUB * MOE_BLOCK
    rows_map = lambda i, be, nu, nx, sl: (0, jnp.minimum(i, (nu[0] - 1) // MOE_SUB), 0)
    whole = pl.BlockSpec(memory_space=pl.ANY)
    grid_spec = pltpu.PrefetchScalarGridSpec(
        num_scalar_prefetch=4,
        grid=(n_rows // step_rows,),
        in_specs=[whole, whole, whole, whole],
        out_specs=pl.BlockSpec((2, step_rows, PACK_W), rows_map),
        scratch_shapes=[pltpu.VMEM((D_MODEL, D_EXPERT), BF16),
                        pltpu.VMEM((D_MODEL, D_EXPERT), BF16),
                        pltpu.VMEM((D_EXPERT, D_MODEL), BF16),
                        pltpu.VMEM((2, D_MODEL, D_EXPERT), F32),
                        pltpu.VMEM((2, D_MODEL, D_EXPERT), F32),
                        pltpu.VMEM((2, D_EXPERT, D_MODEL), F32),
                        pltpu.SemaphoreType.DMA((2, 3)),
                        pltpu.VMEM((MOE_RING, 2, step_rows, PACK_W), jnp.uint32),
                        pltpu.SemaphoreType.DMA((MOE_RING,))],
    )
    return pl.pallas_call(
        _expert_kernel,
        grid_spec=grid_spec,
        out_shape=jax.ShapeDtypeStruct((2, n_rows, PACK_W), jnp.uint32),
        compiler_params=_params(56, ("arbitrary",)),
        name="experts",
    )(*plan, xs, wg, wu, wd)


ROUTE_TILE = 128
ROUTE_STEP = 1024
NEG_INF = float("-inf")


def _first_max(x, idx, big):
    m = jnp.max(x, axis=0, keepdims=True)
    first = jnp.min(jnp.where(x == m, idx, big), axis=0, keepdims=True)
    return m, first


def _route_tile(lg, bias, tri, base):
    per_group = N_EXPERTS // N_GROUPS
    scores = jax.nn.sigmoid(lg)
    sel = scores + bias
    lrow = lax.broadcasted_iota(jnp.int32, (per_group, ROUTE_TILE), 0)
    gs = []
    for g in range(N_GROUPS):
        x = sel[g * per_group:(g + 1) * per_group]
        m1, first = _first_max(x, lrow, per_group)
        m2 = jnp.max(jnp.where(lrow == first, NEG_INF, x), axis=0, keepdims=True)
        gs.append(m1 + m2)
    cur = jnp.concatenate(gs, axis=0)
    grow = lax.broadcasted_iota(jnp.int32, (N_GROUPS, ROUTE_TILE), 0)
    chosen = jnp.zeros((N_GROUPS, ROUTE_TILE), jnp.int32)
    for _ in range(TOPK_GROUPS):
        _, first = _first_max(cur, grow, N_GROUPS)
        hit = grow == first
        chosen = jnp.where(hit, 1, chosen)
        cur = jnp.where(hit, NEG_INF, cur)
    cur = jnp.concatenate(
        [jnp.where(chosen[g:g + 1] > 0, sel[g * per_group:(g + 1) * per_group], NEG_INF)
         for g in range(N_GROUPS)], axis=0)
    row = lax.broadcasted_iota(jnp.int32, (N_EXPERTS, ROUTE_TILE), 0)
    member = jnp.zeros((N_EXPERTS, ROUTE_TILE), F32)
    es, ws = [], []
    for _ in range(TOP_K):
        _, first = _first_max(cur, row, N_EXPERTS)
        hit = row == first
        es.append(first)
        ws.append(jnp.sum(jnp.where(hit, scores, 0.0), axis=0, keepdims=True))
        cur = jnp.where(hit, NEG_INF, cur)
        member = jnp.where(hit, 1.0, member)
    e = jnp.concatenate(es, axis=0)
    w = jnp.concatenate(ws, axis=0)
    w = w / jnp.sum(w, axis=0, keepdims=True) * ROUTE_SCALE
    before = jnp.dot(member.astype(BF16), tri, preferred_element_type=F32) + base
    rank = jnp.concatenate(
        [jnp.sum(jnp.where(row == es[k], before, 0.0), axis=0, keepdims=True) for k in range(TOP_K)],
        axis=0)
    return e, w, rank.astype(jnp.int32), member


def _route_kernel(lg_ref, b_ref, e_ref, w_ref, r_ref, cnt_ref, base):
    @pl.when(pl.program_id(0) == 0)
    def _():
        base[...] = jnp.zeros_like(base)

    r_i = lax.broadcasted_iota(jnp.int32, (ROUTE_TILE, ROUTE_TILE), 0)
    c_i = lax.broadcasted_iota(jnp.int32, (ROUTE_TILE, ROUTE_TILE), 1)
    tri = (r_i < c_i).astype(BF16)
    bias = b_ref[...]
    for j in range(ROUTE_STEP // ROUTE_TILE):
        ls = slice(j * ROUTE_TILE, (j + 1) * ROUTE_TILE)
        e, w, rank, member = _route_tile(lg_ref[:, ls], bias, tri, base[...])
        e_ref[:, ls] = e
        w_ref[:, ls] = w
        r_ref[:, ls] = rank
        base[...] = base[...] + jnp.sum(member, axis=1, keepdims=True)
    cnt_ref[...] = base[...]


def _route(logits_t, b_router):
    n = logits_t.shape[1]
    tok = lambda i: (0, i)
    return pl.pallas_call(
        _route_kernel,
        grid=(n // ROUTE_STEP,),
        in_specs=[pl.BlockSpec((N_EXPERTS, ROUTE_STEP), tok),
                  pl.BlockSpec((N_EXPERTS, 1), lambda i: (0, 0))],
        out_specs=[pl.BlockSpec((TOP_K, ROUTE_STEP), tok),
                   pl.BlockSpec((TOP_K, ROUTE_STEP), tok),
                   pl.BlockSpec((TOP_K, ROUTE_STEP), tok),
                   pl.BlockSpec((N_EXPERTS, 1), lambda i: (0, 0))],
        out_shape=[jax.ShapeDtypeStruct((TOP_K, n), jnp.int32),
                   jax.ShapeDtypeStruct((TOP_K, n), F32),
                   jax.ShapeDtypeStruct((TOP_K, n), jnp.int32),
                   jax.ShapeDtypeStruct((N_EXPERTS, 1), F32)],
        scratch_shapes=[pltpu.VMEM((N_EXPERTS, 1), F32)],
        compiler_params=_params(32, ("arbitrary",)),
        name="route",
    )(logits_t, b_router.reshape(N_EXPERTS, 1).astype(F32))


def _pos_kernel(e_ref, r_ref, ps_ref, pos_ref):
    row = lax.broadcasted_iota(jnp.int32, (N_EXPERTS, ROUTE_TILE), 0)
    pstart = ps_ref[...]
    for j in range(ROUTE_STEP // ROUTE_TILE):
        ls = slice(j * ROUTE_TILE, (j + 1) * ROUTE_TILE)
        e = e_ref[:, ls]
        off = jnp.concatenate(
            [jnp.sum(jnp.where(row == e[k:k + 1], pstart, 0.0), axis=0, keepdims=True)
             for k in range(TOP_K)], axis=0)
        pos_ref[:, ls] = off.astype(jnp.int32) + r_ref[:, ls]


def _positions(eidx, rank, pstarts):
    n = eidx.shape[1]
    tok = lambda i: (0, i)
    return pl.pallas_call(
        _pos_kernel,
        grid=(n // ROUTE_STEP,),
        in_specs=[pl.BlockSpec((TOP_K, ROUTE_STEP), tok),
                  pl.BlockSpec((TOP_K, ROUTE_STEP), tok),
                  pl.BlockSpec((N_EXPERTS, 1), lambda i: (0, 0))],
        out_specs=pl.BlockSpec((TOP_K, ROUTE_STEP), tok),
        out_shape=jax.ShapeDtypeStruct((TOP_K, n), jnp.int32),
        compiler_params=_params(32, ("arbitrary",)),
        name="positions",
    )(eidx, rank, pstarts.reshape(N_EXPERTS, 1).astype(F32))


def _block_tables(counts, n_blocks):
    counts = counts.reshape(N_EXPERTS).astype(jnp.int32)
    padded = (counts + MOE_BLOCK - 1) // MOE_BLOCK * MOE_BLOCK
    pend = jnp.cumsum(padded)
    pstarts = pend - padded
    first_row = jnp.arange(n_blocks, dtype=jnp.int32) * MOE_BLOCK
    block_e = jnp.sum((pend[None, :] <= first_row[:, None]).astype(jnp.int32), axis=1)
    block_e = jnp.minimum(block_e, N_EXPERTS - 1).astype(jnp.int32)
    n_used = (pend[-1] // MOE_BLOCK).astype(jnp.int32).reshape(1)
    ids = jnp.arange(N_EXPERTS, dtype=jnp.int32)
    used = padded > 0
    ordinal = jnp.cumsum(used.astype(jnp.int32)) - 1
    from_here = jnp.flip(lax.cummin(jnp.flip(jnp.where(used, ids, N_EXPERTS))))
    after = jnp.concatenate([from_here[1:], jnp.full((1,), N_EXPERTS, jnp.int32)])
    after = jnp.where(after >= N_EXPERTS, -1, after)
    mine = block_e[:, None] == ids[None, :]
    next_e = jnp.sum(jnp.where(mine, after[None, :], 0), axis=1).astype(jnp.int32)
    slot = jnp.sum(jnp.where(mine, ordinal[None, :] % 2, 0), axis=1).astype(jnp.int32)
    return pstarts, (block_e, n_used, next_e, slot)


SC_WINDOW = 128


def _sc_mesh():
    return plsc.VectorSubcoreMesh(core_axis_name="core", subcore_axis_name="subcore")


def _both_halves(pos, n_rows):
    return jnp.concatenate([pos, pos + n_rows]).reshape(1, -1)


def _sc_dispatch(rows, pos, n_rows):
    _, n, w = rows.shape
    tiles = n // SC_WINDOW
    top_k = pos.shape[0] // n
    idx = jnp.transpose(pos.reshape(top_k, tiles, SC_WINDOW), (1, 0, 2))
    idx = jnp.stack([idx, idx + n_rows]).reshape(1, -1)

    @functools.partial(pl.kernel, out_type=jax.ShapeDtypeStruct((2 * n_rows, w), rows.dtype),
                       mesh=_sc_mesh(), scratch_types=[])
    def scatter_kernel(x_hbm, i_hbm, o_hbm):
        def body(x_vmem, i_vmem):
            pltpu.sync_copy(x_vmem, o_hbm.at[i_vmem.at[0]])

        pltpu.emit_pipeline(
            body,
            grid=(2 * tiles * top_k,),
            in_specs=[pl.BlockSpec((SC_WINDOW, w), lambda i: (i // top_k, 0)),
                      pl.BlockSpec((1, SC_WINDOW), lambda i: (0, i))],
            out_specs=[],
            core_axis_name=("core", "subcore"),
            dimension_semantics=(pltpu.PARALLEL,),
        )(x_hbm, i_hbm)

    out = scatter_kernel(rows.reshape(2 * n, w), idx)
    return out.reshape(2, n_rows, w)


def _sc_gather(table, pos):
    _, n_rows, w = table.shape
    m = pos.shape[0]

    @functools.partial(pl.kernel, out_type=jax.ShapeDtypeStruct((2 * m, w), table.dtype),
                       mesh=_sc_mesh(), scratch_types=[])
    def gather_kernel(t_hbm, i_hbm, o_hbm):
        def body(i_vmem, o_vmem):
            pltpu.sync_copy(t_hbm.at[i_vmem.at[0]], o_vmem)

        pltpu.emit_pipeline(
            body,
            grid=(2 * m // SC_WINDOW,),
            in_specs=[pl.BlockSpec((1, SC_WINDOW), lambda i: (0, i))],
            out_specs=[pl.BlockSpec((SC_WINDOW, w), lambda i: (i, 0))],
            core_axis_name=("core", "subcore"),
            dimension_semantics=(pltpu.PARALLEL,),
        )(i_hbm, o_hbm)

    out = gather_kernel(table.reshape(2 * n_rows, w), _both_halves(pos, n_rows))
    return out.reshape(2, m, w)


def _final_kernel(npt, x1_ref, u2_ref, yg_ref, w_ref, mod_ref, nf_ref, wsg_ref, wsu_ref, wsd_ref,
                  op_ref, os_ref):
    uq = _unpack_rows(u2_ref[0], u2_ref[1])
    a = _dot_quarters(uq, wsg_ref)
    b = _dot_quarters(uq, wsu_ref)
    f = jnp.dot((a * _sigmoid(a) * b).astype(BF16), wsd_ref[...], preferred_element_type=F32)
    w = w_ref[...]
    fq = [f[:, q * PACK_W:(q + 1) * PACK_W] for q in range(4)]
    for k in range(TOP_K):
        yq = _unpack_rows(yg_ref[0, k], yg_ref[1, k], F32)
        fq = [fq[q] + w[:, k:k + 1] * yq[q] for q in range(4)]
    x2 = x1_ref[...] + mod_ref[5:6, :] * jnp.concatenate(fq, axis=1)
    out = _rms(x2) * nf_ref[...]
    i = pl.program_id(0)

    @pl.when(i < npt)
    def _():
        op_ref[...] = out

    @pl.when(i >= npt)
    def _():
        os_ref[...] = out


def _final(seqs, x1, u2, yg, w, mod, norm_final, wsg, wsu, wsd):
    n = seqs.n
    tm = 512
    tok = lambda i: (i, 0)
    const = lambda i: (0, 0)
    npt, p_spec, s_spec = _two_stream_specs(seqs, tm)
    return pl.pallas_call(
        functools.partial(_final_kernel, npt),
        grid=(n // tm,),
        in_specs=[pl.BlockSpec((tm, D_MODEL), tok),
                  pl.BlockSpec((2, tm, PACK_W), lambda i: (0, i, 0)),
                  pl.BlockSpec((2, TOP_K, tm, PACK_W), lambda i: (0, 0, i, 0)),
                  pl.BlockSpec((tm, TOP_K), tok),
                  pl.BlockSpec((None, N_MOD, D_MODEL), lambda i: (seqs.info(i * tm)[0], 0, 0)),
                  pl.BlockSpec((1, D_MODEL), const),
                  pl.BlockSpec((D_MODEL, D_EXPERT), const),
                  pl.BlockSpec((D_MODEL, D_EXPERT), const),
                  pl.BlockSpec((D_EXPERT, D_MODEL), const)],
        out_specs=[p_spec, s_spec],
        out_shape=[jax.ShapeDtypeStruct((seqs.np_, D_MODEL), F32),
                   jax.ShapeDtypeStruct((n - seqs.np_, D_MODEL), F32)],
        compiler_params=_params(48, ("arbitrary",)),
        name="final",
    )(x1, u2, yg, w, mod, norm_final.reshape(1, D_MODEL), wsg, wsu, wsd)


def _layer(seqs, x_p, x_s, c, w_ada, b_ada, norm_mix, w_in, na_rpb, hg_lb, hg_norm, w_branch_a,
           w_branch_b, w_out, norm_ffn, w_router, b_router, w_exp_gate, w_exp_up, w_exp_down,
           w_sh_gate, w_sh_up, w_sh_down, norm_final):
    n = seqs.n
    c_rows = -(-seqs.nseq // 8) * 8
    c_pad = jnp.zeros((c_rows, D_MODEL), F32).at[:seqs.nseq].set(c)
    mod = _ada(c_pad, w_ada[0], b_ada[0])[:seqs.nseq].reshape(seqs.nseq, N_MOD, D_MODEL)
    lb = jnp.cumsum(jax.nn.softmax(hg_lb.astype(F32), axis=0), axis=0)[0]

    proj = _inproj(seqs, x_p, x_s, mod, norm_mix[0], w_in[0].astype(BF16))
    att = _na(seqs, proj, _na_bias_table(na_rpb[0]))
    o_f, o_b = _hgrn(seqs, proj, lb)
    x1, u2, logits = _merge(seqs, x_p, x_s, att, o_f, o_b, proj, mod, hg_norm[0], norm_ffn[0],
                            w_branch_a[0].astype(BF16), w_branch_b[0].astype(BF16),
                            w_out[0].astype(BF16), _split_hi_lo(w_router[0].T))

    eidx, w, rank, counts = _route(logits, b_router[0])
    step_rows = MOE_SUB * MOE_BLOCK
    n_rows = -(-(n * TOP_K + N_EXPERTS * MOE_BLOCK) // step_rows) * step_rows
    pstarts, plan = _block_tables(counts, n_rows // MOE_BLOCK)
    pos = _positions(eidx, rank, pstarts).reshape(-1)
    xs = _sc_dispatch(u2, pos, n_rows)
    ys = _experts(xs, plan, w_exp_gate[0], w_exp_up[0], w_exp_down[0])
    yg = _sc_gather(ys, pos).reshape(2, TOP_K, n, PACK_W)
    return _final(seqs, x1, u2, yg, w.T, mod, norm_final, w_sh_gate[0].astype(BF16),
                  w_sh_up[0].astype(BF16), w_sh_down[0].astype(BF16))


def kernel(x_prompt, x_sample, c_prompt, c_sample, w_ada, b_ada, norm_mix, w_in, na_rpb, hg_lb, hg_norm, w_branch_a, w_branch_b, w_out, norm_ffn, w_router, b_router, w_exp_gate, w_exp_up, w_exp_down, w_sh_gate, w_sh_up, w_sh_down, norm_final):
    bp, tp, _ = x_prompt.shape
    bs, ts, _ = x_sample.shape
    for t in (tp, ts):
        assert t % 1024 == 0, "sequence lengths must be multiples of the 1024-token tiles"
        assert t // GRID_W >= NA_WIN, "a sequence needs at least NA_WIN grid rows"
    seqs = _Seqs(bp, tp, bs, ts)
    c = jnp.concatenate([c_prompt, c_sample])
    y_p, y_s = _layer(seqs, x_prompt.reshape(bp * tp, D_MODEL), x_sample.reshape(bs * ts, D_MODEL),
                      c, w_ada, b_ada, norm_mix, w_in, na_rpb, hg_lb, hg_norm, w_branch_a,
                      w_branch_b, w_out, norm_ffn, w_router, b_router, w_exp_gate, w_exp_up,
                      w_exp_down, w_sh_gate, w_sh_up, w_sh_down, norm_final)
    return (y_p.reshape(bp, tp, D_MODEL), y_s.reshape(bs, ts, D_MODEL))
```
